```python
import math
import jax, jax.numpy as jnp
from jax import lax
import numpy as np

D_MODEL = 1024
BATCH = 32
SEQ = 256
DEPTH = 4
DEC_BATCH = 2
DEC_SEQ = 4096
PAST_LEN = 256

GRID_W = 64
EPS = 1e-6
ROPE_THETA = 10000.0
Q_BLOCK = 128
H_A = 6
Q_RANK = 256
KV_RANK = 128
NOPE_A = 64
ROPE_A = 32
V_A = 64
QK_A = NOPE_A + ROPE_A
HY_CH = 256
HY_ORDER = 2
HY_EMB = 33
HY_FO = 64
HY_INNER = 2
HY_MIN_DECAY = math.log(1e-2) / 1.5
HY_MAX_DECAY = math.log(1e-2) / 0.3
H_C = 6
KV_C = 2
HD_C = 64
IN_W = Q_RANK + KV_RANK + ROPE_A + 3 * HY_CH + (H_C + 2 * KV_C) * HD_C
W_A = H_A * V_A
W_B = HY_CH
W_C = H_C * HD_C
MIX_W = W_A + W_B + W_C
N_EXPERTS = 32
TOP_K = 4
D_FF = 1024
SWIGLU_LIMIT = 7.0
SWIGLU_ALPHA = 1.702
MOE_BLOCK = 128

kernel_name = 'hybrid_mla_hyena_gqa_moe_diffusion_step'


def rms_norm(x, g=None):
    xf = x.astype(jnp.float32)
    y = xf * lax.rsqrt(jnp.mean(xf * xf, axis=-1, keepdims=True) + EPS)
    if g is not None:
        y = y * g.astype(jnp.float32)
    return y.astype(x.dtype)


def axial_rope(seq_len, rot_dim):
    n_rows = seq_len // GRID_W
    rows = jnp.repeat(jnp.arange(n_rows, dtype=jnp.float32), GRID_W)
    cols = jnp.tile(jnp.arange(GRID_W, dtype=jnp.float32), n_rows)
    axis_dim = rot_dim // 2
    inv_freq = ROPE_THETA ** (-jnp.arange(0, axis_dim, 2, dtype=jnp.float32) / axis_dim)
    ang = jnp.concatenate([rows[:, None] * inv_freq, cols[:, None] * inv_freq], axis=-1)
    return jnp.cos(ang), jnp.sin(ang)


def apply_rope(x, cos, sin):
    xf = x.astype(jnp.float32)
    half = x.shape[-1] // 2
    x1, x2 = xf[..., :half], xf[..., half:]
    c = cos[None, :, None, :]
    s = sin[None, :, None, :]
    return jnp.concatenate([x1 * c - x2 * s, x1 * s + x2 * c], axis=-1).astype(x.dtype)


def block_attention(q, k, v, scale):
    B, Lq, Hk, G, dk = q.shape
    dv = v.shape[-1]
    nb = Lq // Q_BLOCK
    qb = jnp.moveaxis(q.reshape(B, nb, Q_BLOCK, Hk, G, dk), 1, 0)

    def one_block(qi):
        s = jnp.einsum('bqhgd,bkhd->bhgqk', qi, k, preferred_element_type=jnp.float32) * scale
        p = jax.nn.softmax(s, axis=-1)
        return jnp.einsum('bhgqk,bkhd->bqhgd', p.astype(v.dtype), v)

    o = lax.map(one_block, qb)
    return jnp.moveaxis(o, 0, 1).reshape(B, Lq, Hk * G * dv)


def mla_keys_values(c_kv, k_pe, pl):
    B, L, _ = c_kv.shape
    kv = (c_kv @ pl['mla_w_ukv']).reshape(B, L, H_A, NOPE_A + V_A)
    k = jnp.concatenate([kv[..., :NOPE_A], jnp.broadcast_to(k_pe[:, :, None, :], (B, L, H_A, ROPE_A))], axis=-1)
    return rms_norm(k, pl['mla_g_k']), kv[..., NOPE_A:]


def hyena_kernel_fft(L, pl):
    f32 = jnp.float32
    t = jnp.linspace(0.0, 1.0, L, dtype=f32)[:, None]
    bands = (HY_EMB - 1) // 2
    f = jnp.linspace(1e-4, bands - 1, bands, dtype=f32)[None, :]
    w = 2.0 * math.pi * jnp.arange(L, dtype=f32)[:, None] / L
    z = jnp.concatenate([t, jnp.cos(f * w), jnp.sin(f * w)], axis=-1)
    freq = pl['hy_freq'].astype(f32)
    a = jnp.sin(freq * (z @ pl['hy_w_in'].astype(f32) + pl['hy_b_in'].astype(f32)))
    for i in range(HY_INNER):
        a = jnp.sin(freq * (a @ pl['hy_w_mid'][i].astype(f32) + pl['hy_b_mid'][i].astype(f32)))
    h = (a @ pl['hy_w_out'].astype(f32) + pl['hy_b_out'].astype(f32)).reshape(L, 2, HY_ORDER, HY_CH)
    deltas = jnp.abs(jnp.linspace(HY_MIN_DECAY, HY_MAX_DECAY, HY_CH, dtype=f32))
    h = h * jnp.exp(-t[:, :, None, None] * deltas)
    kern = jnp.concatenate([h[:, 0], jnp.zeros((1, HY_ORDER, HY_CH), f32), h[:0:-1, 1]], axis=0)
    kern = kern / (jnp.sum(jnp.abs(kern), axis=0, keepdims=True) + EPS)
    return jnp.fft.rfft(kern, axis=0)


def hyena(u, pl):
    B, L, _ = u.shape
    cw = pl['hy_conv_w']
    up = jnp.pad(u, ((0, 0), (1, 1), (0, 0)))
    z = up[:, :-2] * cw[0] + up[:, 1:-1] * cw[1] + up[:, 2:] * cw[2] + pl['hy_conv_b']
    v, x1, x2 = jnp.split(z.astype(jnp.float32), 3, axis=-1)
    kf = hyena_kernel_fft(L, pl)
    skip = pl['hy_skip'].astype(jnp.float32)

    def long_conv(s, o):
        sf = jnp.fft.rfft(s, n=2 * L, axis=1)
        y = jnp.fft.irfft(sf * kf[:, o], n=2 * L, axis=1)[:, :L]
        return y + s * skip[o]

    y = x1 * long_conv(v, 0)
    y = x2 * long_conv(y, 1)
    return y.astype(u.dtype)


def mixers(h, pl, ctx, rope):
    B, L, _ = h.shape
    cuts = tuple(int(i) for i in np.cumsum([Q_RANK, KV_RANK, ROPE_A, 3 * HY_CH, H_C * HD_C, KV_C * HD_C]))
    c_q, c_kv, k_pe, u_hy, q_c, k_c, v_c = jnp.split(h @ pl['w_in'], cuts, axis=-1)
    q_a = (rms_norm(c_q, pl['mla_g_qa']) @ pl['mla_w_uq']).reshape(B, L, H_A, QK_A)
    q_a = rms_norm(q_a, pl['mla_g_q'])
    c_kv = rms_norm(c_kv, pl['mla_g_kva'])
    k_a, v_a = mla_keys_values(c_kv, k_pe, pl)
    q_c = rms_norm(q_c.reshape(B, L, H_C, HD_C), pl['gqa_g_q'])
    k_c = rms_norm(k_c.reshape(B, L, KV_C, HD_C), pl['gqa_g_k'])
    v_c = v_c.reshape(B, L, KV_C, HD_C)
    if rope is None:
        ctx_out = (c_kv, k_pe, k_c, v_c)
    else:
        ctx_out = None
        cos_a, sin_a, cos_c, sin_c = rope
        q_a = jnp.concatenate([q_a[..., :NOPE_A], apply_rope(q_a[..., NOPE_A:], cos_a, sin_a)], axis=-1)
        k_a = jnp.concatenate([k_a[..., :NOPE_A], apply_rope(k_a[..., NOPE_A:], cos_a, sin_a)], axis=-1)
        q_c = apply_rope(q_c, cos_c, sin_c)
        k_c = apply_rope(k_c, cos_c, sin_c)
        ckv_x, kpe_x, kc_x, vc_x = ctx
        kx_a, vx_a = mla_keys_values(ckv_x, kpe_x, pl)
        k_a = jnp.concatenate([kx_a, k_a], axis=1)
        v_a = jnp.concatenate([vx_a, v_a], axis=1)
        k_c = jnp.concatenate([kc_x, k_c], axis=1)
        v_c = jnp.concatenate([vc_x, v_c], axis=1)
    o_a = block_attention(q_a[:, :, :, None, :], k_a, v_a, QK_A ** -0.5)
    o_b = hyena(u_hy, pl)
    o_c = block_attention(q_c.reshape(B, L, KV_C, H_C // KV_C, HD_C), k_c, v_c, HD_C ** -0.5)
    g = pl['g_out']
    merged = jnp.concatenate([rms_norm(o_a, g[:W_A]), rms_norm(o_b, g[W_A:W_A + W_B]),
                              rms_norm(o_c, g[W_A + W_B:])], axis=-1)
    return merged @ pl['w_out'], ctx_out


def moe(h, pl):
    B, L, D = h.shape
    x = h.reshape(B * L, D)
    M = B * L * TOP_K
    logits = (x @ pl['w_router'] + pl['b_router']).astype(jnp.float32)
    top_val, top_idx = lax.top_k(logits, TOP_K)
    gates = jax.nn.softmax(top_val, axis=-1)
    e_flat = top_idx.reshape(M)
    order = jnp.argsort(e_flat)
    e_sorted = e_flat[order]
    tok_sorted = order // TOP_K
    sizes = jnp.bincount(e_flat, length=N_EXPERTS)
    starts = jnp.cumsum(sizes) - sizes
    padded = (sizes + MOE_BLOCK - 1) // MOE_BLOCK * MOE_BLOCK
    pends = jnp.cumsum(padded)
    dest = (pends - padded)[e_sorted] + jnp.arange(M) - starts[e_sorted]
    n_blocks = -(-M // MOE_BLOCK) + N_EXPERTS
    buf = jnp.zeros((n_blocks * MOE_BLOCK, D), x.dtype).at[dest].set(x[tok_sorted])
    block_e = jnp.minimum(jnp.searchsorted(pends, jnp.arange(n_blocks) * MOE_BLOCK, side='right'), N_EXPERTS - 1)
    w_in, b_in, w_o, b_o = pl['w_moe_in'], pl['b_moe_in'], pl['w_moe_out'], pl['b_moe_out']

    def expert_block(args):
        xb, e = args
        gu = xb @ w_in[e] + b_in[e]
        gt = jnp.minimum(gu[:, :D_FF], SWIGLU_LIMIT)
        lin = jnp.clip(gu[:, D_FF:], -SWIGLU_LIMIT, SWIGLU_LIMIT)
        return ((lin + 1.0) * gt * jax.nn.sigmoid(SWIGLU_ALPHA * gt)) @ w_o[e] + b_o[e]

    yb = lax.map(expert_block, (buf.reshape(n_blocks, MOE_BLOCK, D), block_e))
    y_sorted = yb.reshape(n_blocks * MOE_BLOCK, D)[dest] * gates.reshape(M)[order][:, None].astype(x.dtype)
    y = jnp.zeros_like(x).at[tok_sorted].add(y_sorted)
    return y.reshape(B, L, D)


def trunk_layer(x, cond, pl, ctx=None, rope=None):
    mod = jax.nn.silu(cond) @ pl['w_mod'] + pl['b_mod']
    sh_m, sc_m, g_m, sh_f, sc_f, g_f = jnp.split(mod[:, None, :], 6, axis=-1)
    h = rms_norm(x) * (1.0 + sc_m) + sh_m
    mix, ctx_out = mixers(h, pl, ctx, rope)
    x = x + g_m * mix
    h = rms_norm(x) * (1.0 + sc_f) + sh_f
    x = x + g_f * moe(h, pl)
    return x, ctx_out


def setup_inputs(seed: int = 0) -> dict:
    key = jax.random.key(seed)
    kit = iter(jax.random.split(key, 64))

    def nrm(shape, scale=1.0):
        return scale * jax.random.normal(next(kit), shape, jnp.float32)

    def gain(shape):
        return 1.0 + nrm(shape, 0.02)

    d = D_MODEL
    return {
        'x_prompt': nrm((BATCH, SEQ, d)),
        'x_sample': nrm((DEC_BATCH, DEC_SEQ, d)),
        'c': nrm((DEC_BATCH, d)),
        'c_ctx': nrm((d,)),
        'cache_mla_ckv': nrm((DEC_BATCH, DEPTH, PAST_LEN, KV_RANK)),
        'cache_mla_kpe': nrm((DEC_BATCH, DEPTH, PAST_LEN, ROPE_A)),
        'cache_gqa_k': nrm((DEC_BATCH, DEPTH, PAST_LEN, KV_C, HD_C)),
        'cache_gqa_v': nrm((DEC_BATCH, DEPTH, PAST_LEN, KV_C, HD_C)),
        'w_mod': nrm((DEPTH, d, 6 * d), 0.5 * d ** -0.5),
        'b_mod': nrm((DEPTH, 6 * d), 0.02),
        'w_in': nrm((DEPTH, d, IN_W), d ** -0.5),
        'mla_g_qa': gain((DEPTH, Q_RANK)),
        'mla_w_uq': nrm((DEPTH, Q_RANK, H_A * QK_A), Q_RANK ** -0.5),
        'mla_g_kva': gain((DEPTH, KV_RANK)),
        'mla_w_ukv': nrm((DEPTH, KV_RANK, H_A * (NOPE_A + V_A)), KV_RANK ** -0.5),
        'mla_g_q': gain((DEPTH, QK_A)),
        'mla_g_k': gain((DEPTH, QK_A)),
        'hy_conv_w': nrm((DEPTH, 3, 3 * HY_CH), 0.5),
        'hy_conv_b': nrm((DEPTH, 3 * HY_CH), 0.02),
        'hy_w_in': nrm((DEPTH, HY_EMB, HY_FO), HY_EMB ** -0.5),
        'hy_b_in': nrm((DEPTH, HY_FO), 0.1),
        'hy_w_mid': nrm((DEPTH, HY_INNER, HY_FO, HY_FO), HY_FO ** -0.5),
        'hy_b_mid': nrm((DEPTH, HY_INNER, HY_FO), 0.1),
        'hy_w_out': nrm((DEPTH, HY_FO, 2 * HY_ORDER * HY_CH), HY_FO ** -0.5),
        'hy_b_out': nrm((DEPTH, 2 * HY_ORDER * HY_CH), 0.1),
        'hy_freq': gain((DEPTH, HY_FO)),
        'hy_skip': nrm((DEPTH, HY_ORDER, HY_CH), 0.5),
        'gqa_g_q': gain((DEPTH, HD_C)),
        'gqa_g_k': gain((DEPTH, HD_C)),
        'g_out': gain((DEPTH, MIX_W)),
        'w_out': nrm((DEPTH, MIX_W, d), MIX_W ** -0.5),
        'w_router': nrm((DEPTH, d, N_EXPERTS), d ** -0.5),
        'b_router': nrm((DEPTH, N_EXPERTS), 0.01),
        'w_moe_in': nrm((DEPTH, N_EXPERTS, d, 2 * D_FF), d ** -0.5),
        'b_moe_in': nrm((DEPTH, N_EXPERTS, 2 * D_FF), 0.02),
        'w_moe_out': nrm((DEPTH, N_EXPERTS, D_FF, d), D_FF ** -0.5),
        'b_moe_out': nrm((DEPTH, N_EXPERTS, d), 0.02),
    }


def reference(x_prompt, x_sample, c, c_ctx, cache_mla_ckv, cache_mla_kpe, cache_gqa_k, cache_gqa_v,
              w_mod, b_mod, w_in, mla_g_qa, mla_w_uq, mla_g_kva, mla_w_ukv, mla_g_q, mla_g_k,
              hy_conv_w, hy_conv_b, hy_w_in, hy_b_in, hy_w_mid, hy_b_mid, hy_w_out, hy_b_out, hy_freq, hy_skip,
              gqa_g_q, gqa_g_k, g_out, w_out, w_router, b_router, w_moe_in, b_moe_in, w_moe_out, b_moe_out):
    W = dict(w_mod=w_mod, b_mod=b_mod, w_in=w_in, mla_g_qa=mla_g_qa, mla_w_uq=mla_w_uq,
             mla_g_kva=mla_g_kva, mla_w_ukv=mla_w_ukv, mla_g_q=mla_g_q, mla_g_k=mla_g_k,
             hy_conv_w=hy_conv_w, hy_conv_b=hy_conv_b, hy_w_in=hy_w_in, hy_b_in=hy_b_in,
             hy_w_mid=hy_w_mid, hy_b_mid=hy_b_mid, hy_w_out=hy_w_out, hy_b_out=hy_b_out,
             hy_freq=hy_freq, hy_skip=hy_skip, gqa_g_q=gqa_g_q, gqa_g_k=gqa_g_k, g_out=g_out,
             w_out=w_out, w_router=w_router, b_router=b_router, w_moe_in=w_moe_in,
             b_moe_in=b_moe_in, w_moe_out=w_moe_out, b_moe_out=b_moe_out)

    y_prompt = x_prompt
    ctx_states = []
    for l in range(DEPTH):
        pl = {name: arr[l] for name, arr in W.items()}
        y_prompt, ctx_t = trunk_layer(y_prompt, c_ctx[None, :], pl)
        ctx_states.append(ctx_t)
    new_mla_ckv = jnp.stack([s[0] for s in ctx_states], axis=1)
    new_mla_kpe = jnp.stack([s[1] for s in ctx_states], axis=1)
    new_gqa_k = jnp.stack([s[2] for s in ctx_states], axis=1)
    new_gqa_v = jnp.stack([s[3] for s in ctx_states], axis=1)

    n_lat = x_sample.shape[1]
    cos_a, sin_a = axial_rope(n_lat, ROPE_A)
    cos_c, sin_c = axial_rope(n_lat, HD_C)
    rope = (cos_a, sin_a, cos_c, sin_c)
    y_sample = x_sample
    for l in range(DEPTH):
        pl = {name: arr[l] for name, arr in W.items()}
        ctx = (cache_mla_ckv[:, l], cache_mla_kpe[:, l], cache_gqa_k[:, l], cache_gqa_v[:, l])
        y_sample, _ = trunk_layer(y_sample, c, pl, ctx, rope)

    return (y_prompt, y_sample, new_mla_ckv, new_mla_kpe, new_gqa_k, new_gqa_v)
```

```python
import functools
import math

import numpy as np
import jax
import jax.numpy as jnp
from jax import lax
from jax.experimental import pallas as pl
from jax.experimental.pallas import tpu as pltpu

F32 = jnp.float32
BF16 = jnp.bfloat16

D_MODEL = 1024
GRID_W = 64
EPS = 1e-6
ROPE_THETA = 10000.0
H_A = 6
Q_RANK = 256
KV_RANK = 128
NOPE_A = 64
ROPE_A = 32
V_A = 64
QK_A = NOPE_A + ROPE_A
HY_CH = 256
HY_ORDER = 2
HY_EMB = 33
HY_FO = 64
HY_INNER = 2
HY_MIN_DECAY = math.log(1e-2) / 1.5
HY_MAX_DECAY = math.log(1e-2) / 0.3
H_C = 6
KV_C = 2
HD_C = 64
N_EXPERTS = 32
TOP_K = 4
D_FF = 1024
SWIGLU_LIMIT = 7.0
SWIGLU_ALPHA = 1.702

LANE = 128
HQ = H_A * LANE
W_A = H_A * V_A
W_B = HY_CH
W_C = H_C * HD_C
IN_P = Q_RANK + KV_RANK + LANE + 3 * HY_CH + HQ + 2 * LANE
ROW_BLOCK = 128
VMEM_LIMIT = 56 * 1024 * 1024
FFT_N1 = 128
FFT_N2 = 64


def _cp(sem, vmem=None):
    return pltpu.CompilerParams(dimension_semantics=sem, vmem_limit_bytes=vmem)


def _dot(a, b):
    return jnp.dot(a, b, preferred_element_type=F32)


def _split(a):
    hi = a.astype(BF16)
    return hi, (a - hi.astype(F32)).astype(BF16)


def _dot3(a, b):
    ah, al = _split(a)
    bh, bl = _split(b)
    return _dot(ah, bh) + _dot(ah, bl) + _dot(al, bh)


def _bdot(a, b):
    return lax.dot_general(a, b, (((2,), (1,)), ((0,), (0,))), preferred_element_type=F32)


def _rms(x, n=None):
    ss = jnp.sum(x * x, axis=-1, keepdims=True) * (1.0 / (n or x.shape[-1]))
    return x * lax.rsqrt(ss + EPS)


def _head_norm(x, nvalid):
    outs = []
    for h in range(x.shape[1] // LANE):
        blk = x[:, h * LANE:(h + 1) * LANE]
        ss = jnp.sum(blk * blk, axis=-1, keepdims=True) * (1.0 / nvalid)
        outs.append(blk * lax.rsqrt(ss + EPS))
    return outs[0] if len(outs) == 1 else jnp.concatenate(outs, axis=1)


def _rope(x, tab_ref, shift):
    c, sm, sp = tab_ref[0], tab_ref[1], tab_ref[2]
    outs = []
    for h in range(x.shape[1] // LANE):
        blk = x[:, h * LANE:(h + 1) * LANE]
        outs.append(blk * c + pltpu.roll(blk, LANE - shift, 1) * sm + pltpu.roll(blk, shift, 1) * sp)
    return outs[0] if len(outs) == 1 else jnp.concatenate(outs, axis=1)


def _mod_kernel(c_ref, w_ref, b_ref, o_ref):
    c = c_ref[...]
    s = c * (1.0 / (1.0 + jnp.exp(-c)))
    o_ref[...] = _dot3(s, w_ref[...]) + b_ref[...]


def _mod_call(conds, w_mod, b_mod):
    depth, d, n6 = w_mod.shape
    tn = 1536
    return pl.pallas_call(
        _mod_kernel,
        out_shape=jax.ShapeDtypeStruct((depth, 8, n6), F32),
        grid=(depth, n6 // tn),
        in_specs=[pl.BlockSpec((8, d), lambda l, j: (0, 0)),
                  pl.BlockSpec((None, d, tn), lambda l, j: (l, 0, j)),
                  pl.BlockSpec((None, 1, tn), lambda l, j: (l, 0, j))],
        out_specs=pl.BlockSpec((None, 8, tn), lambda l, j: (l, 0, j)),
        compiler_params=_cp(("parallel", "parallel"), VMEM_LIMIT),
        name="mod",
    )(conds, w_mod, b_mod.reshape(depth, 1, n6))


def _cast_kernel(x_ref, o_ref):
    o_ref[...] = x_ref[...].astype(BF16)


def _cast_call(w, tr):
    n, r, c = w.shape
    return pl.pallas_call(
        _cast_kernel,
        out_shape=jax.ShapeDtypeStruct(w.shape, BF16),
        grid=(n, r // tr),
        in_specs=[pl.BlockSpec((None, tr, c), lambda i, j: (i, j, 0))],
        out_specs=pl.BlockSpec((None, tr, c), lambda i, j: (i, j, 0)),
        compiler_params=_cp(("parallel", "parallel"), VMEM_LIMIT),
        name="cast_bf16",
    )(w)


def _kv_heads(ckv_bf, kpe, w_kv_ref, gk, rope_ref, use_rope):
    kvp = _dot(ckv_bf, w_kv_ref[...])
    kn = kvp[:, :HQ]
    ka = jnp.concatenate([kn[:, h * LANE:(h + 1) * LANE] + kpe for h in range(H_A)], axis=1)
    ka = _head_norm(ka, QK_A) * gk
    if use_rope:
        ka = _rope(ka, rope_ref, ROPE_A // 2)
    return ka, kvp[:, HQ:]


def _premix_kernel(has_prev, use_rope, *refs):
    it = iter(refs)
    x_ref = next(it)
    if has_prev:
        y_ref, modp_ref = next(it), next(it)
    mod_ref, w_in_ref, w_uq_ref, w_kv_ref, g_ref = next(it), next(it), next(it), next(it), next(it)
    if use_rope:
        ra_ref, rc_ref = next(it), next(it)
    else:
        ra_ref = rc_ref = None
    (xo_ref, qa_ref, ka_ref, va_ref, ckv_ref, kpe_ref, qc_ref, kc_ref, vc_ref, kcf_ref, vcf_ref, u_ref) = it

    x = x_ref[...]
    if has_prev:
        x = x + modp_ref[5:6, :] * y_ref[...]
    xo_ref[...] = x
    h = _rms(x) * (1.0 + mod_ref[1:2, :]) + mod_ref[0:1, :]
    proj = _dot(h.astype(BF16), w_in_ref[...])
    o = 0
    c_q = proj[:, o:o + Q_RANK]; o += Q_RANK
    c_kv = proj[:, o:o + KV_RANK]; o += KV_RANK
    kpe = proj[:, o:o + LANE]; o += LANE
    u_ref[...] = proj[:, o:o + 3 * HY_CH]; o += 3 * HY_CH
    q_c = proj[:, o:o + HQ]; o += HQ
    k_c = proj[:, o:o + LANE]; o += LANE
    v_c = proj[:, o:o + LANE]

    cqn = _rms(c_q) * g_ref[0:1, :Q_RANK]
    qa = _head_norm(_dot(cqn.astype(BF16), w_uq_ref[...]), QK_A) * g_ref[2:3, :]
    if use_rope:
        qa = _rope(qa, ra_ref, ROPE_A // 2)
    qa_ref[...] = qa.astype(BF16)
    ckv = _rms(c_kv) * g_ref[1:2, :KV_RANK]
    ckv_ref[...] = ckv
    kpe_ref[...] = kpe
    ka, va = _kv_heads(ckv.astype(BF16), kpe, w_kv_ref, g_ref[3:4, :], ra_ref, use_rope)
    ka_ref[...] = ka.astype(BF16)
    va_ref[...] = va.astype(BF16)

    qc = _head_norm(q_c, HD_C) * g_ref[4:5, :]
    if use_rope:
        qc = _rope(qc, rc_ref, HD_C // 2)
    qc_ref[...] = qc.astype(BF16)
    lane = lax.broadcasted_iota(jnp.int32, k_c.shape, 1)
    k2 = k_c * k_c
    s0 = jnp.sum(jnp.where(lane < HD_C, k2, 0.0), axis=-1, keepdims=True) * (1.0 / HD_C)
    s1 = jnp.sum(jnp.where(lane >= HD_C, k2, 0.0), axis=-1, keepdims=True) * (1.0 / HD_C)
    kcn = k_c * jnp.where(lane < HD_C, lax.rsqrt(s0 + EPS), lax.rsqrt(s1 + EPS)) * g_ref[5:6, :LANE]
    kcf_ref[...] = kcn
    vcf_ref[...] = v_c
    kc_ref[...] = (_rope(kcn, rc_ref, HD_C // 2) if use_rope else kcn).astype(BF16)
    vc_ref[...] = v_c.astype(BF16)


def _premix_call(l, x, yprev, mod, seq, use_rope, wts, rope_tabs):
    t, d = x.shape
    tm = 256
    ncond = mod.shape[1]
    has_prev = yprev is not None

    def cond(i):
        return (i * tm) // seq if ncond > 1 else 0

    row = lambda w: pl.BlockSpec((tm, w), lambda i: (i, 0))
    ins, specs = [x], [row(d)]
    if has_prev:
        ins += [yprev, mod]
        specs += [row(d), pl.BlockSpec((None, None, 6, d), lambda i: (l - 1, cond(i), 0, 0))]
    ins += [mod, wts["w_in"], wts["w_uq"], wts["w_kv"], wts["gains"]]
    specs += [pl.BlockSpec((None, None, 6, d), lambda i: (l, cond(i), 0, 0)),
              pl.BlockSpec((None, d, IN_P), lambda i: (l, 0, 0)),
              pl.BlockSpec((None, Q_RANK, HQ), lambda i: (l, 0, 0)),
              pl.BlockSpec((None, KV_RANK, HQ + W_A), lambda i: (l, 0, 0)),
              pl.BlockSpec((None, 8, HQ), lambda i: (l, 0, 0))]
    if use_rope:
        nt = seq // tm
        ins += [rope_tabs[0], rope_tabs[1]]
        specs += [pl.BlockSpec((3, tm, LANE), lambda i: (0, i % nt, 0))] * 2
    outs = [(d, F32), (HQ, BF16), (HQ, BF16), (W_A, BF16), (KV_RANK, F32), (LANE, F32), (HQ, BF16),
            (LANE, BF16), (LANE, BF16), (LANE, F32), (LANE, F32), (3 * HY_CH, F32)]
    return pl.pallas_call(
        functools.partial(_premix_kernel, has_prev, use_rope),
        out_shape=[jax.ShapeDtypeStruct((t, w), dt) for w, dt in outs],
        grid=(t // tm,),
        in_specs=specs,
        out_specs=[row(w) for w, _ in outs],
        compiler_params=_cp(("parallel",), VMEM_LIMIT),
        name="premix",
    )(*ins)


def _cachekv_kernel(ckv_ref, kpe_ref, w_kv_ref, g_ref, ka_ref, va_ref):
    ka, va = _kv_heads(ckv_ref[...].astype(BF16), kpe_ref[...], w_kv_ref, g_ref[3:4, :], None, False)
    ka_ref[...] = ka.astype(BF16)
    va_ref[...] = va.astype(BF16)


def _cachekv_call(cache_ckv, cache_kpe_p, wts):
    nb, depth, past, _ = cache_ckv.shape
    return pl.pallas_call(
        _cachekv_kernel,
        out_shape=[jax.ShapeDtypeStruct((depth, nb, past, HQ), BF16),
                   jax.ShapeDtypeStruct((depth, nb, past, W_A), BF16)],
        grid=(depth, nb),
        in_specs=[pl.BlockSpec((None, None, past, KV_RANK), lambda l, b: (b, l, 0, 0)),
                  pl.BlockSpec((None, None, past, LANE), lambda l, b: (b, l, 0, 0)),
                  pl.BlockSpec((None, KV_RANK, HQ + W_A), lambda l, b: (l, 0, 0)),
                  pl.BlockSpec((None, 8, HQ), lambda l, b: (l, 0, 0))],
        out_specs=[pl.BlockSpec((None, None, past, HQ), lambda l, b: (l, b, 0, 0)),
                   pl.BlockSpec((None, None, past, W_A), lambda l, b: (l, b, 0, 0))],
        compiler_params=_cp(("parallel", "parallel"), VMEM_LIMIT),
        name="cache_kv",
    )(cache_ckv, cache_kpe_p, wts["w_kv"], wts["gains"])


def _nt(q, k):
    return lax.dot_general(q, k, (((1,), (1,)), ((), ())), preferred_element_type=F32)


def _attend(q, ks, vs):
    ss = [_nt(q, k) for k in ks]
    m = ss[0].max(axis=-1, keepdims=True)
    for s in ss[1:]:
        m = jnp.maximum(m, s.max(axis=-1, keepdims=True))
    ps = [jnp.exp(s - m) for s in ss]
    den = ps[0].sum(axis=-1, keepdims=True)
    for p in ps[1:]:
        den = den + p.sum(axis=-1, keepdims=True)
    acc = _dot(ps[0].astype(BF16), vs[0])
    for p, v in zip(ps[1:], vs[1:]):
        acc = acc + _dot(p.astype(BF16), v)
    return acc / den


def _attn_kernel(nseg, *refs):
    qa_ref, qc_ref = refs[0], refs[1]
    segs = refs[2:2 + 4 * nseg]
    oa_ref, oc_ref = refs[2 + 4 * nseg:]
    ka_refs, va_refs, kc_refs, vc_refs = (segs[i::4] for i in range(4))
    lane = lax.broadcasted_iota(jnp.int32, (qa_ref.shape[0], LANE), 1)
    low = lane < V_A
    for j in range(H_A // 2):
        vs = [r[:, j * LANE:(j + 1) * LANE] for r in va_refs]
        pv = []
        for h in (2 * j, 2 * j + 1):
            hs = slice(h * LANE, (h + 1) * LANE)
            pv.append(_attend(qa_ref[:, hs], [r[:, hs] for r in ka_refs], vs))
        oa_ref[:, j * LANE:(j + 1) * LANE] = jnp.where(low, pv[0], pv[1])
    kcs = [r[...] for r in kc_refs]
    vcs = [r[...] for r in vc_refs]
    g_per = H_C // KV_C
    for g in range(g_per):
        pv = [_attend(qc_ref[:, h * LANE:(h + 1) * LANE], kcs, vcs) for h in (g, g + g_per)]
        oc_ref[:, g * LANE:(g + 1) * LANE] = jnp.where(low, pv[0], pv[1])


def _attn_ctx_call(qa, ka, va, qc, kc, vc, seq):
    t = qa.shape[0]
    blk = lambda w: pl.BlockSpec((seq, w), lambda b: (b, 0))
    return pl.pallas_call(
        functools.partial(_attn_kernel, 1),
        out_shape=[jax.ShapeDtypeStruct((t, W_A), F32), jax.ShapeDtypeStruct((t, W_C), F32)],
        grid=(t // seq,),
        in_specs=[blk(HQ), blk(HQ), blk(HQ), blk(W_A), blk(LANE), blk(LANE)],
        out_specs=[blk(W_A), blk(W_C)],
        compiler_params=_cp(("parallel",), VMEM_LIMIT),
        name="attn_ctx",
    )(qa, qc, ka, va, kc, vc)


def _attn_lat_call(l, qa, ka, va, qc, kc, vc, kax, vax, kcx, vcx, seq):
    t = qa.shape[0]
    nb = t // seq
    tq = min(256, seq)
    nq = seq // tq
    past = kax.shape[2]
    qblk = lambda w: pl.BlockSpec((tq, w), lambda b, i: (b * nq + i, 0))
    sblk = lambda w: pl.BlockSpec((seq, w), lambda b, i: (b, 0))
    xblk = lambda w: pl.BlockSpec((None, None, past, w), lambda b, i: (l, b, 0, 0))
    cblk = lambda w: pl.BlockSpec((None, None, past, w), lambda b, i: (b, l, 0, 0))
    return pl.pallas_call(
        functools.partial(_attn_kernel, 2),
        out_shape=[jax.ShapeDtypeStruct((t, W_A), F32), jax.ShapeDtypeStruct((t, W_C), F32)],
        grid=(nb, nq),
        in_specs=[qblk(HQ), qblk(HQ),
                  xblk(HQ), xblk(W_A), cblk(LANE), cblk(LANE),
                  sblk(HQ), sblk(W_A), sblk(LANE), sblk(LANE)],
        out_specs=[qblk(W_A), qblk(W_C)],
        compiler_params=_cp(("parallel", "parallel"), VMEM_LIMIT),
        name="attn_lat",
    )(qa, qc, kax, vax, kcx, vcx, ka, va, kc, vc)


def _filter_mlp(z, w_in_ref, b_in_ref, w_mid_ref, b_mid_ref, w_out_ref, b_out_ref, freq_ref):
    freq = freq_ref[...]
    a = jnp.sin(freq * (_dot3(z, w_in_ref[...]) + b_in_ref[...]))
    for i in range(HY_INNER):
        a = jnp.sin(freq * (_dot3(a, w_mid_ref[i]) + b_mid_ref[i]))
    return _dot3(a, w_out_ref[...]) + b_out_ref[...]


def _conv3(u, up, dn, cw_ref, cb_ref):
    return up * cw_ref[0:1, :] + u * cw_ref[1:2, :] + dn * cw_ref[2:3, :] + cb_ref[...]


def _filt_ctx_kernel(z_ref, dec_ref, w_in_ref, b_in_ref, w_mid_ref, b_mid_ref, w_out_ref, b_out_ref, freq_ref,
                     c_ref, s_ref, o_ref):
    n = z_ref.shape[0]
    h = _filter_mlp(z_ref[...], w_in_ref, b_in_ref, w_mid_ref, b_mid_ref, w_out_ref, b_out_ref, freq_ref)
    dec = dec_ref[...]
    row = lax.broadcasted_iota(jnp.int32, dec.shape, 0)
    half = HY_ORDER * HY_CH
    for o in range(HY_ORDER):
        hf = h[:, o * HY_CH:(o + 1) * HY_CH] * dec
        hb = jnp.where(row > 0, h[:, half + o * HY_CH:half + (o + 1) * HY_CH] * dec, 0.0)
        nrm = jnp.sum(jnp.abs(hf) + jnp.abs(hb), axis=0, keepdims=True) + EPS
        scale = (1.0 / n) / nrm
        o_ref[o, 0] = _dot3(c_ref[...], hf + hb) * scale
        o_ref[o, 1] = -_dot3(s_ref[...], hf - hb) * scale


def _filt_ctx_call(l, z, dec, hw, cmat, smat):
    n = z.shape[0]
    full = lambda a: pl.BlockSpec((None,) + a.shape[1:], lambda i: (l,) + (0,) * (a.ndim - 1))
    const = lambda a: pl.BlockSpec(a.shape, lambda i: (0,) * a.ndim)
    names = ["hy_w_in", "hy_b_in", "hy_w_mid", "hy_b_mid", "hy_w_out", "hy_b_out", "hy_freq"]
    return pl.pallas_call(
        _filt_ctx_kernel,
        out_shape=jax.ShapeDtypeStruct((HY_ORDER, 2, n, HY_CH), F32),
        grid=(1,),
        in_specs=[const(z), const(dec)] + [full(hw[k]) for k in names] + [const(cmat), const(smat)],
        out_specs=pl.BlockSpec((HY_ORDER, 2, n, HY_CH), lambda i: (0, 0, 0, 0)),
        compiler_params=_cp(("arbitrary",), VMEM_LIMIT),
        name="hy_filter_ctx",
    )(z, dec, *[hw[k] for k in names], cmat, smat)


def _hy_ctx_kernel(u_ref, cw_ref, cb_ref, skip_ref, kf_ref, fwd_ref, inv_ref, o_ref):
    u = u_ref[...]
    n = u.shape[0]
    row = lax.broadcasted_iota(jnp.int32, u.shape, 0)
    up = jnp.where(row > 0, pltpu.roll(u, 1, 0), 0.0)
    dn = jnp.where(row < n - 1, pltpu.roll(u, n - 1, 0), 0.0)
    z = _conv3(u, up, dn, cw_ref, cb_ref)
    s = z[:, :HY_CH]
    gates = (z[:, HY_CH:2 * HY_CH], z[:, 2 * HY_CH:])
    for o in range(HY_ORDER):
        xs = _dot(fwd_ref[...], s.astype(BF16))
        xr, xi = xs[:n], xs[n:]
        kr, ki = kf_ref[o, 0], kf_ref[o, 1]
        ycat = jnp.concatenate([xr * kr - xi * ki, xr * ki + xi * kr], axis=0)
        y = _dot(inv_ref[...], ycat.astype(BF16))
        s = gates[o] * (y + s * skip_ref[o:o + 1, :])
    o_ref[...] = s


def _hy_ctx_call(l, u, hw, kf, fwd, inv, seq):
    t = u.shape[0]
    full = lambda a: pl.BlockSpec((None,) + a.shape[1:], lambda b: (l,) + (0,) * (a.ndim - 1))
    const = lambda a: pl.BlockSpec(a.shape, lambda b: (0,) * a.ndim)
    return pl.pallas_call(
        _hy_ctx_kernel,
        out_shape=jax.ShapeDtypeStruct((t, HY_CH), F32),
        grid=(t // seq,),
        in_specs=[pl.BlockSpec((seq, 3 * HY_CH), lambda b: (b, 0)),
                  full(hw["hy_conv_w"]), full(hw["hy_conv_b"]), full(hw["hy_skip"]),
                  const(kf), const(fwd), const(inv)],
        out_specs=pl.BlockSpec((seq, HY_CH), lambda b: (b, 0)),
        compiler_params=_cp(("parallel",), VMEM_LIMIT),
        name="hyena_ctx",
    )(u, hw["hy_conv_w"], hw["hy_conv_b"], hw["hy_skip"], kf, fwd, inv)


def _filt_lat_kernel(seq, z_ref, dec_ref, w_in_ref, b_in_ref, w_mid_ref, b_mid_ref, w_out_ref, b_out_ref, freq_ref,
                     k_ref, n_ref):
    i = pl.program_id(0)
    tr = z_ref.shape[0]
    h = _filter_mlp(z_ref[...], w_in_ref, b_in_ref, w_mid_ref, b_mid_ref, w_out_ref, b_out_ref, freq_ref)
    half = HY_ORDER * HY_CH
    row = i * tr + lax.broadcasted_iota(jnp.int32, (tr, half), 0)
    dec = dec_ref[...]
    kern = jnp.where(row < seq, h[:, :half], h[:, half:]) * jnp.concatenate([dec] * HY_ORDER, axis=1)
    k_ref[...] = kern

    @pl.when(i == 0)
    def _():
        n_ref[...] = jnp.zeros_like(n_ref)

    n_ref[...] += jnp.sum(jnp.abs(kern), axis=0, keepdims=True)


def _filt_lat_call(l, z, dec, hw, seq):
    n = z.shape[0]
    tr = min(512, n)
    half = HY_ORDER * HY_CH
    full = lambda a: pl.BlockSpec((None,) + a.shape[1:], lambda i: (l,) + (0,) * (a.ndim - 1))
    names = ["hy_w_in", "hy_b_in", "hy_w_mid", "hy_b_mid", "hy_w_out", "hy_b_out", "hy_freq"]
    return pl.pallas_call(
        functools.partial(_filt_lat_kernel, seq),
        out_shape=[jax.ShapeDtypeStruct((n, half), F32), jax.ShapeDtypeStruct((1, half), F32)],
        grid=(n // tr,),
        in_specs=[pl.BlockSpec((tr, z.shape[1]), lambda i: (i, 0)), pl.BlockSpec((tr, HY_CH), lambda i: (i, 0))]
        + [full(hw[k]) for k in names],
        out_specs=[pl.BlockSpec((tr, half), lambda i: (i, 0)), pl.BlockSpec((1, half), lambda i: (0, 0))],
        compiler_params=_cp(("arbitrary",), VMEM_LIMIT),
        name="hy_filter_lat",
    )(z, dec, *[hw[k] for k in names])


def _fa_kernel(f1_ref, k_ref, o_ref):
    r = _dot3(f1_ref[...], k_ref[...])
    o_ref[0] = r[:FFT_N1]
    o_ref[1] = r[FFT_N1:]


def _fa_call(f1f, kern2d):
    n1, w = kern2d.shape
    tn = min(2048, w)
    return pl.pallas_call(
        _fa_kernel,
        out_shape=jax.ShapeDtypeStruct((2, FFT_N1, w), F32),
        grid=(w // tn,),
        in_specs=[pl.BlockSpec(f1f.shape, lambda j: (0, 0)), pl.BlockSpec((n1, tn), lambda j: (0, j))],
        out_specs=pl.BlockSpec((2, FFT_N1, tn), lambda j: (0, 0, j)),
        compiler_params=_cp(("parallel",), VMEM_LIMIT),
        name="hy_filter_dft1",
    )(f1f, kern2d)


def _fb_kernel(n_total, a_ref, fh_ref, fl_ref, n_ref, o_ref):
    a = jnp.concatenate([a_ref[0], a_ref[1]], axis=1)
    ah, al = _split(a)
    x = _bdot(fh_ref[...], ah) + _bdot(fh_ref[...], al) + _bdot(fl_ref[...], ah)
    scale = (1.0 / n_total) / (n_ref[...] + EPS)
    o_ref[...] = x * scale[None]


def _fb_call(af5, f2h, f2l, nrm, n_total):
    _, n1, n2, c = af5.shape[0], af5.shape[1], af5.shape[2], af5.shape[3]
    k1t = 8
    return pl.pallas_call(
        functools.partial(_fb_kernel, n_total),
        out_shape=jax.ShapeDtypeStruct((n1, 2 * n2, c), F32),
        grid=(n1 // k1t,),
        in_specs=[pl.BlockSpec((2, k1t, n2, c), lambda j: (0, j, 0, 0)),
                  pl.BlockSpec((k1t, 2 * n2, 2 * n2), lambda j: (j, 0, 0)),
                  pl.BlockSpec((k1t, 2 * n2, 2 * n2), lambda j: (j, 0, 0)),
                  pl.BlockSpec((1, c), lambda j: (0, 0))],
        out_specs=pl.BlockSpec((k1t, 2 * n2, c), lambda j: (j, 0, 0)),
        compiler_params=_cp(("parallel",), VMEM_LIMIT),
        name="hy_filter_dft2",
    )(af5, f2h, f2l, nrm)


def _hconv_kernel(seq, u_ref, p_ref, n_ref, cw_ref, cb_ref, v_ref, x1_ref, x2_ref):
    i = pl.program_id(0)
    u = u_ref[...]
    tt = u.shape[0]
    row = lax.broadcasted_iota(jnp.int32, u.shape, 0)
    pos = (i * tt) % seq
    prev = jnp.where(pos > 0, p_ref[7:8, :], 0.0)
    nxt = jnp.where(pos + tt < seq, n_ref[0:1, :], 0.0)
    up = jnp.where(row > 0, pltpu.roll(u, 1, 0), prev)
    dn = jnp.where(row < tt - 1, pltpu.roll(u, tt - 1, 0), nxt)
    z = _conv3(u, up, dn, cw_ref, cb_ref)
    v_ref[...] = z[:, :HY_CH]
    x1_ref[...] = z[:, HY_CH:2 * HY_CH]
    x2_ref[...] = z[:, 2 * HY_CH:]


def _hconv_call(l, u, hw, seq):
    t, w = u.shape
    tt = min(512, seq)
    nblk8 = t // 8
    full = lambda a: pl.BlockSpec((None,) + a.shape[1:], lambda i: (l,) + (0,) * (a.ndim - 1))
    ob = pl.BlockSpec((tt, HY_CH), lambda i: (i, 0))
    return pl.pallas_call(
        functools.partial(_hconv_kernel, seq),
        out_shape=[jax.ShapeDtypeStruct((t, HY_CH), F32)] * 3,
        grid=(t // tt,),
        in_specs=[pl.BlockSpec((tt, w), lambda i: (i, 0)),
                  pl.BlockSpec((8, w), lambda i: (jnp.maximum(i * (tt // 8) - 1, 0), 0)),
                  pl.BlockSpec((8, w), lambda i: (jnp.minimum((i + 1) * (tt // 8), nblk8 - 1), 0)),
                  full(hw["hy_conv_w"]), full(hw["hy_conv_b"])],
        out_specs=[ob, ob, ob],
        compiler_params=_cp(("parallel",), VMEM_LIMIT),
        name="hyena_conv3",
    )(u, u, u, hw["hy_conv_w"], hw["hy_conv_b"])


def _ha_kernel(f1_ref, x_ref, o_ref):
    r = _dot(f1_ref[...], x_ref[...].astype(BF16))
    o_ref[0] = r[:FFT_N1].astype(BF16)
    o_ref[1] = r[FFT_N1:].astype(BF16)


def _ha_call(f1d, x2d, nb):
    rows, w = x2d.shape
    n1h = rows // nb
    tn = min(2048, w)
    return pl.pallas_call(
        _ha_kernel,
        out_shape=jax.ShapeDtypeStruct((nb, 2, FFT_N1, w), BF16),
        grid=(nb, w // tn),
        in_specs=[pl.BlockSpec(f1d.shape, lambda b, j: (0, 0)), pl.BlockSpec((n1h, tn), lambda b, j: (b, j))],
        out_specs=pl.BlockSpec((None, 2, FFT_N1, tn), lambda b, j: (b, 0, 0, j)),
        compiler_params=_cp(("parallel", "parallel"), VMEM_LIMIT),
        name="hyena_dft1",
    )(f1d, x2d)


def _hb_kernel(a_ref, f_ref, g_ref, kf_ref, o_ref):
    n2 = a_ref.shape[2]
    a = jnp.concatenate([a_ref[0], a_ref[1]], axis=1)
    x = _bdot(f_ref[...], a)
    xr, xi = x[:, :n2], x[:, n2:]
    kr, ki = kf_ref[:, :n2], kf_ref[:, n2:]
    y = jnp.concatenate([xr * kr - xi * ki, xr * ki + xi * kr], axis=1).astype(BF16)
    b = _bdot(g_ref[...], y)
    o_ref[0] = b[:, :n2].astype(BF16)
    o_ref[1] = b[:, n2:].astype(BF16)


def _hb_call(o, a5, f2, g2, kf):
    nb, _, n1, n2, c = a5.shape
    k1t = 16
    blk = pl.BlockSpec((None, 2, k1t, n2, c), lambda b, j: (b, 0, j, 0, 0))
    mat = pl.BlockSpec((k1t, 2 * n2, 2 * n2), lambda b, j: (j, 0, 0))
    return pl.pallas_call(
        _hb_kernel,
        out_shape=jax.ShapeDtypeStruct(a5.shape, BF16),
        grid=(nb, n1 // k1t),
        in_specs=[blk, mat, mat, pl.BlockSpec((k1t, 2 * n2, c), lambda b, j: (j, 0, o))],
        out_specs=blk,
        compiler_params=_cp(("parallel", "parallel"), VMEM_LIMIT),
        name="hyena_dft2",
    )(a5, f2, g2, kf)


def _hc_kernel(o, fc_ref, b_ref, s_ref, g_ref, skip_ref, o_ref):
    bcat = jnp.concatenate([b_ref[0], b_ref[1]], axis=0)
    y = _dot(fc_ref[...], bcat)
    o_ref[...] = g_ref[...] * (y + s_ref[...] * skip_ref[o:o + 1, :])


def _hc_call(l, o, fc, b4, s2d, g2d, skip_t):
    nb, _, n1, w = b4.shape
    rows = s2d.shape[0] // nb
    tn = min(2048, w)
    blk = pl.BlockSpec((rows, tn), lambda b, j: (b, j))
    return pl.pallas_call(
        functools.partial(_hc_kernel, o),
        out_shape=jax.ShapeDtypeStruct(s2d.shape, F32),
        grid=(nb, w // tn),
        in_specs=[pl.BlockSpec(fc.shape, lambda b, j: (0, 0)),
                  pl.BlockSpec((None, 2, n1, tn), lambda b, j: (b, 0, 0, j)),
                  blk, blk,
                  pl.BlockSpec((None, HY_ORDER, tn), lambda b, j: (l, 0, j))],
        out_specs=blk,
        compiler_params=_cp(("parallel", "parallel"), VMEM_LIMIT),
        name="hyena_dft3",
    )(fc, b4, s2d, g2d, skip_t)


def _postmix_kernel(x_ref, oa_ref, ob_ref, oc_ref, mod_ref, g_ref, wa_ref, wb_ref, wc_ref, wrh_ref, wrl_ref, br_ref,
                    x1_ref, h2_ref, idx_ref, gate_ref):
    na = _rms(oa_ref[...]) * g_ref[:, :W_A]
    nb = _rms(ob_ref[...]) * g_ref[:, W_A:W_A + W_B]
    nc = _rms(oc_ref[...]) * g_ref[:, W_A + W_B:]
    mix = (_dot(na.astype(BF16), wa_ref[...]) + _dot(nb.astype(BF16), wb_ref[...])
           + _dot(nc.astype(BF16), wc_ref[...]))
    x1 = x_ref[...] + mod_ref[2:3, :] * mix
    x1_ref[...] = x1
    h2 = _rms(x1) * (1.0 + mod_ref[4:5, :]) + mod_ref[3:4, :]
    h2_ref[...] = h2
    hh, hl = _split(h2)
    vals = _dot(hh, wrh_ref[...]) + _dot(hh, wrl_ref[...]) + _dot(hl, wrh_ref[...]) + br_ref[...]
    lane = lax.broadcasted_iota(jnp.int32, vals.shape, 1).astype(F32)
    idx_out = jnp.zeros(vals.shape, F32)
    top = jnp.zeros(vals.shape, F32)
    m0 = None
    for k in range(TOP_K):
        m = vals.max(axis=-1, keepdims=True)
        sel = jnp.min(jnp.where(vals == m, lane, float(LANE)), axis=-1, keepdims=True)
        if m0 is None:
            m0 = m
        idx_out = jnp.where(lane == k, sel, idx_out)
        top = jnp.where(lane == k, jnp.exp(m - m0), top)
        vals = jnp.where(lane == sel, -jnp.inf, vals)
    idx_ref[...] = idx_out.astype(jnp.int32)
    gate_ref[...] = top / jnp.sum(top, axis=-1, keepdims=True)


def _postmix_call(l, x, oa, ob, oc, mod, seq, wts):
    t, d = x.shape
    tm = 256
    ncond = mod.shape[1]

    def cond(i):
        return (i * tm) // seq if ncond > 1 else 0

    row = lambda w: pl.BlockSpec((tm, w), lambda i: (i, 0))
    lay = lambda a: pl.BlockSpec((None,) + a.shape[1:], lambda i: (l,) + (0,) * (a.ndim - 1))
    names = ["g_out", "wo_a", "wo_b", "wo_c", "wr_hi", "wr_lo", "b_router"]
    return pl.pallas_call(
        _postmix_kernel,
        out_shape=[jax.ShapeDtypeStruct((t, d), F32), jax.ShapeDtypeStruct((t, d), F32),
                   jax.ShapeDtypeStruct((t, LANE), jnp.int32), jax.ShapeDtypeStruct((t, LANE), F32)],
        grid=(t // tm,),
        in_specs=[row(d), row(W_A), row(W_B), row(W_C),
                  pl.BlockSpec((None, None, 6, d), lambda i: (l, cond(i), 0, 0))] + [lay(wts[k]) for k in names],
        out_specs=[row(d), row(d), row(LANE), row(LANE)],
        compiler_params=_cp(("parallel",), VMEM_LIMIT),
        name="postmix",
    )(x, oa, ob, oc, mod, *[wts[k] for k in names])


def _moe_kernel(be_ref, nu_ref, tok_ref, gate_ref, h_ref, wi_ref, bi_ref, wo_ref, bo_ref, y_ref, buf, obuf):
    i = pl.program_id(0)

    @pl.when(i == 0)
    def _():
        y_ref[...] = jnp.zeros_like(y_ref)

    @pl.when(i < nu_ref[0])
    def _():
        base = i * ROW_BLOCK

        def gather(r, c):
            t = tok_ref[base + r]
            buf[pl.ds(r, 1), :] = h_ref[pl.ds(t, 1), :]
            return c

        lax.fori_loop(0, ROW_BLOCK, gather, 0, unroll=8)
        gu = _dot(buf[...].astype(BF16), wi_ref[...]) + bi_ref[...]
        gt = jnp.minimum(gu[:, :D_FF], SWIGLU_LIMIT)
        lin = jnp.clip(gu[:, D_FF:], -SWIGLU_LIMIT, SWIGLU_LIMIT)
        act = (lin + 1.0) * gt * (1.0 / (1.0 + jnp.exp(-SWIGLU_ALPHA * gt)))
        obuf[...] = _dot(act.astype(BF16), wo_ref[...]) + bo_ref[...]

        def scatter(r, c):
            t = tok_ref[base + r]
            g = gate_ref[base + r]
            y_ref[pl.ds(t, 1), :] = y_ref[pl.ds(t, 1), :] + g * obuf[pl.ds(r, 1), :]
            return c

        lax.fori_loop(0, ROW_BLOCK, scatter, 0, unroll=8)


def _route(idx, gates, n_blocks):
    m = idx.shape[0] * TOP_K
    e = idx.reshape(m)
    oh = (e[:, None] == jnp.arange(N_EXPERTS, dtype=jnp.int32)[None, :]).astype(jnp.int32)
    cum = jnp.cumsum(oh, axis=0)
    rank = jnp.sum((cum - 1) * oh, axis=1)
    cnt = cum[-1]
    padded = (cnt + ROW_BLOCK - 1) // ROW_BLOCK * ROW_BLOCK
    pend = jnp.cumsum(padded)
    dest = (pend - padded)[e] + rank
    n_slots = n_blocks * ROW_BLOCK
    tok = jnp.zeros((n_slots,), jnp.int32).at[dest].set(jnp.arange(m, dtype=jnp.int32) // TOP_K)
    gate = jnp.zeros((n_slots,), F32).at[dest].set(gates.reshape(m))
    n_used = pend[-1] // ROW_BLOCK
    blk = jnp.arange(n_blocks, dtype=jnp.int32)
    be = jnp.searchsorted(pend, jnp.minimum(blk, n_used - 1) * ROW_BLOCK, side="right").astype(jnp.int32)
    return jnp.minimum(be, N_EXPERTS - 1), n_used.reshape(1).astype(jnp.int32), tok, gate


def _moe_call(l, h2, idx, gates, wts):
    t, d = h2.shape
    tc = min(4096, t)
    n_blocks = tc * TOP_K // ROW_BLOCK + N_EXPERTS
    ys = []
    for c in range(t // tc):
        be, nu, tok, gate = _route(idx[c * tc:(c + 1) * tc], gates[c * tc:(c + 1) * tc], n_blocks)
        grid_spec = pltpu.PrefetchScalarGridSpec(
            num_scalar_prefetch=4,
            grid=(n_blocks,),
            in_specs=[pl.BlockSpec((tc, d), lambda i, *_: (c, 0), pipeline_mode=pl.Buffered(1)),
                      pl.BlockSpec((None, None, d, 2 * D_FF), lambda i, be, *_: (l, be[i], 0, 0)),
                      pl.BlockSpec((None, None, 1, 2 * D_FF), lambda i, be, *_: (l, be[i], 0, 0)),
                      pl.BlockSpec((None, None, D_FF, d), lambda i, be, *_: (l, be[i], 0, 0)),
                      pl.BlockSpec((None, None, 1, d), lambda i, be, *_: (l, be[i], 0, 0))],
            out_specs=pl.BlockSpec((tc, d), lambda i, *_: (0, 0), pipeline_mode=pl.Buffered(1)),
            scratch_shapes=[pltpu.VMEM((ROW_BLOCK, d), F32), pltpu.VMEM((ROW_BLOCK, d), F32)],
        )
        ys.append(pl.pallas_call(
            _moe_kernel,
            out_shape=jax.ShapeDtypeStruct((tc, d), F32),
            grid_spec=grid_spec,
            compiler_params=_cp(("arbitrary",), VMEM_LIMIT),
            name="moe_experts",
        )(be, nu, tok, gate, h2, wts["w_moe_in"], wts["b_moe_in"], wts["w_moe_out"], wts["b_moe_out"]))
    return ys[0] if len(ys) == 1 else jnp.concatenate(ys, axis=0)


def _final_kernel(x_ref, y_ref, mod_ref, o_ref):
    o_ref[...] = x_ref[...] + mod_ref[5:6, :] * y_ref[...]


def _final_call(l, x, y, mod, seq):
    t, d = x.shape
    tm = 256
    ncond = mod.shape[1]
    row = pl.BlockSpec((tm, d), lambda i: (i, 0))
    return pl.pallas_call(
        _final_kernel,
        out_shape=jax.ShapeDtypeStruct((t, d), F32),
        grid=(t // tm,),
        in_specs=[row, row, pl.BlockSpec((None, None, 6, d),
                                         lambda i: (l, (i * tm) // seq if ncond > 1 else 0, 0, 0))],
        out_specs=row,
        compiler_params=_cp(("parallel",), VMEM_LIMIT),
        name="final_residual",
    )(x, y, mod)


def _rope_tables(seq, rot_dim, lane_map):
    n_rows = seq // GRID_W
    rows = jnp.repeat(jnp.arange(n_rows, dtype=F32), GRID_W)
    cols = jnp.tile(jnp.arange(GRID_W, dtype=F32), n_rows)
    axis_dim = rot_dim // 2
    inv_freq = ROPE_THETA ** (-jnp.arange(0, axis_dim, 2, dtype=F32) / axis_dim)
    ang = jnp.concatenate([rows[:, None] * inv_freq, cols[:, None] * inv_freq], axis=-1)
    cos, sin = jnp.cos(ang), jnp.sin(ang)
    pair = np.zeros((LANE,), np.int32)
    in_rot = np.zeros((LANE,), np.float32)
    first = np.zeros((LANE,), np.float32)
    for ln in range(LANE):
        m = lane_map(ln)
        if m is not None:
            pair[ln], in_rot[ln], first[ln] = m[0], 1.0, 1.0 if m[1] == 0 else 0.0
    c = jnp.where(in_rot[None, :] > 0, cos[:, pair], 1.0)
    s = sin[:, pair] * in_rot[None, :]
    return jnp.stack([c, -s * first[None, :], s * (1.0 - first[None, :])]).astype(F32)


def _lane_map_a(ln):
    o = ln - NOPE_A
    if 0 <= o < ROPE_A:
        return (o % (ROPE_A // 2), o // (ROPE_A // 2))
    return None


def _lane_map_c(ln):
    o = ln % HD_C
    return (o % (HD_C // 2), o // (HD_C // 2))


def _phase(num, den):
    ang = (2.0 * math.pi / den) * (num % den).astype(F32)
    return jnp.cos(ang), jnp.sin(ang)


def _ctx_dft(n):
    f = jnp.arange(n, dtype=jnp.int32)[:, None]
    t = jnp.arange(n, dtype=jnp.int32)[None, :]
    c, s = _phase((2 * f + 1) * t, 4 * n)
    fwd = jnp.concatenate([c, -s], axis=0).astype(BF16)
    inv = jnp.concatenate([c.T, -s.T], axis=1).astype(BF16)
    return c, s, fwd, inv


def _lat_dft(seq):
    n = 2 * seq
    n1, n2 = FFT_N1, n // FFT_N1
    k1 = jnp.arange(n1, dtype=jnp.int32)
    c1, s1 = _phase(k1[:, None] * k1[None, :], n1)
    f1f = jnp.concatenate([c1, -s1], axis=0)
    f1d = f1f[:, :n1 // 2].astype(BF16)
    fc = jnp.concatenate([c1[:, :n1 // 2].T, -s1[:, :n1 // 2].T], axis=1).astype(BF16)
    k2 = jnp.arange(n2, dtype=jnp.int32)
    num = (k2[None, :, None] * k2[None, None, :]) * n1 + k2[None, None, :] * k1[:, None, None]
    cm, sm = _phase(num, n)
    mr, mi = cm, -sm
    f2 = jnp.concatenate([jnp.concatenate([mr, -mi], axis=2), jnp.concatenate([mi, mr], axis=2)], axis=1)
    mrt, mit = jnp.swapaxes(mr, 1, 2), jnp.swapaxes(mi, 1, 2)
    g2 = jnp.concatenate([jnp.concatenate([mrt, mit], axis=2), jnp.concatenate([-mit, mrt], axis=2)], axis=1)
    f2h, f2l = _split(f2)
    return dict(f1f=f1f, f1d=f1d, fc=fc, f2=f2h, f2l=f2l, g2=g2.astype(BF16), n1=n1, n2=n2, n=n)


def _hy_features(seq):
    t = jnp.linspace(0.0, 1.0, seq, dtype=F32)[:, None]
    bands = (HY_EMB - 1) // 2
    f = jnp.linspace(1e-4, bands - 1, bands, dtype=F32)[None, :]
    w = 2.0 * math.pi * jnp.arange(seq, dtype=F32)[:, None] / seq
    z = jnp.concatenate([t, jnp.cos(f * w), jnp.sin(f * w)], axis=-1)
    z = jnp.pad(z, ((0, 0), (0, HY_FO - HY_EMB)))
    deltas = jnp.abs(jnp.linspace(HY_MIN_DECAY, HY_MAX_DECAY, HY_CH, dtype=F32))
    return z, jnp.exp(-t * deltas)


def _prep_weights(w_in, mla_g_qa, mla_w_uq, mla_g_kva, mla_w_ukv, mla_g_q, mla_g_k, gqa_g_q, gqa_g_k,
                  g_out, w_out, w_router, b_router, hy_w_in):
    depth = w_in.shape[0]
    cuts = np.cumsum([0, Q_RANK, KV_RANK, ROPE_A, 3 * HY_CH, H_C * HD_C, KV_C * HD_C, KV_C * HD_C])
    src = np.zeros((IN_P,), np.int32)
    keep = np.zeros((IN_P,), np.float32)

    def put(dst, s0, width):
        src[dst:dst + width] = np.arange(s0, s0 + width)
        keep[dst:dst + width] = 1.0

    o = 0
    put(o, cuts[0], Q_RANK); o += Q_RANK
    put(o, cuts[1], KV_RANK); o += KV_RANK
    put(o + NOPE_A, cuts[2], ROPE_A); o += LANE
    put(o, cuts[3], 3 * HY_CH); o += 3 * HY_CH
    for h in range(H_C):
        put(o + h * LANE + (h // (H_C // KV_C)) * HD_C, cuts[4] + h * HD_C, HD_C)
    o += HQ
    put(o, cuts[5], LANE); o += LANE
    put(o, cuts[6], LANE)
    w_in_p = (w_in[:, :, src] * keep[None, None, :]).astype(BF16)

    w_uq = jnp.pad(mla_w_uq.reshape(depth, Q_RANK, H_A, QK_A), ((0, 0), (0, 0), (0, 0), (0, LANE - QK_A)))
    w_uq = w_uq.reshape(depth, Q_RANK, HQ).astype(BF16)
    ukv = mla_w_ukv.reshape(depth, KV_RANK, H_A, NOPE_A + V_A)
    wk = jnp.pad(ukv[..., :NOPE_A], ((0, 0), (0, 0), (0, 0), (0, LANE - NOPE_A))).reshape(depth, KV_RANK, HQ)
    wv = ukv[..., NOPE_A:].reshape(depth, KV_RANK, W_A)
    w_kv = jnp.concatenate([wk, wv], axis=-1).astype(BF16)

    def pad_row(v):
        return jnp.pad(v, ((0, 0), (0, HQ - v.shape[1])))

    head_a = lambda g: jnp.tile(jnp.pad(g, ((0, 0), (0, LANE - QK_A))), (1, H_A))
    head_c = lambda g, reps: jnp.tile(g, (1, reps))
    gains = jnp.stack([
        pad_row(mla_g_qa), pad_row(mla_g_kva),
        head_a(mla_g_q) * (QK_A ** -0.5), head_a(mla_g_k),
        head_c(gqa_g_q, 2 * H_C) * (HD_C ** -0.5), pad_row(head_c(gqa_g_k, 2)),
        jnp.zeros((depth, HQ), F32), jnp.zeros((depth, HQ), F32)], axis=1).astype(F32)

    g_per = H_C // KV_C
    perm_c = np.concatenate([np.arange(h * HD_C, (h + 1) * HD_C) for g in range(g_per) for h in (g, g + g_per)])
    rows_c = W_A + W_B + perm_c
    g_o = jnp.concatenate([g_out[:, :W_A + W_B], g_out[:, rows_c]], axis=1).reshape(depth, 1, -1)
    wr = jnp.pad(w_router, ((0, 0), (0, 0), (0, LANE - N_EXPERTS)))
    wr_hi, wr_lo = _split(wr)
    br = jnp.pad(b_router, ((0, 0), (0, LANE - N_EXPERTS)), constant_values=-1e30).reshape(depth, 1, LANE)
    return dict(w_in=w_in_p, w_uq=w_uq, w_kv=w_kv, gains=gains, g_out=g_o,
                wo_a=w_out[:, :W_A].astype(BF16), wo_b=w_out[:, W_A:W_A + W_B].astype(BF16),
                wo_c=w_out[:, rows_c].astype(BF16), wr_hi=wr_hi, wr_lo=wr_lo, b_router=br,
                hy_w_in=jnp.pad(hy_w_in, ((0, 0), (0, HY_FO - HY_EMB), (0, 0))))


def _hyena_lat(l, u, hw, kf, dft, seq):
    t = u.shape[0]
    nb = t // seq
    n1, n2 = dft["n1"], dft["n2"]
    w2 = n2 * HY_CH
    v, x1, x2 = _hconv_call(l, u, hw, seq)
    to2d = lambda a: a.reshape(t // n2, w2)
    s = to2d(v)
    for o, gate in enumerate((x1, x2)):
        a = _ha_call(dft["f1d"], s, nb)
        b = _hb_call(o, a.reshape(nb, 2, n1, n2, HY_CH), dft["f2"], dft["g2"], kf)
        s = _hc_call(l, o, dft["fc"], b.reshape(nb, 2, n1, w2), s, to2d(gate), hw["skip_t"])
    return s.reshape(t, HY_CH)


def _filter_lat(l, z, dec, hw, dft, seq):
    kern, nrm = _filt_lat_call(l, z, dec, hw, seq)
    n1, n2, n = dft["n1"], dft["n2"], dft["n"]
    c = kern.shape[1]
    af = _fa_call(dft["f1f"], kern.reshape(n1, n2 * c))
    return _fb_call(af.reshape(2, n1, n2, c), dft["f2"], dft["f2l"], nrm, n)


def kernel(x_prompt, x_sample, c, c_ctx, cache_mla_ckv, cache_mla_kpe, cache_gqa_k, cache_gqa_v, w_mod, b_mod, w_in, mla_g_qa, mla_w_uq, mla_g_kva, mla_w_ukv, mla_g_q, mla_g_k, hy_conv_w, hy_conv_b, hy_w_in, hy_b_in, hy_w_mid, hy_b_mid, hy_w_out, hy_b_out, hy_freq, hy_skip, gqa_g_q, gqa_g_k, g_out, w_out, w_router, b_router, w_moe_in, b_moe_in, w_moe_out, b_moe_out):
    batch, seq_c, d = x_prompt.shape
    nb_l, seq_l, _ = x_sample.shape
    depth = w_in.shape[0]
    past = cache_mla_ckv.shape[2]
    assert d == D_MODEL and seq_l % GRID_W == 0 and (2 * seq_l) % FFT_N1 == 0

    wts = _prep_weights(w_in, mla_g_qa, mla_w_uq, mla_g_kva, mla_w_ukv, mla_g_q, mla_g_k, gqa_g_q, gqa_g_k,
                        g_out, w_out, w_router, b_router, hy_w_in)
    wts["w_moe_in"] = _cast_call(w_moe_in.reshape(depth * N_EXPERTS, d, 2 * D_FF), 512).reshape(w_moe_in.shape)
    wts["w_moe_out"] = _cast_call(w_moe_out.reshape(depth * N_EXPERTS, D_FF, d), 512).reshape(w_moe_out.shape)
    wts["b_moe_in"] = b_moe_in.reshape(depth, N_EXPERTS, 1, 2 * D_FF)
    wts["b_moe_out"] = b_moe_out.reshape(depth, N_EXPERTS, 1, d)
    hw = dict(hy_w_in=wts["hy_w_in"], hy_b_in=hy_b_in.reshape(depth, 1, HY_FO), hy_w_mid=hy_w_mid,
              hy_b_mid=hy_b_mid.reshape(depth, HY_INNER, 1, HY_FO), hy_w_out=hy_w_out,
              hy_b_out=hy_b_out.reshape(depth, 1, -1), hy_freq=hy_freq.reshape(depth, 1, HY_FO),
              hy_conv_w=hy_conv_w, hy_conv_b=hy_conv_b.reshape(depth, 1, -1), hy_skip=hy_skip)
    dft_l = _lat_dft(seq_l)
    hw["skip_t"] = jnp.tile(hy_skip, (1, 1, dft_l["n2"]))

    conds = jnp.zeros((8, d), F32).at[0].set(c_ctx).at[1:1 + nb_l].set(c)
    mod = _mod_call(conds, w_mod, b_mod).reshape(depth, 8, 6, d)
    mod_c, mod_l = mod[:, 0:1], mod[:, 1:1 + nb_l]

    rope_tabs = (_rope_tables(seq_l, ROPE_A, _lane_map_a), _rope_tables(seq_l, HD_C, _lane_map_c))
    kax, vax = _cachekv_call(cache_mla_ckv, jnp.pad(cache_mla_kpe, ((0, 0), (0, 0), (0, 0), (NOPE_A, LANE - QK_A))), wts)
    kcx = cache_gqa_k.reshape(nb_l, depth, past, KV_C * HD_C).astype(BF16)
    vcx = cache_gqa_v.reshape(nb_l, depth, past, KV_C * HD_C).astype(BF16)

    z_c, dec_c = _hy_features(seq_c)
    cmat, smat, fwd_c, inv_c = _ctx_dft(seq_c)
    z_l, dec_l = _hy_features(seq_l)
    z_full = jnp.concatenate([z_l, jnp.zeros((1, HY_FO), F32), z_l[:0:-1]], axis=0)
    dec_full = jnp.concatenate([dec_l, jnp.zeros((1, HY_CH), F32), dec_l[:0:-1]], axis=0)

    xc = x_prompt.reshape(batch * seq_c, d)
    xl = x_sample.reshape(nb_l * seq_l, d)
    yc = yl = None
    new_ckv, new_kpe, new_k, new_v = [], [], [], []
    for l in range(depth):
        (xc, qa, ka, va, ckv, kpe, qc, kc, vc, kcf, vcf, u) = _premix_call(l, xc, yc, mod_c, seq_c, False, wts, None)
        new_ckv.append(ckv)
        new_kpe.append(kpe[:, NOPE_A:QK_A])
        new_k.append(kcf)
        new_v.append(vcf)
        oa, oc = _attn_ctx_call(qa, ka, va, qc, kc, vc, seq_c)
        kf_c = _filt_ctx_call(l, z_c, dec_c, hw, cmat, smat)
        ob = _hy_ctx_call(l, u, hw, kf_c, fwd_c, inv_c, seq_c)
        xc, h2, idx, gates = _postmix_call(l, xc, oa, ob, oc, mod_c, seq_c, wts)
        yc = _moe_call(l, h2, idx[:, :TOP_K], gates[:, :TOP_K], wts)
        (xl, qa, ka, va, _, _, qc, kc, vc, _, _, u) = _premix_call(l, xl, yl, mod_l, seq_l, True, wts, rope_tabs)
        oa, oc = _attn_lat_call(l, qa, ka, va, qc, kc, vc, kax, vax, kcx, vcx, seq_l)
        kf_l = _filter_lat(l, z_full, dec_full, hw, dft_l, seq_l)
        ob = _hyena_lat(l, u, hw, kf_l, dft_l, seq_l)
        xl, h2, idx, gates = _postmix_call(l, xl, oa, ob, oc, mod_l, seq_l, wts)
        yl = _moe_call(l, h2, idx[:, :TOP_K], gates[:, :TOP_K], wts)
    y_prompt = _final_call(depth - 1, xc, yc, mod_c, seq_c).reshape(batch, seq_c, d)
    y_sample = _final_call(depth - 1, xl, yl, mod_l, seq_l).reshape(nb_l, seq_l, d)
    stack = lambda xs, tail: jnp.stack([a.reshape((batch, seq_c) + tail) for a in xs], axis=1)
    return (y_prompt, y_sample, stack(new_ckv, (KV_RANK,)), stack(new_kpe, (ROPE_A,)),
            stack(new_k, (KV_C, HD_C)), stack(new_v, (KV_C, HD_C)))
```

```python
import functools
import math

import numpy as np
import jax
import jax.numpy as jnp
from jax import lax
from jax.experimental import pallas as pl
from jax.experimental.pallas import tpu as pltpu

F32 = jnp.float32
BF16 = jnp.bfloat16

D_MODEL = 1024
GRID_W = 64
EPS = 1e-6
ROPE_THETA = 10000.0
H_A = 6
Q_RANK = 256
KV_RANK = 128
NOPE_A = 64
ROPE_A = 32
V_A = 64
QK_A = NOPE_A + ROPE_A
HY_CH = 256
HY_ORDER = 2
HY_EMB = 33
HY_FO = 64
HY_INNER = 2
HY_MIN_DECAY = math.log(1e-2) / 1.5
HY_MAX_DECAY = math.log(1e-2) / 0.3
H_C = 6
KV_C = 2
HD_C = 64
N_EXPERTS = 32
TOP_K = 4
D_FF = 1024
SWIGLU_LIMIT = 7.0
SWIGLU_ALPHA = 1.702
LOG2E = 1.4426950408889634

LANE = 128
HQ = H_A * LANE
W_A = H_A * V_A
W_B = HY_CH
W_C = H_C * HD_C
IN_P = Q_RANK + KV_RANK + LANE + 3 * HY_CH + HQ + 2 * LANE
ROW_BLOCK = 128
VMEM_LIMIT = 56 * 1024 * 1024
FFT_N1 = 128
FFT_N2 = 64


def _cp(sem, vmem=None):
    return pltpu.CompilerParams(dimension_semantics=sem, vmem_limit_bytes=vmem)


def _dot(a, b):
    return jnp.dot(a, b, preferred_element_type=F32)


def _split(a):
    hi = a.astype(BF16)
    return hi, (a - hi.astype(F32)).astype(BF16)


def _dot3(a, b):
    ah, al = _split(a)
    bh, bl = _split(b)
    return _dot(ah, bh) + _dot(ah, bl) + _dot(al, bh)


def _bdot(a, b):
    return lax.dot_general(a, b, (((2,), (1,)), ((0,), (0,))), preferred_element_type=F32)


def _rms(x, n=None):
    ss = jnp.sum(x * x, axis=-1, keepdims=True) * (1.0 / (n or x.shape[-1]))
    return x * lax.rsqrt(ss + EPS)


def _head_norm(x, nvalid):
    outs = []
    for h in range(x.shape[1] // LANE):
        blk = x[:, h * LANE:(h + 1) * LANE]
        ss = jnp.sum(blk * blk, axis=-1, keepdims=True) * (1.0 / nvalid)
        outs.append(blk * lax.rsqrt(ss + EPS))
    return outs[0] if len(outs) == 1 else jnp.concatenate(outs, axis=1)


def _rope(x, tab_ref, shift):
    c, sm, sp = tab_ref[0], tab_ref[1], tab_ref[2]
    outs = []
    for h in range(x.shape[1] // LANE):
        blk = x[:, h * LANE:(h + 1) * LANE]
        outs.append(blk * c + pltpu.roll(blk, LANE - shift, 1) * sm + pltpu.roll(blk, shift, 1) * sp)
    return outs[0] if len(outs) == 1 else jnp.concatenate(outs, axis=1)


def _mod_kernel(c_ref, w_ref, b_ref, o_ref):
    c = c_ref[...]
    s = c * (1.0 / (1.0 + jnp.exp(-c)))
    o_ref[...] = _dot3(s, w_ref[...]) + b_ref[...]


def _mod_call(conds, w_mod, b_mod):
    depth, d, n6 = w_mod.shape
    tn = 1536
    return pl.pallas_call(
        _mod_kernel,
        out_shape=jax.ShapeDtypeStruct((depth, 8, n6), F32),
        grid=(depth, n6 // tn),
        in_specs=[pl.BlockSpec((8, d), lambda l, j: (0, 0)),
                  pl.BlockSpec((None, d, tn), lambda l, j: (l, 0, j)),
                  pl.BlockSpec((None, 1, tn), lambda l, j: (l, 0, j))],
        out_specs=pl.BlockSpec((None, 8, tn), lambda l, j: (l, 0, j)),
        compiler_params=_cp(("parallel", "parallel"), VMEM_LIMIT),
        name="mod",
    )(conds, w_mod, b_mod.reshape(depth, 1, n6))


def _cast_kernel(x_ref, o_ref):
    o_ref[...] = x_ref[...].astype(BF16)


def _cast_call(w, tr):
    n, r, c = w.shape
    return pl.pallas_call(
        _cast_kernel,
        out_shape=jax.ShapeDtypeStruct(w.shape, BF16),
        grid=(n, r // tr),
        in_specs=[pl.BlockSpec((None, tr, c), lambda i, j: (i, j, 0))],
        out_specs=pl.BlockSpec((None, tr, c), lambda i, j: (i, j, 0)),
        compiler_params=_cp(("parallel", "parallel"), VMEM_LIMIT),
        name="cast_bf16",
    )(w)


def _kv_heads(ckv_bf, kpe, w_kv_ref, gk, rope_ref, use_rope):
    kvp = _dot(ckv_bf, w_kv_ref[...])
    kn = kvp[:, :HQ]
    ka = jnp.concatenate([kn[:, h * LANE:(h + 1) * LANE] + kpe for h in range(H_A)], axis=1)
    ka = _head_norm(ka, QK_A) * gk
    if use_rope:
        ka = _rope(ka, rope_ref, ROPE_A // 2)
    return ka, kvp[:, HQ:]


def _premix_kernel(has_prev, use_rope, *refs):
    it = iter(refs)
    x_ref = next(it)
    if has_prev:
        y_ref, modp_ref = next(it), next(it)
    mod_ref, w_in_ref, w_uq_ref, w_kv_ref, g_ref = next(it), next(it), next(it), next(it), next(it)
    if use_rope:
        ra_ref, rc_ref = next(it), next(it)
    else:
        ra_ref = rc_ref = None
    (xo_ref, qa_ref, ka_ref, va_ref, ckv_ref, kpe_ref, qc_ref, kc_ref, vc_ref, kcf_ref, vcf_ref, u_ref) = it

    x = x_ref[...]
    if has_prev:
        x = x + modp_ref[5:6, :] * y_ref[...]
    xo_ref[...] = x
    h = _rms(x) * (1.0 + mod_ref[1:2, :]) + mod_ref[0:1, :]
    proj = _dot(h.astype(BF16), w_in_ref[...])
    o = 0
    c_q = proj[:, o:o + Q_RANK]; o += Q_RANK
    c_kv = proj[:, o:o + KV_RANK]; o += KV_RANK
    kpe = proj[:, o:o + LANE]; o += LANE
    u_ref[...] = proj[:, o:o + 3 * HY_CH]; o += 3 * HY_CH
    q_c = proj[:, o:o + HQ]; o += HQ
    k_c = proj[:, o:o + LANE]; o += LANE
    v_c = proj[:, o:o + LANE]

    cqn = _rms(c_q) * g_ref[0:1, :Q_RANK]
    qa = _head_norm(_dot(cqn.astype(BF16), w_uq_ref[...]), QK_A) * g_ref[2:3, :]
    if use_rope:
        qa = _rope(qa, ra_ref, ROPE_A // 2)
    qa_ref[...] = qa.astype(BF16)
    ckv = _rms(c_kv) * g_ref[1:2, :KV_RANK]
    ckv_ref[...] = ckv
    kpe_ref[...] = kpe
    ka, va = _kv_heads(ckv.astype(BF16), kpe, w_kv_ref, g_ref[3:4, :], ra_ref, use_rope)
    ka_ref[...] = ka.astype(BF16)
    va_ref[...] = _with_ones(va.astype(BF16))

    qc = _head_norm(q_c, HD_C) * g_ref[4:5, :]
    if use_rope:
        qc = _rope(qc, rc_ref, HD_C // 2)
    qc_ref[...] = qc.astype(BF16)
    lane = lax.broadcasted_iota(jnp.int32, k_c.shape, 1)
    k2 = k_c * k_c
    s0 = jnp.sum(jnp.where(lane < HD_C, k2, 0.0), axis=-1, keepdims=True) * (1.0 / HD_C)
    s1 = jnp.sum(jnp.where(lane >= HD_C, k2, 0.0), axis=-1, keepdims=True) * (1.0 / HD_C)
    kcn = k_c * jnp.where(lane < HD_C, lax.rsqrt(s0 + EPS), lax.rsqrt(s1 + EPS)) * g_ref[5:6, :LANE]
    kcf_ref[...] = kcn
    vcf_ref[...] = v_c
    kc_ref[...] = (_rope(kcn, rc_ref, HD_C // 2) if use_rope else kcn).astype(BF16)
    vc_ref[...] = _with_ones(v_c.astype(BF16))


def _premix_call(l, x, yprev, mod, seq, use_rope, wts, rope_tabs):
    t, d = x.shape
    tm = 256
    ncond = mod.shape[1]
    has_prev = yprev is not None

    def cond(i):
        return (i * tm) // seq if ncond > 1 else 0

    row = lambda w: pl.BlockSpec((tm, w), lambda i: (i, 0))
    ins, specs = [x], [row(d)]
    if has_prev:
        ins += [yprev, mod]
        specs += [row(d), pl.BlockSpec((None, None, 6, d), lambda i: (l - 1, cond(i), 0, 0))]
    ins += [mod, wts["w_in"], wts["w_uq"], wts["w_kv"], wts["gains"]]
    specs += [pl.BlockSpec((None, None, 6, d), lambda i: (l, cond(i), 0, 0)),
              pl.BlockSpec((None, d, IN_P), lambda i: (l, 0, 0)),
              pl.BlockSpec((None, Q_RANK, HQ), lambda i: (l, 0, 0)),
              pl.BlockSpec((None, KV_RANK, HQ + W_A), lambda i: (l, 0, 0)),
              pl.BlockSpec((None, 8, HQ), lambda i: (l, 0, 0))]
    if use_rope:
        nt = seq // tm
        ins += [rope_tabs[0], rope_tabs[1]]
        specs += [pl.BlockSpec((3, tm, LANE), lambda i: (0, i % nt, 0))] * 2
    outs = [(d, F32), (HQ, BF16), (HQ, BF16), (2 * W_A, BF16), (KV_RANK, F32), (LANE, F32), (HQ, BF16),
            (LANE, BF16), (2 * LANE, BF16), (LANE, F32), (LANE, F32), (3 * HY_CH, F32)]
    return pl.pallas_call(
        functools.partial(_premix_kernel, has_prev, use_rope),
        out_shape=[jax.ShapeDtypeStruct((t, w), dt) for w, dt in outs],
        grid=(t // tm,),
        in_specs=specs,
        out_specs=[row(w) for w, _ in outs],
        compiler_params=_cp(("parallel",), VMEM_LIMIT),
        name="premix",
    )(*ins)


def _cachekv_kernel(ckv_ref, kpe_ref, w_kv_ref, g_ref, ka_ref, va_ref):
    ka, va = _kv_heads(ckv_ref[...].astype(BF16), kpe_ref[...], w_kv_ref, g_ref[3:4, :], None, False)
    ka_ref[...] = ka.astype(BF16)
    va_ref[...] = _with_ones(va.astype(BF16))


def _cachekv_call(cache_ckv, cache_kpe_p, wts):
    nb, depth, past, _ = cache_ckv.shape
    return pl.pallas_call(
        _cachekv_kernel,
        out_shape=[jax.ShapeDtypeStruct((depth, nb, past, HQ), BF16),
                   jax.ShapeDtypeStruct((depth, nb, past, 2 * W_A), BF16)],
        grid=(depth, nb),
        in_specs=[pl.BlockSpec((None, None, past, KV_RANK), lambda l, b: (b, l, 0, 0)),
                  pl.BlockSpec((None, None, past, LANE), lambda l, b: (b, l, 0, 0)),
                  pl.BlockSpec((None, KV_RANK, HQ + W_A), lambda l, b: (l, 0, 0)),
                  pl.BlockSpec((None, 8, HQ), lambda l, b: (l, 0, 0))],
        out_specs=[pl.BlockSpec((None, None, past, HQ), lambda l, b: (l, b, 0, 0)),
                   pl.BlockSpec((None, None, past, 2 * W_A), lambda l, b: (l, b, 0, 0))],
        compiler_params=_cp(("parallel", "parallel"), VMEM_LIMIT),
        name="cache_kv",
    )(cache_ckv, cache_kpe_p, wts["w_kv"], wts["gains"])


def _nt(q, k):
    return lax.dot_general(q, k, (((1,), (1,)), ((), ())), preferred_element_type=F32)


def _with_ones(v):
    ones = jnp.ones((v.shape[0], LANE), v.dtype)
    parts = []
    for j in range(v.shape[1] // LANE):
        parts += [v[:, j * LANE:(j + 1) * LANE], ones]
    return jnp.concatenate(parts, axis=1)


def _attend(q, ks, vs):
    ss = [_nt(q, k) for k in ks]
    m = ss[0].max(axis=-1, keepdims=True)
    for s in ss[1:]:
        m = jnp.maximum(m, s.max(axis=-1, keepdims=True))
    acc = None
    for s, v in zip(ss, vs):
        pv = _dot(jnp.exp2(s - m).astype(BF16), v)
        acc = pv if acc is None else acc + pv
    return acc[:, :LANE] / acc[:, LANE:]


def _attn_kernel(nseg, *refs):
    qa_ref, qc_ref = refs[0], refs[1]
    segs = refs[2:2 + 4 * nseg]
    oa_ref, oc_ref = refs[2 + 4 * nseg:]
    ka_refs, va_refs, kc_refs, vc_refs = (segs[i::4] for i in range(4))
    lane = lax.broadcasted_iota(jnp.int32, (qa_ref.shape[0], LANE), 1)
    low = lane < V_A
    for j in range(H_A // 2):
        vs = [r[:, 2 * j * LANE:2 * (j + 1) * LANE] for r in va_refs]
        pv = []
        for h in (2 * j, 2 * j + 1):
            hs = slice(h * LANE, (h + 1) * LANE)
            pv.append(_attend(qa_ref[:, hs], [r[:, hs] for r in ka_refs], vs))
        oa_ref[:, j * LANE:(j + 1) * LANE] = jnp.where(low, pv[0], pv[1])
    kcs = [r[...] for r in kc_refs]
    vcs = [r[...] for r in vc_refs]
    g_per = H_C // KV_C
    for g in range(g_per):
        pv = [_attend(qc_ref[:, h * LANE:(h + 1) * LANE], kcs, vcs) for h in (g, g + g_per)]
        oc_ref[:, g * LANE:(g + 1) * LANE] = jnp.where(low, pv[0], pv[1])


def _attn_ctx_call(qa, ka, va, qc, kc, vc, seq):
    t = qa.shape[0]
    blk = lambda w: pl.BlockSpec((seq, w), lambda b: (b, 0))
    return pl.pallas_call(
        functools.partial(_attn_kernel, 1),
        out_shape=[jax.ShapeDtypeStruct((t, W_A), F32), jax.ShapeDtypeStruct((t, W_C), F32)],
        grid=(t // seq,),
        in_specs=[blk(HQ), blk(HQ), blk(HQ), blk(2 * W_A), blk(LANE), blk(2 * LANE)],
        out_specs=[blk(W_A), blk(W_C)],
        compiler_params=_cp(("parallel",), VMEM_LIMIT),
        name="attn_ctx",
    )(qa, qc, ka, va, kc, vc)


def _attn_lat_call(l, qa, ka, va, qc, kc, vc, kax, vax, kcx, vcx, seq):
    t = qa.shape[0]
    nb = t // seq
    tq = min(256, seq)
    nq = seq // tq
    past = kax.shape[2]
    qblk = lambda w: pl.BlockSpec((tq, w), lambda b, i: (b * nq + i, 0))
    sblk = lambda w: pl.BlockSpec((seq, w), lambda b, i: (b, 0))
    xblk = lambda w: pl.BlockSpec((None, None, past, w), lambda b, i: (l, b, 0, 0))
    cblk = lambda w: pl.BlockSpec((None, None, past, w), lambda b, i: (b, l, 0, 0))
    return pl.pallas_call(
        functools.partial(_attn_kernel, 2),
        out_shape=[jax.ShapeDtypeStruct((t, W_A), F32), jax.ShapeDtypeStruct((t, W_C), F32)],
        grid=(nb, nq),
        in_specs=[qblk(HQ), qblk(HQ),
                  xblk(HQ), xblk(2 * W_A), cblk(LANE), cblk(2 * LANE),
                  sblk(HQ), sblk(2 * W_A), sblk(LANE), sblk(2 * LANE)],
        out_specs=[qblk(W_A), qblk(W_C)],
        compiler_params=_cp(("parallel", "parallel"), VMEM_LIMIT),
        name="attn_lat",
    )(qa, qc, kax, vax, kcx, vcx, ka, va, kc, vc)


def _filter_mlp(z, w_in_ref, b_in_ref, w_mid_ref, b_mid_ref, w_out_ref, b_out_ref, freq_ref):
    freq = freq_ref[...]
    a = jnp.sin(freq * (_dot3(z, w_in_ref[...]) + b_in_ref[...]))
    for i in range(HY_INNER):
        a = jnp.sin(freq * (_dot3(a, w_mid_ref[i]) + b_mid_ref[i]))
    return _dot3(a, w_out_ref[...]) + b_out_ref[...]


def _conv3(u, up, dn, cw_ref, cb_ref):
    return up * cw_ref[0:1, :] + u * cw_ref[1:2, :] + dn * cw_ref[2:3, :] + cb_ref[...]


def _filt_ctx_kernel(z_ref, dec_ref, w_in_ref, b_in_ref, w_mid_ref, b_mid_ref, w_out_ref, b_out_ref, freq_ref,
                     c_ref, s_ref, o_ref):
    n = z_ref.shape[0]
    h = _filter_mlp(z_ref[...], w_in_ref, b_in_ref, w_mid_ref, b_mid_ref, w_out_ref, b_out_ref, freq_ref)
    dec = dec_ref[...]
    row = lax.broadcasted_iota(jnp.int32, dec.shape, 0)
    half = HY_ORDER * HY_CH
    for o in range(HY_ORDER):
        hf = h[:, o * HY_CH:(o + 1) * HY_CH] * dec
        hb = jnp.where(row > 0, h[:, half + o * HY_CH:half + (o + 1) * HY_CH] * dec, 0.0)
        nrm = jnp.sum(jnp.abs(hf) + jnp.abs(hb), axis=0, keepdims=True) + EPS
        scale = (1.0 / n) / nrm
        o_ref[o, 0] = _dot3(c_ref[...], hf + hb) * scale
        o_ref[o, 1] = -_dot3(s_ref[...], hf - hb) * scale


def _filt_ctx_call(l, z, dec, hw, cmat, smat):
    n = z.shape[0]
    full = lambda a: pl.BlockSpec((None,) + a.shape[1:], lambda i: (l,) + (0,) * (a.ndim - 1))
    const = lambda a: pl.BlockSpec(a.shape, lambda i: (0,) * a.ndim)
    names = ["hy_w_in", "hy_b_in", "hy_w_mid", "hy_b_mid", "hy_w_out", "hy_b_out", "hy_freq"]
    return pl.pallas_call(
        _filt_ctx_kernel,
        out_shape=jax.ShapeDtypeStruct((HY_ORDER, 2, n, HY_CH), F32),
        grid=(1,),
        in_specs=[const(z), const(dec)] + [full(hw[k]) for k in names] + [const(cmat), const(smat)],
        out_specs=pl.BlockSpec((HY_ORDER, 2, n, HY_CH), lambda i: (0, 0, 0, 0)),
        compiler_params=_cp(("arbitrary",), VMEM_LIMIT),
        name="hy_filter_ctx",
    )(z, dec, *[hw[k] for k in names], cmat, smat)


def _hy_ctx_kernel(u_ref, cw_ref, cb_ref, skip_ref, kf_ref, fwd_ref, inv_ref, o_ref):
    u = u_ref[...]
    n = u.shape[0]
    row = lax.broadcasted_iota(jnp.int32, u.shape, 0)
    up = jnp.where(row > 0, pltpu.roll(u, 1, 0), 0.0)
    dn = jnp.where(row < n - 1, pltpu.roll(u, n - 1, 0), 0.0)
    z = _conv3(u, up, dn, cw_ref, cb_ref)
    s = z[:, :HY_CH]
    gates = (z[:, HY_CH:2 * HY_CH], z[:, 2 * HY_CH:])
    for o in range(HY_ORDER):
        xs = _dot(fwd_ref[...], s.astype(BF16))
        xr, xi = xs[:n], xs[n:]
        kr, ki = kf_ref[o, 0], kf_ref[o, 1]
        ycat = jnp.concatenate([xr * kr - xi * ki, xr * ki + xi * kr], axis=0)
        y = _dot(inv_ref[...], ycat.astype(BF16))
        s = gates[o] * (y + s * skip_ref[o:o + 1, :])
    o_ref[...] = s


def _hy_ctx_call(l, u, hw, kf, fwd, inv, seq):
    t = u.shape[0]
    full = lambda a: pl.BlockSpec((None,) + a.shape[1:], lambda b: (l,) + (0,) * (a.ndim - 1))
    const = lambda a: pl.BlockSpec(a.shape, lambda b: (0,) * a.ndim)
    return pl.pallas_call(
        _hy_ctx_kernel,
        out_shape=jax.ShapeDtypeStruct((t, HY_CH), F32),
        grid=(t // seq,),
        in_specs=[pl.BlockSpec((seq, 3 * HY_CH), lambda b: (b, 0)),
                  full(hw["hy_conv_w"]), full(hw["hy_conv_b"]), full(hw["hy_skip"]),
                  const(kf), const(fwd), const(inv)],
        out_specs=pl.BlockSpec((seq, HY_CH), lambda b: (b, 0)),
        compiler_params=_cp(("parallel",), VMEM_LIMIT),
        name="hyena_ctx",
    )(u, hw["hy_conv_w"], hw["hy_conv_b"], hw["hy_skip"], kf, fwd, inv)


def _filt_lat_kernel(seq, z_ref, dec_ref, w_in_ref, b_in_ref, w_mid_ref, b_mid_ref, w_out_ref, b_out_ref, freq_ref,
                     k_ref, n_ref):
    i = pl.program_id(0)
    tr = z_ref.shape[0]
    h = _filter_mlp(z_ref[...], w_in_ref, b_in_ref, w_mid_ref, b_mid_ref, w_out_ref, b_out_ref, freq_ref)
    half = HY_ORDER * HY_CH
    row = i * tr + lax.broadcasted_iota(jnp.int32, (tr, half), 0)
    dec = dec_ref[...]
    kern = jnp.where(row < seq, h[:, :half], h[:, half:]) * jnp.concatenate([dec] * HY_ORDER, axis=1)
    k_ref[...] = kern

    @pl.when(i == 0)
    def _():
        n_ref[...] = jnp.zeros_like(n_ref)

    n_ref[...] += jnp.sum(jnp.abs(kern), axis=0, keepdims=True)


def _filt_lat_call(l, z, dec, hw, seq):
    n = z.shape[0]
    tr = min(512, n)
    half = HY_ORDER * HY_CH
    full = lambda a: pl.BlockSpec((None,) + a.shape[1:], lambda i: (l,) + (0,) * (a.ndim - 1))
    names = ["hy_w_in", "hy_b_in", "hy_w_mid", "hy_b_mid", "hy_w_out", "hy_b_out", "hy_freq"]
    return pl.pallas_call(
        functools.partial(_filt_lat_kernel, seq),
        out_shape=[jax.ShapeDtypeStruct((n, half), F32), jax.ShapeDtypeStruct((1, half), F32)],
        grid=(n // tr,),
        in_specs=[pl.BlockSpec((tr, z.shape[1]), lambda i: (i, 0)), pl.BlockSpec((tr, HY_CH), lambda i: (i, 0))]
        + [full(hw[k]) for k in names],
        out_specs=[pl.BlockSpec((tr, half), lambda i: (i, 0)), pl.BlockSpec((1, half), lambda i: (0, 0))],
        compiler_params=_cp(("arbitrary",), VMEM_LIMIT),
        name="hy_filter_lat",
    )(z, dec, *[hw[k] for k in names])


def _fa_kernel(f1_ref, k_ref, o_ref):
    r = _dot3(f1_ref[...], k_ref[...])
    o_ref[0] = r[:FFT_N1]
    o_ref[1] = r[FFT_N1:]


def _fa_call(f1f, kern2d):
    n1, w = kern2d.shape
    tn = min(2048, w)
    return pl.pallas_call(
        _fa_kernel,
        out_shape=jax.ShapeDtypeStruct((2, FFT_N1, w), F32),
        grid=(w // tn,),
        in_specs=[pl.BlockSpec(f1f.shape, lambda j: (0, 0)), pl.BlockSpec((n1, tn), lambda j: (0, j))],
        out_specs=pl.BlockSpec((2, FFT_N1, tn), lambda j: (0, 0, j)),
        compiler_params=_cp(("parallel",), VMEM_LIMIT),
        name="hy_filter_dft1",
    )(f1f, kern2d)


def _fb_kernel(n_total, a_ref, fh_ref, fl_ref, n_ref, o_ref):
    a = jnp.concatenate([a_ref[0], a_ref[1]], axis=1)
    ah, al = _split(a)
    x = _bdot(fh_ref[...], ah) + _bdot(fh_ref[...], al) + _bdot(fl_ref[...], ah)
    scale = (1.0 / n_total) / (n_ref[...] + EPS)
    o_ref[...] = x * scale[None]


def _fb_call(af5, f2h, f2l, nrm, n_total):
    _, n1, n2, c = af5.shape[0], af5.shape[1], af5.shape[2], af5.shape[3]
    k1t = 8
    return pl.pallas_call(
        functools.partial(_fb_kernel, n_total),
        out_shape=jax.ShapeDtypeStruct((n1, 2 * n2, c), F32),
        grid=(n1 // k1t,),
        in_specs=[pl.BlockSpec((2, k1t, n2, c), lambda j: (0, j, 0, 0)),
                  pl.BlockSpec((k1t, 2 * n2, 2 * n2), lambda j: (j, 0, 0)),
                  pl.BlockSpec((k1t, 2 * n2, 2 * n2), lambda j: (j, 0, 0)),
                  pl.BlockSpec((1, c), lambda j: (0, 0))],
        out_specs=pl.BlockSpec((k1t, 2 * n2, c), lambda j: (j, 0, 0)),
        compiler_params=_cp(("parallel",), VMEM_LIMIT),
        name="hy_filter_dft2",
    )(af5, f2h, f2l, nrm)


def _hconv_kernel(seq, u_ref, p_ref, n_ref, cw_ref, cb_ref, v_ref, x1_ref, x2_ref):
    i = pl.program_id(0)
    u = u_ref[...]
    tt = u.shape[0]
    row = lax.broadcasted_iota(jnp.int32, u.shape, 0)
    pos = (i * tt) % seq
    prev = jnp.where(pos > 0, p_ref[7:8, :], 0.0)
    nxt = jnp.where(pos + tt < seq, n_ref[0:1, :], 0.0)
    up = jnp.where(row > 0, pltpu.roll(u, 1, 0), prev)
    dn = jnp.where(row < tt - 1, pltpu.roll(u, tt - 1, 0), nxt)
    z = _conv3(u, up, dn, cw_ref, cb_ref)
    v_ref[...] = z[:, :HY_CH]
    x1_ref[...] = z[:, HY_CH:2 * HY_CH]
    x2_ref[...] = z[:, 2 * HY_CH:]


def _hconv_call(l, u, hw, seq):
    t, w = u.shape
    tt = min(512, seq)
    nblk8 = t // 8
    full = lambda a: pl.BlockSpec((None,) + a.shape[1:], lambda i: (l,) + (0,) * (a.ndim - 1))
    ob = pl.BlockSpec((tt, HY_CH), lambda i: (i, 0))
    return pl.pallas_call(
        functools.partial(_hconv_kernel, seq),
        out_shape=[jax.ShapeDtypeStruct((t, HY_CH), F32)] * 3,
        grid=(t // tt,),
        in_specs=[pl.BlockSpec((tt, w), lambda i: (i, 0)),
                  pl.BlockSpec((8, w), lambda i: (jnp.maximum(i * (tt // 8) - 1, 0), 0)),
                  pl.BlockSpec((8, w), lambda i: (jnp.minimum((i + 1) * (tt // 8), nblk8 - 1), 0)),
                  full(hw["hy_conv_w"]), full(hw["hy_conv_b"])],
        out_specs=[ob, ob, ob],
        compiler_params=_cp(("parallel",), VMEM_LIMIT),
        name="hyena_conv3",
    )(u, u, u, hw["hy_conv_w"], hw["hy_conv_b"])


def _ha_kernel(f1_ref, x_ref, o_ref):
    r = _dot(f1_ref[...], x_ref[...].astype(BF16))
    o_ref[0] = r[:FFT_N1].astype(BF16)
    o_ref[1] = r[FFT_N1:].astype(BF16)


def _ha_call(f1d, x2d, nb):
    rows, w = x2d.shape
    n1h = rows // nb
    tn = min(2048, w)
    return pl.pallas_call(
        _ha_kernel,
        out_shape=jax.ShapeDtypeStruct((nb, 2, FFT_N1, w), BF16),
        grid=(nb, w // tn),
        in_specs=[pl.BlockSpec(f1d.shape, lambda b, j: (0, 0)), pl.BlockSpec((n1h, tn), lambda b, j: (b, j))],
        out_specs=pl.BlockSpec((None, 2, FFT_N1, tn), lambda b, j: (b, 0, 0, j)),
        compiler_params=_cp(("parallel", "parallel"), VMEM_LIMIT),
        name="hyena_dft1",
    )(f1d, x2d)


def _hb_kernel(a_ref, f_ref, g_ref, kf_ref, o_ref):
    n2 = a_ref.shape[2]
    a = jnp.concatenate([a_ref[0], a_ref[1]], axis=1)
    x = _bdot(f_ref[...], a)
    xr, xi = x[:, :n2], x[:, n2:]
    kr, ki = kf_ref[:, :n2], kf_ref[:, n2:]
    y = jnp.concatenate([xr * kr - xi * ki, xr * ki + xi * kr], axis=1).astype(BF16)
    b = _bdot(g_ref[...], y)
    o_ref[0] = b[:, :n2].astype(BF16)
    o_ref[1] = b[:, n2:].astype(BF16)


def _hb_call(o, a5, f2, g2, kf):
    nb, _, n1, n2, c = a5.shape
    k1t = 16
    blk = pl.BlockSpec((None, 2, k1t, n2, c), lambda b, j: (b, 0, j, 0, 0))
    mat = pl.BlockSpec((k1t, 2 * n2, 2 * n2), lambda b, j: (j, 0, 0))
    return pl.pallas_call(
        _hb_kernel,
        out_shape=jax.ShapeDtypeStruct(a5.shape, BF16),
        grid=(nb, n1 // k1t),
        in_specs=[blk, mat, mat, pl.BlockSpec((k1t, 2 * n2, c), lambda b, j: (j, 0, o))],
        out_specs=blk,
        compiler_params=_cp(("parallel", "parallel"), VMEM_LIMIT),
        name="hyena_dft2",
    )(a5, f2, g2, kf)


def _hc_kernel(o, fc_ref, b_ref, s_ref, g_ref, skip_ref, o_ref):
    bcat = jnp.concatenate([b_ref[0], b_ref[1]], axis=0)
    y = _dot(fc_ref[...], bcat)
    o_ref[...] = g_ref[...] * (y + s_ref[...] * skip_ref[o:o + 1, :])


def _hc_call(l, o, fc, b4, s2d, g2d, skip_t):
    nb, _, n1, w = b4.shape
    rows = s2d.shape[0] // nb
    tn = min(2048, w)
    blk = pl.BlockSpec((rows, tn), lambda b, j: (b, j))
    return pl.pallas_call(
        functools.partial(_hc_kernel, o),
        out_shape=jax.ShapeDtypeStruct(s2d.shape, F32),
        grid=(nb, w // tn),
        in_specs=[pl.BlockSpec(fc.shape, lambda b, j: (0, 0)),
                  pl.BlockSpec((None, 2, n1, tn), lambda b, j: (b, 0, 0, j)),
                  blk, blk,
                  pl.BlockSpec((None, HY_ORDER, tn), lambda b, j: (l, 0, j))],
        out_specs=blk,
        compiler_params=_cp(("parallel", "parallel"), VMEM_LIMIT),
        name="hyena_dft3",
    )(fc, b4, s2d, g2d, skip_t)


def _postmix_kernel(x_ref, oa_ref, ob_ref, oc_ref, mod_ref, g_ref, wa_ref, wb_ref, wc_ref, wrh_ref, wrl_ref, br_ref,
                    x1_ref, h2_ref, idx_ref, gate_ref):
    na = _rms(oa_ref[...]) * g_ref[:, :W_A]
    nb = _rms(ob_ref[...]) * g_ref[:, W_A:W_A + W_B]
    nc = _rms(oc_ref[...]) * g_ref[:, W_A + W_B:]
    mix = (_dot(na.astype(BF16), wa_ref[...]) + _dot(nb.astype(BF16), wb_ref[...])
           + _dot(nc.astype(BF16), wc_ref[...]))
    x1 = x_ref[...] + mod_ref[2:3, :] * mix
    x1_ref[...] = x1
    h2 = _rms(x1) * (1.0 + mod_ref[4:5, :]) + mod_ref[3:4, :]
    h2_ref[...] = h2
    hh, hl = _split(h2)
    vals = _dot(hh, wrh_ref[...]) + _dot(hh, wrl_ref[...]) + _dot(hl, wrh_ref[...]) + br_ref[...]
    lane = lax.broadcasted_iota(jnp.int32, vals.shape, 1).astype(F32)
    idx_out = jnp.zeros(vals.shape, F32)
    top = jnp.zeros(vals.shape, F32)
    m0 = None
    for k in range(TOP_K):
        m = vals.max(axis=-1, keepdims=True)
        sel = jnp.min(jnp.where(vals == m, lane, float(LANE)), axis=-1, keepdims=True)
        if m0 is None:
            m0 = m
        idx_out = jnp.where(lane == k, sel, idx_out)
        top = jnp.where(lane == k, jnp.exp(m - m0), top)
        vals = jnp.where(lane == sel, -jnp.inf, vals)
    idx_ref[...] = idx_out.astype(jnp.int32)
    gate_ref[...] = top / jnp.sum(top, axis=-1, keepdims=True)


def _postmix_call(l, x, oa, ob, oc, mod, seq, wts):
    t, d = x.shape
    tm = 256
    ncond = mod.shape[1]

    def cond(i):
        return (i * tm) // seq if ncond > 1 else 0

    row = lambda w: pl.BlockSpec((tm, w), lambda i: (i, 0))
    lay = lambda a: pl.BlockSpec((None,) + a.shape[1:], lambda i: (l,) + (0,) * (a.ndim - 1))
    names = ["g_out", "wo_a", "wo_b", "wo_c", "wr_hi", "wr_lo", "b_router"]
    return pl.pallas_call(
        _postmix_kernel,
        out_shape=[jax.ShapeDtypeStruct((t, d), F32), jax.ShapeDtypeStruct((t, d), F32),
                   jax.ShapeDtypeStruct((t, LANE), jnp.int32), jax.ShapeDtypeStruct((t, LANE), F32)],
        grid=(t // tm,),
        in_specs=[row(d), row(W_A), row(W_B), row(W_C),
                  pl.BlockSpec((None, None, 6, d), lambda i: (l, cond(i), 0, 0))] + [lay(wts[k]) for k in names],
        out_specs=[row(d), row(d), row(LANE), row(LANE)],
        compiler_params=_cp(("parallel",), VMEM_LIMIT),
        name="postmix",
    )(x, oa, ob, oc, mod, *[wts[k] for k in names])


SUB = 8


def _moe_kernel(be_ref, rs_ref, nv_ref, nu_ref, tok_ref, gate_ref, h_ref, wi_ref, bi_ref, wo_ref, bo_ref, y_ref,
                buf, obuf):
    i = pl.program_id(0)

    @pl.when(i == 0)
    def _():
        y_ref[...] = jnp.zeros_like(y_ref)
        buf[...] = jnp.zeros_like(buf)

    @pl.when(i < nu_ref[0])
    def _():
        base = rs_ref[i]
        nv = nv_ref[i]

        def gather(g, c):
            rows = [h_ref[pl.ds(tok_ref[base + g * SUB + j], 1), :] for j in range(SUB)]
            for j in range(SUB):
                buf[g, j:j + 1, :] = rows[j]
            return c

        lax.fori_loop(0, (nv + SUB - 1) // SUB, gather, 0)
        x = buf[...].reshape(ROW_BLOCK, buf.shape[2])
        gu = _dot(x.astype(BF16), wi_ref[...]) + bi_ref[...]
        gt = jnp.minimum(gu[:, :D_FF], SWIGLU_LIMIT)
        lin = jnp.clip(gu[:, D_FF:], -SWIGLU_LIMIT, SWIGLU_LIMIT)
        act = (lin + 1.0) * gt * (1.0 / (1.0 + jnp.exp(-SWIGLU_ALPHA * gt)))
        out = _dot(act.astype(BF16), wo_ref[...]) + bo_ref[...]
        obuf[...] = out.reshape(obuf.shape)

        def scatter(g, c):
            ts = [tok_ref[base + g * SUB + j] for j in range(SUB)]
            new = [y_ref[pl.ds(ts[j], 1), :] + gate_ref[base + g * SUB + j] * obuf[g, j:j + 1, :]
                   for j in range(SUB)]
            for j in range(SUB):
                y_ref[pl.ds(ts[j], 1), :] = new[j]
            return c

        full = nv // SUB
        lax.fori_loop(0, full, scatter, 0)

        def scatter_row(r, c):
            t = tok_ref[base + r]
            y_ref[pl.ds(t, 1), :] = y_ref[pl.ds(t, 1), :] + gate_ref[base + r] * obuf[full, pl.ds(r - full * SUB, 1), :]
            return c

        lax.fori_loop(full * SUB, nv, scatter_row, 0)


def _moe_kernel_entry(has_alias, *refs):
    refs = list(refs)
    if has_alias:
        del refs[11]
    _moe_kernel(*refs)


def _route(idx, gates, n_blocks):
    m = idx.shape[0] * TOP_K
    e = idx.reshape(m)
    flat = jnp.arange(m, dtype=jnp.int32)
    skey, gate = lax.sort((e * m + flat, gates.reshape(m)), num_keys=1)
    tok = (skey % m) // TOP_K
    experts = jnp.arange(N_EXPERTS, dtype=jnp.int32)
    cnt = jnp.sum((e[:, None] == experts[None, :]).astype(jnp.int32), axis=0)
    nblk = (cnt + ROW_BLOCK - 1) // ROW_BLOCK
    bend = jnp.cumsum(nblk)
    n_used = bend[-1]
    blk = jnp.arange(n_blocks, dtype=jnp.int32)
    bcl = jnp.minimum(blk, n_used - 1)
    be = jnp.sum((bend[None, :] <= bcl[:, None]).astype(jnp.int32), axis=1)
    oh = (be[:, None] == experts[None, :]).astype(jnp.int32)
    pick = lambda v: jnp.sum(oh * v[None, :], axis=1)
    off = (bcl - pick(bend - nblk)) * ROW_BLOCK
    rs = pick(jnp.cumsum(cnt) - cnt) + off
    nv = jnp.where(blk < n_used, jnp.clip(pick(cnt) - off, 0, ROW_BLOCK), 0)
    pad = jnp.zeros((ROW_BLOCK,), jnp.int32)
    return (be, rs, nv, n_used.reshape(1), jnp.concatenate([tok, pad]),
            jnp.concatenate([gate, pad.astype(F32)]))


def _moe_call(l, h2, idx, gates, wts):
    t, d = h2.shape
    tc = min(4096, t)
    n_blocks = tc * TOP_K // ROW_BLOCK + N_EXPERTS
    y = None
    for c in range(t // tc):
        route = _route(idx[c * tc:(c + 1) * tc], gates[c * tc:(c + 1) * tc], n_blocks)
        wspec = lambda r, w: pl.BlockSpec((None, None, r, w), lambda i, be, *_: (l, be[i], 0, 0))
        in_specs = [pl.BlockSpec((tc, d), lambda i, *_: (c, 0), pipeline_mode=pl.Buffered(1)),
                    wspec(d, 2 * D_FF), wspec(1, 2 * D_FF), wspec(D_FF, d), wspec(1, d)]
        args = [h2, wts["w_moe_in"], wts["b_moe_in"], wts["w_moe_out"], wts["b_moe_out"]]
        aliases = {}
        if y is not None:
            in_specs.append(pl.BlockSpec(memory_space=pl.ANY))
            args.append(y)
            aliases = {len(route) + len(args) - 1: 0}
        grid_spec = pltpu.PrefetchScalarGridSpec(
            num_scalar_prefetch=len(route),
            grid=(n_blocks,),
            in_specs=in_specs,
            out_specs=pl.BlockSpec((tc, d), lambda i, *_: (c, 0), pipeline_mode=pl.Buffered(1)),
            scratch_shapes=[pltpu.VMEM((ROW_BLOCK // SUB, SUB, d), F32)] * 2,
        )
        y = pl.pallas_call(
            functools.partial(_moe_kernel_entry, y is not None),
            out_shape=jax.ShapeDtypeStruct((t, d), F32),
            grid_spec=grid_spec,
            input_output_aliases=aliases,
            compiler_params=_cp(("arbitrary",), VMEM_LIMIT),
            name="moe_experts",
        )(*route, *args)
    return y


def _final_kernel(x_ref, y_ref, mod_ref, o_ref):
    o_ref[...] = x_ref[...] + mod_ref[5:6, :] * y_ref[...]


def _final_call(l, x, y, mod, seq):
    t, d = x.shape
    tm = 256
    ncond = mod.shape[1]
    row = pl.BlockSpec((tm, d), lambda i: (i, 0))
    return pl.pallas_call(
        _final_kernel,
        out_shape=jax.ShapeDtypeStruct((t, d), F32),
        grid=(t // tm,),
        in_specs=[row, row, pl.BlockSpec((None, None, 6, d),
                                         lambda i: (l, (i * tm) // seq if ncond > 1 else 0, 0, 0))],
        out_specs=row,
        compiler_params=_cp(("parallel",), VMEM_LIMIT),
        name="final_residual",
    )(x, y, mod)


def _rope_tables(seq, rot_dim, lane_map):
    n_rows = seq // GRID_W
    rows = jnp.repeat(jnp.arange(n_rows, dtype=F32), GRID_W)
    cols = jnp.tile(jnp.arange(GRID_W, dtype=F32), n_rows)
    axis_dim = rot_dim // 2
    inv_freq = ROPE_THETA ** (-jnp.arange(0, axis_dim, 2, dtype=F32) / axis_dim)
    ang = jnp.concatenate([rows[:, None] * inv_freq, cols[:, None] * inv_freq], axis=-1)
    cos, sin = jnp.cos(ang), jnp.sin(ang)
    pair = np.zeros((LANE,), np.int32)
    in_rot = np.zeros((LANE,), np.float32)
    first = np.zeros((LANE,), np.float32)
    for ln in range(LANE):
        m = lane_map(ln)
        if m is not None:
            pair[ln], in_rot[ln], first[ln] = m[0], 1.0, 1.0 if m[1] == 0 else 0.0
    c = jnp.where(in_rot[None, :] > 0, cos[:, pair], 1.0)
    s = sin[:, pair] * in_rot[None, :]
    return jnp.stack([c, -s * first[None, :], s * (1.0 - first[None, :])]).astype(F32)


def _lane_map_a(ln):
    o = ln - NOPE_A
    if 0 <= o < ROPE_A:
        return (o % (ROPE_A // 2), o // (ROPE_A // 2))
    return None


def _lane_map_c(ln):
    o = ln % HD_C
    return (o % (HD_C // 2), o // (HD_C // 2))


def _phase(num, den):
    ang = (2.0 * math.pi / den) * (num % den).astype(F32)
    return jnp.cos(ang), jnp.sin(ang)


def _ctx_dft(n):
    f = jnp.arange(n, dtype=jnp.int32)[:, None]
    t = jnp.arange(n, dtype=jnp.int32)[None, :]
    c, s = _phase((2 * f + 1) * t, 4 * n)
    fwd = jnp.concatenate([c, -s], axis=0).astype(BF16)
    inv = jnp.concatenate([c.T, -s.T], axis=1).astype(BF16)
    return c, s, fwd, inv


def _lat_dft(seq):
    n = 2 * seq
    n1, n2 = FFT_N1, n // FFT_N1
    k1 = jnp.arange(n1, dtype=jnp.int32)
    c1, s1 = _phase(k1[:, None] * k1[None, :], n1)
    f1f = jnp.concatenate([c1, -s1], axis=0)
    f1d = f1f[:, :n1 // 2].astype(BF16)
    fc = jnp.concatenate([c1[:, :n1 // 2].T, -s1[:, :n1 // 2].T], axis=1).astype(BF16)
    k2 = jnp.arange(n2, dtype=jnp.int32)
    num = (k2[None, :, None] * k2[None, None, :]) * n1 + k2[None, None, :] * k1[:, None, None]
    cm, sm = _phase(num, n)
    mr, mi = cm, -sm
    f2 = jnp.concatenate([jnp.concatenate([mr, -mi], axis=2), jnp.concatenate([mi, mr], axis=2)], axis=1)
    mrt, mit = jnp.swapaxes(mr, 1, 2), jnp.swapaxes(mi, 1, 2)
    g2 = jnp.concatenate([jnp.concatenate([mrt, mit], axis=2), jnp.concatenate([-mit, mrt], axis=2)], axis=1)
    f2h, f2l = _split(f2)
    return dict(f1f=f1f, f1d=f1d, fc=fc, f2=f2h, f2l=f2l, g2=g2.astype(BF16), n1=n1, n2=n2, n=n)


def _hy_features(seq):
    t = jnp.linspace(0.0, 1.0, seq, dtype=F32)[:, None]
    bands = (HY_EMB - 1) // 2
    f = jnp.linspace(1e-4, bands - 1, bands, dtype=F32)[None, :]
    w = 2.0 * math.pi * jnp.arange(seq, dtype=F32)[:, None] / seq
    z = jnp.concatenate([t, jnp.cos(f * w), jnp.sin(f * w)], axis=-1)
    z = jnp.pad(z, ((0, 0), (0, HY_FO - HY_EMB)))
    deltas = jnp.abs(jnp.linspace(HY_MIN_DECAY, HY_MAX_DECAY, HY_CH, dtype=F32))
    return z, jnp.exp(-t * deltas)


def _prep_weights(w_in, mla_g_qa, mla_w_uq, mla_g_kva, mla_w_ukv, mla_g_q, mla_g_k, gqa_g_q, gqa_g_k,
                  g_out, w_out, w_router, b_router, hy_w_in):
    depth = w_in.shape[0]
    cuts = np.cumsum([0, Q_RANK, KV_RANK, ROPE_A, 3 * HY_CH, H_C * HD_C, KV_C * HD_C, KV_C * HD_C])
    wb = w_in.astype(BF16)
    zeros = lambda w: jnp.zeros(wb.shape[:2] + (w,), BF16)
    parts = [wb[:, :, cuts[0]:cuts[2]], zeros(NOPE_A), wb[:, :, cuts[2]:cuts[3]], zeros(LANE - QK_A),
             wb[:, :, cuts[3]:cuts[4]]]
    for h in range(H_C):
        wh = wb[:, :, cuts[4] + h * HD_C:cuts[4] + (h + 1) * HD_C]
        parts += [wh, zeros(HD_C)] if h < H_C // KV_C else [zeros(HD_C), wh]
    w_in_p = jnp.concatenate(parts + [wb[:, :, cuts[5]:cuts[7]]], axis=-1)

    w_uq = jnp.pad(mla_w_uq.reshape(depth, Q_RANK, H_A, QK_A), ((0, 0), (0, 0), (0, 0), (0, LANE - QK_A)))
    w_uq = w_uq.reshape(depth, Q_RANK, HQ).astype(BF16)
    ukv = mla_w_ukv.reshape(depth, KV_RANK, H_A, NOPE_A + V_A)
    wk = jnp.pad(ukv[..., :NOPE_A], ((0, 0), (0, 0), (0, 0), (0, LANE - NOPE_A))).reshape(depth, KV_RANK, HQ)
    wv = ukv[..., NOPE_A:].reshape(depth, KV_RANK, W_A)
    w_kv = jnp.concatenate([wk, wv], axis=-1).astype(BF16)

    def pad_row(v):
        return jnp.pad(v, ((0, 0), (0, HQ - v.shape[1])))

    head_a = lambda g: jnp.tile(jnp.pad(g, ((0, 0), (0, LANE - QK_A))), (1, H_A))
    head_c = lambda g, reps: jnp.tile(g, (1, reps))
    gains = jnp.stack([
        pad_row(mla_g_qa), pad_row(mla_g_kva),
        head_a(mla_g_q) * (QK_A ** -0.5 * LOG2E), head_a(mla_g_k),
        head_c(gqa_g_q, 2 * H_C) * (HD_C ** -0.5 * LOG2E), pad_row(head_c(gqa_g_k, 2)),
        jnp.zeros((depth, HQ), F32), jnp.zeros((depth, HQ), F32)], axis=1).astype(F32)

    g_per = H_C // KV_C
    perm_c = np.concatenate([np.arange(h * HD_C, (h + 1) * HD_C) for g in range(g_per) for h in (g, g + g_per)])
    rows_c = W_A + W_B + perm_c
    g_o = jnp.concatenate([g_out[:, :W_A + W_B], g_out[:, rows_c]], axis=1).reshape(depth, 1, -1)
    wr = jnp.pad(w_router, ((0, 0), (0, 0), (0, LANE - N_EXPERTS)))
    wr_hi, wr_lo = _split(wr)
    br = jnp.pad(b_router, ((0, 0), (0, LANE - N_EXPERTS)), constant_values=-1e30).reshape(depth, 1, LANE)
    return dict(w_in=w_in_p, w_uq=w_uq, w_kv=w_kv, gains=gains, g_out=g_o,
                wo_a=w_out[:, :W_A].astype(BF16), wo_b=w_out[:, W_A:W_A + W_B].astype(BF16),
                wo_c=w_out[:, rows_c].astype(BF16), wr_hi=wr_hi, wr_lo=wr_lo, b_router=br,
                hy_w_in=jnp.pad(hy_w_in, ((0, 0), (0, HY_FO - HY_EMB), (0, 0))))


def _hyena_lat(l, u, hw, kf, dft, seq):
    t = u.shape[0]
    nb = t // seq
    n1, n2 = dft["n1"], dft["n2"]
    w2 = n2 * HY_CH
    v, x1, x2 = _hconv_call(l, u, hw, seq)
    to2d = lambda a: a.reshape(t // n2, w2)
    s = to2d(v)
    for o, gate in enumerate((x1, x2)):
        a = _ha_call(dft["f1d"], s, nb)
        b = _hb_call(o, a.reshape(nb, 2, n1, n2, HY_CH), dft["f2"], dft["g2"], kf)
        s = _hc_call(l, o, dft["fc"], b.reshape(nb, 2, n1, w2), s, to2d(gate), hw["skip_t"])
    return s.reshape(t, HY_CH)


def _filter_lat(l, z, dec, hw, dft, seq):
    kern, nrm = _filt_lat_call(l, z, dec, hw, seq)
    n1, n2, n = dft["n1"], dft["n2"], dft["n"]
    c = kern.shape[1]
    af = _fa_call(dft["f1f"], kern.reshape(n1, n2 * c))
    return _fb_call(af.reshape(2, n1, n2, c), dft["f2"], dft["f2l"], nrm, n)


def kernel(x_prompt, x_sample, c, c_ctx, cache_mla_ckv, cache_mla_kpe, cache_gqa_k, cache_gqa_v, w_mod, b_mod, w_in, mla_g_qa, mla_w_uq, mla_g_kva, mla_w_ukv, mla_g_q, mla_g_k, hy_conv_w, hy_conv_b, hy_w_in, hy_b_in, hy_w_mid, hy_b_mid, hy_w_out, hy_b_out, hy_freq, hy_skip, gqa_g_q, gqa_g_k, g_out, w_out, w_router, b_router, w_moe_in, b_moe_in, w_moe_out, b_moe_out):
    batch, seq_c, d = x_prompt.shape
    nb_l, seq_l, _ = x_sample.shape
    depth = w_in.shape[0]
    past = cache_mla_ckv.shape[2]
    assert d == D_MODEL and seq_l % GRID_W == 0 and (2 * seq_l) % FFT_N1 == 0

    wts = _prep_weights(w_in, mla_g_qa, mla_w_uq, mla_g_kva, mla_w_ukv, mla_g_q, mla_g_k, gqa_g_q, gqa_g_k,
                        g_out, w_out, w_router, b_router, hy_w_in)
    wts["w_moe_in"] = _cast_call(w_moe_in.reshape(depth * N_EXPERTS, d, 2 * D_FF), 512).reshape(w_moe_in.shape)
    wts["w_moe_out"] = _cast_call(w_moe_out.reshape(depth * N_EXPERTS, D_FF, d), 512).reshape(w_moe_out.shape)
    wts["b_moe_in"] = b_moe_in.reshape(depth, N_EXPERTS, 1, 2 * D_FF)
    wts["b_moe_out"] = b_moe_out.reshape(depth, N_EXPERTS, 1, d)
    hw = dict(hy_w_in=wts["hy_w_in"], hy_b_in=hy_b_in.reshape(depth, 1, HY_FO), hy_w_mid=hy_w_mid,
              hy_b_mid=hy_b_mid.reshape(depth, HY_INNER, 1, HY_FO), hy_w_out=hy_w_out,
              hy_b_out=hy_b_out.reshape(depth, 1, -1), hy_freq=hy_freq.reshape(depth, 1, HY_FO),
              hy_conv_w=hy_conv_w, hy_conv_b=hy_conv_b.reshape(depth, 1, -1), hy_skip=hy_skip)
    dft_l = _lat_dft(seq_l)
    hw["skip_t"] = jnp.tile(hy_skip, (1, 1, dft_l["n2"]))

    conds = jnp.zeros((8, d), F32).at[0].set(c_ctx).at[1:1 + nb_l].set(c)
    mod = _mod_call(conds, w_mod, b_mod).reshape(depth, 8, 6, d)
    mod_c, mod_l = mod[:, 0:1], mod[:, 1:1 + nb_l]

    rope_tabs = (_rope_tables(seq_l, ROPE_A, _lane_map_a), _rope_tables(seq_l, HD_C, _lane_map_c))
    kax, vax = _cachekv_call(cache_mla_ckv, jnp.pad(cache_mla_kpe, ((0, 0), (0, 0), (0, 0), (NOPE_A, LANE - QK_A))), wts)
    kcx = cache_gqa_k.reshape(nb_l, depth, past, KV_C * HD_C).astype(BF16)
    vcx = cache_gqa_v.reshape(nb_l, depth, past, KV_C * HD_C).astype(BF16)
    vcx = jnp.concatenate([vcx, jnp.ones_like(vcx)], axis=-1)

    z_c, dec_c = _hy_features(seq_c)
    cmat, smat, fwd_c, inv_c = _ctx_dft(seq_c)
    z_l, dec_l = _hy_features(seq_l)
    z_full = jnp.concatenate([z_l, jnp.zeros((1, HY_FO), F32), z_l[:0:-1]], axis=0)
    dec_full = jnp.concatenate([dec_l, jnp.zeros((1, HY_CH), F32), dec_l[:0:-1]], axis=0)

    xc = x_prompt.reshape(batch * seq_c, d)
    xl = x_sample.reshape(nb_l * seq_l, d)
    yc = yl = None
    new_ckv, new_kpe, new_k, new_v = [], [], [], []
    for l in range(depth):
        (xc, qa, ka, va, ckv, kpe, qc, kc, vc, kcf, vcf, u) = _premix_call(l, xc, yc, mod_c, seq_c, False, wts, None)
        new_ckv.append(ckv)
        new_kpe.append(kpe[:, NOPE_A:QK_A])
        new_k.append(kcf)
        new_v.append(vcf)
        oa, oc = _attn_ctx_call(qa, ka, va, qc, kc, vc, seq_c)
        kf_c = _filt_ctx_call(l, z_c, dec_c, hw, cmat, smat)
        ob = _hy_ctx_call(l, u, hw, kf_c, fwd_c, inv_c, seq_c)
        xc, h2, idx, gates = _postmix_call(l, xc, oa, ob, oc, mod_c, seq_c, wts)
        yc = _moe_call(l, h2, idx[:, :TOP_K], gates[:, :TOP_K], wts)
        (xl, qa, ka, va, _, _, qc, kc, vc, _, _, u) = _premix_call(l, xl, yl, mod_l, seq_l, True, wts, rope_tabs)
        oa, oc = _attn_lat_call(l, qa, ka, va, qc, kc, vc, kax, vax, kcx, vcx, seq_l)
        kf_l = _filter_lat(l, z_full, dec_full, hw, dft_l, seq_l)
        ob = _hyena_lat(l, u, hw, kf_l, dft_l, seq_l)
        xl, h2, idx, gates = _postmix_call(l, xl, oa, ob, oc, mod_l, seq_l, wts)
        yl = _moe_call(l, h2, idx[:, :TOP_K], gates[:, :TOP_K], wts)
    y_prompt = _final_call(depth - 1, xc, yc, mod_c, seq_c).reshape(batch, seq_c, d)
    y_sample = _final_call(depth - 1, xl, yl, mod_l, seq_l).reshape(nb_l, seq_l, d)
    stack = lambda xs, tail: jnp.stack([a.reshape((batch, seq_c) + tail) for a in xs], axis=1)
    return (y_prompt, y_sample, stack(new_ckv, (KV_RANK,)), stack(new_kpe, (ROPE_A,)),
            stack(new_k, (KV_C, HD_C)), stack(new_v, (KV_C, HD_C)))
```

```python
import functools
import math

import numpy as np
import jax
import jax.numpy as jnp
from jax import lax
from jax.experimental import pallas as pl
from jax.experimental.pallas import tpu as pltpu

F32 = jnp.float32
BF16 = jnp.bfloat16

D_MODEL = 1024
GRID_W = 64
EPS = 1e-6
ROPE_THETA = 10000.0
H_A = 6
Q_RANK = 256
KV_RANK = 128
NOPE_A = 64
ROPE_A = 32
V_A = 64
QK_A = NOPE_A + ROPE_A
HY_CH = 256
HY_ORDER = 2
HY_EMB = 33
HY_FO = 64
HY_INNER = 2
HY_MIN_DECAY = math.log(1e-2) / 1.5
HY_MAX_DECAY = math.log(1e-2) / 0.3
H_C = 6
KV_C = 2
HD_C = 64
N_EXPERTS = 32
TOP_K = 4
D_FF = 1024
SWIGLU_LIMIT = 7.0
SWIGLU_ALPHA = 1.702
LOG2E = 1.4426950408889634

LANE = 128
HQ = H_A * LANE
W_A = H_A * V_A
W_B = HY_CH
W_C = H_C * HD_C
IN_P = Q_RANK + KV_RANK + LANE + 3 * HY_CH + HQ + 2 * LANE
ROW_BLOCK = 128
TOKEN_TILE = 512
VMEM_LIMIT = 56 * 1024 * 1024
FFT_N1 = 128
FFT_N2 = 64


def _cp(sem, vmem=None):
    return pltpu.CompilerParams(dimension_semantics=sem, vmem_limit_bytes=vmem)


def _dot(a, b):
    return jnp.dot(a, b, preferred_element_type=F32)


def _split(a):
    hi = a.astype(BF16)
    return hi, (a - hi.astype(F32)).astype(BF16)


def _dot3(a, b):
    ah, al = _split(a)
    bh, bl = _split(b)
    return _dot(ah, bh) + _dot(ah, bl) + _dot(al, bh)


def _bdot(a, b):
    return lax.dot_general(a, b, (((2,), (1,)), ((0,), (0,))), preferred_element_type=F32)


def _rms(x, n=None):
    ss = jnp.sum(x * x, axis=-1, keepdims=True) * (1.0 / (n or x.shape[-1]))
    return x * lax.rsqrt(ss + EPS)


def _head_norm(x, nvalid):
    outs = []
    for h in range(x.shape[1] // LANE):
        blk = x[:, h * LANE:(h + 1) * LANE]
        ss = jnp.sum(blk * blk, axis=-1, keepdims=True) * (1.0 / nvalid)
        outs.append(blk * lax.rsqrt(ss + EPS))
    return outs[0] if len(outs) == 1 else jnp.concatenate(outs, axis=1)


def _rope(x, tab_ref, shift):
    c, sm, sp = tab_ref[0], tab_ref[1], tab_ref[2]
    outs = []
    for h in range(x.shape[1] // LANE):
        blk = x[:, h * LANE:(h + 1) * LANE]
        outs.append(blk * c + pltpu.roll(blk, LANE - shift, 1) * sm + pltpu.roll(blk, shift, 1) * sp)
    return outs[0] if len(outs) == 1 else jnp.concatenate(outs, axis=1)


def _mod_kernel(c_ref, w_ref, b_ref, o_ref):
    c = c_ref[...]
    s = c * (1.0 / (1.0 + jnp.exp(-c)))
    o_ref[...] = _dot3(s, w_ref[...]) + b_ref[...]


def _mod_call(conds, w_mod, b_mod):
    depth, d, n6 = w_mod.shape
    tn = 1536
    return pl.pallas_call(
        _mod_kernel,
        out_shape=jax.ShapeDtypeStruct((depth, 8, n6), F32),
        grid=(depth, n6 // tn),
        in_specs=[pl.BlockSpec((8, d), lambda l, j: (0, 0)),
                  pl.BlockSpec((None, d, tn), lambda l, j: (l, 0, j)),
                  pl.BlockSpec((None, 1, tn), lambda l, j: (l, 0, j))],
        out_specs=pl.BlockSpec((None, 8, tn), lambda l, j: (l, 0, j)),
        compiler_params=_cp(("parallel", "parallel"), VMEM_LIMIT),
        name="mod",
    )(conds, w_mod, b_mod.reshape(depth, 1, n6))


def _cast_kernel(x_ref, o_ref):
    o_ref[...] = x_ref[...].astype(BF16)


def _cast_call(w, tr):
    n, r, c = w.shape
    return pl.pallas_call(
        _cast_kernel,
        out_shape=jax.ShapeDtypeStruct(w.shape, BF16),
        grid=(n, r // tr),
        in_specs=[pl.BlockSpec((None, tr, c), lambda i, j: (i, j, 0))],
        out_specs=pl.BlockSpec((None, tr, c), lambda i, j: (i, j, 0)),
        compiler_params=_cp(("parallel", "parallel"), VMEM_LIMIT),
        name="cast_bf16",
    )(w)


def _kv_heads(ckv_bf, kpe, w_kv_ref, gk, rope_ref, use_rope):
    kvp = _dot(ckv_bf, w_kv_ref[...])
    kn = kvp[:, :HQ]
    ka = jnp.concatenate([kn[:, h * LANE:(h + 1) * LANE] + kpe for h in range(H_A)], axis=1)
    ka = _head_norm(ka, QK_A) * gk
    if use_rope:
        ka = _rope(ka, rope_ref, ROPE_A // 2)
    return ka, kvp[:, HQ:]


def _premix_kernel(has_prev, use_rope, *refs):
    it = iter(refs)
    x_ref = next(it)
    if has_prev:
        y_ref, modp_ref = next(it), next(it)
    mod_ref, w_in_ref, w_uq_ref, w_kv_ref, g_ref = next(it), next(it), next(it), next(it), next(it)
    if use_rope:
        ra_ref, rc_ref = next(it), next(it)
    else:
        ra_ref = rc_ref = None
    (xo_ref, qa_ref, ka_ref, va_ref, ckv_ref, kpe_ref, qc_ref, kc_ref, vc_ref, kcf_ref, vcf_ref, u_ref) = it

    x = x_ref[...]
    if has_prev:
        x = x + modp_ref[5:6, :] * y_ref[...]
    xo_ref[...] = x
    h = _rms(x) * (1.0 + mod_ref[1:2, :]) + mod_ref[0:1, :]
    proj = _dot(h.astype(BF16), w_in_ref[...])
    o = 0
    c_q = proj[:, o:o + Q_RANK]; o += Q_RANK
    c_kv = proj[:, o:o + KV_RANK]; o += KV_RANK
    kpe = proj[:, o:o + LANE]; o += LANE
    u_ref[...] = proj[:, o:o + 3 * HY_CH]; o += 3 * HY_CH
    q_c = proj[:, o:o + HQ]; o += HQ
    k_c = proj[:, o:o + LANE]; o += LANE
    v_c = proj[:, o:o + LANE]

    cqn = _rms(c_q) * g_ref[0:1, :Q_RANK]
    qa = _head_norm(_dot(cqn.astype(BF16), w_uq_ref[...]), QK_A) * g_ref[2:3, :]
    if use_rope:
        qa = _rope(qa, ra_ref, ROPE_A // 2)
    qa_ref[...] = qa.astype(BF16)
    ckv = _rms(c_kv) * g_ref[1:2, :KV_RANK]
    ckv_ref[...] = ckv
    kpe_ref[...] = kpe
    ka, va = _kv_heads(ckv.astype(BF16), kpe, w_kv_ref, g_ref[3:4, :], ra_ref, use_rope)
    ka_ref[...] = ka.astype(BF16)
    va_ref[...] = _with_ones(va.astype(BF16))

    qc = _head_norm(q_c, HD_C) * g_ref[4:5, :]
    if use_rope:
        qc = _rope(qc, rc_ref, HD_C // 2)
    qc_ref[...] = qc.astype(BF16)
    lane = lax.broadcasted_iota(jnp.int32, k_c.shape, 1)
    k2 = k_c * k_c
    s0 = jnp.sum(jnp.where(lane < HD_C, k2, 0.0), axis=-1, keepdims=True) * (1.0 / HD_C)
    s1 = jnp.sum(jnp.where(lane >= HD_C, k2, 0.0), axis=-1, keepdims=True) * (1.0 / HD_C)
    kcn = k_c * jnp.where(lane < HD_C, lax.rsqrt(s0 + EPS), lax.rsqrt(s1 + EPS)) * g_ref[5:6, :LANE]
    kcf_ref[...] = kcn
    vcf_ref[...] = v_c
    kc_ref[...] = (_rope(kcn, rc_ref, HD_C // 2) if use_rope else kcn).astype(BF16)
    vc_ref[...] = _with_ones(v_c.astype(BF16))


def _premix_call(l, x, yprev, mod, seq, use_rope, wts, rope_tabs):
    t, d = x.shape
    tm = TOKEN_TILE
    ncond = mod.shape[1]
    has_prev = yprev is not None

    def cond(i):
        return (i * tm) // seq if ncond > 1 else 0

    row = lambda w: pl.BlockSpec((tm, w), lambda i: (i, 0))
    ins, specs = [x], [row(d)]
    if has_prev:
        ins += [yprev, mod]
        specs += [row(d), pl.BlockSpec((None, None, 6, d), lambda i: (l - 1, cond(i), 0, 0))]
    ins += [mod, wts["w_in"], wts["w_uq"], wts["w_kv"], wts["gains"]]
    specs += [pl.BlockSpec((None, None, 6, d), lambda i: (l, cond(i), 0, 0)),
              pl.BlockSpec((None, d, IN_P), lambda i: (l, 0, 0)),
              pl.BlockSpec((None, Q_RANK, HQ), lambda i: (l, 0, 0)),
              pl.BlockSpec((None, KV_RANK, HQ + W_A), lambda i: (l, 0, 0)),
              pl.BlockSpec((None, 8, HQ), lambda i: (l, 0, 0))]
    if use_rope:
        nt = seq // tm
        ins += [rope_tabs[0], rope_tabs[1]]
        specs += [pl.BlockSpec((3, tm, LANE), lambda i: (0, i % nt, 0))] * 2
    outs = [(d, F32), (HQ, BF16), (HQ, BF16), (2 * W_A, BF16), (KV_RANK, F32), (LANE, F32), (HQ, BF16),
            (LANE, BF16), (2 * LANE, BF16), (LANE, F32), (LANE, F32), (3 * HY_CH, F32)]
    return pl.pallas_call(
        functools.partial(_premix_kernel, has_prev, use_rope),
        out_shape=[jax.ShapeDtypeStruct((t, w), dt) for w, dt in outs],
        grid=(t // tm,),
        in_specs=specs,
        out_specs=[row(w) for w, _ in outs],
        compiler_params=_cp(("parallel",), VMEM_LIMIT),
        name="premix",
    )(*ins)


def _cachekv_kernel(ckv_ref, kpe_ref, w_kv_ref, g_ref, ka_ref, va_ref):
    ka, va = _kv_heads(ckv_ref[...].astype(BF16), kpe_ref[...], w_kv_ref, g_ref[3:4, :], None, False)
    ka_ref[...] = ka.astype(BF16)
    va_ref[...] = _with_ones(va.astype(BF16))


def _cachekv_call(cache_ckv, cache_kpe_p, wts):
    nb, depth, past, _ = cache_ckv.shape
    return pl.pallas_call(
        _cachekv_kernel,
        out_shape=[jax.ShapeDtypeStruct((depth, nb, past, HQ), BF16),
                   jax.ShapeDtypeStruct((depth, nb, past, 2 * W_A), BF16)],
        grid=(depth, nb),
        in_specs=[pl.BlockSpec((None, None, past, KV_RANK), lambda l, b: (b, l, 0, 0)),
                  pl.BlockSpec((None, None, past, LANE), lambda l, b: (b, l, 0, 0)),
                  pl.BlockSpec((None, KV_RANK, HQ + W_A), lambda l, b: (l, 0, 0)),
                  pl.BlockSpec((None, 8, HQ), lambda l, b: (l, 0, 0))],
        out_specs=[pl.BlockSpec((None, None, past, HQ), lambda l, b: (l, b, 0, 0)),
                   pl.BlockSpec((None, None, past, 2 * W_A), lambda l, b: (l, b, 0, 0))],
        compiler_params=_cp(("parallel", "parallel"), VMEM_LIMIT),
        name="cache_kv",
    )(cache_ckv, cache_kpe_p, wts["w_kv"], wts["gains"])


def _nt(q, k):
    return lax.dot_general(q, k, (((1,), (1,)), ((), ())), preferred_element_type=F32)


def _with_ones(v):
    ones = jnp.ones((v.shape[0], LANE), v.dtype)
    parts = []
    for j in range(v.shape[1] // LANE):
        parts += [v[:, j * LANE:(j + 1) * LANE], ones]
    return jnp.concatenate(parts, axis=1)


def _attend(q, ks, vs):
    return _softmax_pv([_nt(q, k) for k in ks], vs)


def _softmax_pv(ss, vs):
    m = ss[0].max(axis=-1, keepdims=True)
    for s in ss[1:]:
        m = jnp.maximum(m, s.max(axis=-1, keepdims=True))
    acc = None
    for s, v in zip(ss, vs):
        pv = _dot(jnp.exp2(s - m).astype(BF16), v)
        acc = pv if acc is None else acc + pv
    return acc[:, :LANE] / acc[:, LANE:]


def _attn_kernel(nseg, *refs):
    qa_ref, qc_ref = refs[0], refs[1]
    segs = refs[2:2 + 4 * nseg]
    oa_ref, oc_ref = refs[2 + 4 * nseg:]
    ka_refs, va_refs, kc_refs, vc_refs = (segs[i::4] for i in range(4))
    lane = lax.broadcasted_iota(jnp.int32, (qa_ref.shape[0], LANE), 1)
    low = lane < V_A
    g_per = H_C // KV_C
    jobs = [("a", j, h) for j in range(H_A // 2) for h in (2 * j, 2 * j + 1)]
    jobs += [("c", g, h) for g in range(g_per) for h in (g, g + g_per)]

    def scores(job):
        kind, _, h = job
        hs = slice(h * LANE, (h + 1) * LANE)
        if kind == "a":
            return [_nt(qa_ref[:, hs], r[:, hs]) for r in ka_refs]
        return [_nt(qc_ref[:, hs], r[...]) for r in kc_refs]

    def values(job):
        kind, j, _ = job
        if kind == "a":
            return [r[:, 2 * j * LANE:2 * (j + 1) * LANE] for r in va_refs]
        return [r[...] for r in vc_refs]

    ss_next = scores(jobs[0])
    pv = []
    for n, job in enumerate(jobs):
        ss = ss_next
        if n + 1 < len(jobs):
            ss_next = scores(jobs[n + 1])
        pv.append(_softmax_pv(ss, values(job)))
        if len(pv) == 2:
            o_ref = oa_ref if job[0] == "a" else oc_ref
            o_ref[:, job[1] * LANE:(job[1] + 1) * LANE] = jnp.where(low, pv[0], pv[1])
            pv = []


def _attn_ctx_call(qa, ka, va, qc, kc, vc, seq):
    t = qa.shape[0]
    blk = lambda w: pl.BlockSpec((seq, w), lambda b: (b, 0))
    return pl.pallas_call(
        functools.partial(_attn_kernel, 1),
        out_shape=[jax.ShapeDtypeStruct((t, W_A), F32), jax.ShapeDtypeStruct((t, W_C), F32)],
        grid=(t // seq,),
        in_specs=[blk(HQ), blk(HQ), blk(HQ), blk(2 * W_A), blk(LANE), blk(2 * LANE)],
        out_specs=[blk(W_A), blk(W_C)],
        compiler_params=_cp(("parallel",), VMEM_LIMIT),
        name="attn_ctx",
    )(qa, qc, ka, va, kc, vc)


def _attn_lat_call(l, qa, ka, va, qc, kc, vc, kax, vax, kcx, vcx, seq):
    t = qa.shape[0]
    nb = t // seq
    tq = min(256, seq)
    nq = seq // tq
    past = kax.shape[2]
    qblk = lambda w: pl.BlockSpec((tq, w), lambda b, i: (b * nq + i, 0))
    sblk = lambda w: pl.BlockSpec((seq, w), lambda b, i: (b, 0))
    xblk = lambda w: pl.BlockSpec((None, None, past, w), lambda b, i: (l, b, 0, 0))
    cblk = lambda w: pl.BlockSpec((None, None, past, w), lambda b, i: (b, l, 0, 0))
    return pl.pallas_call(
        functools.partial(_attn_kernel, 2),
        out_shape=[jax.ShapeDtypeStruct((t, W_A), F32), jax.ShapeDtypeStruct((t, W_C), F32)],
        grid=(nb, nq),
        in_specs=[qblk(HQ), qblk(HQ),
                  xblk(HQ), xblk(2 * W_A), cblk(LANE), cblk(2 * LANE),
                  sblk(HQ), sblk(2 * W_A), sblk(LANE), sblk(2 * LANE)],
        out_specs=[qblk(W_A), qblk(W_C)],
        compiler_params=_cp(("parallel", "parallel"), VMEM_LIMIT),
        name="attn_lat",
    )(qa, qc, kax, vax, kcx, vcx, ka, va, kc, vc)


def _filter_mlp(z, w_in_ref, b_in_ref, w_mid_ref, b_mid_ref, w_out_ref, b_out_ref, freq_ref):
    freq = freq_ref[...]
    a = jnp.sin(freq * (_dot3(z, w_in_ref[...]) + b_in_ref[...]))
    for i in range(HY_INNER):
        a = jnp.sin(freq * (_dot3(a, w_mid_ref[i]) + b_mid_ref[i]))
    return _dot3(a, w_out_ref[...]) + b_out_ref[...]


def _conv3(u, up, dn, cw_ref, cb_ref):
    return up * cw_ref[0:1, :] + u * cw_ref[1:2, :] + dn * cw_ref[2:3, :] + cb_ref[...]


def _filt_ctx_kernel(z_ref, dec_ref, w_in_ref, b_in_ref, w_mid_ref, b_mid_ref, w_out_ref, b_out_ref, freq_ref,
                     c_ref, s_ref, o_ref):
    n = z_ref.shape[0]
    h = _filter_mlp(z_ref[...], w_in_ref, b_in_ref, w_mid_ref, b_mid_ref, w_out_ref, b_out_ref, freq_ref)
    dec = dec_ref[...]
    row = lax.broadcasted_iota(jnp.int32, dec.shape, 0)
    half = HY_ORDER * HY_CH
    for o in range(HY_ORDER):
        hf = h[:, o * HY_CH:(o + 1) * HY_CH] * dec
        hb = jnp.where(row > 0, h[:, half + o * HY_CH:half + (o + 1) * HY_CH] * dec, 0.0)
        nrm = jnp.sum(jnp.abs(hf) + jnp.abs(hb), axis=0, keepdims=True) + EPS
        scale = (1.0 / n) / nrm
        o_ref[o, 0] = _dot3(c_ref[...], hf + hb) * scale
        o_ref[o, 1] = -_dot3(s_ref[...], hf - hb) * scale


def _filt_ctx_call(l, z, dec, hw, cmat, smat):
    n = z.shape[0]
    full = lambda a: pl.BlockSpec((None,) + a.shape[1:], lambda i: (l,) + (0,) * (a.ndim - 1))
    const = lambda a: pl.BlockSpec(a.shape, lambda i: (0,) * a.ndim)
    names = ["hy_w_in", "hy_b_in", "hy_w_mid", "hy_b_mid", "hy_w_out", "hy_b_out", "hy_freq"]
    return pl.pallas_call(
        _filt_ctx_kernel,
        out_shape=jax.ShapeDtypeStruct((HY_ORDER, 2, n, HY_CH), F32),
        grid=(1,),
        in_specs=[const(z), const(dec)] + [full(hw[k]) for k in names] + [const(cmat), const(smat)],
        out_specs=pl.BlockSpec((HY_ORDER, 2, n, HY_CH), lambda i: (0, 0, 0, 0)),
        compiler_params=_cp(("arbitrary",), VMEM_LIMIT),
        name="hy_filter_ctx",
    )(z, dec, *[hw[k] for k in names], cmat, smat)


def _hy_ctx_kernel(u_ref, cw_ref, cb_ref, skip_ref, kf_ref, fwd_ref, inv_ref, o_ref):
    u = u_ref[...]
    n = u.shape[0]
    row = lax.broadcasted_iota(jnp.int32, u.shape, 0)
    up = jnp.where(row > 0, pltpu.roll(u, 1, 0), 0.0)
    dn = jnp.where(row < n - 1, pltpu.roll(u, n - 1, 0), 0.0)
    z = _conv3(u, up, dn, cw_ref, cb_ref)
    s = z[:, :HY_CH]
    gates = (z[:, HY_CH:2 * HY_CH], z[:, 2 * HY_CH:])
    for o in range(HY_ORDER):
        xs = _dot(fwd_ref[...], s.astype(BF16))
        xr, xi = xs[:n], xs[n:]
        kr, ki = kf_ref[o, 0], kf_ref[o, 1]
        ycat = jnp.concatenate([xr * kr - xi * ki, xr * ki + xi * kr], axis=0)
        y = _dot(inv_ref[...], ycat.astype(BF16))
        s = gates[o] * (y + s * skip_ref[o:o + 1, :])
    o_ref[...] = s


def _hy_ctx_call(l, u, hw, kf, fwd, inv, seq):
    t = u.shape[0]
    full = lambda a: pl.BlockSpec((None,) + a.shape[1:], lambda b: (l,) + (0,) * (a.ndim - 1))
    const = lambda a: pl.BlockSpec(a.shape, lambda b: (0,) * a.ndim)
    return pl.pallas_call(
        _hy_ctx_kernel,
        out_shape=jax.ShapeDtypeStruct((t, HY_CH), F32),
        grid=(t // seq,),
        in_specs=[pl.BlockSpec((seq, 3 * HY_CH), lambda b: (b, 0)),
                  full(hw["hy_conv_w"]), full(hw["hy_conv_b"]), full(hw["hy_skip"]),
                  const(kf), const(fwd), const(inv)],
        out_specs=pl.BlockSpec((seq, HY_CH), lambda b: (b, 0)),
        compiler_params=_cp(("parallel",), VMEM_LIMIT),
        name="hyena_ctx",
    )(u, hw["hy_conv_w"], hw["hy_conv_b"], hw["hy_skip"], kf, fwd, inv)


def _filt_lat_kernel(seq, z_ref, dec_ref, w_in_ref, b_in_ref, w_mid_ref, b_mid_ref, w_out_ref, b_out_ref, freq_ref,
                     k_ref, n_ref):
    i = pl.program_id(0)
    tr = z_ref.shape[0]
    h = _filter_mlp(z_ref[...], w_in_ref, b_in_ref, w_mid_ref, b_mid_ref, w_out_ref, b_out_ref, freq_ref)
    half = HY_ORDER * HY_CH
    row = i * tr + lax.broadcasted_iota(jnp.int32, (tr, half), 0)
    dec = dec_ref[...]
    kern = jnp.where(row < seq, h[:, :half], h[:, half:]) * jnp.concatenate([dec] * HY_ORDER, axis=1)
    k_ref[...] = kern

    @pl.when(i == 0)
    def _():
        n_ref[...] = jnp.zeros_like(n_ref)

    n_ref[...] += jnp.sum(jnp.abs(kern), axis=0, keepdims=True)


def _filt_lat_call(l, z, dec, hw, seq):
    n = z.shape[0]
    tr = min(512, n)
    half = HY_ORDER * HY_CH
    full = lambda a: pl.BlockSpec((None,) + a.shape[1:], lambda i: (l,) + (0,) * (a.ndim - 1))
    names = ["hy_w_in", "hy_b_in", "hy_w_mid", "hy_b_mid", "hy_w_out", "hy_b_out", "hy_freq"]
    return pl.pallas_call(
        functools.partial(_filt_lat_kernel, seq),
        out_shape=[jax.ShapeDtypeStruct((n, half), F32), jax.ShapeDtypeStruct((1, half), F32)],
        grid=(n // tr,),
        in_specs=[pl.BlockSpec((tr, z.shape[1]), lambda i: (i, 0)), pl.BlockSpec((tr, HY_CH), lambda i: (i, 0))]
        + [full(hw[k]) for k in names],
        out_specs=[pl.BlockSpec((tr, half), lambda i: (i, 0)), pl.BlockSpec((1, half), lambda i: (0, 0))],
        compiler_params=_cp(("arbitrary",), VMEM_LIMIT),
        name="hy_filter_lat",
    )(z, dec, *[hw[k] for k in names])


def _fa_kernel(f1_ref, k_ref, o_ref):
    r = _dot3(f1_ref[...], k_ref[...])
    o_ref[0] = r[:FFT_N1]
    o_ref[1] = r[FFT_N1:]


def _fa_call(f1f, kern2d):
    n1, w = kern2d.shape
    tn = min(2048, w)
    return pl.pallas_call(
        _fa_kernel,
        out_shape=jax.ShapeDtypeStruct((2, FFT_N1, w), F32),
        grid=(w // tn,),
        in_specs=[pl.BlockSpec(f1f.shape, lambda j: (0, 0)), pl.BlockSpec((n1, tn), lambda j: (0, j))],
        out_specs=pl.BlockSpec((2, FFT_N1, tn), lambda j: (0, 0, j)),
        compiler_params=_cp(("parallel",), VMEM_LIMIT),
        name="hy_filter_dft1",
    )(f1f, kern2d)


def _fb_kernel(n_total, a_ref, fh_ref, fl_ref, n_ref, o_ref):
    a = jnp.concatenate([a_ref[0], a_ref[1]], axis=1)
    ah, al = _split(a)
    x = _bdot(fh_ref[...], ah) + _bdot(fh_ref[...], al) + _bdot(fl_ref[...], ah)
    scale = (1.0 / n_total) / (n_ref[...] + EPS)
    o_ref[...] = x * scale[None]


def _fb_call(af5, f2h, f2l, nrm, n_total):
    _, n1, n2, c = af5.shape[0], af5.shape[1], af5.shape[2], af5.shape[3]
    k1t = 8
    return pl.pallas_call(
        functools.partial(_fb_kernel, n_total),
        out_shape=jax.ShapeDtypeStruct((n1, 2 * n2, c), F32),
        grid=(n1 // k1t,),
        in_specs=[pl.BlockSpec((2, k1t, n2, c), lambda j: (0, j, 0, 0)),
                  pl.BlockSpec((k1t, 2 * n2, 2 * n2), lambda j: (j, 0, 0)),
                  pl.BlockSpec((k1t, 2 * n2, 2 * n2), lambda j: (j, 0, 0)),
                  pl.BlockSpec((1, c), lambda j: (0, 0))],
        out_specs=pl.BlockSpec((k1t, 2 * n2, c), lambda j: (j, 0, 0)),
        compiler_params=_cp(("parallel",), VMEM_LIMIT),
        name="hy_filter_dft2",
    )(af5, f2h, f2l, nrm)


def _hconv_kernel(seq, u_ref, p_ref, n_ref, cw_ref, cb_ref, v_ref, x1_ref, x2_ref):
    i = pl.program_id(0)
    u = u_ref[...]
    tt = u.shape[0]
    row = lax.broadcasted_iota(jnp.int32, u.shape, 0)
    pos = (i * tt) % seq
    prev = jnp.where(pos > 0, p_ref[7:8, :], 0.0)
    nxt = jnp.where(pos + tt < seq, n_ref[0:1, :], 0.0)
    up = jnp.where(row > 0, pltpu.roll(u, 1, 0), prev)
    dn = jnp.where(row < tt - 1, pltpu.roll(u, tt - 1, 0), nxt)
    z = _conv3(u, up, dn, cw_ref, cb_ref)
    v_ref[...] = z[:, :HY_CH]
    x1_ref[...] = z[:, HY_CH:2 * HY_CH]
    x2_ref[...] = z[:, 2 * HY_CH:]


def _hconv_call(l, u, hw, seq):
    t, w = u.shape
    tt = min(512, seq)
    nblk8 = t // 8
    full = lambda a: pl.BlockSpec((None,) + a.shape[1:], lambda i: (l,) + (0,) * (a.ndim - 1))
    ob = pl.BlockSpec((tt, HY_CH), lambda i: (i, 0))
    return pl.pallas_call(
        functools.partial(_hconv_kernel, seq),
        out_shape=[jax.ShapeDtypeStruct((t, HY_CH), F32)] * 3,
        grid=(t // tt,),
        in_specs=[pl.BlockSpec((tt, w), lambda i: (i, 0)),
                  pl.BlockSpec((8, w), lambda i: (jnp.maximum(i * (tt // 8) - 1, 0), 0)),
                  pl.BlockSpec((8, w), lambda i: (jnp.minimum((i + 1) * (tt // 8), nblk8 - 1), 0)),
                  full(hw["hy_conv_w"]), full(hw["hy_conv_b"])],
        out_specs=[ob, ob, ob],
        compiler_params=_cp(("parallel",), VMEM_LIMIT),
        name="hyena_conv3",
    )(u, u, u, hw["hy_conv_w"], hw["hy_conv_b"])


def _ha_kernel(f1_ref, x_ref, o_ref):
    r = _dot(f1_ref[...], x_ref[...].astype(BF16))
    o_ref[0] = r[:FFT_N1].astype(BF16)
    o_ref[1] = r[FFT_N1:].astype(BF16)


def _ha_call(f1d, x2d, nb):
    rows, w = x2d.shape
    n1h = rows // nb
    tn = min(2048, w)
    return pl.pallas_call(
        _ha_kernel,
        out_shape=jax.ShapeDtypeStruct((nb, 2, FFT_N1, w), BF16),
        grid=(nb, w // tn),
        in_specs=[pl.BlockSpec(f1d.shape, lambda b, j: (0, 0)), pl.BlockSpec((n1h, tn), lambda b, j: (b, j))],
        out_specs=pl.BlockSpec((None, 2, FFT_N1, tn), lambda b, j: (b, 0, 0, j)),
        compiler_params=_cp(("parallel", "parallel"), VMEM_LIMIT),
        name="hyena_dft1",
    )(f1d, x2d)


def _hb_kernel(a_ref, f_ref, g_ref, kf_ref, o_ref):
    n2 = a_ref.shape[2]
    a = jnp.concatenate([a_ref[0], a_ref[1]], axis=1)
    x = _bdot(f_ref[...], a)
    xr, xi = x[:, :n2], x[:, n2:]
    kr, ki = kf_ref[:, :n2], kf_ref[:, n2:]
    y = jnp.concatenate([xr * kr - xi * ki, xr * ki + xi * kr], axis=1).astype(BF16)
    b = _bdot(g_ref[...], y)
    o_ref[0] = b[:, :n2].astype(BF16)
    o_ref[1] = b[:, n2:].astype(BF16)


def _hb_call(o, a5, f2, g2, kf):
    nb, _, n1, n2, c = a5.shape
    k1t = 16
    blk = pl.BlockSpec((None, 2, k1t, n2, c), lambda b, j: (b, 0, j, 0, 0))
    mat = pl.BlockSpec((k1t, 2 * n2, 2 * n2), lambda b, j: (j, 0, 0))
    return pl.pallas_call(
        _hb_kernel,
        out_shape=jax.ShapeDtypeStruct(a5.shape, BF16),
        grid=(nb, n1 // k1t),
        in_specs=[blk, mat, mat, pl.BlockSpec((k1t, 2 * n2, c), lambda b, j: (j, 0, o))],
        out_specs=blk,
        compiler_params=_cp(("parallel", "parallel"), VMEM_LIMIT),
        name="hyena_dft2",
    )(a5, f2, g2, kf)


def _hc_kernel(o, fc_ref, b_ref, s_ref, g_ref, skip_ref, o_ref):
    bcat = jnp.concatenate([b_ref[0], b_ref[1]], axis=0)
    y = _dot(fc_ref[...], bcat)
    o_ref[...] = g_ref[...] * (y + s_ref[...] * skip_ref[o:o + 1, :])


def _hc_call(l, o, fc, b4, s2d, g2d, skip_t):
    nb, _, n1, w = b4.shape
    rows = s2d.shape[0] // nb
    tn = min(2048, w)
    blk = pl.BlockSpec((rows, tn), lambda b, j: (b, j))
    return pl.pallas_call(
        functools.partial(_hc_kernel, o),
        out_shape=jax.ShapeDtypeStruct(s2d.shape, F32),
        grid=(nb, w // tn),
        in_specs=[pl.BlockSpec(fc.shape, lambda b, j: (0, 0)),
                  pl.BlockSpec((None, 2, n1, tn), lambda b, j: (b, 0, 0, j)),
                  blk, blk,
                  pl.BlockSpec((None, HY_ORDER, tn), lambda b, j: (l, 0, j))],
        out_specs=blk,
        compiler_params=_cp(("parallel", "parallel"), VMEM_LIMIT),
        name="hyena_dft3",
    )(fc, b4, s2d, g2d, skip_t)


def _postmix_kernel(x_ref, oa_ref, ob_ref, oc_ref, mod_ref, g_ref, wa_ref, wb_ref, wc_ref, wrh_ref, wrl_ref, br_ref,
                    x1_ref, h2_ref, idx_ref, gate_ref):
    na = _rms(oa_ref[...]) * g_ref[:, :W_A]
    nb = _rms(ob_ref[...]) * g_ref[:, W_A:W_A + W_B]
    nc = _rms(oc_ref[...]) * g_ref[:, W_A + W_B:]
    mix = (_dot(na.astype(BF16), wa_ref[...]) + _dot(nb.astype(BF16), wb_ref[...])
           + _dot(nc.astype(BF16), wc_ref[...]))
    x1 = x_ref[...] + mod_ref[2:3, :] * mix
    x1_ref[...] = x1
    h2 = _rms(x1) * (1.0 + mod_ref[4:5, :]) + mod_ref[3:4, :]
    h2_ref[...] = h2
    hh, hl = _split(h2)
    vals = _dot(hh, wrh_ref[...]) + _dot(hh, wrl_ref[...]) + _dot(hl, wrh_ref[...]) + br_ref[...]
    lane = lax.broadcasted_iota(jnp.int32, vals.shape, 1).astype(F32)
    idx_out = jnp.zeros(vals.shape, F32)
    top = jnp.zeros(vals.shape, F32)
    m0 = None
    for k in range(TOP_K):
        m = vals.max(axis=-1, keepdims=True)
        sel = jnp.min(jnp.where(vals == m, lane, float(LANE)), axis=-1, keepdims=True)
        if m0 is None:
            m0 = m
        idx_out = jnp.where(lane == k, sel, idx_out)
        top = jnp.where(lane == k, jnp.exp(m - m0), top)
        vals = jnp.where(lane == sel, -jnp.inf, vals)
    idx_ref[...] = idx_out.astype(jnp.int32)
    gate_ref[...] = top / jnp.sum(top, axis=-1, keepdims=True)


def _postmix_call(l, x, oa, ob, oc, mod, seq, wts):
    t, d = x.shape
    tm = TOKEN_TILE
    ncond = mod.shape[1]

    def cond(i):
        return (i * tm) // seq if ncond > 1 else 0

    row = lambda w: pl.BlockSpec((tm, w), lambda i: (i, 0))
    lay = lambda a: pl.BlockSpec((None,) + a.shape[1:], lambda i: (l,) + (0,) * (a.ndim - 1))
    names = ["g_out", "wo_a", "wo_b", "wo_c", "wr_hi", "wr_lo", "b_router"]
    return pl.pallas_call(
        _postmix_kernel,
        out_shape=[jax.ShapeDtypeStruct((t, d), F32), jax.ShapeDtypeStruct((t, d), F32),
                   jax.ShapeDtypeStruct((t, LANE), jnp.int32), jax.ShapeDtypeStruct((t, LANE), F32)],
        grid=(t // tm,),
        in_specs=[row(d), row(W_A), row(W_B), row(W_C),
                  pl.BlockSpec((None, None, 6, d), lambda i: (l, cond(i), 0, 0))] + [lay(wts[k]) for k in names],
        out_specs=[row(d), row(d), row(LANE), row(LANE)],
        compiler_params=_cp(("parallel",), VMEM_LIMIT),
        name="postmix",
    )(x, oa, ob, oc, mod, *[wts[k] for k in names])


SUB = 8


def _moe_kernel(cs_ref, cnt_ref, tok_ref, gate_ref, h_ref, wi_ref, bi_ref, wo_ref, bo_ref, y_ref, buf, obuf):
    e = pl.program_id(0)

    @pl.when(e == 0)
    def _():
        y_ref[...] = jnp.zeros_like(y_ref)
        buf[...] = jnp.zeros_like(buf)

    start = cs_ref[e]
    cnt = cnt_ref[e]

    def block(b, carry):
        base = start + b * ROW_BLOCK
        nv = jnp.minimum(cnt - b * ROW_BLOCK, ROW_BLOCK)

        def gather(g, c):
            rows = [h_ref[pl.ds(tok_ref[base + g * SUB + j], 1), :] for j in range(SUB)]
            for j in range(SUB):
                buf[g, j:j + 1, :] = rows[j]
            return c

        lax.fori_loop(0, (nv + SUB - 1) // SUB, gather, 0)
        x = buf[...].reshape(ROW_BLOCK, buf.shape[2])
        gu = _dot(x.astype(BF16), wi_ref[...]) + bi_ref[...]
        gt = jnp.minimum(gu[:, :D_FF], SWIGLU_LIMIT)
        lin = jnp.clip(gu[:, D_FF:], -SWIGLU_LIMIT, SWIGLU_LIMIT)
        act = (lin + 1.0) * gt * (1.0 / (1.0 + jnp.exp(-SWIGLU_ALPHA * gt)))
        out = _dot(act.astype(BF16), wo_ref[...]) + bo_ref[...]
        obuf[...] = out.reshape(obuf.shape)

        def scatter(g, c):
            ts = [tok_ref[base + g * SUB + j] for j in range(SUB)]
            new = [y_ref[pl.ds(ts[j], 1), :] + gate_ref[base + g * SUB + j] * obuf[g, j:j + 1, :]
                   for j in range(SUB)]
            for j in range(SUB):
                y_ref[pl.ds(ts[j], 1), :] = new[j]
            return c

        full = nv // SUB
        lax.fori_loop(0, full, scatter, 0)

        def scatter_row(r, c):
            t = tok_ref[base + r]
            y_ref[pl.ds(t, 1), :] = y_ref[pl.ds(t, 1), :] + gate_ref[base + r] * obuf[full, pl.ds(r - full * SUB, 1), :]
            return c

        lax.fori_loop(full * SUB, nv, scatter_row, 0)
        return carry

    lax.fori_loop(0, (cnt + ROW_BLOCK - 1) // ROW_BLOCK, block, 0)


def _moe_kernel_entry(has_alias, *refs):
    refs = list(refs)
    if has_alias:
        del refs[9]
    _moe_kernel(*refs)


def _route(idx, gates):
    m = idx.shape[0] * TOP_K
    e = idx.reshape(m)
    flat = jnp.arange(m, dtype=jnp.int32)
    skey, gate = lax.sort((e * m + flat, gates.reshape(m)), num_keys=1)
    tok = (skey % m) // TOP_K
    experts = jnp.arange(N_EXPERTS, dtype=jnp.int32)
    cnt = jnp.sum((e[:, None] == experts[None, :]).astype(jnp.int32), axis=0)
    pad = jnp.zeros((ROW_BLOCK,), jnp.int32)
    return (jnp.cumsum(cnt) - cnt, cnt, jnp.concatenate([tok, pad]), jnp.concatenate([gate, pad.astype(F32)]))


def _moe_call(l, h2, idx, gates, wts):
    t, d = h2.shape
    tc = min(4096, t)
    y = None
    for c in range(t // tc):
        route = _route(idx[c * tc:(c + 1) * tc], gates[c * tc:(c + 1) * tc])
        wspec = lambda r, w: pl.BlockSpec((None, None, r, w), lambda e, *_: (l, e, 0, 0))
        in_specs = [pl.BlockSpec((tc, d), lambda i, *_: (c, 0), pipeline_mode=pl.Buffered(1)),
                    wspec(d, 2 * D_FF), wspec(1, 2 * D_FF), wspec(D_FF, d), wspec(1, d)]
        args = [h2, wts["w_moe_in"], wts["b_moe_in"], wts["w_moe_out"], wts["b_moe_out"]]
        aliases = {}
        if y is not None:
            in_specs.append(pl.BlockSpec(memory_space=pl.ANY))
            args.append(y)
            aliases = {len(route) + len(args) - 1: 0}
        grid_spec = pltpu.PrefetchScalarGridSpec(
            num_scalar_prefetch=len(route),
            grid=(N_EXPERTS,),
            in_specs=in_specs,
            out_specs=pl.BlockSpec((tc, d), lambda i, *_: (c, 0), pipeline_mode=pl.Buffered(1)),
            scratch_shapes=[pltpu.VMEM((ROW_BLOCK // SUB, SUB, d), F32)] * 2,
        )
        y = pl.pallas_call(
            functools.partial(_moe_kernel_entry, y is not None),
            out_shape=jax.ShapeDtypeStruct((t, d), F32),
            grid_spec=grid_spec,
            input_output_aliases=aliases,
            compiler_params=_cp(("arbitrary",), VMEM_LIMIT),
            name="moe_experts",
        )(*route, *args)
    return y


def _final_kernel(x_ref, y_ref, mod_ref, o_ref):
    o_ref[...] = x_ref[...] + mod_ref[5:6, :] * y_ref[...]


def _final_call(l, x, y, mod, seq):
    t, d = x.shape
    tm = TOKEN_TILE
    ncond = mod.shape[1]
    row = pl.BlockSpec((tm, d), lambda i: (i, 0))
    return pl.pallas_call(
        _final_kernel,
        out_shape=jax.ShapeDtypeStruct((t, d), F32),
        grid=(t // tm,),
        in_specs=[row, row, pl.BlockSpec((None, None, 6, d),
                                         lambda i: (l, (i * tm) // seq if ncond > 1 else 0, 0, 0))],
        out_specs=row,
        compiler_params=_cp(("parallel",), VMEM_LIMIT),
        name="final_residual",
    )(x, y, mod)


def _rope_tables(seq, rot_dim, lane_map):
    n_rows = seq // GRID_W
    rows = jnp.repeat(jnp.arange(n_rows, dtype=F32), GRID_W)
    cols = jnp.tile(jnp.arange(GRID_W, dtype=F32), n_rows)
    axis_dim = rot_dim // 2
    inv_freq = ROPE_THETA ** (-jnp.arange(0, axis_dim, 2, dtype=F32) / axis_dim)
    ang = jnp.concatenate([rows[:, None] * inv_freq, cols[:, None] * inv_freq], axis=-1)
    cos, sin = jnp.cos(ang), jnp.sin(ang)
    pair = np.zeros((LANE,), np.int32)
    in_rot = np.zeros((LANE,), np.float32)
    first = np.zeros((LANE,), np.float32)
    for ln in range(LANE):
        m = lane_map(ln)
        if m is not None:
            pair[ln], in_rot[ln], first[ln] = m[0], 1.0, 1.0 if m[1] == 0 else 0.0
    c = jnp.where(in_rot[None, :] > 0, cos[:, pair], 1.0)
    s = sin[:, pair] * in_rot[None, :]
    return jnp.stack([c, -s * first[None, :], s * (1.0 - first[None, :])]).astype(F32)


def _lane_map_a(ln):
    o = ln - NOPE_A
    if 0 <= o < ROPE_A:
        return (o % (ROPE_A // 2), o // (ROPE_A // 2))
    return None


def _lane_map_c(ln):
    o = ln % HD_C
    return (o % (HD_C // 2), o // (HD_C // 2))


def _phase(num, den):
    ang = (2.0 * math.pi / den) * (num % den).astype(F32)
    return jnp.cos(ang), jnp.sin(ang)


def _ctx_dft(n):
    f = jnp.arange(n, dtype=jnp.int32)[:, None]
    t = jnp.arange(n, dtype=jnp.int32)[None, :]
    c, s = _phase((2 * f + 1) * t, 4 * n)
    fwd = jnp.concatenate([c, -s], axis=0).astype(BF16)
    inv = jnp.concatenate([c.T, -s.T], axis=1).astype(BF16)
    return c, s, fwd, inv


def _lat_dft(seq):
    n = 2 * seq
    n1, n2 = FFT_N1, n // FFT_N1
    k1 = jnp.arange(n1, dtype=jnp.int32)
    c1, s1 = _phase(k1[:, None] * k1[None, :], n1)
    f1f = jnp.concatenate([c1, -s1], axis=0)
    f1d = f1f[:, :n1 // 2].astype(BF16)
    fc = jnp.concatenate([c1[:, :n1 // 2].T, -s1[:, :n1 // 2].T], axis=1).astype(BF16)
    k2 = jnp.arange(n2, dtype=jnp.int32)
    num = (k2[None, :, None] * k2[None, None, :]) * n1 + k2[None, None, :] * k1[:, None, None]
    cm, sm = _phase(num, n)
    mr, mi = cm, -sm
    f2 = jnp.concatenate([jnp.concatenate([mr, -mi], axis=2), jnp.concatenate([mi, mr], axis=2)], axis=1)
    mrt, mit = jnp.swapaxes(mr, 1, 2), jnp.swapaxes(mi, 1, 2)
    g2 = jnp.concatenate([jnp.concatenate([mrt, mit], axis=2), jnp.concatenate([-mit, mrt], axis=2)], axis=1)
    f2h, f2l = _split(f2)
    return dict(f1f=f1f, f1d=f1d, fc=fc, f2=f2h, f2l=f2l, g2=g2.astype(BF16), n1=n1, n2=n2, n=n)


def _hy_features(seq):
    t = jnp.linspace(0.0, 1.0, seq, dtype=F32)[:, None]
    bands = (HY_EMB - 1) // 2
    f = jnp.linspace(1e-4, bands - 1, bands, dtype=F32)[None, :]
    w = 2.0 * math.pi * jnp.arange(seq, dtype=F32)[:, None] / seq
    z = jnp.concatenate([t, jnp.cos(f * w), jnp.sin(f * w)], axis=-1)
    z = jnp.pad(z, ((0, 0), (0, HY_FO - HY_EMB)))
    deltas = jnp.abs(jnp.linspace(HY_MIN_DECAY, HY_MAX_DECAY, HY_CH, dtype=F32))
    return z, jnp.exp(-t * deltas)


def _prep_weights(w_in, mla_g_qa, mla_w_uq, mla_g_kva, mla_w_ukv, mla_g_q, mla_g_k, gqa_g_q, gqa_g_k,
                  g_out, w_out, w_router, b_router, hy_w_in):
    depth = w_in.shape[0]
    cuts = np.cumsum([0, Q_RANK, KV_RANK, ROPE_A, 3 * HY_CH, H_C * HD_C, KV_C * HD_C, KV_C * HD_C])
    wb = w_in.astype(BF16)
    zeros = lambda w: jnp.zeros(wb.shape[:2] + (w,), BF16)
    parts = [wb[:, :, cuts[0]:cuts[2]], zeros(NOPE_A), wb[:, :, cuts[2]:cuts[3]], zeros(LANE - QK_A),
             wb[:, :, cuts[3]:cuts[4]]]
    for h in range(H_C):
        wh = wb[:, :, cuts[4] + h * HD_C:cuts[4] + (h + 1) * HD_C]
        parts += [wh, zeros(HD_C)] if h < H_C // KV_C else [zeros(HD_C), wh]
    w_in_p = jnp.concatenate(parts + [wb[:, :, cuts[5]:cuts[7]]], axis=-1)

    w_uq = jnp.pad(mla_w_uq.reshape(depth, Q_RANK, H_A, QK_A), ((0, 0), (0, 0), (0, 0), (0, LANE - QK_A)))
    w_uq = w_uq.reshape(depth, Q_RANK, HQ).astype(BF16)
    ukv = mla_w_ukv.reshape(depth, KV_RANK, H_A, NOPE_A + V_A)
    wk = jnp.pad(ukv[..., :NOPE_A], ((0, 0), (0, 0), (0, 0), (0, LANE - NOPE_A))).reshape(depth, KV_RANK, HQ)
    wv = ukv[..., NOPE_A:].reshape(depth, KV_RANK, W_A)
    w_kv = jnp.concatenate([wk, wv], axis=-1).astype(BF16)

    def pad_row(v):
        return jnp.pad(v, ((0, 0), (0, HQ - v.shape[1])))

    head_a = lambda g: jnp.tile(jnp.pad(g, ((0, 0), (0, LANE - QK_A))), (1, H_A))
    head_c = lambda g, reps: jnp.tile(g, (1, reps))
    gains = jnp.stack([
        pad_row(mla_g_qa), pad_row(mla_g_kva),
        head_a(mla_g_q) * (QK_A ** -0.5 * LOG2E), head_a(mla_g_k),
        head_c(gqa_g_q, 2 * H_C) * (HD_C ** -0.5 * LOG2E), pad_row(head_c(gqa_g_k, 2)),
        jnp.zeros((depth, HQ), F32), jnp.zeros((depth, HQ), F32)], axis=1).astype(F32)

    g_per = H_C // KV_C
    perm_c = np.concatenate([np.arange(h * HD_C, (h + 1) * HD_C) for g in range(g_per) for h in (g, g + g_per)])
    rows_c = W_A + W_B + perm_c
    g_o = jnp.concatenate([g_out[:, :W_A + W_B], g_out[:, rows_c]], axis=1).reshape(depth, 1, -1)
    wr = jnp.pad(w_router, ((0, 0), (0, 0), (0, LANE - N_EXPERTS)))
    wr_hi, wr_lo = _split(wr)
    br = jnp.pad(b_router, ((0, 0), (0, LANE - N_EXPERTS)), constant_values=-1e30).reshape(depth, 1, LANE)
    return dict(w_in=w_in_p, w_uq=w_uq, w_kv=w_kv, gains=gains, g_out=g_o,
                wo_a=w_out[:, :W_A].astype(BF16), wo_b=w_out[:, W_A:W_A + W_B].astype(BF16),
                wo_c=w_out[:, rows_c].astype(BF16), wr_hi=wr_hi, wr_lo=wr_lo, b_router=br,
                hy_w_in=jnp.pad(hy_w_in, ((0, 0), (0, HY_FO - HY_EMB), (0, 0))))


def _hyena_lat(l, u, hw, kf, dft, seq):
    t = u.shape[0]
    nb = t // seq
    n1, n2 = dft["n1"], dft["n2"]
    w2 = n2 * HY_CH
    v, x1, x2 = _hconv_call(l, u, hw, seq)
    to2d = lambda a: a.reshape(t // n2, w2)
    s = to2d(v)
    for o, gate in enumerate((x1, x2)):
        a = _ha_call(dft["f1d"], s, nb)
        b = _hb_call(o, a.reshape(nb, 2, n1, n2, HY_CH), dft["f2"], dft["g2"], kf)
        s = _hc_call(l, o, dft["fc"], b.reshape(nb, 2, n1, w2), s, to2d(gate), hw["skip_t"])
    return s.reshape(t, HY_CH)


def _filter_lat(l, z, dec, hw, dft, seq):
    kern, nrm = _filt_lat_call(l, z, dec, hw, seq)
    n1, n2, n = dft["n1"], dft["n2"], dft["n"]
    c = kern.shape[1]
    af = _fa_call(dft["f1f"], kern.reshape(n1, n2 * c))
    return _fb_call(af.reshape(2, n1, n2, c), dft["f2"], dft["f2l"], nrm, n)


def kernel(x_prompt, x_sample, c, c_ctx, cache_mla_ckv, cache_mla_kpe, cache_gqa_k, cache_gqa_v, w_mod, b_mod, w_in, mla_g_qa, mla_w_uq, mla_g_kva, mla_w_ukv, mla_g_q, mla_g_k, hy_conv_w, hy_conv_b, hy_w_in, hy_b_in, hy_w_mid, hy_b_mid, hy_w_out, hy_b_out, hy_freq, hy_skip, gqa_g_q, gqa_g_k, g_out, w_out, w_router, b_router, w_moe_in, b_moe_in, w_moe_out, b_moe_out):
    batch, seq_c, d = x_prompt.shape
    nb_l, seq_l, _ = x_sample.shape
    depth = w_in.shape[0]
    past = cache_mla_ckv.shape[2]
    assert d == D_MODEL and seq_l % GRID_W == 0 and (2 * seq_l) % FFT_N1 == 0

    wts = _prep_weights(w_in, mla_g_qa, mla_w_uq, mla_g_kva, mla_w_ukv, mla_g_q, mla_g_k, gqa_g_q, gqa_g_k,
                        g_out, w_out, w_router, b_router, hy_w_in)
    wts["w_moe_in"] = _cast_call(w_moe_in.reshape(depth * N_EXPERTS, d, 2 * D_FF), 512).reshape(w_moe_in.shape)
    wts["w_moe_out"] = _cast_call(w_moe_out.reshape(depth * N_EXPERTS, D_FF, d), 512).reshape(w_moe_out.shape)
    wts["b_moe_in"] = b_moe_in.reshape(depth, N_EXPERTS, 1, 2 * D_FF)
    wts["b_moe_out"] = b_moe_out.reshape(depth, N_EXPERTS, 1, d)
    hw = dict(hy_w_in=wts["hy_w_in"], hy_b_in=hy_b_in.reshape(depth, 1, HY_FO), hy_w_mid=hy_w_mid,
              hy_b_mid=hy_b_mid.reshape(depth, HY_INNER, 1, HY_FO), hy_w_out=hy_w_out,
              hy_b_out=hy_b_out.reshape(depth, 1, -1), hy_freq=hy_freq.reshape(depth, 1, HY_FO),
              hy_conv_w=hy_conv_w, hy_conv_b=hy_conv_b.reshape(depth, 1, -1), hy_skip=hy_skip)
    dft_l = _lat_dft(seq_l)
    hw["skip_t"] = jnp.tile(hy_skip, (1, 1, dft_l["n2"]))

    conds = jnp.zeros((8, d), F32).at[0].set(c_ctx).at[1:1 + nb_l].set(c)
    mod = _mod_call(conds, w_mod, b_mod).reshape(depth, 8, 6, d)
    mod_c, mod_l = mod[:, 0:1], mod[:, 1:1 + nb_l]

    rope_tabs = (_rope_tables(seq_l, ROPE_A, _lane_map_a), _rope_tables(seq_l, HD_C, _lane_map_c))
    kax, vax = _cachekv_call(cache_mla_ckv, jnp.pad(cache_mla_kpe, ((0, 0), (0, 0), (0, 0), (NOPE_A, LANE - QK_A))), wts)
    kcx = cache_gqa_k.reshape(nb_l, depth, past, KV_C * HD_C).astype(BF16)
    vcx = cache_gqa_v.reshape(nb_l, depth, past, KV_C * HD_C).astype(BF16)
    vcx = jnp.concatenate([vcx, jnp.ones_like(vcx)], axis=-1)

    z_c, dec_c = _hy_features(seq_c)
    cmat, smat, fwd_c, inv_c = _ctx_dft(seq_c)
    z_l, dec_l = _hy_features(seq_l)
    z_full = jnp.concatenate([z_l, jnp.zeros((1, HY_FO), F32), z_l[:0:-1]], axis=0)
    dec_full = jnp.concatenate([dec_l, jnp.zeros((1, HY_CH), F32), dec_l[:0:-1]], axis=0)

    xc = x_prompt.reshape(batch * seq_c, d)
    xl = x_sample.reshape(nb_l * seq_l, d)
    yc = yl = None
    new_ckv, new_kpe, new_k, new_v = [], [], [], []
    for l in range(depth):
        (xc, qa, ka, va, ckv, kpe, qc, kc, vc, kcf, vcf, u) = _premix_call(l, xc, yc, mod_c, seq_c, False, wts, None)
        new_ckv.append(ckv)
        new_kpe.append(kpe[:, NOPE_A:QK_A])
        new_k.append(kcf)
        new_v.append(vcf)
        oa, oc = _attn_ctx_call(qa, ka, va, qc, kc, vc, seq_c)
        kf_c = _filt_ctx_call(l, z_c, dec_c, hw, cmat, smat)
        ob = _hy_ctx_call(l, u, hw, kf_c, fwd_c, inv_c, seq_c)
        xc, h2, idx, gates = _postmix_call(l, xc, oa, ob, oc, mod_c, seq_c, wts)
        yc = _moe_call(l, h2, idx[:, :TOP_K], gates[:, :TOP_K], wts)
        (xl, qa, ka, va, _, _, qc, kc, vc, _, _, u) = _premix_call(l, xl, yl, mod_l, seq_l, True, wts, rope_tabs)
        oa, oc = _attn_lat_call(l, qa, ka, va, qc, kc, vc, kax, vax, kcx, vcx, seq_l)
        kf_l = _filter_lat(l, z_full, dec_full, hw, dft_l, seq_l)
        ob = _hyena_lat(l, u, hw, kf_l, dft_l, seq_l)
        xl, h2, idx, gates = _postmix_call(l, xl, oa, ob, oc, mod_l, seq_l, wts)
        yl = _moe_call(l, h2, idx[:, :TOP_K], gates[:, :TOP_K], wts)
    y_prompt = _final_call(depth - 1, xc, yc, mod_c, seq_c).reshape(batch, seq_c, d)
    y_sample = _final_call(depth - 1, xl, yl, mod_l, seq_l).reshape(nb_l, seq_l, d)
    stack = lambda xs, tail: jnp.stack([a.reshape((batch, seq_c) + tail) for a in xs], axis=1)
    return (y_prompt, y_sample, stack(new_ckv, (KV_RANK,)), stack(new_kpe, (ROPE_A,)),
            stack(new_k, (KV_C, HD_C)), stack(new_v, (KV_C, HD_C)))
```

```python
import functools
import math

import numpy as np
import jax
import jax.numpy as jnp
from jax import lax
from jax.experimental import pallas as pl
from jax.experimental.pallas import tpu as pltpu

F32 = jnp.float32
BF16 = jnp.bfloat16

D_MODEL = 1024
GRID_W = 64
EPS = 1e-6
ROPE_THETA = 10000.0
H_A = 6
Q_RANK = 256
KV_RANK = 128
NOPE_A = 64
ROPE_A = 32
V_A = 64
QK_A = NOPE_A + ROPE_A
HY_CH = 256
HY_ORDER = 2
HY_EMB = 33
HY_FO = 64
HY_INNER = 2
HY_MIN_DECAY = math.log(1e-2) / 1.5
HY_MAX_DECAY = math.log(1e-2) / 0.3
H_C = 6
KV_C = 2
HD_C = 64
N_EXPERTS = 32
TOP_K = 4
D_FF = 1024
SWIGLU_LIMIT = 7.0
SWIGLU_ALPHA = 1.702
LOG2E = 1.4426950408889634

LANE = 128
HQ = H_A * LANE
W_A = H_A * V_A
W_B = HY_CH
W_C = H_C * HD_C
IN_P = Q_RANK + KV_RANK + LANE + 3 * HY_CH + HQ + 2 * LANE
ROW_BLOCK = 128
TOKEN_TILE = 512
VMEM_LIMIT = 56 * 1024 * 1024
FFT_N1 = 128
FFT_N2 = 64


def _cp(sem, vmem=None):
    return pltpu.CompilerParams(dimension_semantics=sem, vmem_limit_bytes=vmem)


def _dot(a, b):
    return jnp.dot(a, b, preferred_element_type=F32)


def _split(a):
    hi = a.astype(BF16)
    return hi, (a - hi.astype(F32)).astype(BF16)


def _dot3(a, b):
    ah, al = _split(a)
    bh, bl = _split(b)
    return _dot(ah, bh) + _dot(ah, bl) + _dot(al, bh)


def _bdot(a, b):
    return lax.dot_general(a, b, (((2,), (1,)), ((0,), (0,))), preferred_element_type=F32)


def _rms(x, n=None):
    ss = jnp.sum(x * x, axis=-1, keepdims=True) * (1.0 / (n or x.shape[-1]))
    return x * lax.rsqrt(ss + EPS)


def _head_norm(x, nvalid):
    outs = []
    for h in range(x.shape[1] // LANE):
        blk = x[:, h * LANE:(h + 1) * LANE]
        ss = jnp.sum(blk * blk, axis=-1, keepdims=True) * (1.0 / nvalid)
        outs.append(blk * lax.rsqrt(ss + EPS))
    return outs[0] if len(outs) == 1 else jnp.concatenate(outs, axis=1)


def _rope(x, tab_ref, shift):
    c, sm, sp = tab_ref[0], tab_ref[1], tab_ref[2]
    outs = []
    for h in range(x.shape[1] // LANE):
        blk = x[:, h * LANE:(h + 1) * LANE]
        outs.append(blk * c + pltpu.roll(blk, LANE - shift, 1) * sm + pltpu.roll(blk, shift, 1) * sp)
    return outs[0] if len(outs) == 1 else jnp.concatenate(outs, axis=1)


def _mod_kernel(c_ref, w_ref, b_ref, o_ref):
    c = c_ref[...]
    s = c * (1.0 / (1.0 + jnp.exp(-c)))
    o_ref[...] = _dot3(s, w_ref[...]) + b_ref[...]


def _mod_call(conds, w_mod, b_mod):
    depth, d, n6 = w_mod.shape
    tn = 1536
    return pl.pallas_call(
        _mod_kernel,
        out_shape=jax.ShapeDtypeStruct((depth, 8, n6), F32),
        grid=(depth, n6 // tn),
        in_specs=[pl.BlockSpec((8, d), lambda l, j: (0, 0)),
                  pl.BlockSpec((None, d, tn), lambda l, j: (l, 0, j)),
                  pl.BlockSpec((None, 1, tn), lambda l, j: (l, 0, j))],
        out_specs=pl.BlockSpec((None, 8, tn), lambda l, j: (l, 0, j)),
        compiler_params=_cp(("parallel", "parallel"), VMEM_LIMIT),
        name="mod",
    )(conds, w_mod, b_mod.reshape(depth, 1, n6))


def _cast_kernel(x_ref, o_ref):
    o_ref[...] = x_ref[...].astype(BF16)


def _cast_call(w, tr):
    n, r, c = w.shape
    return pl.pallas_call(
        _cast_kernel,
        out_shape=jax.ShapeDtypeStruct(w.shape, BF16),
        grid=(n, r // tr),
        in_specs=[pl.BlockSpec((None, tr, c), lambda i, j: (i, j, 0))],
        out_specs=pl.BlockSpec((None, tr, c), lambda i, j: (i, j, 0)),
        compiler_params=_cp(("parallel", "parallel"), VMEM_LIMIT),
        name="cast_bf16",
    )(w)


def _kv_heads(ckv_bf, kpe, w_kv_ref, gk, rope_ref, use_rope):
    kvp = _dot(ckv_bf, w_kv_ref[...])
    kn = kvp[:, :HQ]
    ka = jnp.concatenate([kn[:, h * LANE:(h + 1) * LANE] + kpe for h in range(H_A)], axis=1)
    ka = _head_norm(ka, QK_A) * gk
    if use_rope:
        ka = _rope(ka, rope_ref, ROPE_A // 2)
    return ka, kvp[:, HQ:]


def _premix_kernel(has_prev, use_rope, *refs):
    it = iter(refs)
    x_ref = next(it)
    if has_prev:
        y_ref, modp_ref = next(it), next(it)
    mod_ref, w_in_ref, w_uq_ref, w_kv_ref, g_ref = next(it), next(it), next(it), next(it), next(it)
    if use_rope:
        ra_ref, rc_ref = next(it), next(it)
    else:
        ra_ref = rc_ref = None
    (xo_ref, qa_ref, ka_ref, va_ref, ckv_ref, kpe_ref, qc_ref, kc_ref, vc_ref, kcf_ref, vcf_ref, u_ref) = it

    x = x_ref[...]
    if has_prev:
        x = x + modp_ref[5:6, :] * y_ref[...]
    xo_ref[...] = x
    h = _rms(x) * (1.0 + mod_ref[1:2, :]) + mod_ref[0:1, :]
    proj = _dot(h.astype(BF16), w_in_ref[...])
    o = 0
    c_q = proj[:, o:o + Q_RANK]; o += Q_RANK
    c_kv = proj[:, o:o + KV_RANK]; o += KV_RANK
    kpe = proj[:, o:o + LANE]; o += LANE
    u_ref[...] = proj[:, o:o + 3 * HY_CH]; o += 3 * HY_CH
    q_c = proj[:, o:o + HQ]; o += HQ
    k_c = proj[:, o:o + LANE]; o += LANE
    v_c = proj[:, o:o + LANE]

    cqn = _rms(c_q) * g_ref[0:1, :Q_RANK]
    qa = _head_norm(_dot(cqn.astype(BF16), w_uq_ref[...]), QK_A) * g_ref[2:3, :]
    if use_rope:
        qa = _rope(qa, ra_ref, ROPE_A // 2)
    qa_ref[...] = qa.astype(BF16)
    ckv = _rms(c_kv) * g_ref[1:2, :KV_RANK]
    ckv_ref[...] = ckv
    kpe_ref[...] = kpe
    ka, va = _kv_heads(ckv.astype(BF16), kpe, w_kv_ref, g_ref[3:4, :], ra_ref, use_rope)
    ka_ref[...] = ka.astype(BF16)
    va_ref[...] = _with_ones(va.astype(BF16))

    qc = _head_norm(q_c, HD_C) * g_ref[4:5, :]
    if use_rope:
        qc = _rope(qc, rc_ref, HD_C // 2)
    qc_ref[...] = qc.astype(BF16)
    lane = lax.broadcasted_iota(jnp.int32, k_c.shape, 1)
    k2 = k_c * k_c
    s0 = jnp.sum(jnp.where(lane < HD_C, k2, 0.0), axis=-1, keepdims=True) * (1.0 / HD_C)
    s1 = jnp.sum(jnp.where(lane >= HD_C, k2, 0.0), axis=-1, keepdims=True) * (1.0 / HD_C)
    kcn = k_c * jnp.where(lane < HD_C, lax.rsqrt(s0 + EPS), lax.rsqrt(s1 + EPS)) * g_ref[5:6, :LANE]
    kcf_ref[...] = kcn
    vcf_ref[...] = v_c
    kc_ref[...] = (_rope(kcn, rc_ref, HD_C // 2) if use_rope else kcn).astype(BF16)
    vc_ref[...] = _with_ones(v_c.astype(BF16))


def _chunk_rows_spec(y, tm):
    per = (y.shape[1] - SUB) // tm
    return pl.BlockSpec((None, tm, y.shape[2]), lambda i: (i // per, i % per, 0))


def _premix_call(l, x, yprev, mod, seq, use_rope, wts, rope_tabs):
    t, d = x.shape
    tm = TOKEN_TILE
    ncond = mod.shape[1]
    has_prev = yprev is not None

    def cond(i):
        return (i * tm) // seq if ncond > 1 else 0

    row = lambda w: pl.BlockSpec((tm, w), lambda i: (i, 0))
    ins, specs = [x], [row(d)]
    if has_prev:
        ins += [yprev, mod]
        specs += [_chunk_rows_spec(yprev, tm), pl.BlockSpec((None, None, 6, d), lambda i: (l - 1, cond(i), 0, 0))]
    ins += [mod, wts["w_in"], wts["w_uq"], wts["w_kv"], wts["gains"]]
    specs += [pl.BlockSpec((None, None, 6, d), lambda i: (l, cond(i), 0, 0)),
              pl.BlockSpec((None, d, IN_P), lambda i: (l, 0, 0)),
              pl.BlockSpec((None, Q_RANK, HQ), lambda i: (l, 0, 0)),
              pl.BlockSpec((None, KV_RANK, HQ + W_A), lambda i: (l, 0, 0)),
              pl.BlockSpec((None, 8, HQ), lambda i: (l, 0, 0))]
    if use_rope:
        nt = seq // tm
        ins += [rope_tabs[0], rope_tabs[1]]
        specs += [pl.BlockSpec((3, tm, LANE), lambda i: (0, i % nt, 0))] * 2
    outs = [(d, F32), (HQ, BF16), (HQ, BF16), (2 * W_A, BF16), (KV_RANK, F32), (LANE, F32), (HQ, BF16),
            (LANE, BF16), (2 * LANE, BF16), (LANE, F32), (LANE, F32), (3 * HY_CH, F32)]
    return pl.pallas_call(
        functools.partial(_premix_kernel, has_prev, use_rope),
        out_shape=[jax.ShapeDtypeStruct((t, w), dt) for w, dt in outs],
        grid=(t // tm,),
        in_specs=specs,
        out_specs=[row(w) for w, _ in outs],
        compiler_params=_cp(("parallel",), VMEM_LIMIT),
        name="premix",
    )(*ins)


def _cachekv_kernel(ckv_ref, kpe_ref, w_kv_ref, g_ref, ka_ref, va_ref):
    ka, va = _kv_heads(ckv_ref[...].astype(BF16), kpe_ref[...], w_kv_ref, g_ref[3:4, :], None, False)
    ka_ref[...] = ka.astype(BF16)
    va_ref[...] = _with_ones(va.astype(BF16))


def _cachekv_call(cache_ckv, cache_kpe_p, wts):
    nb, depth, past, _ = cache_ckv.shape
    return pl.pallas_call(
        _cachekv_kernel,
        out_shape=[jax.ShapeDtypeStruct((depth, nb, past, HQ), BF16),
                   jax.ShapeDtypeStruct((depth, nb, past, 2 * W_A), BF16)],
        grid=(depth, nb),
        in_specs=[pl.BlockSpec((None, None, past, KV_RANK), lambda l, b: (b, l, 0, 0)),
                  pl.BlockSpec((None, None, past, LANE), lambda l, b: (b, l, 0, 0)),
                  pl.BlockSpec((None, KV_RANK, HQ + W_A), lambda l, b: (l, 0, 0)),
                  pl.BlockSpec((None, 8, HQ), lambda l, b: (l, 0, 0))],
        out_specs=[pl.BlockSpec((None, None, past, HQ), lambda l, b: (l, b, 0, 0)),
                   pl.BlockSpec((None, None, past, 2 * W_A), lambda l, b: (l, b, 0, 0))],
        compiler_params=_cp(("parallel", "parallel"), VMEM_LIMIT),
        name="cache_kv",
    )(cache_ckv, cache_kpe_p, wts["w_kv"], wts["gains"])


def _nt(q, k):
    return lax.dot_general(q, k, (((1,), (1,)), ((), ())), preferred_element_type=F32)


def _with_ones(v):
    ones = jnp.ones((v.shape[0], LANE), v.dtype)
    parts = []
    for j in range(v.shape[1] // LANE):
        parts += [v[:, j * LANE:(j + 1) * LANE], ones]
    return jnp.concatenate(parts, axis=1)


def _attend(q, ks, vs):
    return _softmax_pv([_nt(q, k) for k in ks], vs)


def _softmax_pv(ss, vs):
    m = ss[0].max(axis=-1, keepdims=True)
    for s in ss[1:]:
        m = jnp.maximum(m, s.max(axis=-1, keepdims=True))
    acc = None
    for s, v in zip(ss, vs):
        pv = _dot(jnp.exp2(s - m).astype(BF16), v)
        acc = pv if acc is None else acc + pv
    return acc[:, :LANE] / acc[:, LANE:]


def _attn_kernel(nseg, *refs):
    qa_ref, qc_ref = refs[0], refs[1]
    segs = refs[2:2 + 4 * nseg]
    oa_ref, oc_ref = refs[2 + 4 * nseg:]
    ka_refs, va_refs, kc_refs, vc_refs = (segs[i::4] for i in range(4))
    lane = lax.broadcasted_iota(jnp.int32, (qa_ref.shape[0], LANE), 1)
    low = lane < V_A
    g_per = H_C // KV_C
    jobs = [("a", j, h) for j in range(H_A // 2) for h in (2 * j, 2 * j + 1)]
    jobs += [("c", g, h) for g in range(g_per) for h in (g, g + g_per)]

    def scores(job):
        kind, _, h = job
        hs = slice(h * LANE, (h + 1) * LANE)
        if kind == "a":
            return [_nt(qa_ref[:, hs], r[:, hs]) for r in ka_refs]
        return [_nt(qc_ref[:, hs], r[...]) for r in kc_refs]

    def values(job):
        kind, j, _ = job
        if kind == "a":
            return [r[:, 2 * j * LANE:2 * (j + 1) * LANE] for r in va_refs]
        return [r[...] for r in vc_refs]

    ss_next = scores(jobs[0])
    pv = []
    for n, job in enumerate(jobs):
        ss = ss_next
        if n + 1 < len(jobs):
            ss_next = scores(jobs[n + 1])
        pv.append(_softmax_pv(ss, values(job)))
        if len(pv) == 2:
            o_ref = oa_ref if job[0] == "a" else oc_ref
            o_ref[:, job[1] * LANE:(job[1] + 1) * LANE] = jnp.where(low, pv[0], pv[1])
            pv = []


def _attn_ctx_call(qa, ka, va, qc, kc, vc, seq):
    t = qa.shape[0]
    blk = lambda w: pl.BlockSpec((seq, w), lambda b: (b, 0))
    return pl.pallas_call(
        functools.partial(_attn_kernel, 1),
        out_shape=[jax.ShapeDtypeStruct((t, W_A), F32), jax.ShapeDtypeStruct((t, W_C), F32)],
        grid=(t // seq,),
        in_specs=[blk(HQ), blk(HQ), blk(HQ), blk(2 * W_A), blk(LANE), blk(2 * LANE)],
        out_specs=[blk(W_A), blk(W_C)],
        compiler_params=_cp(("parallel",), VMEM_LIMIT),
        name="attn_ctx",
    )(qa, qc, ka, va, kc, vc)


def _attn_lat_call(l, qa, ka, va, qc, kc, vc, kax, vax, kcx, vcx, seq):
    t = qa.shape[0]
    nb = t // seq
    tq = min(256, seq)
    nq = seq // tq
    past = kax.shape[2]
    qblk = lambda w: pl.BlockSpec((tq, w), lambda b, i: (b * nq + i, 0))
    sblk = lambda w: pl.BlockSpec((seq, w), lambda b, i: (b, 0))
    xblk = lambda w: pl.BlockSpec((None, None, past, w), lambda b, i: (l, b, 0, 0))
    cblk = lambda w: pl.BlockSpec((None, None, past, w), lambda b, i: (b, l, 0, 0))
    return pl.pallas_call(
        functools.partial(_attn_kernel, 2),
        out_shape=[jax.ShapeDtypeStruct((t, W_A), F32), jax.ShapeDtypeStruct((t, W_C), F32)],
        grid=(nb, nq),
        in_specs=[qblk(HQ), qblk(HQ),
                  xblk(HQ), xblk(2 * W_A), cblk(LANE), cblk(2 * LANE),
                  sblk(HQ), sblk(2 * W_A), sblk(LANE), sblk(2 * LANE)],
        out_specs=[qblk(W_A), qblk(W_C)],
        compiler_params=_cp(("parallel", "parallel"), VMEM_LIMIT),
        name="attn_lat",
    )(qa, qc, kax, vax, kcx, vcx, ka, va, kc, vc)


def _filter_mlp(z, w_in_ref, b_in_ref, w_mid_ref, b_mid_ref, w_out_ref, b_out_ref, freq_ref):
    freq = freq_ref[...]
    a = jnp.sin(freq * (_dot3(z, w_in_ref[...]) + b_in_ref[...]))
    for i in range(HY_INNER):
        a = jnp.sin(freq * (_dot3(a, w_mid_ref[i]) + b_mid_ref[i]))
    return _dot3(a, w_out_ref[...]) + b_out_ref[...]


def _conv3(u, up, dn, cw_ref, cb_ref):
    return up * cw_ref[0:1, :] + u * cw_ref[1:2, :] + dn * cw_ref[2:3, :] + cb_ref[...]


def _filt_ctx_kernel(z_ref, dec_ref, w_in_ref, b_in_ref, w_mid_ref, b_mid_ref, w_out_ref, b_out_ref, freq_ref,
                     c_ref, s_ref, o_ref):
    n = z_ref.shape[0]
    h = _filter_mlp(z_ref[...], w_in_ref, b_in_ref, w_mid_ref, b_mid_ref, w_out_ref, b_out_ref, freq_ref)
    dec = dec_ref[...]
    row = lax.broadcasted_iota(jnp.int32, dec.shape, 0)
    half = HY_ORDER * HY_CH
    for o in range(HY_ORDER):
        hf = h[:, o * HY_CH:(o + 1) * HY_CH] * dec
        hb = jnp.where(row > 0, h[:, half + o * HY_CH:half + (o + 1) * HY_CH] * dec, 0.0)
        nrm = jnp.sum(jnp.abs(hf) + jnp.abs(hb), axis=0, keepdims=True) + EPS
        scale = (1.0 / n) / nrm
        o_ref[o, 0] = _dot3(c_ref[...], hf + hb) * scale
        o_ref[o, 1] = -_dot3(s_ref[...], hf - hb) * scale


def _filt_ctx_call(l, z, dec, hw, cmat, smat):
    n = z.shape[0]
    full = lambda a: pl.BlockSpec((None,) + a.shape[1:], lambda i: (l,) + (0,) * (a.ndim - 1))
    const = lambda a: pl.BlockSpec(a.shape, lambda i: (0,) * a.ndim)
    names = ["hy_w_in", "hy_b_in", "hy_w_mid", "hy_b_mid", "hy_w_out", "hy_b_out", "hy_freq"]
    return pl.pallas_call(
        _filt_ctx_kernel,
        out_shape=jax.ShapeDtypeStruct((HY_ORDER, 2, n, HY_CH), F32),
        grid=(1,),
        in_specs=[const(z), const(dec)] + [full(hw[k]) for k in names] + [const(cmat), const(smat)],
        out_specs=pl.BlockSpec((HY_ORDER, 2, n, HY_CH), lambda i: (0, 0, 0, 0)),
        compiler_params=_cp(("arbitrary",), VMEM_LIMIT),
        name="hy_filter_ctx",
    )(z, dec, *[hw[k] for k in names], cmat, smat)


def _hy_ctx_kernel(u_ref, cw_ref, cb_ref, skip_ref, kf_ref, fwd_ref, inv_ref, o_ref):
    u = u_ref[...]
    n = u.shape[0]
    row = lax.broadcasted_iota(jnp.int32, u.shape, 0)
    up = jnp.where(row > 0, pltpu.roll(u, 1, 0), 0.0)
    dn = jnp.where(row < n - 1, pltpu.roll(u, n - 1, 0), 0.0)
    z = _conv3(u, up, dn, cw_ref, cb_ref)
    s = z[:, :HY_CH]
    gates = (z[:, HY_CH:2 * HY_CH], z[:, 2 * HY_CH:])
    for o in range(HY_ORDER):
        xs = _dot(fwd_ref[...], s.astype(BF16))
        xr, xi = xs[:n], xs[n:]
        kr, ki = kf_ref[o, 0], kf_ref[o, 1]
        ycat = jnp.concatenate([xr * kr - xi * ki, xr * ki + xi * kr], axis=0)
        y = _dot(inv_ref[...], ycat.astype(BF16))
        s = gates[o] * (y + s * skip_ref[o:o + 1, :])
    o_ref[...] = s


def _hy_ctx_call(l, u, hw, kf, fwd, inv, seq):
    t = u.shape[0]
    full = lambda a: pl.BlockSpec((None,) + a.shape[1:], lambda b: (l,) + (0,) * (a.ndim - 1))
    const = lambda a: pl.BlockSpec(a.shape, lambda b: (0,) * a.ndim)
    return pl.pallas_call(
        _hy_ctx_kernel,
        out_shape=jax.ShapeDtypeStruct((t, HY_CH), F32),
        grid=(t // seq,),
        in_specs=[pl.BlockSpec((seq, 3 * HY_CH), lambda b: (b, 0)),
                  full(hw["hy_conv_w"]), full(hw["hy_conv_b"]), full(hw["hy_skip"]),
                  const(kf), const(fwd), const(inv)],
        out_specs=pl.BlockSpec((seq, HY_CH), lambda b: (b, 0)),
        compiler_params=_cp(("parallel",), VMEM_LIMIT),
        name="hyena_ctx",
    )(u, hw["hy_conv_w"], hw["hy_conv_b"], hw["hy_skip"], kf, fwd, inv)


def _filt_lat_kernel(seq, z_ref, dec_ref, w_in_ref, b_in_ref, w_mid_ref, b_mid_ref, w_out_ref, b_out_ref, freq_ref,
                     k_ref, n_ref):
    i = pl.program_id(0)
    tr = z_ref.shape[0]
    h = _filter_mlp(z_ref[...], w_in_ref, b_in_ref, w_mid_ref, b_mid_ref, w_out_ref, b_out_ref, freq_ref)
    half = HY_ORDER * HY_CH
    row = i * tr + lax.broadcasted_iota(jnp.int32, (tr, half), 0)
    dec = dec_ref[...]
    kern = jnp.where(row < seq, h[:, :half], h[:, half:]) * jnp.concatenate([dec] * HY_ORDER, axis=1)
    k_ref[...] = kern

    @pl.when(i == 0)
    def _():
        n_ref[...] = jnp.zeros_like(n_ref)

    n_ref[...] += jnp.sum(jnp.abs(kern), axis=0, keepdims=True)


def _filt_lat_call(l, z, dec, hw, seq):
    n = z.shape[0]
    tr = min(512, n)
    half = HY_ORDER * HY_CH
    full = lambda a: pl.BlockSpec((None,) + a.shape[1:], lambda i: (l,) + (0,) * (a.ndim - 1))
    names = ["hy_w_in", "hy_b_in", "hy_w_mid", "hy_b_mid", "hy_w_out", "hy_b_out", "hy_freq"]
    return pl.pallas_call(
        functools.partial(_filt_lat_kernel, seq),
        out_shape=[jax.ShapeDtypeStruct((n, half), F32), jax.ShapeDtypeStruct((1, half), F32)],
        grid=(n // tr,),
        in_specs=[pl.BlockSpec((tr, z.shape[1]), lambda i: (i, 0)), pl.BlockSpec((tr, HY_CH), lambda i: (i, 0))]
        + [full(hw[k]) for k in names],
        out_specs=[pl.BlockSpec((tr, half), lambda i: (i, 0)), pl.BlockSpec((1, half), lambda i: (0, 0))],
        compiler_params=_cp(("arbitrary",), VMEM_LIMIT),
        name="hy_filter_lat",
    )(z, dec, *[hw[k] for k in names])


def _fa_kernel(f1_ref, k_ref, o_ref):
    r = _dot3(f1_ref[...], k_ref[...])
    o_ref[0] = r[:FFT_N1]
    o_ref[1] = r[FFT_N1:]


def _fa_call(f1f, kern2d):
    n1, w = kern2d.shape
    tn = min(2048, w)
    return pl.pallas_call(
        _fa_kernel,
        out_shape=jax.ShapeDtypeStruct((2, FFT_N1, w), F32),
        grid=(w // tn,),
        in_specs=[pl.BlockSpec(f1f.shape, lambda j: (0, 0)), pl.BlockSpec((n1, tn), lambda j: (0, j))],
        out_specs=pl.BlockSpec((2, FFT_N1, tn), lambda j: (0, 0, j)),
        compiler_params=_cp(("parallel",), VMEM_LIMIT),
        name="hy_filter_dft1",
    )(f1f, kern2d)


def _fb_kernel(n_total, a_ref, fh_ref, fl_ref, n_ref, o_ref):
    a = jnp.concatenate([a_ref[0], a_ref[1]], axis=1)
    ah, al = _split(a)
    x = _bdot(fh_ref[...], ah) + _bdot(fh_ref[...], al) + _bdot(fl_ref[...], ah)
    scale = (1.0 / n_total) / (n_ref[...] + EPS)
    o_ref[...] = x * scale[None]


def _fb_call(af5, f2h, f2l, nrm, n_total):
    _, n1, n2, c = af5.shape[0], af5.shape[1], af5.shape[2], af5.shape[3]
    k1t = 8
    return pl.pallas_call(
        functools.partial(_fb_kernel, n_total),
        out_shape=jax.ShapeDtypeStruct((n1, 2 * n2, c), F32),
        grid=(n1 // k1t,),
        in_specs=[pl.BlockSpec((2, k1t, n2, c), lambda j: (0, j, 0, 0)),
                  pl.BlockSpec((k1t, 2 * n2, 2 * n2), lambda j: (j, 0, 0)),
                  pl.BlockSpec((k1t, 2 * n2, 2 * n2), lambda j: (j, 0, 0)),
                  pl.BlockSpec((1, c), lambda j: (0, 0))],
        out_specs=pl.BlockSpec((k1t, 2 * n2, c), lambda j: (j, 0, 0)),
        compiler_params=_cp(("parallel",), VMEM_LIMIT),
        name="hy_filter_dft2",
    )(af5, f2h, f2l, nrm)


def _hconv_kernel(seq, u_ref, p_ref, n_ref, cw_ref, cb_ref, v_ref, x1_ref, x2_ref):
    i = pl.program_id(0)
    u = u_ref[...]
    tt = u.shape[0]
    row = lax.broadcasted_iota(jnp.int32, u.shape, 0)
    pos = (i * tt) % seq
    prev = jnp.where(pos > 0, p_ref[7:8, :], 0.0)
    nxt = jnp.where(pos + tt < seq, n_ref[0:1, :], 0.0)
    up = jnp.where(row > 0, pltpu.roll(u, 1, 0), prev)
    dn = jnp.where(row < tt - 1, pltpu.roll(u, tt - 1, 0), nxt)
    z = _conv3(u, up, dn, cw_ref, cb_ref)
    v_ref[...] = z[:, :HY_CH]
    x1_ref[...] = z[:, HY_CH:2 * HY_CH]
    x2_ref[...] = z[:, 2 * HY_CH:]


def _hconv_call(l, u, hw, seq):
    t, w = u.shape
    tt = min(512, seq)
    nblk8 = t // 8
    full = lambda a: pl.BlockSpec((None,) + a.shape[1:], lambda i: (l,) + (0,) * (a.ndim - 1))
    ob = pl.BlockSpec((tt, HY_CH), lambda i: (i, 0))
    return pl.pallas_call(
        functools.partial(_hconv_kernel, seq),
        out_shape=[jax.ShapeDtypeStruct((t, HY_CH), F32)] * 3,
        grid=(t // tt,),
        in_specs=[pl.BlockSpec((tt, w), lambda i: (i, 0)),
                  pl.BlockSpec((8, w), lambda i: (jnp.maximum(i * (tt // 8) - 1, 0), 0)),
                  pl.BlockSpec((8, w), lambda i: (jnp.minimum((i + 1) * (tt // 8), nblk8 - 1), 0)),
                  full(hw["hy_conv_w"]), full(hw["hy_conv_b"])],
        out_specs=[ob, ob, ob],
        compiler_params=_cp(("parallel",), VMEM_LIMIT),
        name="hyena_conv3",
    )(u, u, u, hw["hy_conv_w"], hw["hy_conv_b"])


def _ha_kernel(f1_ref, x_ref, o_ref):
    r = _dot(f1_ref[...], x_ref[...].astype(BF16))
    o_ref[0] = r[:FFT_N1].astype(BF16)
    o_ref[1] = r[FFT_N1:].astype(BF16)


def _ha_call(f1d, x2d, nb):
    rows, w = x2d.shape
    n1h = rows // nb
    tn = min(2048, w)
    return pl.pallas_call(
        _ha_kernel,
        out_shape=jax.ShapeDtypeStruct((nb, 2, FFT_N1, w), BF16),
        grid=(nb, w // tn),
        in_specs=[pl.BlockSpec(f1d.shape, lambda b, j: (0, 0)), pl.BlockSpec((n1h, tn), lambda b, j: (b, j))],
        out_specs=pl.BlockSpec((None, 2, FFT_N1, tn), lambda b, j: (b, 0, 0, j)),
        compiler_params=_cp(("parallel", "parallel"), VMEM_LIMIT),
        name="hyena_dft1",
    )(f1d, x2d)


def _hb_kernel(a_ref, f_ref, g_ref, kf_ref, o_ref):
    n2 = a_ref.shape[2]
    a = jnp.concatenate([a_ref[0], a_ref[1]], axis=1)
    x = _bdot(f_ref[...], a)
    xr, xi = x[:, :n2], x[:, n2:]
    kr, ki = kf_ref[:, :n2], kf_ref[:, n2:]
    y = jnp.concatenate([xr * kr - xi * ki, xr * ki + xi * kr], axis=1).astype(BF16)
    b = _bdot(g_ref[...], y)
    o_ref[0] = b[:, :n2].astype(BF16)
    o_ref[1] = b[:, n2:].astype(BF16)


def _hb_call(o, a5, f2, g2, kf):
    nb, _, n1, n2, c = a5.shape
    k1t = 16
    blk = pl.BlockSpec((None, 2, k1t, n2, c), lambda b, j: (b, 0, j, 0, 0))
    mat = pl.BlockSpec((k1t, 2 * n2, 2 * n2), lambda b, j: (j, 0, 0))
    return pl.pallas_call(
        _hb_kernel,
        out_shape=jax.ShapeDtypeStruct(a5.shape, BF16),
        grid=(nb, n1 // k1t),
        in_specs=[blk, mat, mat, pl.BlockSpec((k1t, 2 * n2, c), lambda b, j: (j, 0, o))],
        out_specs=blk,
        compiler_params=_cp(("parallel", "parallel"), VMEM_LIMIT),
        name="hyena_dft2",
    )(a5, f2, g2, kf)


def _hc_kernel(o, fc_ref, b_ref, s_ref, g_ref, skip_ref, o_ref):
    bcat = jnp.concatenate([b_ref[0], b_ref[1]], axis=0)
    y = _dot(fc_ref[...], bcat)
    o_ref[...] = g_ref[...] * (y + s_ref[...] * skip_ref[o:o + 1, :])


def _hc_call(l, o, fc, b4, s2d, g2d, skip_t):
    nb, _, n1, w = b4.shape
    rows = s2d.shape[0] // nb
    tn = min(2048, w)
    blk = pl.BlockSpec((rows, tn), lambda b, j: (b, j))
    return pl.pallas_call(
        functools.partial(_hc_kernel, o),
        out_shape=jax.ShapeDtypeStruct(s2d.shape, F32),
        grid=(nb, w // tn),
        in_specs=[pl.BlockSpec(fc.shape, lambda b, j: (0, 0)),
                  pl.BlockSpec((None, 2, n1, tn), lambda b, j: (b, 0, 0, j)),
                  blk, blk,
                  pl.BlockSpec((None, HY_ORDER, tn), lambda b, j: (l, 0, j))],
        out_specs=blk,
        compiler_params=_cp(("parallel", "parallel"), VMEM_LIMIT),
        name="hyena_dft3",
    )(fc, b4, s2d, g2d, skip_t)


def _postmix_kernel(x_ref, oa_ref, ob_ref, oc_ref, mod_ref, g_ref, wa_ref, wb_ref, wc_ref, wrh_ref, wrl_ref, br_ref,
                    x1_ref, h2_ref, idx_ref, gate_ref):
    na = _rms(oa_ref[...]) * g_ref[:, :W_A]
    nb = _rms(ob_ref[...]) * g_ref[:, W_A:W_A + W_B]
    nc = _rms(oc_ref[...]) * g_ref[:, W_A + W_B:]
    mix = (_dot(na.astype(BF16), wa_ref[...]) + _dot(nb.astype(BF16), wb_ref[...])
           + _dot(nc.astype(BF16), wc_ref[...]))
    x1 = x_ref[...] + mod_ref[2:3, :] * mix
    x1_ref[...] = x1
    h2 = _rms(x1) * (1.0 + mod_ref[4:5, :]) + mod_ref[3:4, :]
    h2_ref[...] = h2
    hh, hl = _split(h2)
    vals = _dot(hh, wrh_ref[...]) + _dot(hh, wrl_ref[...]) + _dot(hl, wrh_ref[...]) + br_ref[...]
    lane = lax.broadcasted_iota(jnp.int32, vals.shape, 1).astype(F32)
    idx_out = jnp.zeros(vals.shape, F32)
    top = jnp.zeros(vals.shape, F32)
    m0 = None
    for k in range(TOP_K):
        m = vals.max(axis=-1, keepdims=True)
        sel = jnp.min(jnp.where(vals == m, lane, float(LANE)), axis=-1, keepdims=True)
        if m0 is None:
            m0 = m
        idx_out = jnp.where(lane == k, sel, idx_out)
        top = jnp.where(lane == k, jnp.exp(m - m0), top)
        vals = jnp.where(lane == sel, -jnp.inf, vals)
    idx_ref[...] = idx_out.astype(jnp.int32)
    gate_ref[...] = top / jnp.sum(top, axis=-1, keepdims=True)


def _postmix_call(l, x, oa, ob, oc, mod, seq, wts):
    t, d = x.shape
    tm = TOKEN_TILE
    ncond = mod.shape[1]

    def cond(i):
        return (i * tm) // seq if ncond > 1 else 0

    row = lambda w: pl.BlockSpec((tm, w), lambda i: (i, 0))
    lay = lambda a: pl.BlockSpec((None,) + a.shape[1:], lambda i: (l,) + (0,) * (a.ndim - 1))
    names = ["g_out", "wo_a", "wo_b", "wo_c", "wr_hi", "wr_lo", "b_router"]
    return pl.pallas_call(
        _postmix_kernel,
        out_shape=[jax.ShapeDtypeStruct((t, d), F32), jax.ShapeDtypeStruct((t, d), F32),
                   jax.ShapeDtypeStruct((t, LANE), jnp.int32), jax.ShapeDtypeStruct((t, LANE), F32)],
        grid=(t // tm,),
        in_specs=[row(d), row(W_A), row(W_B), row(W_C),
                  pl.BlockSpec((None, None, 6, d), lambda i: (l, cond(i), 0, 0))] + [lay(wts[k]) for k in names],
        out_specs=[row(d), row(d), row(LANE), row(LANE)],
        compiler_params=_cp(("parallel",), VMEM_LIMIT),
        name="postmix",
    )(x, oa, ob, oc, mod, *[wts[k] for k in names])


SUB = 8


NGRP = ROW_BLOCK // SUB


def _moe_kernel(be_ref, rs_ref, nv_ref, nu_ref, tok_ref, gate_ref, h_ref, wi_ref, bi_ref, wo_ref, bo_ref, y_ref,
                buf, obuf, xb):
    s = pl.program_id(0)
    tc = h_ref.shape[0]

    @pl.when(s == 0)
    def _():
        y_ref[...] = jnp.zeros_like(y_ref)
        buf[...] = jnp.zeros_like(buf)
        obuf[...] = jnp.zeros_like(obuf)

    @pl.when(s < nu_ref[0] + 2)
    def _():
        slot_g = lax.rem(s, 2)
        slot_c = 1 - slot_g

        xb[...] = buf[slot_c].reshape(ROW_BLOCK, buf.shape[3]).astype(BF16)

        base_s = rs_ref[s]
        nv = nv_ref[s]
        for g in range(NGRP):
            ts = [jnp.where(g * SUB + j < nv, tok_ref[base_s + g * SUB + j], tc) for j in range(SUB)]
            new = [y_ref[pl.ds(ts[j], 1), :] + gate_ref[base_s + g * SUB + j] * obuf[slot_g, g, j:j + 1, :]
                   for j in range(SUB)]
            for j in range(SUB):
                y_ref[pl.ds(ts[j], 1), :] = new[j]

        base_g = rs_ref[s + 2]
        for g in range(NGRP):
            rows = [h_ref[pl.ds(tok_ref[base_g + g * SUB + j], 1), :] for j in range(SUB)]
            for j in range(SUB):
                buf[slot_g, g, j:j + 1, :] = rows[j]

        gu = _dot(xb[...], wi_ref[...]) + bi_ref[...]
        gt = jnp.minimum(gu[:, :D_FF], SWIGLU_LIMIT)
        lin = jnp.clip(gu[:, D_FF:], -SWIGLU_LIMIT, SWIGLU_LIMIT)
        act = (lin + 1.0) * gt * (1.0 / (1.0 + jnp.exp(-SWIGLU_ALPHA * gt)))
        out = _dot(act.astype(BF16), wo_ref[...]) + bo_ref[...]
        obuf[slot_c] = out.reshape(obuf.shape[1:])


def _moe_kernel_entry(has_alias, *refs):
    refs = list(refs)
    if has_alias:
        del refs[11]
    _moe_kernel(*refs)


def _route(idx, gates, n_blocks):
    m = idx.shape[0] * TOP_K
    e = idx.reshape(m)
    flat = jnp.arange(m, dtype=jnp.int32)
    skey, gate = lax.sort((e * m + flat, gates.reshape(m)), num_keys=1)
    tok = (skey % m) // TOP_K
    experts = jnp.arange(N_EXPERTS, dtype=jnp.int32)
    cnt = jnp.sum((e[:, None] == experts[None, :]).astype(jnp.int32), axis=0)
    nblk = (cnt + ROW_BLOCK - 1) // ROW_BLOCK
    bend = jnp.cumsum(nblk)
    n_used = bend[-1]
    blk = jnp.arange(-2, n_blocks + 2, dtype=jnp.int32)
    bcl = jnp.clip(blk, 0, n_used - 1)
    be = jnp.sum((bend[None, :] <= bcl[:, None]).astype(jnp.int32), axis=1)
    oh = (be[:, None] == experts[None, :]).astype(jnp.int32)
    pick = lambda v: jnp.sum(oh * v[None, :], axis=1)
    off = (bcl - pick(bend - nblk)) * ROW_BLOCK
    valid = (blk >= 0) & (blk < n_used)
    rs = jnp.where(valid, pick(jnp.cumsum(cnt) - cnt) + off, 0)
    nv = jnp.where(valid, jnp.clip(pick(cnt) - off, 0, ROW_BLOCK), 0)
    pad = jnp.zeros((ROW_BLOCK,), jnp.int32)
    return (be[1:n_blocks + 3], rs, nv, n_used.reshape(1), jnp.concatenate([tok, pad]),
            jnp.concatenate([gate, pad.astype(F32)]))


def _moe_call(l, h2, idx, gates, wts):
    t, d = h2.shape
    tc = min(4096, t)
    n_blocks = tc * TOP_K // ROW_BLOCK + N_EXPERTS
    y = None
    for c in range(t // tc):
        route = _route(idx[c * tc:(c + 1) * tc], gates[c * tc:(c + 1) * tc], n_blocks)
        wspec = lambda r, w: pl.BlockSpec((None, None, r, w), lambda s, be, *_: (l, be[s], 0, 0))
        in_specs = [pl.BlockSpec((tc, d), lambda i, *_: (c, 0), pipeline_mode=pl.Buffered(1)),
                    wspec(d, 2 * D_FF), wspec(1, 2 * D_FF), wspec(D_FF, d), wspec(1, d)]
        args = [h2, wts["w_moe_in"], wts["b_moe_in"], wts["w_moe_out"], wts["b_moe_out"]]
        aliases = {}
        if y is not None:
            in_specs.append(pl.BlockSpec(memory_space=pl.ANY))
            args.append(y)
            aliases = {len(route) + len(args) - 1: 0}
        grid_spec = pltpu.PrefetchScalarGridSpec(
            num_scalar_prefetch=len(route),
            grid=(n_blocks + 2,),
            in_specs=in_specs,
            out_specs=pl.BlockSpec((None, tc + SUB, d), lambda i, *_: (c, 0, 0), pipeline_mode=pl.Buffered(1)),
            scratch_shapes=[pltpu.VMEM((2, NGRP, SUB, d), F32), pltpu.VMEM((2, NGRP, SUB, d), F32),
                            pltpu.VMEM((ROW_BLOCK, d), BF16)],
        )
        y = pl.pallas_call(
            functools.partial(_moe_kernel_entry, y is not None),
            out_shape=jax.ShapeDtypeStruct((t // tc, tc + SUB, d), F32),
            grid_spec=grid_spec,
            input_output_aliases=aliases,
            compiler_params=_cp(("arbitrary",), VMEM_LIMIT),
            name="moe_experts",
        )(*route, *args)
    return y


def _final_kernel(x_ref, y_ref, mod_ref, o_ref):
    o_ref[...] = x_ref[...] + mod_ref[5:6, :] * y_ref[...]


def _final_call(l, x, y, mod, seq):
    t, d = x.shape
    tm = TOKEN_TILE
    ncond = mod.shape[1]
    row = pl.BlockSpec((tm, d), lambda i: (i, 0))
    return pl.pallas_call(
        _final_kernel,
        out_shape=jax.ShapeDtypeStruct((t, d), F32),
        grid=(t // tm,),
        in_specs=[row, _chunk_rows_spec(y, tm),
                  pl.BlockSpec((None, None, 6, d), lambda i: (l, (i * tm) // seq if ncond > 1 else 0, 0, 0))],
        out_specs=row,
        compiler_params=_cp(("parallel",), VMEM_LIMIT),
        name="final_residual",
    )(x, y, mod)


def _rope_tables(seq, rot_dim, lane_map):
    n_rows = seq // GRID_W
    rows = jnp.repeat(jnp.arange(n_rows, dtype=F32), GRID_W)
    cols = jnp.tile(jnp.arange(GRID_W, dtype=F32), n_rows)
    axis_dim = rot_dim // 2
    inv_freq = ROPE_THETA ** (-jnp.arange(0, axis_dim, 2, dtype=F32) / axis_dim)
    ang = jnp.concatenate([rows[:, None] * inv_freq, cols[:, None] * inv_freq], axis=-1)
    cos, sin = jnp.cos(ang), jnp.sin(ang)
    pair = np.zeros((LANE,), np.int32)
    in_rot = np.zeros((LANE,), np.float32)
    first = np.zeros((LANE,), np.float32)
    for ln in range(LANE):
        m = lane_map(ln)
        if m is not None:
            pair[ln], in_rot[ln], first[ln] = m[0], 1.0, 1.0 if m[1] == 0 else 0.0
    c = jnp.where(in_rot[None, :] > 0, cos[:, pair], 1.0)
    s = sin[:, pair] * in_rot[None, :]
    return jnp.stack([c, -s * first[None, :], s * (1.0 - first[None, :])]).astype(F32)


def _lane_map_a(ln):
    o = ln - NOPE_A
    if 0 <= o < ROPE_A:
        return (o % (ROPE_A // 2), o // (ROPE_A // 2))
    return None


def _lane_map_c(ln):
    o = ln % HD_C
    return (o % (HD_C // 2), o // (HD_C // 2))


def _phase(num, den):
    ang = (2.0 * math.pi / den) * (num % den).astype(F32)
    return jnp.cos(ang), jnp.sin(ang)


def _ctx_dft(n):
    f = jnp.arange(n, dtype=jnp.int32)[:, None]
    t = jnp.arange(n, dtype=jnp.int32)[None, :]
    c, s = _phase((2 * f + 1) * t, 4 * n)
    fwd = jnp.concatenate([c, -s], axis=0).astype(BF16)
    inv = jnp.concatenate([c.T, -s.T], axis=1).astype(BF16)
    return c, s, fwd, inv


def _lat_dft(seq):
    n = 2 * seq
    n1, n2 = FFT_N1, n // FFT_N1
    k1 = jnp.arange(n1, dtype=jnp.int32)
    c1, s1 = _phase(k1[:, None] * k1[None, :], n1)
    f1f = jnp.concatenate([c1, -s1], axis=0)
    f1d = f1f[:, :n1 // 2].astype(BF16)
    fc = jnp.concatenate([c1[:, :n1 // 2].T, -s1[:, :n1 // 2].T], axis=1).astype(BF16)
    k2 = jnp.arange(n2, dtype=jnp.int32)
    num = (k2[None, :, None] * k2[None, None, :]) * n1 + k2[None, None, :] * k1[:, None, None]
    cm, sm = _phase(num, n)
    mr, mi = cm, -sm
    f2 = jnp.concatenate([jnp.concatenate([mr, -mi], axis=2), jnp.concatenate([mi, mr], axis=2)], axis=1)
    mrt, mit = jnp.swapaxes(mr, 1, 2), jnp.swapaxes(mi, 1, 2)
    g2 = jnp.concatenate([jnp.concatenate([mrt, mit], axis=2), jnp.concatenate([-mit, mrt], axis=2)], axis=1)
    f2h, f2l = _split(f2)
    return dict(f1f=f1f, f1d=f1d, fc=fc, f2=f2h, f2l=f2l, g2=g2.astype(BF16), n1=n1, n2=n2, n=n)


def _hy_features(seq):
    t = jnp.linspace(0.0, 1.0, seq, dtype=F32)[:, None]
    bands = (HY_EMB - 1) // 2
    f = jnp.linspace(1e-4, bands - 1, bands, dtype=F32)[None, :]
    w = 2.0 * math.pi * jnp.arange(seq, dtype=F32)[:, None] / seq
    z = jnp.concatenate([t, jnp.cos(f * w), jnp.sin(f * w)], axis=-1)
    z = jnp.pad(z, ((0, 0), (0, HY_FO - HY_EMB)))
    deltas = jnp.abs(jnp.linspace(HY_MIN_DECAY, HY_MAX_DECAY, HY_CH, dtype=F32))
    return z, jnp.exp(-t * deltas)


def _prep_weights(w_in, mla_g_qa, mla_w_uq, mla_g_kva, mla_w_ukv, mla_g_q, mla_g_k, gqa_g_q, gqa_g_k,
                  g_out, w_out, w_router, b_router, hy_w_in):
    depth = w_in.shape[0]
    cuts = np.cumsum([0, Q_RANK, KV_RANK, ROPE_A, 3 * HY_CH, H_C * HD_C, KV_C * HD_C, KV_C * HD_C])
    wb = w_in.astype(BF16)
    zeros = lambda w: jnp.zeros(wb.shape[:2] + (w,), BF16)
    parts = [wb[:, :, cuts[0]:cuts[2]], zeros(NOPE_A), wb[:, :, cuts[2]:cuts[3]], zeros(LANE - QK_A),
             wb[:, :, cuts[3]:cuts[4]]]
    for h in range(H_C):
        wh = wb[:, :, cuts[4] + h * HD_C:cuts[4] + (h + 1) * HD_C]
        parts += [wh, zeros(HD_C)] if h < H_C // KV_C else [zeros(HD_C), wh]
    w_in_p = jnp.concatenate(parts + [wb[:, :, cuts[5]:cuts[7]]], axis=-1)

    w_uq = jnp.pad(mla_w_uq.reshape(depth, Q_RANK, H_A, QK_A), ((0, 0), (0, 0), (0, 0), (0, LANE - QK_A)))
    w_uq = w_uq.reshape(depth, Q_RANK, HQ).astype(BF16)
    ukv = mla_w_ukv.reshape(depth, KV_RANK, H_A, NOPE_A + V_A)
    wk = jnp.pad(ukv[..., :NOPE_A], ((0, 0), (0, 0), (0, 0), (0, LANE - NOPE_A))).reshape(depth, KV_RANK, HQ)
    wv = ukv[..., NOPE_A:].reshape(depth, KV_RANK, W_A)
    w_kv = jnp.concatenate([wk, wv], axis=-1).astype(BF16)

    def pad_row(v):
        return jnp.pad(v, ((0, 0), (0, HQ - v.shape[1])))

    head_a = lambda g: jnp.tile(jnp.pad(g, ((0, 0), (0, LANE - QK_A))), (1, H_A))
    head_c = lambda g, reps: jnp.tile(g, (1, reps))
    gains = jnp.stack([
        pad_row(mla_g_qa), pad_row(mla_g_kva),
        head_a(mla_g_q) * (QK_A ** -0.5 * LOG2E), head_a(mla_g_k),
        head_c(gqa_g_q, 2 * H_C) * (HD_C ** -0.5 * LOG2E), pad_row(head_c(gqa_g_k, 2)),
        jnp.zeros((depth, HQ), F32), jnp.zeros((depth, HQ), F32)], axis=1).astype(F32)

    g_per = H_C // KV_C
    perm_c = np.concatenate([np.arange(h * HD_C, (h + 1) * HD_C) for g in range(g_per) for h in (g, g + g_per)])
    rows_c = W_A + W_B + perm_c
    g_o = jnp.concatenate([g_out[:, :W_A + W_B], g_out[:, rows_c]], axis=1).reshape(depth, 1, -1)
    wr = jnp.pad(w_router, ((0, 0), (0, 0), (0, LANE - N_EXPERTS)))
    wr_hi, wr_lo = _split(wr)
    br = jnp.pad(b_router, ((0, 0), (0, LANE - N_EXPERTS)), constant_values=-1e30).reshape(depth, 1, LANE)
    return dict(w_in=w_in_p, w_uq=w_uq, w_kv=w_kv, gains=gains, g_out=g_o,
                wo_a=w_out[:, :W_A].astype(BF16), wo_b=w_out[:, W_A:W_A + W_B].astype(BF16),
                wo_c=w_out[:, rows_c].astype(BF16), wr_hi=wr_hi, wr_lo=wr_lo, b_router=br,
                hy_w_in=jnp.pad(hy_w_in, ((0, 0), (0, HY_FO - HY_EMB), (0, 0))))


def _hyena_lat(l, u, hw, kf, dft, seq):
    t = u.shape[0]
    nb = t // seq
    n1, n2 = dft["n1"], dft["n2"]
    w2 = n2 * HY_CH
    v, x1, x2 = _hconv_call(l, u, hw, seq)
    to2d = lambda a: a.reshape(t // n2, w2)
    s = to2d(v)
    for o, gate in enumerate((x1, x2)):
        a = _ha_call(dft["f1d"], s, nb)
        b = _hb_call(o, a.reshape(nb, 2, n1, n2, HY_CH), dft["f2"], dft["g2"], kf)
        s = _hc_call(l, o, dft["fc"], b.reshape(nb, 2, n1, w2), s, to2d(gate), hw["skip_t"])
    return s.reshape(t, HY_CH)


def _filter_lat(l, z, dec, hw, dft, seq):
    kern, nrm = _filt_lat_call(l, z, dec, hw, seq)
    n1, n2, n = dft["n1"], dft["n2"], dft["n"]
    c = kern.shape[1]
    af = _fa_call(dft["f1f"], kern.reshape(n1, n2 * c))
    return _fb_call(af.reshape(2, n1, n2, c), dft["f2"], dft["f2l"], nrm, n)


def kernel(x_prompt, x_sample, c, c_ctx, cache_mla_ckv, cache_mla_kpe, cache_gqa_k, cache_gqa_v, w_mod, b_mod, w_in, mla_g_qa, mla_w_uq, mla_g_kva, mla_w_ukv, mla_g_q, mla_g_k, hy_conv_w, hy_conv_b, hy_w_in, hy_b_in, hy_w_mid, hy_b_mid, hy_w_out, hy_b_out, hy_freq, hy_skip, gqa_g_q, gqa_g_k, g_out, w_out, w_router, b_router, w_moe_in, b_moe_in, w_moe_out, b_moe_out):
    batch, seq_c, d = x_prompt.shape
    nb_l, seq_l, _ = x_sample.shape
    depth = w_in.shape[0]
    past = cache_mla_ckv.shape[2]
    assert d == D_MODEL and seq_l % GRID_W == 0 and (2 * seq_l) % FFT_N1 == 0

    wts = _prep_weights(w_in, mla_g_qa, mla_w_uq, mla_g_kva, mla_w_ukv, mla_g_q, mla_g_k, gqa_g_q, gqa_g_k,
                        g_out, w_out, w_router, b_router, hy_w_in)
    wts["w_moe_in"] = _cast_call(w_moe_in.reshape(depth * N_EXPERTS, d, 2 * D_FF), 512).reshape(w_moe_in.shape)
    wts["w_moe_out"] = _cast_call(w_moe_out.reshape(depth * N_EXPERTS, D_FF, d), 512).reshape(w_moe_out.shape)
    wts["b_moe_in"] = b_moe_in.reshape(depth, N_EXPERTS, 1, 2 * D_FF)
    wts["b_moe_out"] = b_moe_out.reshape(depth, N_EXPERTS, 1, d)
    hw = dict(hy_w_in=wts["hy_w_in"], hy_b_in=hy_b_in.reshape(depth, 1, HY_FO), hy_w_mid=hy_w_mid,
              hy_b_mid=hy_b_mid.reshape(depth, HY_INNER, 1, HY_FO), hy_w_out=hy_w_out,
              hy_b_out=hy_b_out.reshape(depth, 1, -1), hy_freq=hy_freq.reshape(depth, 1, HY_FO),
              hy_conv_w=hy_conv_w, hy_conv_b=hy_conv_b.reshape(depth, 1, -1), hy_skip=hy_skip)
    dft_l = _lat_dft(seq_l)
    hw["skip_t"] = jnp.tile(hy_skip, (1, 1, dft_l["n2"]))

    conds = jnp.zeros((8, d), F32).at[0].set(c_ctx).at[1:1 + nb_l].set(c)
    mod = _mod_call(conds, w_mod, b_mod).reshape(depth, 8, 6, d)
    mod_c, mod_l = mod[:, 0:1], mod[:, 1:1 + nb_l]

    rope_tabs = (_rope_tables(seq_l, ROPE_A, _lane_map_a), _rope_tables(seq_l, HD_C, _lane_map_c))
    kax, vax = _cachekv_call(cache_mla_ckv, jnp.pad(cache_mla_kpe, ((0, 0), (0, 0), (0, 0), (NOPE_A, LANE - QK_A))), wts)
    kcx = cache_gqa_k.reshape(nb_l, depth, past, KV_C * HD_C).astype(BF16)
    vcx = cache_gqa_v.reshape(nb_l, depth, past, KV_C * HD_C).astype(BF16)
    vcx = jnp.concatenate([vcx, jnp.ones_like(vcx)], axis=-1)

    z_c, dec_c = _hy_features(seq_c)
    cmat, smat, fwd_c, inv_c = _ctx_dft(seq_c)
    z_l, dec_l = _hy_features(seq_l)
    z_full = jnp.concatenate([z_l, jnp.zeros((1, HY_FO), F32), z_l[:0:-1]], axis=0)
    dec_full = jnp.concatenate([dec_l, jnp.zeros((1, HY_CH), F32), dec_l[:0:-1]], axis=0)

    xc = x_prompt.reshape(batch * seq_c, d)
    xl = x_sample.reshape(nb_l * seq_l, d)
    yc = yl = None
    new_ckv, new_kpe, new_k, new_v = [], [], [], []
    for l in range(depth):
        (xc, qa, ka, va, ckv, kpe, qc, kc, vc, kcf, vcf, u) = _premix_call(l, xc, yc, mod_c, seq_c, False, wts, None)
        new_ckv.append(ckv)
        new_kpe.append(kpe[:, NOPE_A:QK_A])
        new_k.append(kcf)
        new_v.append(vcf)
        oa, oc = _attn_ctx_call(qa, ka, va, qc, kc, vc, seq_c)
        kf_c = _filt_ctx_call(l, z_c, dec_c, hw, cmat, smat)
        ob = _hy_ctx_call(l, u, hw, kf_c, fwd_c, inv_c, seq_c)
        xc, h2, idx, gates = _postmix_call(l, xc, oa, ob, oc, mod_c, seq_c, wts)
        yc = _moe_call(l, h2, idx[:, :TOP_K], gates[:, :TOP_K], wts)
        (xl, qa, ka, va, _, _, qc, kc, vc, _, _, u) = _premix_call(l, xl, yl, mod_l, seq_l, True, wts, rope_tabs)
        oa, oc = _attn_lat_call(l, qa, ka, va, qc, kc, vc, kax, vax, kcx, vcx, seq_l)
        kf_l = _filter_lat(l, z_full, dec_full, hw, dft_l, seq_l)
        ob = _hyena_lat(l, u, hw, kf_l, dft_l, seq_l)
        xl, h2, idx, gates = _postmix_call(l, xl, oa, ob, oc, mod_l, seq_l, wts)
        yl = _moe_call(l, h2, idx[:, :TOP_K], gates[:, :TOP_K], wts)
    y_prompt = _final_call(depth - 1, xc, yc, mod_c, seq_c).reshape(batch, seq_c, d)
    y_sample = _final_call(depth - 1, xl, yl, mod_l, seq_l).reshape(nb_l, seq_l, d)
    stack = lambda xs, tail: jnp.stack([a.reshape((batch, seq_c) + tail) for a in xs], axis=1)
    return (y_prompt, y_sample, stack(new_ckv, (KV_RANK,)), stack(new_kpe, (ROPE_A,)),
            stack(new_k, (KV_C, HD_C)), stack(new_v, (KV_C, HD_C)))
```

```python
import functools
import math

import numpy as np
import jax
import jax.numpy as jnp
from jax import lax
from jax.experimental import pallas as pl
from jax.experimental.pallas import tpu as pltpu

F32 = jnp.float32
BF16 = jnp.bfloat16

D_MODEL = 1024
GRID_W = 64
EPS = 1e-6
ROPE_THETA = 10000.0
H_A = 6
Q_RANK = 256
KV_RANK = 128
NOPE_A = 64
ROPE_A = 32
V_A = 64
QK_A = NOPE_A + ROPE_A
HY_CH = 256
HY_ORDER = 2
HY_EMB = 33
HY_FO = 64
HY_INNER = 2
HY_MIN_DECAY = math.log(1e-2) / 1.5
HY_MAX_DECAY = math.log(1e-2) / 0.3
H_C = 6
KV_C = 2
HD_C = 64
N_EXPERTS = 32
TOP_K = 4
D_FF = 1024
SWIGLU_LIMIT = 7.0
SWIGLU_ALPHA = 1.702
LOG2E = 1.4426950408889634

LANE = 128
HQ = H_A * LANE
W_A = H_A * V_A
W_B = HY_CH
W_C = H_C * HD_C
IN_P = Q_RANK + KV_RANK + LANE + 3 * HY_CH + HQ + 2 * LANE
ROW_BLOCK = 128
TOKEN_TILE = 512
VMEM_LIMIT = 56 * 1024 * 1024
FFT_N1 = 128
FFT_N2 = 64


def _cp(sem, vmem=None):
    return pltpu.CompilerParams(dimension_semantics=sem, vmem_limit_bytes=vmem)


def _dot(a, b):
    return jnp.dot(a, b, preferred_element_type=F32)


def _split(a):
    hi = a.astype(BF16)
    return hi, (a - hi.astype(F32)).astype(BF16)


def _dot3(a, b):
    ah, al = _split(a)
    bh, bl = _split(b)
    return _dot(ah, bh) + _dot(ah, bl) + _dot(al, bh)


def _bdot(a, b):
    return lax.dot_general(a, b, (((2,), (1,)), ((0,), (0,))), preferred_element_type=F32)


def _rms(x, n=None):
    ss = jnp.sum(x * x, axis=-1, keepdims=True) * (1.0 / (n or x.shape[-1]))
    return x * lax.rsqrt(ss + EPS)


def _head_norm(x, nvalid):
    outs = []
    for h in range(x.shape[1] // LANE):
        blk = x[:, h * LANE:(h + 1) * LANE]
        ss = jnp.sum(blk * blk, axis=-1, keepdims=True) * (1.0 / nvalid)
        outs.append(blk * lax.rsqrt(ss + EPS))
    return outs[0] if len(outs) == 1 else jnp.concatenate(outs, axis=1)


def _rope(x, tab_ref, shift):
    c, sm, sp = tab_ref[0], tab_ref[1], tab_ref[2]
    outs = []
    for h in range(x.shape[1] // LANE):
        blk = x[:, h * LANE:(h + 1) * LANE]
        outs.append(blk * c + pltpu.roll(blk, LANE - shift, 1) * sm + pltpu.roll(blk, shift, 1) * sp)
    return outs[0] if len(outs) == 1 else jnp.concatenate(outs, axis=1)


def _mod_kernel(c_ref, w_ref, b_ref, o_ref):
    c = c_ref[...]
    s = c * (1.0 / (1.0 + jnp.exp(-c)))
    o_ref[...] = _dot3(s, w_ref[...]) + b_ref[...]


def _mod_call(conds, w_mod, b_mod):
    depth, d, n6 = w_mod.shape
    tn = 1536
    return pl.pallas_call(
        _mod_kernel,
        out_shape=jax.ShapeDtypeStruct((depth, 8, n6), F32),
        grid=(depth, n6 // tn),
        in_specs=[pl.BlockSpec((8, d), lambda l, j: (0, 0)),
                  pl.BlockSpec((None, d, tn), lambda l, j: (l, 0, j)),
                  pl.BlockSpec((None, 1, tn), lambda l, j: (l, 0, j))],
        out_specs=pl.BlockSpec((None, 8, tn), lambda l, j: (l, 0, j)),
        compiler_params=_cp(("parallel", "parallel"), VMEM_LIMIT),
        name="mod",
    )(conds, w_mod, b_mod.reshape(depth, 1, n6))


def _cast_kernel(x_ref, o_ref):
    o_ref[...] = x_ref[...].astype(BF16)


def _cast_call(w, tr):
    n, r, c = w.shape
    return pl.pallas_call(
        _cast_kernel,
        out_shape=jax.ShapeDtypeStruct(w.shape, BF16),
        grid=(n, r // tr),
        in_specs=[pl.BlockSpec((None, tr, c), lambda i, j: (i, j, 0))],
        out_specs=pl.BlockSpec((None, tr, c), lambda i, j: (i, j, 0)),
        compiler_params=_cp(("parallel", "parallel"), VMEM_LIMIT),
        name="cast_bf16",
    )(w)


def _kv_heads(ckv_bf, kpe, w_kv_ref, gk, rope_ref, use_rope):
    kvp = _dot(ckv_bf, w_kv_ref[...])
    kn = kvp[:, :HQ]
    ka = jnp.concatenate([kn[:, h * LANE:(h + 1) * LANE] + kpe for h in range(H_A)], axis=1)
    ka = _head_norm(ka, QK_A) * gk
    if use_rope:
        ka = _rope(ka, rope_ref, ROPE_A // 2)
    return ka, kvp[:, HQ:]


def _premix_kernel(has_prev, use_rope, *refs):
    it = iter(refs)
    x_ref = next(it)
    if has_prev:
        y_ref, modp_ref = next(it), next(it)
    mod_ref, w_in_ref, w_uq_ref, w_kv_ref, g_ref = next(it), next(it), next(it), next(it), next(it)
    if use_rope:
        ra_ref, rc_ref = next(it), next(it)
    else:
        ra_ref = rc_ref = None
    (xo_ref, qa_ref, ka_ref, va_ref, ckv_ref, kpe_ref, qc_ref, kc_ref, vc_ref, kcf_ref, vcf_ref, u_ref) = it

    x = x_ref[...]
    if has_prev:
        x = x + modp_ref[5:6, :] * y_ref[...]
    xo_ref[...] = x
    h = _rms(x) * (1.0 + mod_ref[1:2, :]) + mod_ref[0:1, :]
    proj = _dot(h.astype(BF16), w_in_ref[...])
    o = 0
    c_q = proj[:, o:o + Q_RANK]; o += Q_RANK
    c_kv = proj[:, o:o + KV_RANK]; o += KV_RANK
    kpe = proj[:, o:o + LANE]; o += LANE
    u_ref[...] = proj[:, o:o + 3 * HY_CH]; o += 3 * HY_CH
    q_c = proj[:, o:o + HQ]; o += HQ
    k_c = proj[:, o:o + LANE]; o += LANE
    v_c = proj[:, o:o + LANE]

    cqn = _rms(c_q) * g_ref[0:1, :Q_RANK]
    qa = _head_norm(_dot(cqn.astype(BF16), w_uq_ref[...]), QK_A) * g_ref[2:3, :]
    if use_rope:
        qa = _rope(qa, ra_ref, ROPE_A // 2)
    qa_ref[...] = qa.astype(BF16)
    ckv = _rms(c_kv) * g_ref[1:2, :KV_RANK]
    ckv_ref[...] = ckv
    kpe_ref[...] = kpe
    ka, va = _kv_heads(ckv.astype(BF16), kpe, w_kv_ref, g_ref[3:4, :], ra_ref, use_rope)
    ka_ref[...] = ka.astype(BF16)
    va_ref[...] = _with_ones(va.astype(BF16))

    qc = _head_norm(q_c, HD_C) * g_ref[4:5, :]
    if use_rope:
        qc = _rope(qc, rc_ref, HD_C // 2)
    qc_ref[...] = qc.astype(BF16)
    lane = lax.broadcasted_iota(jnp.int32, k_c.shape, 1)
    k2 = k_c * k_c
    s0 = jnp.sum(jnp.where(lane < HD_C, k2, 0.0), axis=-1, keepdims=True) * (1.0 / HD_C)
    s1 = jnp.sum(jnp.where(lane >= HD_C, k2, 0.0), axis=-1, keepdims=True) * (1.0 / HD_C)
    kcn = k_c * jnp.where(lane < HD_C, lax.rsqrt(s0 + EPS), lax.rsqrt(s1 + EPS)) * g_ref[5:6, :LANE]
    kcf_ref[...] = kcn
    vcf_ref[...] = v_c
    kc_ref[...] = (_rope(kcn, rc_ref, HD_C // 2) if use_rope else kcn).astype(BF16)
    vc_ref[...] = _with_ones(v_c.astype(BF16))


def _chunk_rows_spec(y, tm):
    per = (y.shape[1] - SUB) // tm
    return pl.BlockSpec((None, tm, y.shape[2]), lambda i: (i // per, i % per, 0))


def _premix_call(l, x, yprev, mod, seq, use_rope, wts, rope_tabs):
    t, d = x.shape
    tm = TOKEN_TILE
    ncond = mod.shape[1]
    has_prev = yprev is not None

    def cond(i):
        return (i * tm) // seq if ncond > 1 else 0

    row = lambda w: pl.BlockSpec((tm, w), lambda i: (i, 0))
    ins, specs = [x], [row(d)]
    if has_prev:
        ins += [yprev, mod]
        specs += [_chunk_rows_spec(yprev, tm), pl.BlockSpec((None, None, 6, d), lambda i: (l - 1, cond(i), 0, 0))]
    ins += [mod, wts["w_in"], wts["w_uq"], wts["w_kv"], wts["gains"]]
    specs += [pl.BlockSpec((None, None, 6, d), lambda i: (l, cond(i), 0, 0)),
              pl.BlockSpec((None, d, IN_P), lambda i: (l, 0, 0)),
              pl.BlockSpec((None, Q_RANK, HQ), lambda i: (l, 0, 0)),
              pl.BlockSpec((None, KV_RANK, HQ + W_A), lambda i: (l, 0, 0)),
              pl.BlockSpec((None, 8, HQ), lambda i: (l, 0, 0))]
    if use_rope:
        nt = seq // tm
        ins += [rope_tabs[0], rope_tabs[1]]
        specs += [pl.BlockSpec((3, tm, LANE), lambda i: (0, i % nt, 0))] * 2
    outs = [(d, F32), (HQ, BF16), (HQ, BF16), (2 * W_A, BF16), (KV_RANK, F32), (LANE, F32), (HQ, BF16),
            (LANE, BF16), (2 * LANE, BF16), (LANE, F32), (LANE, F32), (3 * HY_CH, F32)]
    return pl.pallas_call(
        functools.partial(_premix_kernel, has_prev, use_rope),
        out_shape=[jax.ShapeDtypeStruct((t, w), dt) for w, dt in outs],
        grid=(t // tm,),
        in_specs=specs,
        out_specs=[row(w) for w, _ in outs],
        compiler_params=_cp(("parallel",), VMEM_LIMIT),
        name="premix",
    )(*ins)


def _cachekv_kernel(ckv_ref, kpe_ref, w_kv_ref, g_ref, ka_ref, va_ref):
    ka, va = _kv_heads(ckv_ref[...].astype(BF16), kpe_ref[...], w_kv_ref, g_ref[3:4, :], None, False)
    ka_ref[...] = ka.astype(BF16)
    va_ref[...] = _with_ones(va.astype(BF16))


def _cachekv_call(cache_ckv, cache_kpe_p, wts):
    nb, depth, past, _ = cache_ckv.shape
    return pl.pallas_call(
        _cachekv_kernel,
        out_shape=[jax.ShapeDtypeStruct((depth, nb, past, HQ), BF16),
                   jax.ShapeDtypeStruct((depth, nb, past, 2 * W_A), BF16)],
        grid=(depth, nb),
        in_specs=[pl.BlockSpec((None, None, past, KV_RANK), lambda l, b: (b, l, 0, 0)),
                  pl.BlockSpec((None, None, past, LANE), lambda l, b: (b, l, 0, 0)),
                  pl.BlockSpec((None, KV_RANK, HQ + W_A), lambda l, b: (l, 0, 0)),
                  pl.BlockSpec((None, 8, HQ), lambda l, b: (l, 0, 0))],
        out_specs=[pl.BlockSpec((None, None, past, HQ), lambda l, b: (l, b, 0, 0)),
                   pl.BlockSpec((None, None, past, 2 * W_A), lambda l, b: (l, b, 0, 0))],
        compiler_params=_cp(("parallel", "parallel"), VMEM_LIMIT),
        name="cache_kv",
    )(cache_ckv, cache_kpe_p, wts["w_kv"], wts["gains"])


def _nt(q, k):
    return lax.dot_general(q, k, (((1,), (1,)), ((), ())), preferred_element_type=F32)


def _with_ones(v):
    ones = jnp.ones((v.shape[0], LANE), v.dtype)
    parts = []
    for j in range(v.shape[1] // LANE):
        parts += [v[:, j * LANE:(j + 1) * LANE], ones]
    return jnp.concatenate(parts, axis=1)


def _attend(q, ks, vs):
    return _softmax_pv([_nt(q, k) for k in ks], vs)


def _softmax_pv(ss, vs, m=None):
    if m is None:
        m = ss[0].max(axis=-1, keepdims=True)
        for s in ss[1:]:
            m = jnp.maximum(m, s.max(axis=-1, keepdims=True))
    acc = None
    for s, v in zip(ss, vs):
        pv = _dot(jnp.exp2(s - m).astype(BF16), v)
        acc = pv if acc is None else acc + pv
    return acc[:, :LANE] / acc[:, LANE:]


BOUND_SLACK = 1.01
BOUND_LIMIT = 60.0


def _attn_kernel(nseg, bound, *refs):
    qa_ref, qc_ref = refs[0], refs[1]
    if bound:
        kq_ref, refs = refs[2], refs[:2] + refs[3:]
    segs = refs[2:2 + 4 * nseg]
    oa_ref, oc_ref = refs[2 + 4 * nseg:]
    ka_refs, va_refs, kc_refs, vc_refs = (segs[i::4] for i in range(4))
    lane = lax.broadcasted_iota(jnp.int32, (qa_ref.shape[0], LANE), 1)
    low = lane < V_A
    g_per = H_C // KV_C
    jobs = [("a", j, h) for j in range(H_A // 2) for h in (2 * j, 2 * j + 1)]
    jobs += [("c", g, h) for g in range(g_per) for h in (g, g + g_per)]

    def scores(job):
        kind, _, h = job
        hs = slice(h * LANE, (h + 1) * LANE)
        if kind == "a":
            return [_nt(qa_ref[:, hs], r[:, hs]) for r in ka_refs]
        return [_nt(qc_ref[:, hs], r[...]) for r in kc_refs]

    def values(job):
        kind, j, _ = job
        if kind == "a":
            return [r[:, 2 * j * LANE:2 * (j + 1) * LANE] for r in va_refs]
        return [r[...] for r in vc_refs]

    def row_bound(job):
        kind, _, h = job
        q = (qa_ref if kind == "a" else qc_ref)[:, h * LANE:(h + 1) * LANE].astype(F32)
        r = h if kind == "a" else H_A + h // g_per
        q2 = jnp.sum(q * q, axis=-1, keepdims=True)
        return jnp.sqrt(q2 * kq_ref[r:r + 1, 0:1]) * BOUND_SLACK

    ss_next = None if bound else scores(jobs[0])
    pv = []
    for n, job in enumerate(jobs):
        if bound:
            pv.append(_softmax_pv(scores(job), values(job), row_bound(job)))
        else:
            ss = ss_next
            if n + 1 < len(jobs):
                ss_next = scores(jobs[n + 1])
            pv.append(_softmax_pv(ss, values(job)))
        if len(pv) == 2:
            o_ref = oa_ref if job[0] == "a" else oc_ref
            o_ref[:, job[1] * LANE:(job[1] + 1) * LANE] = jnp.where(low, pv[0], pv[1])
            pv = []


def _attn_ctx_call(qa, ka, va, qc, kc, vc, seq):
    t = qa.shape[0]
    blk = lambda w: pl.BlockSpec((seq, w), lambda b: (b, 0))
    return pl.pallas_call(
        functools.partial(_attn_kernel, 1, False),
        out_shape=[jax.ShapeDtypeStruct((t, W_A), F32), jax.ShapeDtypeStruct((t, W_C), F32)],
        grid=(t // seq,),
        in_specs=[blk(HQ), blk(HQ), blk(HQ), blk(2 * W_A), blk(LANE), blk(2 * LANE)],
        out_specs=[blk(W_A), blk(W_C)],
        compiler_params=_cp(("parallel",), VMEM_LIMIT),
        name="attn_ctx",
    )(qa, qc, ka, va, kc, vc)


def _attn_lat_call(l, qa, ka, va, qc, kc, vc, kax, vax, kcx, vcx, seq):
    t = qa.shape[0]
    nb = t // seq
    tq = min(256, seq)
    nq = seq // tq
    past = kax.shape[2]
    qblk = lambda w: pl.BlockSpec((tq, w), lambda b, i: (b * nq + i, 0))
    sblk = lambda w: pl.BlockSpec((seq, w), lambda b, i: (b, 0))
    xblk = lambda w: pl.BlockSpec((None, None, past, w), lambda b, i: (l, b, 0, 0))
    cblk = lambda w: pl.BlockSpec((None, None, past, w), lambda b, i: (b, l, 0, 0))
    full = lambda w: pl.BlockSpec((seq, w), lambda b: (b, 0))

    kq = pl.pallas_call(
        _knorm_kernel,
        out_shape=jax.ShapeDtypeStruct((nb, NORM_ROWS, LANE), F32),
        grid=(nb,),
        in_specs=[full(HQ), pl.BlockSpec((None, None, past, HQ), lambda b: (l, b, 0, 0)),
                  full(LANE), pl.BlockSpec((None, None, past, LANE), lambda b: (b, l, 0, 0)), full(HQ), full(HQ)],
        out_specs=pl.BlockSpec((None, NORM_ROWS, LANE), lambda b: (b, 0, 0)),
        compiler_params=_cp(("parallel",), VMEM_LIMIT),
        name="attn_norms",
    )(ka, kax, kc, kcx, qa, qc)
    k2, q2a, q2c = kq[:, :H_A + KV_C, 0], kq[:, 8:8 + H_A, 0], kq[:, 8 + H_A:8 + H_A + H_C, 0]
    k2c = jnp.repeat(k2[:, H_A:], H_C // KV_C, axis=1)
    worst = jnp.sqrt(jnp.maximum(jnp.max(q2a * k2[:, :H_A]), jnp.max(q2c * k2c))) * BOUND_SLACK

    def call(bound):
        extra_specs = [pl.BlockSpec((None, NORM_ROWS, LANE), lambda b, i: (b, 0, 0))] if bound else []
        return pl.pallas_call(
            functools.partial(_attn_kernel, 2, bound),
            out_shape=[jax.ShapeDtypeStruct((t, W_A), F32), jax.ShapeDtypeStruct((t, W_C), F32)],
            grid=(nb, nq),
            in_specs=[qblk(HQ), qblk(HQ)] + extra_specs
            + [xblk(HQ), xblk(2 * W_A), cblk(LANE), cblk(2 * LANE),
               sblk(HQ), sblk(2 * W_A), sblk(LANE), sblk(2 * LANE)],
            out_specs=[qblk(W_A), qblk(W_C)],
            compiler_params=_cp(("parallel", "parallel"), VMEM_LIMIT),
            name="attn_lat_bound" if bound else "attn_lat",
        )(qa, qc, *([kq] if bound else []), kax, vax, kcx, vcx, ka, va, kc, vc)

    return lax.cond(worst < BOUND_LIMIT, lambda: call(True), lambda: call(False))


NORM_ROWS = 24


def _knorm_kernel(ka_ref, kax_ref, kc_ref, kcx_ref, qa_ref, qc_ref, o_ref):
    def max_n2(x, lanes=None):
        xf = x.astype(F32)
        x2 = xf * xf
        if lanes is not None:
            lane = lax.broadcasted_iota(jnp.int32, x2.shape, 1)
            x2 = jnp.where((lane >= lanes[0]) & (lane < lanes[1]), x2, 0.0)
        return jnp.max(jnp.sum(x2, axis=-1, keepdims=True), axis=0, keepdims=True)

    rows = []
    for h in range(H_A):
        hs = slice(h * LANE, (h + 1) * LANE)
        rows.append(jnp.maximum(max_n2(ka_ref[:, hs]), max_n2(kax_ref[:, hs])))
    for kv in range(KV_C):
        lanes = (kv * HD_C, (kv + 1) * HD_C)
        rows.append(jnp.maximum(max_n2(kc_ref[...], lanes), max_n2(kcx_ref[...], lanes)))
    rows += [max_n2(qa_ref[:, h * LANE:(h + 1) * LANE]) for h in range(H_A)]
    rows += [max_n2(qc_ref[:, h * LANE:(h + 1) * LANE]) for h in range(H_C)]
    o_ref[...] = jnp.zeros_like(o_ref)
    for r, v in enumerate(rows):
        o_ref[r:r + 1, :] = jnp.broadcast_to(v, (1, LANE))


def _filter_mlp(z, w_in_ref, b_in_ref, w_mid_ref, b_mid_ref, w_out_ref, b_out_ref, freq_ref):
    freq = freq_ref[...]
    a = jnp.sin(freq * (_dot3(z, w_in_ref[...]) + b_in_ref[...]))
    for i in range(HY_INNER):
        a = jnp.sin(freq * (_dot3(a, w_mid_ref[i]) + b_mid_ref[i]))
    return _dot3(a, w_out_ref[...]) + b_out_ref[...]


def _conv3(u, up, dn, cw_ref, cb_ref):
    return up * cw_ref[0:1, :] + u * cw_ref[1:2, :] + dn * cw_ref[2:3, :] + cb_ref[...]


def _filt_ctx_kernel(z_ref, dec_ref, w_in_ref, b_in_ref, w_mid_ref, b_mid_ref, w_out_ref, b_out_ref, freq_ref,
                     c_ref, s_ref, o_ref):
    n = z_ref.shape[0]
    h = _filter_mlp(z_ref[...], w_in_ref, b_in_ref, w_mid_ref, b_mid_ref, w_out_ref, b_out_ref, freq_ref)
    dec = dec_ref[...]
    row = lax.broadcasted_iota(jnp.int32, dec.shape, 0)
    half = HY_ORDER * HY_CH
    for o in range(HY_ORDER):
        hf = h[:, o * HY_CH:(o + 1) * HY_CH] * dec
        hb = jnp.where(row > 0, h[:, half + o * HY_CH:half + (o + 1) * HY_CH] * dec, 0.0)
        nrm = jnp.sum(jnp.abs(hf) + jnp.abs(hb), axis=0, keepdims=True) + EPS
        scale = (1.0 / n) / nrm
        o_ref[o, 0] = _dot3(c_ref[...], hf + hb) * scale
        o_ref[o, 1] = -_dot3(s_ref[...], hf - hb) * scale


def _filt_ctx_call(l, z, dec, hw, cmat, smat):
    n = z.shape[0]
    full = lambda a: pl.BlockSpec((None,) + a.shape[1:], lambda i: (l,) + (0,) * (a.ndim - 1))
    const = lambda a: pl.BlockSpec(a.shape, lambda i: (0,) * a.ndim)
    names = ["hy_w_in", "hy_b_in", "hy_w_mid", "hy_b_mid", "hy_w_out", "hy_b_out", "hy_freq"]
    return pl.pallas_call(
        _filt_ctx_kernel,
        out_shape=jax.ShapeDtypeStruct((HY_ORDER, 2, n, HY_CH), F32),
        grid=(1,),
        in_specs=[const(z), const(dec)] + [full(hw[k]) for k in names] + [const(cmat), const(smat)],
        out_specs=pl.BlockSpec((HY_ORDER, 2, n, HY_CH), lambda i: (0, 0, 0, 0)),
        compiler_params=_cp(("arbitrary",), VMEM_LIMIT),
        name="hy_filter_ctx",
    )(z, dec, *[hw[k] for k in names], cmat, smat)


def _hy_ctx_kernel(u_ref, cw_ref, cb_ref, skip_ref, kf_ref, fwd_ref, inv_ref, o_ref):
    u = u_ref[...]
    n = u.shape[0]
    row = lax.broadcasted_iota(jnp.int32, u.shape, 0)
    up = jnp.where(row > 0, pltpu.roll(u, 1, 0), 0.0)
    dn = jnp.where(row < n - 1, pltpu.roll(u, n - 1, 0), 0.0)
    z = _conv3(u, up, dn, cw_ref, cb_ref)
    s = z[:, :HY_CH]
    gates = (z[:, HY_CH:2 * HY_CH], z[:, 2 * HY_CH:])
    for o in range(HY_ORDER):
        xs = _dot(fwd_ref[...], s.astype(BF16))
        xr, xi = xs[:n], xs[n:]
        kr, ki = kf_ref[o, 0], kf_ref[o, 1]
        ycat = jnp.concatenate([xr * kr - xi * ki, xr * ki + xi * kr], axis=0)
        y = _dot(inv_ref[...], ycat.astype(BF16))
        s = gates[o] * (y + s * skip_ref[o:o + 1, :])
    o_ref[...] = s


def _hy_ctx_call(l, u, hw, kf, fwd, inv, seq):
    t = u.shape[0]
    full = lambda a: pl.BlockSpec((None,) + a.shape[1:], lambda b: (l,) + (0,) * (a.ndim - 1))
    const = lambda a: pl.BlockSpec(a.shape, lambda b: (0,) * a.ndim)
    return pl.pallas_call(
        _hy_ctx_kernel,
        out_shape=jax.ShapeDtypeStruct((t, HY_CH), F32),
        grid=(t // seq,),
        in_specs=[pl.BlockSpec((seq, 3 * HY_CH), lambda b: (b, 0)),
                  full(hw["hy_conv_w"]), full(hw["hy_conv_b"]), full(hw["hy_skip"]),
                  const(kf), const(fwd), const(inv)],
        out_specs=pl.BlockSpec((seq, HY_CH), lambda b: (b, 0)),
        compiler_params=_cp(("parallel",), VMEM_LIMIT),
        name="hyena_ctx",
    )(u, hw["hy_conv_w"], hw["hy_conv_b"], hw["hy_skip"], kf, fwd, inv)


def _filt_lat_kernel(seq, z_ref, dec_ref, w_in_ref, b_in_ref, w_mid_ref, b_mid_ref, w_out_ref, b_out_ref, freq_ref,
                     k_ref, n_ref):
    i = pl.program_id(0)
    tr = z_ref.shape[0]
    h = _filter_mlp(z_ref[...], w_in_ref, b_in_ref, w_mid_ref, b_mid_ref, w_out_ref, b_out_ref, freq_ref)
    half = HY_ORDER * HY_CH
    row = i * tr + lax.broadcasted_iota(jnp.int32, (tr, half), 0)
    dec = dec_ref[...]
    kern = jnp.where(row < seq, h[:, :half], h[:, half:]) * jnp.concatenate([dec] * HY_ORDER, axis=1)
    k_ref[...] = kern

    @pl.when(i == 0)
    def _():
        n_ref[...] = jnp.zeros_like(n_ref)

    n_ref[...] += jnp.sum(jnp.abs(kern), axis=0, keepdims=True)


def _filt_lat_call(l, z, dec, hw, seq):
    n = z.shape[0]
    tr = min(512, n)
    half = HY_ORDER * HY_CH
    full = lambda a: pl.BlockSpec((None,) + a.shape[1:], lambda i: (l,) + (0,) * (a.ndim - 1))
    names = ["hy_w_in", "hy_b_in", "hy_w_mid", "hy_b_mid", "hy_w_out", "hy_b_out", "hy_freq"]
    return pl.pallas_call(
        functools.partial(_filt_lat_kernel, seq),
        out_shape=[jax.ShapeDtypeStruct((n, half), F32), jax.ShapeDtypeStruct((1, half), F32)],
        grid=(n // tr,),
        in_specs=[pl.BlockSpec((tr, z.shape[1]), lambda i: (i, 0)), pl.BlockSpec((tr, HY_CH), lambda i: (i, 0))]
        + [full(hw[k]) for k in names],
        out_specs=[pl.BlockSpec((tr, half), lambda i: (i, 0)), pl.BlockSpec((1, half), lambda i: (0, 0))],
        compiler_params=_cp(("arbitrary",), VMEM_LIMIT),
        name="hy_filter_lat",
    )(z, dec, *[hw[k] for k in names])


def _fa_kernel(f1_ref, k_ref, o_ref):
    r = _dot3(f1_ref[...], k_ref[...])
    o_ref[0] = r[:FFT_N1]
    o_ref[1] = r[FFT_N1:]


def _fa_call(f1f, kern2d):
    n1, w = kern2d.shape
    tn = min(2048, w)
    return pl.pallas_call(
        _fa_kernel,
        out_shape=jax.ShapeDtypeStruct((2, FFT_N1, w), F32),
        grid=(w // tn,),
        in_specs=[pl.BlockSpec(f1f.shape, lambda j: (0, 0)), pl.BlockSpec((n1, tn), lambda j: (0, j))],
        out_specs=pl.BlockSpec((2, FFT_N1, tn), lambda j: (0, 0, j)),
        compiler_params=_cp(("parallel",), VMEM_LIMIT),
        name="hy_filter_dft1",
    )(f1f, kern2d)


def _fb_kernel(n_total, a_ref, fh_ref, fl_ref, n_ref, o_ref):
    a = jnp.concatenate([a_ref[0], a_ref[1]], axis=1)
    ah, al = _split(a)
    x = _bdot(fh_ref[...], ah) + _bdot(fh_ref[...], al) + _bdot(fl_ref[...], ah)
    scale = (1.0 / n_total) / (n_ref[...] + EPS)
    o_ref[...] = x * scale[None]


def _fb_call(af5, f2h, f2l, nrm, n_total):
    _, n1, n2, c = af5.shape[0], af5.shape[1], af5.shape[2], af5.shape[3]
    k1t = 8
    return pl.pallas_call(
        functools.partial(_fb_kernel, n_total),
        out_shape=jax.ShapeDtypeStruct((n1, 2 * n2, c), F32),
        grid=(n1 // k1t,),
        in_specs=[pl.BlockSpec((2, k1t, n2, c), lambda j: (0, j, 0, 0)),
                  pl.BlockSpec((k1t, 2 * n2, 2 * n2), lambda j: (j, 0, 0)),
                  pl.BlockSpec((k1t, 2 * n2, 2 * n2), lambda j: (j, 0, 0)),
                  pl.BlockSpec((1, c), lambda j: (0, 0))],
        out_specs=pl.BlockSpec((k1t, 2 * n2, c), lambda j: (j, 0, 0)),
        compiler_params=_cp(("parallel",), VMEM_LIMIT),
        name="hy_filter_dft2",
    )(af5, f2h, f2l, nrm)


def _hconv_kernel(seq, u_ref, p_ref, n_ref, cw_ref, cb_ref, v_ref, x1_ref, x2_ref):
    i = pl.program_id(0)
    u = u_ref[...]
    tt = u.shape[0]
    row = lax.broadcasted_iota(jnp.int32, u.shape, 0)
    pos = (i * tt) % seq
    prev = jnp.where(pos > 0, p_ref[7:8, :], 0.0)
    nxt = jnp.where(pos + tt < seq, n_ref[0:1, :], 0.0)
    up = jnp.where(row > 0, pltpu.roll(u, 1, 0), prev)
    dn = jnp.where(row < tt - 1, pltpu.roll(u, tt - 1, 0), nxt)
    z = _conv3(u, up, dn, cw_ref, cb_ref)
    v_ref[...] = z[:, :HY_CH]
    x1_ref[...] = z[:, HY_CH:2 * HY_CH]
    x2_ref[...] = z[:, 2 * HY_CH:]


def _hconv_call(l, u, hw, seq):
    t, w = u.shape
    tt = min(512, seq)
    nblk8 = t // 8
    full = lambda a: pl.BlockSpec((None,) + a.shape[1:], lambda i: (l,) + (0,) * (a.ndim - 1))
    ob = pl.BlockSpec((tt, HY_CH), lambda i: (i, 0))
    return pl.pallas_call(
        functools.partial(_hconv_kernel, seq),
        out_shape=[jax.ShapeDtypeStruct((t, HY_CH), F32)] * 3,
        grid=(t // tt,),
        in_specs=[pl.BlockSpec((tt, w), lambda i: (i, 0)),
                  pl.BlockSpec((8, w), lambda i: (jnp.maximum(i * (tt // 8) - 1, 0), 0)),
                  pl.BlockSpec((8, w), lambda i: (jnp.minimum((i + 1) * (tt // 8), nblk8 - 1), 0)),
                  full(hw["hy_conv_w"]), full(hw["hy_conv_b"])],
        out_specs=[ob, ob, ob],
        compiler_params=_cp(("parallel",), VMEM_LIMIT),
        name="hyena_conv3",
    )(u, u, u, hw["hy_conv_w"], hw["hy_conv_b"])


def _ha_kernel(f1_ref, x_ref, o_ref):
    r = _dot(f1_ref[...], x_ref[...].astype(BF16))
    o_ref[0] = r[:FFT_N1].astype(BF16)
    o_ref[1] = r[FFT_N1:].astype(BF16)


def _ha_call(f1d, x2d, nb):
    rows, w = x2d.shape
    n1h = rows // nb
    tn = min(2048, w)
    return pl.pallas_call(
        _ha_kernel,
        out_shape=jax.ShapeDtypeStruct((nb, 2, FFT_N1, w), BF16),
        grid=(nb, w // tn),
        in_specs=[pl.BlockSpec(f1d.shape, lambda b, j: (0, 0)), pl.BlockSpec((n1h, tn), lambda b, j: (b, j))],
        out_specs=pl.BlockSpec((None, 2, FFT_N1, tn), lambda b, j: (b, 0, 0, j)),
        compiler_params=_cp(("parallel", "parallel"), VMEM_LIMIT),
        name="hyena_dft1",
    )(f1d, x2d)


def _hb_kernel(a_ref, f_ref, g_ref, kf_ref, o_ref):
    n2 = a_ref.shape[2]
    a = jnp.concatenate([a_ref[0], a_ref[1]], axis=1)
    x = _bdot(f_ref[...], a)
    xr, xi = x[:, :n2], x[:, n2:]
    kr, ki = kf_ref[:, :n2], kf_ref[:, n2:]
    y = jnp.concatenate([xr * kr - xi * ki, xr * ki + xi * kr], axis=1).astype(BF16)
    b = _bdot(g_ref[...], y)
    o_ref[0] = b[:, :n2].astype(BF16)
    o_ref[1] = b[:, n2:].astype(BF16)


def _hb_call(o, a5, f2, g2, kf):
    nb, _, n1, n2, c = a5.shape
    k1t = 16
    blk = pl.BlockSpec((None, 2, k1t, n2, c), lambda b, j: (b, 0, j, 0, 0))
    mat = pl.BlockSpec((k1t, 2 * n2, 2 * n2), lambda b, j: (j, 0, 0))
    return pl.pallas_call(
        _hb_kernel,
        out_shape=jax.ShapeDtypeStruct(a5.shape, BF16),
        grid=(nb, n1 // k1t),
        in_specs=[blk, mat, mat, pl.BlockSpec((k1t, 2 * n2, c), lambda b, j: (j, 0, o))],
        out_specs=blk,
        compiler_params=_cp(("parallel", "parallel"), VMEM_LIMIT),
        name="hyena_dft2",
    )(a5, f2, g2, kf)


def _hc_kernel(o, fc_ref, b_ref, s_ref, g_ref, skip_ref, o_ref):
    bcat = jnp.concatenate([b_ref[0], b_ref[1]], axis=0)
    y = _dot(fc_ref[...], bcat)
    o_ref[...] = g_ref[...] * (y + s_ref[...] * skip_ref[o:o + 1, :])


def _hc_call(l, o, fc, b4, s2d, g2d, skip_t):
    nb, _, n1, w = b4.shape
    rows = s2d.shape[0] // nb
    tn = min(2048, w)
    blk = pl.BlockSpec((rows, tn), lambda b, j: (b, j))
    return pl.pallas_call(
        functools.partial(_hc_kernel, o),
        out_shape=jax.ShapeDtypeStruct(s2d.shape, F32),
        grid=(nb, w // tn),
        in_specs=[pl.BlockSpec(fc.shape, lambda b, j: (0, 0)),
                  pl.BlockSpec((None, 2, n1, tn), lambda b, j: (b, 0, 0, j)),
                  blk, blk,
                  pl.BlockSpec((None, HY_ORDER, tn), lambda b, j: (l, 0, j))],
        out_specs=blk,
        compiler_params=_cp(("parallel", "parallel"), VMEM_LIMIT),
        name="hyena_dft3",
    )(fc, b4, s2d, g2d, skip_t)


def _postmix_kernel(x_ref, oa_ref, ob_ref, oc_ref, mod_ref, g_ref, wa_ref, wb_ref, wc_ref, wrh_ref, wrl_ref, br_ref,
                    x1_ref, h2_ref, idx_ref, gate_ref):
    na = _rms(oa_ref[...]) * g_ref[:, :W_A]
    nb = _rms(ob_ref[...]) * g_ref[:, W_A:W_A + W_B]
    nc = _rms(oc_ref[...]) * g_ref[:, W_A + W_B:]
    mix = (_dot(na.astype(BF16), wa_ref[...]) + _dot(nb.astype(BF16), wb_ref[...])
           + _dot(nc.astype(BF16), wc_ref[...]))
    x1 = x_ref[...] + mod_ref[2:3, :] * mix
    x1_ref[...] = x1
    h2 = _rms(x1) * (1.0 + mod_ref[4:5, :]) + mod_ref[3:4, :]
    h2_ref[...] = h2
    hh, hl = _split(h2)
    vals = _dot(hh, wrh_ref[...]) + _dot(hh, wrl_ref[...]) + _dot(hl, wrh_ref[...]) + br_ref[...]
    lane = lax.broadcasted_iota(jnp.int32, vals.shape, 1).astype(F32)
    idx_out = jnp.zeros(vals.shape, F32)
    top = jnp.zeros(vals.shape, F32)
    m0 = None
    for k in range(TOP_K):
        m = vals.max(axis=-1, keepdims=True)
        sel = jnp.min(jnp.where(vals == m, lane, float(LANE)), axis=-1, keepdims=True)
        if m0 is None:
            m0 = m
        idx_out = jnp.where(lane == k, sel, idx_out)
        top = jnp.where(lane == k, jnp.exp(m - m0), top)
        vals = jnp.where(lane == sel, -jnp.inf, vals)
    idx_ref[...] = idx_out.astype(jnp.int32)
    gate_ref[...] = top / jnp.sum(top, axis=-1, keepdims=True)


def _postmix_call(l, x, oa, ob, oc, mod, seq, wts):
    t, d = x.shape
    tm = TOKEN_TILE
    ncond = mod.shape[1]

    def cond(i):
        return (i * tm) // seq if ncond > 1 else 0

    row = lambda w: pl.BlockSpec((tm, w), lambda i: (i, 0))
    lay = lambda a: pl.BlockSpec((None,) + a.shape[1:], lambda i: (l,) + (0,) * (a.ndim - 1))
    names = ["g_out", "wo_a", "wo_b", "wo_c", "wr_hi", "wr_lo", "b_router"]
    return pl.pallas_call(
        _postmix_kernel,
        out_shape=[jax.ShapeDtypeStruct((t, d), F32), jax.ShapeDtypeStruct((t, d), F32),
                   jax.ShapeDtypeStruct((t, LANE), jnp.int32), jax.ShapeDtypeStruct((t, LANE), F32)],
        grid=(t // tm,),
        in_specs=[row(d), row(W_A), row(W_B), row(W_C),
                  pl.BlockSpec((None, None, 6, d), lambda i: (l, cond(i), 0, 0))] + [lay(wts[k]) for k in names],
        out_specs=[row(d), row(d), row(LANE), row(LANE)],
        compiler_params=_cp(("parallel",), VMEM_LIMIT),
        name="postmix",
    )(x, oa, ob, oc, mod, *[wts[k] for k in names])


SUB = 8


NGRP = ROW_BLOCK // SUB


def _moe_kernel(be_ref, rs_ref, nv_ref, nu_ref, tok_ref, gate_ref, h_ref, wi_ref, bi_ref, wo_ref, bo_ref, y_ref,
                buf, obuf, xb):
    s = pl.program_id(0)
    tc = h_ref.shape[0]

    @pl.when(s == 0)
    def _():
        y_ref[...] = jnp.zeros_like(y_ref)
        buf[...] = jnp.zeros_like(buf)
        obuf[...] = jnp.zeros_like(obuf)

    @pl.when(s < nu_ref[0] + 2)
    def _():
        slot_g = lax.rem(s, 2)
        slot_c = 1 - slot_g

        xb[...] = buf[slot_c].reshape(ROW_BLOCK, buf.shape[3]).astype(BF16)

        base_s = rs_ref[s]
        nv = nv_ref[s]
        for g in range(NGRP):
            ts = [jnp.where(g * SUB + j < nv, tok_ref[base_s + g * SUB + j], tc) for j in range(SUB)]
            new = [y_ref[pl.ds(ts[j], 1), :] + gate_ref[base_s + g * SUB + j] * obuf[slot_g, g, j:j + 1, :]
                   for j in range(SUB)]
            for j in range(SUB):
                y_ref[pl.ds(ts[j], 1), :] = new[j]

        base_g = rs_ref[s + 2]
        for g in range(NGRP):
            rows = [h_ref[pl.ds(tok_ref[base_g + g * SUB + j], 1), :] for j in range(SUB)]
            for j in range(SUB):
                buf[slot_g, g, j:j + 1, :] = rows[j]

        gu = _dot(xb[...], wi_ref[...]) + bi_ref[...]
        gt = jnp.minimum(gu[:, :D_FF], SWIGLU_LIMIT)
        lin = jnp.clip(gu[:, D_FF:], -SWIGLU_LIMIT, SWIGLU_LIMIT)
        act = (lin + 1.0) * gt * (1.0 / (1.0 + jnp.exp(-SWIGLU_ALPHA * gt)))
        out = _dot(act.astype(BF16), wo_ref[...]) + bo_ref[...]
        obuf[slot_c] = out.reshape(obuf.shape[1:])


def _moe_kernel_entry(has_alias, *refs):
    refs = list(refs)
    if has_alias:
        del refs[11]
    _moe_kernel(*refs)


def _route(idx, gates, n_blocks):
    m = idx.shape[0] * TOP_K
    e = idx.reshape(m)
    flat = jnp.arange(m, dtype=jnp.int32)
    skey, gate = lax.sort((e * m + flat, gates.reshape(m)), num_keys=1)
    tok = (skey % m) // TOP_K
    experts = jnp.arange(N_EXPERTS, dtype=jnp.int32)
    cnt = jnp.sum((e[:, None] == experts[None, :]).astype(jnp.int32), axis=0)
    nblk = (cnt + ROW_BLOCK - 1) // ROW_BLOCK
    bend = jnp.cumsum(nblk)
    n_used = bend[-1]
    blk = jnp.arange(-2, n_blocks + 2, dtype=jnp.int32)
    bcl = jnp.clip(blk, 0, n_used - 1)
    be = jnp.sum((bend[None, :] <= bcl[:, None]).astype(jnp.int32), axis=1)
    oh = (be[:, None] == experts[None, :]).astype(jnp.int32)
    pick = lambda v: jnp.sum(oh * v[None, :], axis=1)
    off = (bcl - pick(bend - nblk)) * ROW_BLOCK
    valid = (blk >= 0) & (blk < n_used)
    rs = jnp.where(valid, pick(jnp.cumsum(cnt) - cnt) + off, 0)
    nv = jnp.where(valid, jnp.clip(pick(cnt) - off, 0, ROW_BLOCK), 0)
    pad = jnp.zeros((ROW_BLOCK,), jnp.int32)
    return (be[1:n_blocks + 3], rs, nv, n_used.reshape(1), jnp.concatenate([tok, pad]),
            jnp.concatenate([gate, pad.astype(F32)]))


def _moe_call(l, h2, idx, gates, wts):
    t, d = h2.shape
    tc = min(4096, t)
    n_blocks = tc * TOP_K // ROW_BLOCK + N_EXPERTS
    y = None
    for c in range(t // tc):
        route = _route(idx[c * tc:(c + 1) * tc], gates[c * tc:(c + 1) * tc], n_blocks)
        wspec = lambda r, w: pl.BlockSpec((None, None, r, w), lambda s, be, *_: (l, be[s], 0, 0))
        in_specs = [pl.BlockSpec((tc, d), lambda i, *_: (c, 0), pipeline_mode=pl.Buffered(1)),
                    wspec(d, 2 * D_FF), wspec(1, 2 * D_FF), wspec(D_FF, d), wspec(1, d)]
        args = [h2, wts["w_moe_in"], wts["b_moe_in"], wts["w_moe_out"], wts["b_moe_out"]]
        aliases = {}
        if y is not None:
            in_specs.append(pl.BlockSpec(memory_space=pl.ANY))
            args.append(y)
            aliases = {len(route) + len(args) - 1: 0}
        grid_spec = pltpu.PrefetchScalarGridSpec(
            num_scalar_prefetch=len(route),
            grid=(n_blocks + 2,),
            in_specs=in_specs,
            out_specs=pl.BlockSpec((None, tc + SUB, d), lambda i, *_: (c, 0, 0), pipeline_mode=pl.Buffered(1)),
            scratch_shapes=[pltpu.VMEM((2, NGRP, SUB, d), F32), pltpu.VMEM((2, NGRP, SUB, d), F32),
                            pltpu.VMEM((ROW_BLOCK, d), BF16)],
        )
        y = pl.pallas_call(
            functools.partial(_moe_kernel_entry, y is not None),
            out_shape=jax.ShapeDtypeStruct((t // tc, tc + SUB, d), F32),
            grid_spec=grid_spec,
            input_output_aliases=aliases,
            compiler_params=_cp(("arbitrary",), VMEM_LIMIT),
            name="moe_experts",
        )(*route, *args)
    return y


def _final_kernel(x_ref, y_ref, mod_ref, o_ref):
    o_ref[...] = x_ref[...] + mod_ref[5:6, :] * y_ref[...]


def _final_call(l, x, y, mod, seq):
    t, d = x.shape
    tm = TOKEN_TILE
    ncond = mod.shape[1]
    row = pl.BlockSpec((tm, d), lambda i: (i, 0))
    return pl.pallas_call(
        _final_kernel,
        out_shape=jax.ShapeDtypeStruct((t, d), F32),
        grid=(t // tm,),
        in_specs=[row, _chunk_rows_spec(y, tm),
                  pl.BlockSpec((None, None, 6, d), lambda i: (l, (i * tm) // seq if ncond > 1 else 0, 0, 0))],
        out_specs=row,
        compiler_params=_cp(("parallel",), VMEM_LIMIT),
        name="final_residual",
    )(x, y, mod)


def _rope_tables(seq, rot_dim, lane_map):
    n_rows = seq // GRID_W
    rows = jnp.repeat(jnp.arange(n_rows, dtype=F32), GRID_W)
    cols = jnp.tile(jnp.arange(GRID_W, dtype=F32), n_rows)
    axis_dim = rot_dim // 2
    inv_freq = ROPE_THETA ** (-jnp.arange(0, axis_dim, 2, dtype=F32) / axis_dim)
    ang = jnp.concatenate([rows[:, None] * inv_freq, cols[:, None] * inv_freq], axis=-1)
    cos, sin = jnp.cos(ang), jnp.sin(ang)
    pair = np.zeros((LANE,), np.int32)
    in_rot = np.zeros((LANE,), np.float32)
    first = np.zeros((LANE,), np.float32)
    for ln in range(LANE):
        m = lane_map(ln)
        if m is not None:
            pair[ln], in_rot[ln], first[ln] = m[0], 1.0, 1.0 if m[1] == 0 else 0.0
    c = jnp.where(in_rot[None, :] > 0, cos[:, pair], 1.0)
    s = sin[:, pair] * in_rot[None, :]
    return jnp.stack([c, -s * first[None, :], s * (1.0 - first[None, :])]).astype(F32)


def _lane_map_a(ln):
    o = ln - NOPE_A
    if 0 <= o < ROPE_A:
        return (o % (ROPE_A // 2), o // (ROPE_A // 2))
    return None


def _lane_map_c(ln):
    o = ln % HD_C
    return (o % (HD_C // 2), o // (HD_C // 2))


def _phase(num, den):
    ang = (2.0 * math.pi / den) * (num % den).astype(F32)
    return jnp.cos(ang), jnp.sin(ang)


def _ctx_dft(n):
    f = jnp.arange(n, dtype=jnp.int32)[:, None]
    t = jnp.arange(n, dtype=jnp.int32)[None, :]
    c, s = _phase((2 * f + 1) * t, 4 * n)
    fwd = jnp.concatenate([c, -s], axis=0).astype(BF16)
    inv = jnp.concatenate([c.T, -s.T], axis=1).astype(BF16)
    return c, s, fwd, inv


def _lat_dft(seq):
    n = 2 * seq
    n1, n2 = FFT_N1, n // FFT_N1
    k1 = jnp.arange(n1, dtype=jnp.int32)
    c1, s1 = _phase(k1[:, None] * k1[None, :], n1)
    f1f = jnp.concatenate([c1, -s1], axis=0)
    f1d = f1f[:, :n1 // 2].astype(BF16)
    fc = jnp.concatenate([c1[:, :n1 // 2].T, -s1[:, :n1 // 2].T], axis=1).astype(BF16)
    k2 = jnp.arange(n2, dtype=jnp.int32)
    num = (k2[None, :, None] * k2[None, None, :]) * n1 + k2[None, None, :] * k1[:, None, None]
    cm, sm = _phase(num, n)
    mr, mi = cm, -sm
    f2 = jnp.concatenate([jnp.concatenate([mr, -mi], axis=2), jnp.concatenate([mi, mr], axis=2)], axis=1)
    mrt, mit = jnp.swapaxes(mr, 1, 2), jnp.swapaxes(mi, 1, 2)
    g2 = jnp.concatenate([jnp.concatenate([mrt, mit], axis=2), jnp.concatenate([-mit, mrt], axis=2)], axis=1)
    f2h, f2l = _split(f2)
    return dict(f1f=f1f, f1d=f1d, fc=fc, f2=f2h, f2l=f2l, g2=g2.astype(BF16), n1=n1, n2=n2, n=n)


def _hy_features(seq):
    t = jnp.linspace(0.0, 1.0, seq, dtype=F32)[:, None]
    bands = (HY_EMB - 1) // 2
    f = jnp.linspace(1e-4, bands - 1, bands, dtype=F32)[None, :]
    w = 2.0 * math.pi * jnp.arange(seq, dtype=F32)[:, None] / seq
    z = jnp.concatenate([t, jnp.cos(f * w), jnp.sin(f * w)], axis=-1)
    z = jnp.pad(z, ((0, 0), (0, HY_FO - HY_EMB)))
    deltas = jnp.abs(jnp.linspace(HY_MIN_DECAY, HY_MAX_DECAY, HY_CH, dtype=F32))
    return z, jnp.exp(-t * deltas)


def _prep_weights(w_in, mla_g_qa, mla_w_uq, mla_g_kva, mla_w_ukv, mla_g_q, mla_g_k, gqa_g_q, gqa_g_k,
                  g_out, w_out, w_router, b_router, hy_w_in):
    depth = w_in.shape[0]
    cuts = np.cumsum([0, Q_RANK, KV_RANK, ROPE_A, 3 * HY_CH, H_C * HD_C, KV_C * HD_C, KV_C * HD_C])
    wb = w_in.astype(BF16)
    zeros = lambda w: jnp.zeros(wb.shape[:2] + (w,), BF16)
    parts = [wb[:, :, cuts[0]:cuts[2]], zeros(NOPE_A), wb[:, :, cuts[2]:cuts[3]], zeros(LANE - QK_A),
             wb[:, :, cuts[3]:cuts[4]]]
    for h in range(H_C):
        wh = wb[:, :, cuts[4] + h * HD_C:cuts[4] + (h + 1) * HD_C]
        parts += [wh, zeros(HD_C)] if h < H_C // KV_C else [zeros(HD_C), wh]
    w_in_p = jnp.concatenate(parts + [wb[:, :, cuts[5]:cuts[7]]], axis=-1)

    w_uq = jnp.pad(mla_w_uq.reshape(depth, Q_RANK, H_A, QK_A), ((0, 0), (0, 0), (0, 0), (0, LANE - QK_A)))
    w_uq = w_uq.reshape(depth, Q_RANK, HQ).astype(BF16)
    ukv = mla_w_ukv.reshape(depth, KV_RANK, H_A, NOPE_A + V_A)
    wk = jnp.pad(ukv[..., :NOPE_A], ((0, 0), (0, 0), (0, 0), (0, LANE - NOPE_A))).reshape(depth, KV_RANK, HQ)
    wv = ukv[..., NOPE_A:].reshape(depth, KV_RANK, W_A)
    w_kv = jnp.concatenate([wk, wv], axis=-1).astype(BF16)

    def pad_row(v):
        return jnp.pad(v, ((0, 0), (0, HQ - v.shape[1])))

    head_a = lambda g: jnp.tile(jnp.pad(g, ((0, 0), (0, LANE - QK_A))), (1, H_A))
    head_c = lambda g, reps: jnp.tile(g, (1, reps))
    gains = jnp.stack([
        pad_row(mla_g_qa), pad_row(mla_g_kva),
        head_a(mla_g_q) * (QK_A ** -0.5 * LOG2E), head_a(mla_g_k),
        head_c(gqa_g_q, 2 * H_C) * (HD_C ** -0.5 * LOG2E), pad_row(head_c(gqa_g_k, 2)),
        jnp.zeros((depth, HQ), F32), jnp.zeros((depth, HQ), F32)], axis=1).astype(F32)

    g_per = H_C // KV_C
    perm_c = np.concatenate([np.arange(h * HD_C, (h + 1) * HD_C) for g in range(g_per) for h in (g, g + g_per)])
    rows_c = W_A + W_B + perm_c
    g_o = jnp.concatenate([g_out[:, :W_A + W_B], g_out[:, rows_c]], axis=1).reshape(depth, 1, -1)
    wr = jnp.pad(w_router, ((0, 0), (0, 0), (0, LANE - N_EXPERTS)))
    wr_hi, wr_lo = _split(wr)
    br = jnp.pad(b_router, ((0, 0), (0, LANE - N_EXPERTS)), constant_values=-1e30).reshape(depth, 1, LANE)
    return dict(w_in=w_in_p, w_uq=w_uq, w_kv=w_kv, gains=gains, g_out=g_o,
                wo_a=w_out[:, :W_A].astype(BF16), wo_b=w_out[:, W_A:W_A + W_B].astype(BF16),
                wo_c=w_out[:, rows_c].astype(BF16), wr_hi=wr_hi, wr_lo=wr_lo, b_router=br,
                hy_w_in=jnp.pad(hy_w_in, ((0, 0), (0, HY_FO - HY_EMB), (0, 0))))


def _hyena_lat(l, u, hw, kf, dft, seq):
    t = u.shape[0]
    nb = t // seq
    n1, n2 = dft["n1"], dft["n2"]
    w2 = n2 * HY_CH
    v, x1, x2 = _hconv_call(l, u, hw, seq)
    to2d = lambda a: a.reshape(t // n2, w2)
    s = to2d(v)
    for o, gate in enumerate((x1, x2)):
        a = _ha_call(dft["f1d"], s, nb)
        b = _hb_call(o, a.reshape(nb, 2, n1, n2, HY_CH), dft["f2"], dft["g2"], kf)
        s = _hc_call(l, o, dft["fc"], b.reshape(nb, 2, n1, w2), s, to2d(gate), hw["skip_t"])
    return s.reshape(t, HY_CH)


def _filter_lat(l, z, dec, hw, dft, seq):
    kern, nrm = _filt_lat_call(l, z, dec, hw, seq)
    n1, n2, n = dft["n1"], dft["n2"], dft["n"]
    c = kern.shape[1]
    af = _fa_call(dft["f1f"], kern.reshape(n1, n2 * c))
    return _fb_call(af.reshape(2, n1, n2, c), dft["f2"], dft["f2l"], nrm, n)


def kernel(x_prompt, x_sample, c, c_ctx, cache_mla_ckv, cache_mla_kpe, cache_gqa_k, cache_gqa_v, w_mod, b_mod, w_in, mla_g_qa, mla_w_uq, mla_g_kva, mla_w_ukv, mla_g_q, mla_g_k, hy_conv_w, hy_conv_b, hy_w_in, hy_b_in, hy_w_mid, hy_b_mid, hy_w_out, hy_b_out, hy_freq, hy_skip, gqa_g_q, gqa_g_k, g_out, w_out, w_router, b_router, w_moe_in, b_moe_in, w_moe_out, b_moe_out):
    batch, seq_c, d = x_prompt.shape
    nb_l, seq_l, _ = x_sample.shape
    depth = w_in.shape[0]
    past = cache_mla_ckv.shape[2]
    assert d == D_MODEL and seq_l % GRID_W == 0 and (2 * seq_l) % FFT_N1 == 0

    wts = _prep_weights(w_in, mla_g_qa, mla_w_uq, mla_g_kva, mla_w_ukv, mla_g_q, mla_g_k, gqa_g_q, gqa_g_k,
                        g_out, w_out, w_router, b_router, hy_w_in)
    wts["w_moe_in"] = _cast_call(w_moe_in.reshape(depth * N_EXPERTS, d, 2 * D_FF), 512).reshape(w_moe_in.shape)
    wts["w_moe_out"] = _cast_call(w_moe_out.reshape(depth * N_EXPERTS, D_FF, d), 512).reshape(w_moe_out.shape)
    wts["b_moe_in"] = b_moe_in.reshape(depth, N_EXPERTS, 1, 2 * D_FF)
    wts["b_moe_out"] = b_moe_out.reshape(depth, N_EXPERTS, 1, d)
    hw = dict(hy_w_in=wts["hy_w_in"], hy_b_in=hy_b_in.reshape(depth, 1, HY_FO), hy_w_mid=hy_w_mid,
              hy_b_mid=hy_b_mid.reshape(depth, HY_INNER, 1, HY_FO), hy_w_out=hy_w_out,
              hy_b_out=hy_b_out.reshape(depth, 1, -1), hy_freq=hy_freq.reshape(depth, 1, HY_FO),
              hy_conv_w=hy_conv_w, hy_conv_b=hy_conv_b.reshape(depth, 1, -1), hy_skip=hy_skip)
    dft_l = _lat_dft(seq_l)
    hw["skip_t"] = jnp.tile(hy_skip, (1, 1, dft_l["n2"]))

    conds = jnp.zeros((8, d), F32).at[0].set(c_ctx).at[1:1 + nb_l].set(c)
    mod = _mod_call(conds, w_mod, b_mod).reshape(depth, 8, 6, d)
    mod_c, mod_l = mod[:, 0:1], mod[:, 1:1 + nb_l]

    rope_tabs = (_rope_tables(seq_l, ROPE_A, _lane_map_a), _rope_tables(seq_l, HD_C, _lane_map_c))
    kax, vax = _cachekv_call(cache_mla_ckv, jnp.pad(cache_mla_kpe, ((0, 0), (0, 0), (0, 0), (NOPE_A, LANE - QK_A))), wts)
    kcx = cache_gqa_k.reshape(nb_l, depth, past, KV_C * HD_C).astype(BF16)
    vcx = cache_gqa_v.reshape(nb_l, depth, past, KV_C * HD_C).astype(BF16)
    vcx = jnp.concatenate([vcx, jnp.ones_like(vcx)], axis=-1)

    z_c, dec_c = _hy_features(seq_c)
    cmat, smat, fwd_c, inv_c = _ctx_dft(seq_c)
    z_l, dec_l = _hy_features(seq_l)
    z_full = jnp.concatenate([z_l, jnp.zeros((1, HY_FO), F32), z_l[:0:-1]], axis=0)
    dec_full = jnp.concatenate([dec_l, jnp.zeros((1, HY_CH), F32), dec_l[:0:-1]], axis=0)

    xc = x_prompt.reshape(batch * seq_c, d)
    xl = x_sample.reshape(nb_l * seq_l, d)
    yc = yl = None
    new_ckv, new_kpe, new_k, new_v = [], [], [], []
    for l in range(depth):
        (xc, qa, ka, va, ckv, kpe, qc, kc, vc, kcf, vcf, u) = _premix_call(l, xc, yc, mod_c, seq_c, False, wts, None)
        new_ckv.append(ckv)
        new_kpe.append(kpe[:, NOPE_A:QK_A])
        new_k.append(kcf)
        new_v.append(vcf)
        oa, oc = _attn_ctx_call(qa, ka, va, qc, kc, vc, seq_c)
        kf_c = _filt_ctx_call(l, z_c, dec_c, hw, cmat, smat)
        ob = _hy_ctx_call(l, u, hw, kf_c, fwd_c, inv_c, seq_c)
        xc, h2, idx, gates = _postmix_call(l, xc, oa, ob, oc, mod_c, seq_c, wts)
        yc = _moe_call(l, h2, idx[:, :TOP_K], gates[:, :TOP_K], wts)
        (xl, qa, ka, va, _, _, qc, kc, vc, _, _, u) = _premix_call(l, xl, yl, mod_l, seq_l, True, wts, rope_tabs)
        oa, oc = _attn_lat_call(l, qa, ka, va, qc, kc, vc, kax, vax, kcx, vcx, seq_l)
        kf_l = _filter_lat(l, z_full, dec_full, hw, dft_l, seq_l)
        ob = _hyena_lat(l, u, hw, kf_l, dft_l, seq_l)
        xl, h2, idx, gates = _postmix_call(l, xl, oa, ob, oc, mod_l, seq_l, wts)
        yl = _moe_call(l, h2, idx[:, :TOP_K], gates[:, :TOP_K], wts)
    y_prompt = _final_call(depth - 1, xc, yc, mod_c, seq_c).reshape(batch, seq_c, d)
    y_sample = _final_call(depth - 1, xl, yl, mod_l, seq_l).reshape(nb_l, seq_l, d)
    stack = lambda xs, tail: jnp.stack([a.reshape((batch, seq_c) + tail) for a in xs], axis=1)
    return (y_prompt, y_sample, stack(new_ckv, (KV_RANK,)), stack(new_kpe, (ROPE_A,)),
            stack(new_k, (KV_C, HD_C)), stack(new_v, (KV_C, HD_C)))
```

```python
import functools
import math

import numpy as np
import jax
import jax.numpy as jnp
from jax import lax
from jax.experimental import pallas as pl
from jax.experimental.pallas import tpu as pltpu

F32 = jnp.float32
BF16 = jnp.bfloat16

D_MODEL = 1024
GRID_W = 64
EPS = 1e-6
ROPE_THETA = 10000.0
H_A = 6
Q_RANK = 256
KV_RANK = 128
NOPE_A = 64
ROPE_A = 32
V_A = 64
QK_A = NOPE_A + ROPE_A
HY_CH = 256
HY_ORDER = 2
HY_EMB = 33
HY_FO = 64
HY_INNER = 2
HY_MIN_DECAY = math.log(1e-2) / 1.5
HY_MAX_DECAY = math.log(1e-2) / 0.3
H_C = 6
KV_C = 2
HD_C = 64
N_EXPERTS = 32
TOP_K = 4
D_FF = 1024
SWIGLU_LIMIT = 7.0
SWIGLU_ALPHA = 1.702
LOG2E = 1.4426950408889634

LANE = 128
HQ = H_A * LANE
W_A = H_A * V_A
W_B = HY_CH
W_C = H_C * HD_C
IN_P = Q_RANK + KV_RANK + LANE + 3 * HY_CH + HQ + 2 * LANE
ROW_BLOCK = 128
TOKEN_TILE = 512
VMEM_LIMIT = 56 * 1024 * 1024
FFT_N1 = 128
FFT_N2 = 64


def _cp(sem, vmem=None):
    return pltpu.CompilerParams(dimension_semantics=sem, vmem_limit_bytes=vmem)


def _dot(a, b):
    return jnp.dot(a, b, preferred_element_type=F32)


def _split(a):
    hi = a.astype(BF16)
    return hi, (a - hi.astype(F32)).astype(BF16)


def _dot3(a, b):
    ah, al = _split(a)
    bh, bl = _split(b)
    return _dot(ah, bh) + _dot(ah, bl) + _dot(al, bh)


def _bdot(a, b):
    return lax.dot_general(a, b, (((2,), (1,)), ((0,), (0,))), preferred_element_type=F32)


def _rms(x, n=None):
    ss = jnp.sum(x * x, axis=-1, keepdims=True) * (1.0 / (n or x.shape[-1]))
    return x * lax.rsqrt(ss + EPS)


def _head_norm(x, nvalid):
    outs = []
    for h in range(x.shape[1] // LANE):
        blk = x[:, h * LANE:(h + 1) * LANE]
        ss = jnp.sum(blk * blk, axis=-1, keepdims=True) * (1.0 / nvalid)
        outs.append(blk * lax.rsqrt(ss + EPS))
    return outs[0] if len(outs) == 1 else jnp.concatenate(outs, axis=1)


def _rope(x, rope):
    tab_ref, perm_ref = rope
    c, s = tab_ref[0], tab_ref[1]
    xb = x.astype(BF16)
    nblk = x.shape[1] // LANE
    outs = []
    b = 0
    while b < nblk:
        w = 2 if nblk - b >= 2 else 1
        perm = perm_ref[...] if w == 2 else perm_ref[:LANE, :LANE]
        sw = _dot(xb[:, b * LANE:(b + w) * LANE], perm)
        for k in range(w):
            outs.append(x[:, (b + k) * LANE:(b + k + 1) * LANE] * c + sw[:, k * LANE:(k + 1) * LANE] * s)
        b += w
    return outs[0] if len(outs) == 1 else jnp.concatenate(outs, axis=1)


def _mod_kernel(c_ref, w_ref, b_ref, o_ref):
    c = c_ref[...]
    s = c * (1.0 / (1.0 + jnp.exp(-c)))
    o_ref[...] = _dot3(s, w_ref[...]) + b_ref[...]


def _mod_call(conds, w_mod, b_mod):
    depth, d, n6 = w_mod.shape
    tn = 1536
    return pl.pallas_call(
        _mod_kernel,
        out_shape=jax.ShapeDtypeStruct((depth, 8, n6), F32),
        grid=(depth, n6 // tn),
        in_specs=[pl.BlockSpec((8, d), lambda l, j: (0, 0)),
                  pl.BlockSpec((None, d, tn), lambda l, j: (l, 0, j)),
                  pl.BlockSpec((None, 1, tn), lambda l, j: (l, 0, j))],
        out_specs=pl.BlockSpec((None, 8, tn), lambda l, j: (l, 0, j)),
        compiler_params=_cp(("parallel", "parallel"), VMEM_LIMIT),
        name="mod",
    )(conds, w_mod, b_mod.reshape(depth, 1, n6))


def _cast_kernel(pair_chunk, x_ref, o_ref):
    if not pair_chunk:
        o_ref[...] = x_ref[...].astype(BF16)
        return
    half = x_ref.shape[1] // 2
    for j in range(half // pair_chunk):
        src = slice(j * pair_chunk, (j + 1) * pair_chunk)
        o_ref[:, 2 * j * pair_chunk:(2 * j + 1) * pair_chunk] = x_ref[:, src].astype(BF16)
        o_ref[:, (2 * j + 1) * pair_chunk:(2 * j + 2) * pair_chunk] = (
            x_ref[:, half + j * pair_chunk:half + (j + 1) * pair_chunk].astype(BF16))


def _cast_call(w, tr, pair_chunk=0):
    n, r, c = w.shape
    return pl.pallas_call(
        functools.partial(_cast_kernel, pair_chunk),
        out_shape=jax.ShapeDtypeStruct(w.shape, BF16),
        grid=(n, r // tr),
        in_specs=[pl.BlockSpec((None, tr, c), lambda i, j: (i, j, 0))],
        out_specs=pl.BlockSpec((None, tr, c), lambda i, j: (i, j, 0)),
        compiler_params=_cp(("parallel", "parallel"), VMEM_LIMIT),
        name="cast_bf16",
    )(w)


def _kv_heads(ckv_bf, kpe, w_kv_ref, gk, rope_ref, use_rope):
    kvp = _dot(ckv_bf, w_kv_ref[...])
    kn = kvp[:, :HQ]
    ka = jnp.concatenate([kn[:, h * LANE:(h + 1) * LANE] + kpe for h in range(H_A)], axis=1)
    ka = _head_norm(ka, QK_A) * gk
    if use_rope:
        ka = _rope(ka, rope_ref)
    return ka, kvp[:, HQ:]


def _premix_kernel(has_prev, use_rope, *refs):
    it = iter(refs)
    x_ref = next(it)
    if has_prev:
        y_ref, modp_ref = next(it), next(it)
    mod_ref, w_in_ref, w_uq_ref, w_kv_ref, g_ref = next(it), next(it), next(it), next(it), next(it)
    if use_rope:
        ra_ref, rc_ref = (next(it), next(it)), (next(it), next(it))
    else:
        ra_ref = rc_ref = None
    (xo_ref, qa_ref, ka_ref, va_ref, ckv_ref, kpe_ref, qc_ref, kc_ref, vc_ref, kcf_ref, vcf_ref, u_ref) = it

    x = x_ref[...]
    if has_prev:
        x = x + modp_ref[5:6, :] * y_ref[...]
    xo_ref[...] = x
    h = _rms(x) * (1.0 + mod_ref[1:2, :]) + mod_ref[0:1, :]
    proj = _dot(h.astype(BF16), w_in_ref[...])
    o = 0
    c_q = proj[:, o:o + Q_RANK]; o += Q_RANK
    c_kv = proj[:, o:o + KV_RANK]; o += KV_RANK
    kpe = proj[:, o:o + LANE]; o += LANE
    u_ref[...] = proj[:, o:o + 3 * HY_CH]; o += 3 * HY_CH
    q_c = proj[:, o:o + HQ]; o += HQ
    k_c = proj[:, o:o + LANE]; o += LANE
    v_c = proj[:, o:o + LANE]

    cqn = _rms(c_q) * g_ref[0:1, :Q_RANK]
    qa = _head_norm(_dot(cqn.astype(BF16), w_uq_ref[...]), QK_A) * g_ref[2:3, :]
    if use_rope:
        qa = _rope(qa, ra_ref)
    qa_ref[...] = qa.astype(BF16)
    ckv = _rms(c_kv) * g_ref[1:2, :KV_RANK]
    ckv_ref[...] = ckv
    kpe_ref[...] = kpe
    ka, va = _kv_heads(ckv.astype(BF16), kpe, w_kv_ref, g_ref[3:4, :], ra_ref, use_rope)
    ka_ref[...] = ka.astype(BF16)
    va_ref[...] = _with_ones(va.astype(BF16))

    qc = _head_norm(q_c, HD_C) * g_ref[4:5, :]
    if use_rope:
        qc = _rope(qc, rc_ref)
    qc_ref[...] = qc.astype(BF16)
    lane = lax.broadcasted_iota(jnp.int32, k_c.shape, 1)
    k2 = k_c * k_c
    s0 = jnp.sum(jnp.where(lane < HD_C, k2, 0.0), axis=-1, keepdims=True) * (1.0 / HD_C)
    s1 = jnp.sum(jnp.where(lane >= HD_C, k2, 0.0), axis=-1, keepdims=True) * (1.0 / HD_C)
    kcn = k_c * jnp.where(lane < HD_C, lax.rsqrt(s0 + EPS), lax.rsqrt(s1 + EPS)) * g_ref[5:6, :LANE]
    kcf_ref[...] = kcn
    vcf_ref[...] = v_c
    kc_ref[...] = (_rope(kcn, rc_ref) if use_rope else kcn).astype(BF16)
    vc_ref[...] = _with_ones(v_c.astype(BF16))


def _chunk_rows_spec(y, tm):
    per = (y.shape[1] - SUB) // tm
    return pl.BlockSpec((None, tm, y.shape[2]), lambda i: (i // per, i % per, 0))


def _premix_call(l, x, yprev, mod, seq, use_rope, wts, rope_tabs):
    t, d = x.shape
    tm = TOKEN_TILE
    ncond = mod.shape[1]
    has_prev = yprev is not None

    def cond(i):
        return (i * tm) // seq if ncond > 1 else 0

    row = lambda w: pl.BlockSpec((tm, w), lambda i: (i, 0))
    ins, specs = [x], [row(d)]
    if has_prev:
        ins += [yprev, mod]
        specs += [_chunk_rows_spec(yprev, tm), pl.BlockSpec((None, None, 6, d), lambda i: (l - 1, cond(i), 0, 0))]
    ins += [mod, wts["w_in"], wts["w_uq"], wts["w_kv"], wts["gains"]]
    specs += [pl.BlockSpec((None, None, 6, d), lambda i: (l, cond(i), 0, 0)),
              pl.BlockSpec((None, d, IN_P), lambda i: (l, 0, 0)),
              pl.BlockSpec((None, Q_RANK, HQ), lambda i: (l, 0, 0)),
              pl.BlockSpec((None, KV_RANK, HQ + W_A), lambda i: (l, 0, 0)),
              pl.BlockSpec((None, 8, HQ), lambda i: (l, 0, 0))]
    if use_rope:
        nt = seq // tm
        tab = pl.BlockSpec((2, tm, LANE), lambda i: (0, i % nt, 0))
        perm = pl.BlockSpec((2 * LANE, 2 * LANE), lambda i: (0, 0))
        ins += [rope_tabs[0][0], rope_tabs[0][1], rope_tabs[1][0], rope_tabs[1][1]]
        specs += [tab, perm, tab, perm]
    outs = [(d, F32), (HQ, BF16), (HQ, BF16), (2 * W_A, BF16), (KV_RANK, F32), (LANE, F32), (HQ, BF16),
            (LANE, BF16), (2 * LANE, BF16), (LANE, F32), (LANE, F32), (3 * HY_CH, F32)]
    return pl.pallas_call(
        functools.partial(_premix_kernel, has_prev, use_rope),
        out_shape=[jax.ShapeDtypeStruct((t, w), dt) for w, dt in outs],
        grid=(t // tm,),
        in_specs=specs,
        out_specs=[row(w) for w, _ in outs],
        compiler_params=_cp(("parallel",), VMEM_LIMIT),
        name="premix",
    )(*ins)


def _cachekv_kernel(ckv_ref, kpe_ref, w_kv_ref, g_ref, ka_ref, va_ref):
    ka, va = _kv_heads(ckv_ref[...].astype(BF16), kpe_ref[...], w_kv_ref, g_ref[3:4, :], None, False)
    ka_ref[...] = ka.astype(BF16)
    va_ref[...] = _with_ones(va.astype(BF16))


def _cachekv_call(cache_ckv, cache_kpe_p, wts):
    nb, depth, past, _ = cache_ckv.shape
    return pl.pallas_call(
        _cachekv_kernel,
        out_shape=[jax.ShapeDtypeStruct((depth, nb, past, HQ), BF16),
                   jax.ShapeDtypeStruct((depth, nb, past, 2 * W_A), BF16)],
        grid=(depth, nb),
        in_specs=[pl.BlockSpec((None, None, past, KV_RANK), lambda l, b: (b, l, 0, 0)),
                  pl.BlockSpec((None, None, past, LANE), lambda l, b: (b, l, 0, 0)),
                  pl.BlockSpec((None, KV_RANK, HQ + W_A), lambda l, b: (l, 0, 0)),
                  pl.BlockSpec((None, 8, HQ), lambda l, b: (l, 0, 0))],
        out_specs=[pl.BlockSpec((None, None, past, HQ), lambda l, b: (l, b, 0, 0)),
                   pl.BlockSpec((None, None, past, 2 * W_A), lambda l, b: (l, b, 0, 0))],
        compiler_params=_cp(("parallel", "parallel"), VMEM_LIMIT),
        name="cache_kv",
    )(cache_ckv, cache_kpe_p, wts["w_kv"], wts["gains"])


def _nt(q, k):
    return lax.dot_general(q, k, (((1,), (1,)), ((), ())), preferred_element_type=F32)


def _with_ones(v):
    ones = jnp.ones((v.shape[0], LANE), v.dtype)
    parts = []
    for j in range(v.shape[1] // LANE):
        parts += [v[:, j * LANE:(j + 1) * LANE], ones]
    return jnp.concatenate(parts, axis=1)


def _attend(q, ks, vs):
    return _softmax_pv([_nt(q, k) for k in ks], vs)


def _softmax_pv(ss, vs, m=None):
    if m is None:
        m = ss[0].max(axis=-1, keepdims=True)
        for s in ss[1:]:
            m = jnp.maximum(m, s.max(axis=-1, keepdims=True))
    acc = None
    for s, v in zip(ss, vs):
        pv = _dot(jnp.exp2(s - m).astype(BF16), v)
        acc = pv if acc is None else acc + pv
    return acc[:, :LANE] / acc[:, LANE:]


BOUND_SLACK = 1.01
BOUND_LIMIT = 60.0


def _attn_kernel(nseg, bound, *refs):
    qa_ref, qc_ref = refs[0], refs[1]
    if bound:
        kq_ref, refs = refs[2], refs[:2] + refs[3:]
    segs = refs[2:2 + 4 * nseg]
    oa_ref, oc_ref = refs[2 + 4 * nseg:]
    ka_refs, va_refs, kc_refs, vc_refs = (segs[i::4] for i in range(4))
    lane = lax.broadcasted_iota(jnp.int32, (qa_ref.shape[0], LANE), 1)
    low = lane < V_A
    g_per = H_C // KV_C
    jobs = [("a", j, h) for j in range(H_A // 2) for h in (2 * j, 2 * j + 1)]
    jobs += [("c", g, h) for g in range(g_per) for h in (g, g + g_per)]

    def scores(job):
        kind, _, h = job
        hs = slice(h * LANE, (h + 1) * LANE)
        if kind == "a":
            return [_nt(qa_ref[:, hs], r[:, hs]) for r in ka_refs]
        return [_nt(qc_ref[:, hs], r[...]) for r in kc_refs]

    def values(job):
        kind, j, _ = job
        if kind == "a":
            return [r[:, 2 * j * LANE:2 * (j + 1) * LANE] for r in va_refs]
        return [r[...] for r in vc_refs]

    def row_bound(job):
        kind, _, h = job
        q = (qa_ref if kind == "a" else qc_ref)[:, h * LANE:(h + 1) * LANE].astype(F32)
        r = h if kind == "a" else H_A + h // g_per
        q2 = jnp.sum(q * q, axis=-1, keepdims=True)
        return jnp.sqrt(q2 * kq_ref[r:r + 1, 0:1]) * BOUND_SLACK

    ss_next = None if bound else scores(jobs[0])
    pv = []
    for n, job in enumerate(jobs):
        if bound:
            pv.append(_softmax_pv(scores(job), values(job), row_bound(job)))
        else:
            ss = ss_next
            if n + 1 < len(jobs):
                ss_next = scores(jobs[n + 1])
            pv.append(_softmax_pv(ss, values(job)))
        if len(pv) == 2:
            o_ref = oa_ref if job[0] == "a" else oc_ref
            o_ref[:, job[1] * LANE:(job[1] + 1) * LANE] = jnp.where(low, pv[0], pv[1])
            pv = []


def _attn_ctx_call(qa, ka, va, qc, kc, vc, seq):
    t = qa.shape[0]
    blk = lambda w: pl.BlockSpec((seq, w), lambda b: (b, 0))
    return pl.pallas_call(
        functools.partial(_attn_kernel, 1, False),
        out_shape=[jax.ShapeDtypeStruct((t, W_A), F32), jax.ShapeDtypeStruct((t, W_C), F32)],
        grid=(t // seq,),
        in_specs=[blk(HQ), blk(HQ), blk(HQ), blk(2 * W_A), blk(LANE), blk(2 * LANE)],
        out_specs=[blk(W_A), blk(W_C)],
        compiler_params=_cp(("parallel",), VMEM_LIMIT),
        name="attn_ctx",
    )(qa, qc, ka, va, kc, vc)


def _attn_lat_call(l, qa, ka, va, qc, kc, vc, kax, vax, kcx, vcx, seq):
    t = qa.shape[0]
    nb = t // seq
    tq = min(256, seq)
    nq = seq // tq
    past = kax.shape[2]
    qblk = lambda w: pl.BlockSpec((tq, w), lambda b, i: (b * nq + i, 0))
    sblk = lambda w: pl.BlockSpec((seq, w), lambda b, i: (b, 0))
    xblk = lambda w: pl.BlockSpec((None, None, past, w), lambda b, i: (l, b, 0, 0))
    cblk = lambda w: pl.BlockSpec((None, None, past, w), lambda b, i: (b, l, 0, 0))
    full = lambda w: pl.BlockSpec((seq, w), lambda b: (b, 0))

    kq = pl.pallas_call(
        _knorm_kernel,
        out_shape=jax.ShapeDtypeStruct((nb, NORM_ROWS, LANE), F32),
        grid=(nb,),
        in_specs=[full(HQ), pl.BlockSpec((None, None, past, HQ), lambda b: (l, b, 0, 0)),
                  full(LANE), pl.BlockSpec((None, None, past, LANE), lambda b: (b, l, 0, 0)), full(HQ), full(HQ)],
        out_specs=pl.BlockSpec((None, NORM_ROWS, LANE), lambda b: (b, 0, 0)),
        compiler_params=_cp(("parallel",), VMEM_LIMIT),
        name="attn_norms",
    )(ka, kax, kc, kcx, qa, qc)
    k2, q2a, q2c = kq[:, :H_A + KV_C, 0], kq[:, 8:8 + H_A, 0], kq[:, 8 + H_A:8 + H_A + H_C, 0]
    k2c = jnp.repeat(k2[:, H_A:], H_C // KV_C, axis=1)
    worst = jnp.sqrt(jnp.maximum(jnp.max(q2a * k2[:, :H_A]), jnp.max(q2c * k2c))) * BOUND_SLACK

    def call(bound):
        extra_specs = [pl.BlockSpec((None, NORM_ROWS, LANE), lambda b, i: (b, 0, 0))] if bound else []
        return pl.pallas_call(
            functools.partial(_attn_kernel, 2, bound),
            out_shape=[jax.ShapeDtypeStruct((t, W_A), F32), jax.ShapeDtypeStruct((t, W_C), F32)],
            grid=(nb, nq),
            in_specs=[qblk(HQ), qblk(HQ)] + extra_specs
            + [xblk(HQ), xblk(2 * W_A), cblk(LANE), cblk(2 * LANE),
               sblk(HQ), sblk(2 * W_A), sblk(LANE), sblk(2 * LANE)],
            out_specs=[qblk(W_A), qblk(W_C)],
            compiler_params=_cp(("parallel", "parallel"), VMEM_LIMIT),
            name="attn_lat_bound" if bound else "attn_lat",
        )(qa, qc, *([kq] if bound else []), kax, vax, kcx, vcx, ka, va, kc, vc)

    return lax.cond(worst < BOUND_LIMIT, lambda: call(True), lambda: call(False))


NORM_ROWS = 24


def _knorm_kernel(ka_ref, kax_ref, kc_ref, kcx_ref, qa_ref, qc_ref, o_ref):
    def max_n2(x, lanes=None):
        xf = x.astype(F32)
        x2 = xf * xf
        if lanes is not None:
            lane = lax.broadcasted_iota(jnp.int32, x2.shape, 1)
            x2 = jnp.where((lane >= lanes[0]) & (lane < lanes[1]), x2, 0.0)
        return jnp.max(jnp.sum(x2, axis=-1, keepdims=True), axis=0, keepdims=True)

    rows = []
    for h in range(H_A):
        hs = slice(h * LANE, (h + 1) * LANE)
        rows.append(jnp.maximum(max_n2(ka_ref[:, hs]), max_n2(kax_ref[:, hs])))
    for kv in range(KV_C):
        lanes = (kv * HD_C, (kv + 1) * HD_C)
        rows.append(jnp.maximum(max_n2(kc_ref[...], lanes), max_n2(kcx_ref[...], lanes)))
    rows += [max_n2(qa_ref[:, h * LANE:(h + 1) * LANE]) for h in range(H_A)]
    rows += [max_n2(qc_ref[:, h * LANE:(h + 1) * LANE]) for h in range(H_C)]
    o_ref[...] = jnp.zeros_like(o_ref)
    for r, v in enumerate(rows):
        o_ref[r:r + 1, :] = jnp.broadcast_to(v, (1, LANE))


def _filter_mlp(z, w_in_ref, b_in_ref, w_mid_ref, b_mid_ref, w_out_ref, b_out_ref, freq_ref):
    freq = freq_ref[...]
    a = jnp.sin(freq * (_dot3(z, w_in_ref[...]) + b_in_ref[...]))
    for i in range(HY_INNER):
        a = jnp.sin(freq * (_dot3(a, w_mid_ref[i]) + b_mid_ref[i]))
    return _dot3(a, w_out_ref[...]) + b_out_ref[...]


def _conv3(u, up, dn, cw_ref, cb_ref):
    return up * cw_ref[0:1, :] + u * cw_ref[1:2, :] + dn * cw_ref[2:3, :] + cb_ref[...]


def _filt_ctx_kernel(z_ref, dec_ref, w_in_ref, b_in_ref, w_mid_ref, b_mid_ref, w_out_ref, b_out_ref, freq_ref,
                     c_ref, s_ref, o_ref):
    n = z_ref.shape[0]
    h = _filter_mlp(z_ref[...], w_in_ref, b_in_ref, w_mid_ref, b_mid_ref, w_out_ref, b_out_ref, freq_ref)
    dec = dec_ref[...]
    row = lax.broadcasted_iota(jnp.int32, dec.shape, 0)
    half = HY_ORDER * HY_CH
    for o in range(HY_ORDER):
        hf = h[:, o * HY_CH:(o + 1) * HY_CH] * dec
        hb = jnp.where(row > 0, h[:, half + o * HY_CH:half + (o + 1) * HY_CH] * dec, 0.0)
        nrm = jnp.sum(jnp.abs(hf) + jnp.abs(hb), axis=0, keepdims=True) + EPS
        scale = (1.0 / n) / nrm
        o_ref[o, 0] = _dot3(c_ref[...], hf + hb) * scale
        o_ref[o, 1] = -_dot3(s_ref[...], hf - hb) * scale


def _filt_ctx_call(l, z, dec, hw, cmat, smat):
    n = z.shape[0]
    full = lambda a: pl.BlockSpec((None,) + a.shape[1:], lambda i: (l,) + (0,) * (a.ndim - 1))
    const = lambda a: pl.BlockSpec(a.shape, lambda i: (0,) * a.ndim)
    names = ["hy_w_in", "hy_b_in", "hy_w_mid", "hy_b_mid", "hy_w_out", "hy_b_out", "hy_freq"]
    return pl.pallas_call(
        _filt_ctx_kernel,
        out_shape=jax.ShapeDtypeStruct((HY_ORDER, 2, n, HY_CH), F32),
        grid=(1,),
        in_specs=[const(z), const(dec)] + [full(hw[k]) for k in names] + [const(cmat), const(smat)],
        out_specs=pl.BlockSpec((HY_ORDER, 2, n, HY_CH), lambda i: (0, 0, 0, 0)),
        compiler_params=_cp(("arbitrary",), VMEM_LIMIT),
        name="hy_filter_ctx",
    )(z, dec, *[hw[k] for k in names], cmat, smat)


def _hy_ctx_kernel(u_ref, cw_ref, cb_ref, skip_ref, kf_ref, fwd_ref, inv_ref, o_ref):
    u = u_ref[...]
    n = u.shape[0]
    row = lax.broadcasted_iota(jnp.int32, u.shape, 0)
    up = jnp.where(row > 0, pltpu.roll(u, 1, 0), 0.0)
    dn = jnp.where(row < n - 1, pltpu.roll(u, n - 1, 0), 0.0)
    z = _conv3(u, up, dn, cw_ref, cb_ref)
    s = z[:, :HY_CH]
    gates = (z[:, HY_CH:2 * HY_CH], z[:, 2 * HY_CH:])
    for o in range(HY_ORDER):
        xs = _dot(fwd_ref[...], s.astype(BF16))
        xr, xi = xs[:n], xs[n:]
        kr, ki = kf_ref[o, 0], kf_ref[o, 1]
        ycat = jnp.concatenate([xr * kr - xi * ki, xr * ki + xi * kr], axis=0)
        y = _dot(inv_ref[...], ycat.astype(BF16))
        s = gates[o] * (y + s * skip_ref[o:o + 1, :])
    o_ref[...] = s


def _hy_ctx_call(l, u, hw, kf, fwd, inv, seq):
    t = u.shape[0]
    full = lambda a: pl.BlockSpec((None,) + a.shape[1:], lambda b: (l,) + (0,) * (a.ndim - 1))
    const = lambda a: pl.BlockSpec(a.shape, lambda b: (0,) * a.ndim)
    return pl.pallas_call(
        _hy_ctx_kernel,
        out_shape=jax.ShapeDtypeStruct((t, HY_CH), F32),
        grid=(t // seq,),
        in_specs=[pl.BlockSpec((seq, 3 * HY_CH), lambda b: (b, 0)),
                  full(hw["hy_conv_w"]), full(hw["hy_conv_b"]), full(hw["hy_skip"]),
                  const(kf), const(fwd), const(inv)],
        out_specs=pl.BlockSpec((seq, HY_CH), lambda b: (b, 0)),
        compiler_params=_cp(("parallel",), VMEM_LIMIT),
        name="hyena_ctx",
    )(u, hw["hy_conv_w"], hw["hy_conv_b"], hw["hy_skip"], kf, fwd, inv)


def _filt_lat_kernel(seq, z_ref, dec_ref, w_in_ref, b_in_ref, w_mid_ref, b_mid_ref, w_out_ref, b_out_ref, freq_ref,
                     k_ref, n_ref):
    i = pl.program_id(0)
    tr = z_ref.shape[0]
    h = _filter_mlp(z_ref[...], w_in_ref, b_in_ref, w_mid_ref, b_mid_ref, w_out_ref, b_out_ref, freq_ref)
    half = HY_ORDER * HY_CH
    row = i * tr + lax.broadcasted_iota(jnp.int32, (tr, half), 0)
    dec = dec_ref[...]
    kern = jnp.where(row < seq, h[:, :half], h[:, half:]) * jnp.concatenate([dec] * HY_ORDER, axis=1)
    k_ref[...] = kern

    @pl.when(i == 0)
    def _():
        n_ref[...] = jnp.zeros_like(n_ref)

    n_ref[...] += jnp.sum(jnp.abs(kern), axis=0, keepdims=True)


def _filt_lat_call(l, z, dec, hw, seq):
    n = z.shape[0]
    tr = min(512, n)
    half = HY_ORDER * HY_CH
    full = lambda a: pl.BlockSpec((None,) + a.shape[1:], lambda i: (l,) + (0,) * (a.ndim - 1))
    names = ["hy_w_in", "hy_b_in", "hy_w_mid", "hy_b_mid", "hy_w_out", "hy_b_out", "hy_freq"]
    return pl.pallas_call(
        functools.partial(_filt_lat_kernel, seq),
        out_shape=[jax.ShapeDtypeStruct((n, half), F32), jax.ShapeDtypeStruct((1, half), F32)],
        grid=(n // tr,),
        in_specs=[pl.BlockSpec((tr, z.shape[1]), lambda i: (i, 0)), pl.BlockSpec((tr, HY_CH), lambda i: (i, 0))]
        + [full(hw[k]) for k in names],
        out_specs=[pl.BlockSpec((tr, half), lambda i: (i, 0)), pl.BlockSpec((1, half), lambda i: (0, 0))],
        compiler_params=_cp(("arbitrary",), VMEM_LIMIT),
        name="hy_filter_lat",
    )(z, dec, *[hw[k] for k in names])


def _fa_kernel(f1_ref, k_ref, o_ref):
    r = _dot3(f1_ref[...], k_ref[...])
    o_ref[0] = r[:FFT_N1]
    o_ref[1] = r[FFT_N1:]


def _fa_call(f1f, kern2d):
    n1, w = kern2d.shape
    tn = min(2048, w)
    return pl.pallas_call(
        _fa_kernel,
        out_shape=jax.ShapeDtypeStruct((2, FFT_N1, w), F32),
        grid=(w // tn,),
        in_specs=[pl.BlockSpec(f1f.shape, lambda j: (0, 0)), pl.BlockSpec((n1, tn), lambda j: (0, j))],
        out_specs=pl.BlockSpec((2, FFT_N1, tn), lambda j: (0, 0, j)),
        compiler_params=_cp(("parallel",), VMEM_LIMIT),
        name="hy_filter_dft1",
    )(f1f, kern2d)


def _fb_kernel(n_total, a_ref, fh_ref, fl_ref, n_ref, o_ref):
    a = jnp.concatenate([a_ref[0], a_ref[1]], axis=1)
    ah, al = _split(a)
    x = _bdot(fh_ref[...], ah) + _bdot(fh_ref[...], al) + _bdot(fl_ref[...], ah)
    scale = (1.0 / n_total) / (n_ref[...] + EPS)
    o_ref[...] = x * scale[None]


def _fb_call(af5, f2h, f2l, nrm, n_total):
    _, n1, n2, c = af5.shape[0], af5.shape[1], af5.shape[2], af5.shape[3]
    k1t = 8
    return pl.pallas_call(
        functools.partial(_fb_kernel, n_total),
        out_shape=jax.ShapeDtypeStruct((n1, 2 * n2, c), F32),
        grid=(n1 // k1t,),
        in_specs=[pl.BlockSpec((2, k1t, n2, c), lambda j: (0, j, 0, 0)),
                  pl.BlockSpec((k1t, 2 * n2, 2 * n2), lambda j: (j, 0, 0)),
                  pl.BlockSpec((k1t, 2 * n2, 2 * n2), lambda j: (j, 0, 0)),
                  pl.BlockSpec((1, c), lambda j: (0, 0))],
        out_specs=pl.BlockSpec((k1t, 2 * n2, c), lambda j: (j, 0, 0)),
        compiler_params=_cp(("parallel",), VMEM_LIMIT),
        name="hy_filter_dft2",
    )(af5, f2h, f2l, nrm)


def _hconv_kernel(seq, u_ref, p_ref, n_ref, cw_ref, cb_ref, v_ref, x1_ref, x2_ref):
    i = pl.program_id(0)
    u = u_ref[...]
    tt = u.shape[0]
    row = lax.broadcasted_iota(jnp.int32, u.shape, 0)
    pos = (i * tt) % seq
    prev = jnp.where(pos > 0, p_ref[7:8, :], 0.0)
    nxt = jnp.where(pos + tt < seq, n_ref[0:1, :], 0.0)
    up = jnp.where(row > 0, pltpu.roll(u, 1, 0), prev)
    dn = jnp.where(row < tt - 1, pltpu.roll(u, tt - 1, 0), nxt)
    z = _conv3(u, up, dn, cw_ref, cb_ref)
    v_ref[...] = z[:, :HY_CH]
    x1_ref[...] = z[:, HY_CH:2 * HY_CH]
    x2_ref[...] = z[:, 2 * HY_CH:]


def _hconv_call(l, u, hw, seq):
    t, w = u.shape
    tt = min(512, seq)
    nblk8 = t // 8
    full = lambda a: pl.BlockSpec((None,) + a.shape[1:], lambda i: (l,) + (0,) * (a.ndim - 1))
    ob = pl.BlockSpec((tt, HY_CH), lambda i: (i, 0))
    return pl.pallas_call(
        functools.partial(_hconv_kernel, seq),
        out_shape=[jax.ShapeDtypeStruct((t, HY_CH), F32)] * 3,
        grid=(t // tt,),
        in_specs=[pl.BlockSpec((tt, w), lambda i: (i, 0)),
                  pl.BlockSpec((8, w), lambda i: (jnp.maximum(i * (tt // 8) - 1, 0), 0)),
                  pl.BlockSpec((8, w), lambda i: (jnp.minimum((i + 1) * (tt // 8), nblk8 - 1), 0)),
                  full(hw["hy_conv_w"]), full(hw["hy_conv_b"])],
        out_specs=[ob, ob, ob],
        compiler_params=_cp(("parallel",), VMEM_LIMIT),
        name="hyena_conv3",
    )(u, u, u, hw["hy_conv_w"], hw["hy_conv_b"])


def _ha_kernel(f1_ref, x_ref, o_ref):
    r = _dot(f1_ref[...], x_ref[...].astype(BF16))
    o_ref[0] = r[:FFT_N1].astype(BF16)
    o_ref[1] = r[FFT_N1:].astype(BF16)


def _ha_call(f1d, x2d, nb):
    rows, w = x2d.shape
    n1h = rows // nb
    tn = min(2048, w)
    return pl.pallas_call(
        _ha_kernel,
        out_shape=jax.ShapeDtypeStruct((nb, 2, FFT_N1, w), BF16),
        grid=(nb, w // tn),
        in_specs=[pl.BlockSpec(f1d.shape, lambda b, j: (0, 0)), pl.BlockSpec((n1h, tn), lambda b, j: (b, j))],
        out_specs=pl.BlockSpec((None, 2, FFT_N1, tn), lambda b, j: (b, 0, 0, j)),
        compiler_params=_cp(("parallel", "parallel"), VMEM_LIMIT),
        name="hyena_dft1",
    )(f1d, x2d)


def _hb_kernel(a_ref, f_ref, g_ref, kf_ref, o_ref):
    n2 = a_ref.shape[2]
    a = jnp.concatenate([a_ref[0], a_ref[1]], axis=1)
    x = _bdot(f_ref[...], a)
    xr, xi = x[:, :n2], x[:, n2:]
    kr, ki = kf_ref[:, :n2], kf_ref[:, n2:]
    y = jnp.concatenate([xr * kr - xi * ki, xr * ki + xi * kr], axis=1).astype(BF16)
    b = _bdot(g_ref[...], y)
    o_ref[0] = b[:, :n2].astype(BF16)
    o_ref[1] = b[:, n2:].astype(BF16)


def _hb_call(o, a5, f2, g2, kf):
    nb, _, n1, n2, c = a5.shape
    k1t = 16
    blk = pl.BlockSpec((None, 2, k1t, n2, c), lambda b, j: (b, 0, j, 0, 0))
    mat = pl.BlockSpec((k1t, 2 * n2, 2 * n2), lambda b, j: (j, 0, 0))
    return pl.pallas_call(
        _hb_kernel,
        out_shape=jax.ShapeDtypeStruct(a5.shape, BF16),
        grid=(nb, n1 // k1t),
        in_specs=[blk, mat, mat, pl.BlockSpec((k1t, 2 * n2, c), lambda b, j: (j, 0, o))],
        out_specs=blk,
        compiler_params=_cp(("parallel", "parallel"), VMEM_LIMIT),
        name="hyena_dft2",
    )(a5, f2, g2, kf)


def _hc_kernel(o, fc_ref, b_ref, s_ref, g_ref, skip_ref, o_ref):
    bcat = jnp.concatenate([b_ref[0], b_ref[1]], axis=0)
    y = _dot(fc_ref[...], bcat)
    o_ref[...] = g_ref[...] * (y + s_ref[...] * skip_ref[o:o + 1, :])


def _hc_call(l, o, fc, b4, s2d, g2d, skip_t):
    nb, _, n1, w = b4.shape
    rows = s2d.shape[0] // nb
    tn = min(2048, w)
    blk = pl.BlockSpec((rows, tn), lambda b, j: (b, j))
    return pl.pallas_call(
        functools.partial(_hc_kernel, o),
        out_shape=jax.ShapeDtypeStruct(s2d.shape, F32),
        grid=(nb, w // tn),
        in_specs=[pl.BlockSpec(fc.shape, lambda b, j: (0, 0)),
                  pl.BlockSpec((None, 2, n1, tn), lambda b, j: (b, 0, 0, j)),
                  blk, blk,
                  pl.BlockSpec((None, HY_ORDER, tn), lambda b, j: (l, 0, j))],
        out_specs=blk,
        compiler_params=_cp(("parallel", "parallel"), VMEM_LIMIT),
        name="hyena_dft3",
    )(fc, b4, s2d, g2d, skip_t)


def _postmix_kernel(x_ref, oa_ref, ob_ref, oc_ref, mod_ref, g_ref, wa_ref, wb_ref, wc_ref, wrh_ref, wrl_ref, br_ref,
                    x1_ref, h2_ref, idx_ref, gate_ref):
    na = _rms(oa_ref[...]) * g_ref[:, :W_A]
    nb = _rms(ob_ref[...]) * g_ref[:, W_A:W_A + W_B]
    nc = _rms(oc_ref[...]) * g_ref[:, W_A + W_B:]
    mix = (_dot(na.astype(BF16), wa_ref[...]) + _dot(nb.astype(BF16), wb_ref[...])
           + _dot(nc.astype(BF16), wc_ref[...]))
    x1 = x_ref[...] + mod_ref[2:3, :] * mix
    x1_ref[...] = x1
    h2 = _rms(x1) * (1.0 + mod_ref[4:5, :]) + mod_ref[3:4, :]
    h2_ref[...] = h2
    hh, hl = _split(h2)
    vals = _dot(hh, wrh_ref[...]) + _dot(hh, wrl_ref[...]) + _dot(hl, wrh_ref[...]) + br_ref[...]
    lane = lax.broadcasted_iota(jnp.int32, vals.shape, 1).astype(F32)
    idx_out = jnp.zeros(vals.shape, F32)
    top = jnp.zeros(vals.shape, F32)
    m0 = None
    for k in range(TOP_K):
        m = vals.max(axis=-1, keepdims=True)
        sel = jnp.min(jnp.where(vals == m, lane, float(LANE)), axis=-1, keepdims=True)
        if m0 is None:
            m0 = m
        idx_out = jnp.where(lane == k, sel, idx_out)
        top = jnp.where(lane == k, jnp.exp(m - m0), top)
        vals = jnp.where(lane == sel, -jnp.inf, vals)
    idx_ref[...] = idx_out.astype(jnp.int32)
    gate_ref[...] = top / jnp.sum(top, axis=-1, keepdims=True)


def _postmix_call(l, x, oa, ob, oc, mod, seq, wts):
    t, d = x.shape
    tm = TOKEN_TILE
    ncond = mod.shape[1]

    def cond(i):
        return (i * tm) // seq if ncond > 1 else 0

    row = lambda w: pl.BlockSpec((tm, w), lambda i: (i, 0))
    lay = lambda a: pl.BlockSpec((None,) + a.shape[1:], lambda i: (l,) + (0,) * (a.ndim - 1))
    names = ["g_out", "wo_a", "wo_b", "wo_c", "wr_hi", "wr_lo", "b_router"]
    return pl.pallas_call(
        _postmix_kernel,
        out_shape=[jax.ShapeDtypeStruct((t, d), F32), jax.ShapeDtypeStruct((t, d), F32),
                   jax.ShapeDtypeStruct((t, LANE), jnp.int32), jax.ShapeDtypeStruct((t, LANE), F32)],
        grid=(t // tm,),
        in_specs=[row(d), row(W_A), row(W_B), row(W_C),
                  pl.BlockSpec((None, None, 6, d), lambda i: (l, cond(i), 0, 0))] + [lay(wts[k]) for k in names],
        out_specs=[row(d), row(d), row(LANE), row(LANE)],
        compiler_params=_cp(("parallel",), VMEM_LIMIT),
        name="postmix",
    )(x, oa, ob, oc, mod, *[wts[k] for k in names])


SUB = 8


NGRP = ROW_BLOCK // SUB
FF_CHUNK = 256


def _moe_kernel(be_ref, rs_ref, nv_ref, nu_ref, tok_ref, gate_ref, h_ref, wi_ref, bi_ref, wo_ref, bo_ref, y_ref,
                buf, obuf, xb):
    s = pl.program_id(0)
    tc = h_ref.shape[0]

    @pl.when(s == 0)
    def _():
        y_ref[...] = jnp.zeros_like(y_ref)
        buf[...] = jnp.zeros_like(buf)
        obuf[...] = jnp.zeros_like(obuf)

    @pl.when(s < nu_ref[0] + 2)
    def _():
        slot_g = lax.rem(s, 2)
        slot_c = 1 - slot_g

        xb[...] = buf[slot_c].reshape(ROW_BLOCK, buf.shape[3]).astype(BF16)

        base_s = rs_ref[s]
        nv = nv_ref[s]
        for g in range(NGRP):
            ts = [jnp.where(g * SUB + j < nv, tok_ref[base_s + g * SUB + j], tc) for j in range(SUB)]
            new = [y_ref[pl.ds(ts[j], 1), :] + gate_ref[base_s + g * SUB + j] * obuf[slot_g, g, j:j + 1, :]
                   for j in range(SUB)]
            for j in range(SUB):
                y_ref[pl.ds(ts[j], 1), :] = new[j]

        base_g = rs_ref[s + 2]
        for g in range(NGRP):
            rows = [h_ref[pl.ds(tok_ref[base_g + g * SUB + j], 1), :] for j in range(SUB)]
            for j in range(SUB):
                buf[slot_g, g, j:j + 1, :] = rows[j]

        out = bo_ref[...]
        for j in range(D_FF // FF_CHUNK):
            cols = slice(2 * j * FF_CHUNK, 2 * (j + 1) * FF_CHUNK)
            gu = _dot(xb[...], wi_ref[:, cols]) + bi_ref[:, cols]
            gt = jnp.minimum(gu[:, :FF_CHUNK], SWIGLU_LIMIT)
            lin = jnp.clip(gu[:, FF_CHUNK:], -SWIGLU_LIMIT, SWIGLU_LIMIT)
            act = (lin + 1.0) * gt * (1.0 / (1.0 + jnp.exp(-SWIGLU_ALPHA * gt)))
            out = out + _dot(act.astype(BF16), wo_ref[j * FF_CHUNK:(j + 1) * FF_CHUNK, :])
        obuf[slot_c] = out.reshape(obuf.shape[1:])


def _moe_kernel_entry(has_alias, *refs):
    refs = list(refs)
    if has_alias:
        del refs[11]
    _moe_kernel(*refs)


def _route(idx, gates, n_blocks):
    m = idx.shape[0] * TOP_K
    e = idx.reshape(m)
    flat = jnp.arange(m, dtype=jnp.int32)
    skey, gate = lax.sort((e * m + flat, gates.reshape(m)), num_keys=1)
    tok = (skey % m) // TOP_K
    experts = jnp.arange(N_EXPERTS, dtype=jnp.int32)
    cnt = jnp.sum((e[:, None] == experts[None, :]).astype(jnp.int32), axis=0)
    nblk = (cnt + ROW_BLOCK - 1) // ROW_BLOCK
    bend = jnp.cumsum(nblk)
    n_used = bend[-1]
    blk = jnp.arange(-2, n_blocks + 2, dtype=jnp.int32)
    bcl = jnp.clip(blk, 0, n_used - 1)
    be = jnp.sum((bend[None, :] <= bcl[:, None]).astype(jnp.int32), axis=1)
    oh = (be[:, None] == experts[None, :]).astype(jnp.int32)
    pick = lambda v: jnp.sum(oh * v[None, :], axis=1)
    off = (bcl - pick(bend - nblk)) * ROW_BLOCK
    valid = (blk >= 0) & (blk < n_used)
    rs = jnp.where(valid, pick(jnp.cumsum(cnt) - cnt) + off, 0)
    nv = jnp.where(valid, jnp.clip(pick(cnt) - off, 0, ROW_BLOCK), 0)
    pad = jnp.zeros((ROW_BLOCK,), jnp.int32)
    return (be[1:n_blocks + 3], rs, nv, n_used.reshape(1), jnp.concatenate([tok, pad]),
            jnp.concatenate([gate, pad.astype(F32)]))


def _moe_call(l, h2, idx, gates, wts):
    t, d = h2.shape
    tc = min(4096, t)
    n_blocks = tc * TOP_K // ROW_BLOCK + N_EXPERTS
    y = None
    for c in range(t // tc):
        route = _route(idx[c * tc:(c + 1) * tc], gates[c * tc:(c + 1) * tc], n_blocks)
        wspec = lambda r, w: pl.BlockSpec((None, None, r, w), lambda s, be, *_: (l, be[s], 0, 0))
        in_specs = [pl.BlockSpec((tc, d), lambda i, *_: (c, 0), pipeline_mode=pl.Buffered(1)),
                    wspec(d, 2 * D_FF), wspec(1, 2 * D_FF), wspec(D_FF, d), wspec(1, d)]
        args = [h2, wts["w_moe_in"], wts["b_moe_in"], wts["w_moe_out"], wts["b_moe_out"]]
        aliases = {}
        if y is not None:
            in_specs.append(pl.BlockSpec(memory_space=pl.ANY))
            args.append(y)
            aliases = {len(route) + len(args) - 1: 0}
        grid_spec = pltpu.PrefetchScalarGridSpec(
            num_scalar_prefetch=len(route),
            grid=(n_blocks + 2,),
            in_specs=in_specs,
            out_specs=pl.BlockSpec((None, tc + SUB, d), lambda i, *_: (c, 0, 0), pipeline_mode=pl.Buffered(1)),
            scratch_shapes=[pltpu.VMEM((2, NGRP, SUB, d), F32), pltpu.VMEM((2, NGRP, SUB, d), F32),
                            pltpu.VMEM((ROW_BLOCK, d), BF16)],
        )
        y = pl.pallas_call(
            functools.partial(_moe_kernel_entry, y is not None),
            out_shape=jax.ShapeDtypeStruct((t // tc, tc + SUB, d), F32),
            grid_spec=grid_spec,
            input_output_aliases=aliases,
            compiler_params=_cp(("arbitrary",), VMEM_LIMIT),
            name="moe_experts",
        )(*route, *args)
    return y


def _final_kernel(x_ref, y_ref, mod_ref, o_ref):
    o_ref[...] = x_ref[...] + mod_ref[5:6, :] * y_ref[...]


def _final_call(l, x, y, mod, seq):
    t, d = x.shape
    tm = TOKEN_TILE
    ncond = mod.shape[1]
    row = pl.BlockSpec((tm, d), lambda i: (i, 0))
    return pl.pallas_call(
        _final_kernel,
        out_shape=jax.ShapeDtypeStruct((t, d), F32),
        grid=(t // tm,),
        in_specs=[row, _chunk_rows_spec(y, tm),
                  pl.BlockSpec((None, None, 6, d), lambda i: (l, (i * tm) // seq if ncond > 1 else 0, 0, 0))],
        out_specs=row,
        compiler_params=_cp(("parallel",), VMEM_LIMIT),
        name="final_residual",
    )(x, y, mod)


def _rope_tables(seq, rot_dim, lane_map):
    shift = rot_dim // 2
    n_rows = seq // GRID_W
    rows = jnp.repeat(jnp.arange(n_rows, dtype=F32), GRID_W)
    cols = jnp.tile(jnp.arange(GRID_W, dtype=F32), n_rows)
    axis_dim = rot_dim // 2
    inv_freq = ROPE_THETA ** (-jnp.arange(0, axis_dim, 2, dtype=F32) / axis_dim)
    ang = jnp.concatenate([rows[:, None] * inv_freq, cols[:, None] * inv_freq], axis=-1)
    cos, sin = jnp.cos(ang), jnp.sin(ang)
    pair = np.zeros((LANE,), np.int32)
    in_rot = np.zeros((LANE,), np.float32)
    first = np.zeros((LANE,), np.float32)
    for ln in range(LANE):
        m = lane_map(ln)
        if m is not None:
            pair[ln], in_rot[ln], first[ln] = m[0], 1.0, 1.0 if m[1] == 0 else 0.0
    c = jnp.where(in_rot[None, :] > 0, cos[:, pair], 1.0)
    s = sin[:, pair] * in_rot[None, :] * (1.0 - 2.0 * first[None, :])
    perm = np.zeros((2 * LANE, 2 * LANE), np.float32)
    for ln in range(LANE):
        if in_rot[ln] > 0:
            src = ln + shift if first[ln] > 0 else ln - shift
            perm[src, ln] = perm[LANE + src, LANE + ln] = 1.0
    return jnp.stack([c, s]).astype(F32), jnp.asarray(perm, BF16)


def _lane_map_a(ln):
    o = ln - NOPE_A
    if 0 <= o < ROPE_A:
        return (o % (ROPE_A // 2), o // (ROPE_A // 2))
    return None


def _lane_map_c(ln):
    o = ln % HD_C
    return (o % (HD_C // 2), o // (HD_C // 2))


def _phase(num, den):
    ang = (2.0 * math.pi / den) * (num % den).astype(F32)
    return jnp.cos(ang), jnp.sin(ang)


def _ctx_dft(n):
    f = jnp.arange(n, dtype=jnp.int32)[:, None]
    t = jnp.arange(n, dtype=jnp.int32)[None, :]
    c, s = _phase((2 * f + 1) * t, 4 * n)
    fwd = jnp.concatenate([c, -s], axis=0).astype(BF16)
    inv = jnp.concatenate([c.T, -s.T], axis=1).astype(BF16)
    return c, s, fwd, inv


def _lat_dft(seq):
    n = 2 * seq
    n1, n2 = FFT_N1, n // FFT_N1
    k1 = jnp.arange(n1, dtype=jnp.int32)
    c1, s1 = _phase(k1[:, None] * k1[None, :], n1)
    f1f = jnp.concatenate([c1, -s1], axis=0)
    f1d = f1f[:, :n1 // 2].astype(BF16)
    fc = jnp.concatenate([c1[:, :n1 // 2].T, -s1[:, :n1 // 2].T], axis=1).astype(BF16)
    k2 = jnp.arange(n2, dtype=jnp.int32)
    num = (k2[None, :, None] * k2[None, None, :]) * n1 + k2[None, None, :] * k1[:, None, None]
    cm, sm = _phase(num, n)
    mr, mi = cm, -sm
    f2 = jnp.concatenate([jnp.concatenate([mr, -mi], axis=2), jnp.concatenate([mi, mr], axis=2)], axis=1)
    mrt, mit = jnp.swapaxes(mr, 1, 2), jnp.swapaxes(mi, 1, 2)
    g2 = jnp.concatenate([jnp.concatenate([mrt, mit], axis=2), jnp.concatenate([-mit, mrt], axis=2)], axis=1)
    f2h, f2l = _split(f2)
    return dict(f1f=f1f, f1d=f1d, fc=fc, f2=f2h, f2l=f2l, g2=g2.astype(BF16), n1=n1, n2=n2, n=n)


def _hy_features(seq):
    t = jnp.linspace(0.0, 1.0, seq, dtype=F32)[:, None]
    bands = (HY_EMB - 1) // 2
    f = jnp.linspace(1e-4, bands - 1, bands, dtype=F32)[None, :]
    w = 2.0 * math.pi * jnp.arange(seq, dtype=F32)[:, None] / seq
    z = jnp.concatenate([t, jnp.cos(f * w), jnp.sin(f * w)], axis=-1)
    z = jnp.pad(z, ((0, 0), (0, HY_FO - HY_EMB)))
    deltas = jnp.abs(jnp.linspace(HY_MIN_DECAY, HY_MAX_DECAY, HY_CH, dtype=F32))
    return z, jnp.exp(-t * deltas)


def _prep_weights(w_in, mla_g_qa, mla_w_uq, mla_g_kva, mla_w_ukv, mla_g_q, mla_g_k, gqa_g_q, gqa_g_k,
                  g_out, w_out, w_router, b_router, hy_w_in):
    depth = w_in.shape[0]
    cuts = np.cumsum([0, Q_RANK, KV_RANK, ROPE_A, 3 * HY_CH, H_C * HD_C, KV_C * HD_C, KV_C * HD_C])
    wb = w_in.astype(BF16)
    zeros = lambda w: jnp.zeros(wb.shape[:2] + (w,), BF16)
    parts = [wb[:, :, cuts[0]:cuts[2]], zeros(NOPE_A), wb[:, :, cuts[2]:cuts[3]], zeros(LANE - QK_A),
             wb[:, :, cuts[3]:cuts[4]]]
    for h in range(H_C):
        wh = wb[:, :, cuts[4] + h * HD_C:cuts[4] + (h + 1) * HD_C]
        parts += [wh, zeros(HD_C)] if h < H_C // KV_C else [zeros(HD_C), wh]
    w_in_p = jnp.concatenate(parts + [wb[:, :, cuts[5]:cuts[7]]], axis=-1)

    w_uq = jnp.pad(mla_w_uq.reshape(depth, Q_RANK, H_A, QK_A), ((0, 0), (0, 0), (0, 0), (0, LANE - QK_A)))
    w_uq = w_uq.reshape(depth, Q_RANK, HQ).astype(BF16)
    ukv = mla_w_ukv.reshape(depth, KV_RANK, H_A, NOPE_A + V_A)
    wk = jnp.pad(ukv[..., :NOPE_A], ((0, 0), (0, 0), (0, 0), (0, LANE - NOPE_A))).reshape(depth, KV_RANK, HQ)
    wv = ukv[..., NOPE_A:].reshape(depth, KV_RANK, W_A)
    w_kv = jnp.concatenate([wk, wv], axis=-1).astype(BF16)

    def pad_row(v):
        return jnp.pad(v, ((0, 0), (0, HQ - v.shape[1])))

    head_a = lambda g: jnp.tile(jnp.pad(g, ((0, 0), (0, LANE - QK_A))), (1, H_A))
    head_c = lambda g, reps: jnp.tile(g, (1, reps))
    gains = jnp.stack([
        pad_row(mla_g_qa), pad_row(mla_g_kva),
        head_a(mla_g_q) * (QK_A ** -0.5 * LOG2E), head_a(mla_g_k),
        head_c(gqa_g_q, 2 * H_C) * (HD_C ** -0.5 * LOG2E), pad_row(head_c(gqa_g_k, 2)),
        jnp.zeros((depth, HQ), F32), jnp.zeros((depth, HQ), F32)], axis=1).astype(F32)

    g_per = H_C // KV_C
    perm_c = np.concatenate([np.arange(h * HD_C, (h + 1) * HD_C) for g in range(g_per) for h in (g, g + g_per)])
    rows_c = W_A + W_B + perm_c
    g_o = jnp.concatenate([g_out[:, :W_A + W_B], g_out[:, rows_c]], axis=1).reshape(depth, 1, -1)
    wr = jnp.pad(w_router, ((0, 0), (0, 0), (0, LANE - N_EXPERTS)))
    wr_hi, wr_lo = _split(wr)
    br = jnp.pad(b_router, ((0, 0), (0, LANE - N_EXPERTS)), constant_values=-1e30).reshape(depth, 1, LANE)
    return dict(w_in=w_in_p, w_uq=w_uq, w_kv=w_kv, gains=gains, g_out=g_o,
                wo_a=w_out[:, :W_A].astype(BF16), wo_b=w_out[:, W_A:W_A + W_B].astype(BF16),
                wo_c=w_out[:, rows_c].astype(BF16), wr_hi=wr_hi, wr_lo=wr_lo, b_router=br,
                hy_w_in=jnp.pad(hy_w_in, ((0, 0), (0, HY_FO - HY_EMB), (0, 0))))


def _hyena_lat(l, u, hw, kf, dft, seq):
    t = u.shape[0]
    nb = t // seq
    n1, n2 = dft["n1"], dft["n2"]
    w2 = n2 * HY_CH
    v, x1, x2 = _hconv_call(l, u, hw, seq)
    to2d = lambda a: a.reshape(t // n2, w2)
    s = to2d(v)
    for o, gate in enumerate((x1, x2)):
        a = _ha_call(dft["f1d"], s, nb)
        b = _hb_call(o, a.reshape(nb, 2, n1, n2, HY_CH), dft["f2"], dft["g2"], kf)
        s = _hc_call(l, o, dft["fc"], b.reshape(nb, 2, n1, w2), s, to2d(gate), hw["skip_t"])
    return s.reshape(t, HY_CH)


def _filter_lat(l, z, dec, hw, dft, seq):
    kern, nrm = _filt_lat_call(l, z, dec, hw, seq)
    n1, n2, n = dft["n1"], dft["n2"], dft["n"]
    c = kern.shape[1]
    af = _fa_call(dft["f1f"], kern.reshape(n1, n2 * c))
    return _fb_call(af.reshape(2, n1, n2, c), dft["f2"], dft["f2l"], nrm, n)


def kernel(x_prompt, x_sample, c, c_ctx, cache_mla_ckv, cache_mla_kpe, cache_gqa_k, cache_gqa_v, w_mod, b_mod, w_in, mla_g_qa, mla_w_uq, mla_g_kva, mla_w_ukv, mla_g_q, mla_g_k, hy_conv_w, hy_conv_b, hy_w_in, hy_b_in, hy_w_mid, hy_b_mid, hy_w_out, hy_b_out, hy_freq, hy_skip, gqa_g_q, gqa_g_k, g_out, w_out, w_router, b_router, w_moe_in, b_moe_in, w_moe_out, b_moe_out):
    batch, seq_c, d = x_prompt.shape
    nb_l, seq_l, _ = x_sample.shape
    depth = w_in.shape[0]
    past = cache_mla_ckv.shape[2]
    assert d == D_MODEL and seq_l % GRID_W == 0 and (2 * seq_l) % FFT_N1 == 0

    wts = _prep_weights(w_in, mla_g_qa, mla_w_uq, mla_g_kva, mla_w_ukv, mla_g_q, mla_g_k, gqa_g_q, gqa_g_k,
                        g_out, w_out, w_router, b_router, hy_w_in)
    wts["w_moe_in"] = _cast_call(w_moe_in.reshape(depth * N_EXPERTS, d, 2 * D_FF), 512,
                                 FF_CHUNK).reshape(w_moe_in.shape)
    wts["w_moe_out"] = _cast_call(w_moe_out.reshape(depth * N_EXPERTS, D_FF, d), 512).reshape(w_moe_out.shape)
    wts["b_moe_in"] = jnp.swapaxes(b_moe_in.reshape(depth, N_EXPERTS, 2, D_FF // FF_CHUNK, FF_CHUNK), 2, 3).reshape(
        depth, N_EXPERTS, 1, 2 * D_FF)
    wts["b_moe_out"] = b_moe_out.reshape(depth, N_EXPERTS, 1, d)
    hw = dict(hy_w_in=wts["hy_w_in"], hy_b_in=hy_b_in.reshape(depth, 1, HY_FO), hy_w_mid=hy_w_mid,
              hy_b_mid=hy_b_mid.reshape(depth, HY_INNER, 1, HY_FO), hy_w_out=hy_w_out,
              hy_b_out=hy_b_out.reshape(depth, 1, -1), hy_freq=hy_freq.reshape(depth, 1, HY_FO),
              hy_conv_w=hy_conv_w, hy_conv_b=hy_conv_b.reshape(depth, 1, -1), hy_skip=hy_skip)
    dft_l = _lat_dft(seq_l)
    hw["skip_t"] = jnp.tile(hy_skip, (1, 1, dft_l["n2"]))

    conds = jnp.zeros((8, d), F32).at[0].set(c_ctx).at[1:1 + nb_l].set(c)
    mod = _mod_call(conds, w_mod, b_mod).reshape(depth, 8, 6, d)
    mod_c, mod_l = mod[:, 0:1], mod[:, 1:1 + nb_l]

    rope_tabs = (_rope_tables(seq_l, ROPE_A, _lane_map_a), _rope_tables(seq_l, HD_C, _lane_map_c))
    kax, vax = _cachekv_call(cache_mla_ckv, jnp.pad(cache_mla_kpe, ((0, 0), (0, 0), (0, 0), (NOPE_A, LANE - QK_A))), wts)
    kcx = cache_gqa_k.reshape(nb_l, depth, past, KV_C * HD_C).astype(BF16)
    vcx = cache_gqa_v.reshape(nb_l, depth, past, KV_C * HD_C).astype(BF16)
    vcx = jnp.concatenate([vcx, jnp.ones_like(vcx)], axis=-1)

    z_c, dec_c = _hy_features(seq_c)
    cmat, smat, fwd_c, inv_c = _ctx_dft(seq_c)
    z_l, dec_l = _hy_features(seq_l)
    z_full = jnp.concatenate([z_l, jnp.zeros((1, HY_FO), F32), z_l[:0:-1]], axis=0)
    dec_full = jnp.concatenate([dec_l, jnp.zeros((1, HY_CH), F32), dec_l[:0:-1]], axis=0)

    xc = x_prompt.reshape(batch * seq_c, d)
    xl = x_sample.reshape(nb_l * seq_l, d)
    yc = yl = None
    new_ckv, new_kpe, new_k, new_v = [], [], [], []
    for l in range(depth):
        (xc, qa, ka, va, ckv, kpe, qc, kc, vc, kcf, vcf, u) = _premix_call(l, xc, yc, mod_c, seq_c, False, wts, None)
        new_ckv.append(ckv)
        new_kpe.append(kpe[:, NOPE_A:QK_A])
        new_k.append(kcf)
        new_v.append(vcf)
        oa, oc = _attn_ctx_call(qa, ka, va, qc, kc, vc, seq_c)
        kf_c = _filt_ctx_call(l, z_c, dec_c, hw, cmat, smat)
        ob = _hy_ctx_call(l, u, hw, kf_c, fwd_c, inv_c, seq_c)
        xc, h2, idx, gates = _postmix_call(l, xc, oa, ob, oc, mod_c, seq_c, wts)
        yc = _moe_call(l, h2, idx[:, :TOP_K], gates[:, :TOP_K], wts)
        (xl, qa, ka, va, _, _, qc, kc, vc, _, _, u) = _premix_call(l, xl, yl, mod_l, seq_l, True, wts, rope_tabs)
        oa, oc = _attn_lat_call(l, qa, ka, va, qc, kc, vc, kax, vax, kcx, vcx, seq_l)
        kf_l = _filter_lat(l, z_full, dec_full, hw, dft_l, seq_l)
        ob = _hyena_lat(l, u, hw, kf_l, dft_l, seq_l)
        xl, h2, idx, gates = _postmix_call(l, xl, oa, ob, oc, mod_l, seq_l, wts)
        yl = _moe_call(l, h2, idx[:, :TOP_K], gates[:, :TOP_K], wts)
    y_prompt = _final_call(depth - 1, xc, yc, mod_c, seq_c).reshape(batch, seq_c, d)
    y_sample = _final_call(depth - 1, xl, yl, mod_l, seq_l).reshape(nb_l, seq_l, d)
    stack = lambda xs, tail: jnp.stack([a.reshape((batch, seq_c) + tail) for a in xs], axis=1)
    return (y_prompt, y_sample, stack(new_ckv, (KV_RANK,)), stack(new_kpe, (ROPE_A,)),
            stack(new_k, (KV_C, HD_C)), stack(new_v, (KV_C, HD_C)))
```

```python
import functools
import math

import numpy as np
import jax
import jax.numpy as jnp
from jax import lax
from jax.experimental import pallas as pl
from jax.experimental.pallas import tpu as pltpu

F32 = jnp.float32
BF16 = jnp.bfloat16

D_MODEL = 1024
GRID_W = 64
EPS = 1e-6
ROPE_THETA = 10000.0
H_A = 6
Q_RANK = 256
KV_RANK = 128
NOPE_A = 64
ROPE_A = 32
V_A = 64
QK_A = NOPE_A + ROPE_A
HY_CH = 256
HY_ORDER = 2
HY_EMB = 33
HY_FO = 64
HY_INNER = 2
HY_MIN_DECAY = math.log(1e-2) / 1.5
HY_MAX_DECAY = math.log(1e-2) / 0.3
H_C = 6
KV_C = 2
HD_C = 64
N_EXPERTS = 32
TOP_K = 4
D_FF = 1024
SWIGLU_LIMIT = 7.0
SWIGLU_ALPHA = 1.702
LOG2E = 1.4426950408889634

LANE = 128
HQ = H_A * LANE
W_A = H_A * V_A
W_B = HY_CH
W_C = H_C * HD_C
IN_P = Q_RANK + KV_RANK + LANE + 3 * HY_CH + HQ + 2 * LANE
ROW_BLOCK = 128
TOKEN_TILE = 512
VMEM_LIMIT = 56 * 1024 * 1024
FFT_N1 = 128
FFT_N2 = 64


def _cp(sem, vmem=None):
    return pltpu.CompilerParams(dimension_semantics=sem, vmem_limit_bytes=vmem)


def _dot(a, b):
    return jnp.dot(a, b, preferred_element_type=F32)


def _split(a):
    hi = a.astype(BF16)
    return hi, (a - hi.astype(F32)).astype(BF16)


def _dot3(a, b):
    ah, al = _split(a)
    bh, bl = _split(b)
    return _dot(ah, bh) + _dot(ah, bl) + _dot(al, bh)


def _bdot(a, b):
    return lax.dot_general(a, b, (((2,), (1,)), ((0,), (0,))), preferred_element_type=F32)


def _rms(x, n=None):
    ss = jnp.sum(x * x, axis=-1, keepdims=True) * (1.0 / (n or x.shape[-1]))
    return x * lax.rsqrt(ss + EPS)


def _head_norm(x, nvalid):
    outs = []
    for h in range(x.shape[1] // LANE):
        blk = x[:, h * LANE:(h + 1) * LANE]
        ss = jnp.sum(blk * blk, axis=-1, keepdims=True) * (1.0 / nvalid)
        outs.append(blk * lax.rsqrt(ss + EPS))
    return outs[0] if len(outs) == 1 else jnp.concatenate(outs, axis=1)


def _rope(x, rope):
    tab_ref, perm_ref = rope
    c, s = tab_ref[0], tab_ref[1]
    xb = x.astype(BF16)
    nblk = x.shape[1] // LANE
    outs = []
    b = 0
    while b < nblk:
        w = 2 if nblk - b >= 2 else 1
        perm = perm_ref[...] if w == 2 else perm_ref[:LANE, :LANE]
        sw = _dot(xb[:, b * LANE:(b + w) * LANE], perm)
        for k in range(w):
            outs.append(x[:, (b + k) * LANE:(b + k + 1) * LANE] * c + sw[:, k * LANE:(k + 1) * LANE] * s)
        b += w
    return outs[0] if len(outs) == 1 else jnp.concatenate(outs, axis=1)


def _mod_kernel(c_ref, w_ref, b_ref, o_ref):
    c = c_ref[...]
    s = c * (1.0 / (1.0 + jnp.exp(-c)))
    o_ref[...] = _dot3(s, w_ref[...]) + b_ref[...]


def _mod_call(conds, w_mod, b_mod):
    depth, d, n6 = w_mod.shape
    tn = 1536
    return pl.pallas_call(
        _mod_kernel,
        out_shape=jax.ShapeDtypeStruct((depth, 8, n6), F32),
        grid=(depth, n6 // tn),
        in_specs=[pl.BlockSpec((8, d), lambda l, j: (0, 0)),
                  pl.BlockSpec((None, d, tn), lambda l, j: (l, 0, j)),
                  pl.BlockSpec((None, 1, tn), lambda l, j: (l, 0, j))],
        out_specs=pl.BlockSpec((None, 8, tn), lambda l, j: (l, 0, j)),
        compiler_params=_cp(("parallel", "parallel"), VMEM_LIMIT),
        name="mod",
    )(conds, w_mod, b_mod.reshape(depth, 1, n6))


def _cast_kernel(x_ref, o_ref):
    o_ref[...] = x_ref[...].astype(BF16)


def _cast_call(w, tr):
    n, r, c = w.shape
    return pl.pallas_call(
        _cast_kernel,
        out_shape=jax.ShapeDtypeStruct(w.shape, BF16),
        grid=(n, r // tr),
        in_specs=[pl.BlockSpec((None, tr, c), lambda i, j: (i, j, 0))],
        out_specs=pl.BlockSpec((None, tr, c), lambda i, j: (i, j, 0)),
        compiler_params=_cp(("parallel", "parallel"), VMEM_LIMIT),
        name="cast_bf16",
    )(w)


def _kv_heads(ckv_bf, kpe, w_kv_ref, gk, rope_ref, use_rope):
    kvp = _dot(ckv_bf, w_kv_ref[...])
    kn = kvp[:, :HQ]
    ka = jnp.concatenate([kn[:, h * LANE:(h + 1) * LANE] + kpe for h in range(H_A)], axis=1)
    ka = _head_norm(ka, QK_A) * gk
    if use_rope:
        ka = _rope(ka, rope_ref)
    return ka, kvp[:, HQ:]


def _premix_kernel(has_prev, use_rope, *refs):
    it = iter(refs)
    x_ref = next(it)
    if has_prev:
        y_ref, modp_ref = next(it), next(it)
    mod_ref, w_in_ref, w_uq_ref, w_kv_ref, g_ref = next(it), next(it), next(it), next(it), next(it)
    if use_rope:
        ra_ref, rc_ref = (next(it), next(it)), (next(it), next(it))
    else:
        ra_ref = rc_ref = None
    (xo_ref, qa_ref, ka_ref, va_ref, ckv_ref, kpe_ref, qc_ref, kc_ref, vc_ref, kcf_ref, vcf_ref, u_ref) = it

    x = x_ref[...]
    if has_prev:
        x = x + modp_ref[5:6, :] * y_ref[...]
    xo_ref[...] = x
    h = _rms(x) * (1.0 + mod_ref[1:2, :]) + mod_ref[0:1, :]
    proj = _dot(h.astype(BF16), w_in_ref[...])
    o = 0
    c_q = proj[:, o:o + Q_RANK]; o += Q_RANK
    c_kv = proj[:, o:o + KV_RANK]; o += KV_RANK
    kpe = proj[:, o:o + LANE]; o += LANE
    u_ref[...] = proj[:, o:o + 3 * HY_CH]; o += 3 * HY_CH
    q_c = proj[:, o:o + HQ]; o += HQ
    k_c = proj[:, o:o + LANE]; o += LANE
    v_c = proj[:, o:o + LANE]

    cqn = _rms(c_q) * g_ref[0:1, :Q_RANK]
    qa = _head_norm(_dot(cqn.astype(BF16), w_uq_ref[...]), QK_A) * g_ref[2:3, :]
    if use_rope:
        qa = _rope(qa, ra_ref)
    qa_ref[...] = qa.astype(BF16)
    ckv = _rms(c_kv) * g_ref[1:2, :KV_RANK]
    ckv_ref[...] = ckv
    kpe_ref[...] = kpe
    ka, va = _kv_heads(ckv.astype(BF16), kpe, w_kv_ref, g_ref[3:4, :], ra_ref, use_rope)
    ka_ref[...] = ka.astype(BF16)
    va_ref[...] = _with_ones(va.astype(BF16))

    qc = _head_norm(q_c, HD_C) * g_ref[4:5, :]
    if use_rope:
        qc = _rope(qc, rc_ref)
    qc_ref[...] = qc.astype(BF16)
    lane = lax.broadcasted_iota(jnp.int32, k_c.shape, 1)
    k2 = k_c * k_c
    s0 = jnp.sum(jnp.where(lane < HD_C, k2, 0.0), axis=-1, keepdims=True) * (1.0 / HD_C)
    s1 = jnp.sum(jnp.where(lane >= HD_C, k2, 0.0), axis=-1, keepdims=True) * (1.0 / HD_C)
    kcn = k_c * jnp.where(lane < HD_C, lax.rsqrt(s0 + EPS), lax.rsqrt(s1 + EPS)) * g_ref[5:6, :LANE]
    kcf_ref[...] = kcn
    vcf_ref[...] = v_c
    kc_ref[...] = (_rope(kcn, rc_ref) if use_rope else kcn).astype(BF16)
    vc_ref[...] = _with_ones(v_c.astype(BF16))


def _chunk_rows_spec(y, tm):
    per = (y.shape[1] - SUB) // tm
    return pl.BlockSpec((None, tm, y.shape[2]), lambda i: (i // per, i % per, 0))


def _premix_call(l, x, yprev, mod, seq, use_rope, wts, rope_tabs):
    t, d = x.shape
    tm = TOKEN_TILE
    ncond = mod.shape[1]
    has_prev = yprev is not None

    def cond(i):
        return (i * tm) // seq if ncond > 1 else 0

    row = lambda w: pl.BlockSpec((tm, w), lambda i: (i, 0))
    ins, specs = [x], [row(d)]
    if has_prev:
        ins += [yprev, mod]
        specs += [_chunk_rows_spec(yprev, tm), pl.BlockSpec((None, None, 6, d), lambda i: (l - 1, cond(i), 0, 0))]
    ins += [mod, wts["w_in"], wts["w_uq"], wts["w_kv"], wts["gains"]]
    specs += [pl.BlockSpec((None, None, 6, d), lambda i: (l, cond(i), 0, 0)),
              pl.BlockSpec((None, d, IN_P), lambda i: (l, 0, 0)),
              pl.BlockSpec((None, Q_RANK, HQ), lambda i: (l, 0, 0)),
              pl.BlockSpec((None, KV_RANK, HQ + W_A), lambda i: (l, 0, 0)),
              pl.BlockSpec((None, 8, HQ), lambda i: (l, 0, 0))]
    if use_rope:
        nt = seq // tm
        tab = pl.BlockSpec((2, tm, LANE), lambda i: (0, i % nt, 0))
        perm = pl.BlockSpec((2 * LANE, 2 * LANE), lambda i: (0, 0))
        ins += [rope_tabs[0][0], rope_tabs[0][1], rope_tabs[1][0], rope_tabs[1][1]]
        specs += [tab, perm, tab, perm]
    outs = [(d, F32), (HQ, BF16), (HQ, BF16), (2 * W_A, BF16), (KV_RANK, F32), (LANE, F32), (HQ, BF16),
            (LANE, BF16), (2 * LANE, BF16), (LANE, F32), (LANE, F32), (3 * HY_CH, F32)]
    return pl.pallas_call(
        functools.partial(_premix_kernel, has_prev, use_rope),
        out_shape=[jax.ShapeDtypeStruct((t, w), dt) for w, dt in outs],
        grid=(t // tm,),
        in_specs=specs,
        out_specs=[row(w) for w, _ in outs],
        compiler_params=_cp(("parallel",), VMEM_LIMIT),
        name="premix",
    )(*ins)


def _cachekv_kernel(ckv_ref, kpe_ref, w_kv_ref, g_ref, ka_ref, va_ref):
    ka, va = _kv_heads(ckv_ref[...].astype(BF16), kpe_ref[...], w_kv_ref, g_ref[3:4, :], None, False)
    ka_ref[...] = ka.astype(BF16)
    va_ref[...] = _with_ones(va.astype(BF16))


def _cachekv_call(cache_ckv, cache_kpe_p, wts):
    nb, depth, past, _ = cache_ckv.shape
    return pl.pallas_call(
        _cachekv_kernel,
        out_shape=[jax.ShapeDtypeStruct((depth, nb, past, HQ), BF16),
                   jax.ShapeDtypeStruct((depth, nb, past, 2 * W_A), BF16)],
        grid=(depth, nb),
        in_specs=[pl.BlockSpec((None, None, past, KV_RANK), lambda l, b: (b, l, 0, 0)),
                  pl.BlockSpec((None, None, past, LANE), lambda l, b: (b, l, 0, 0)),
                  pl.BlockSpec((None, KV_RANK, HQ + W_A), lambda l, b: (l, 0, 0)),
                  pl.BlockSpec((None, 8, HQ), lambda l, b: (l, 0, 0))],
        out_specs=[pl.BlockSpec((None, None, past, HQ), lambda l, b: (l, b, 0, 0)),
                   pl.BlockSpec((None, None, past, 2 * W_A), lambda l, b: (l, b, 0, 0))],
        compiler_params=_cp(("parallel", "parallel"), VMEM_LIMIT),
        name="cache_kv",
    )(cache_ckv, cache_kpe_p, wts["w_kv"], wts["gains"])


def _nt(q, k):
    return lax.dot_general(q, k, (((1,), (1,)), ((), ())), preferred_element_type=F32)


def _with_ones(v):
    ones = jnp.ones((v.shape[0], LANE), v.dtype)
    parts = []
    for j in range(v.shape[1] // LANE):
        parts += [v[:, j * LANE:(j + 1) * LANE], ones]
    return jnp.concatenate(parts, axis=1)


def _attend(q, ks, vs):
    return _softmax_pv([_nt(q, k) for k in ks], vs)


def _softmax_pv(ss, vs, m=None):
    if m is None:
        m = ss[0].max(axis=-1, keepdims=True)
        for s in ss[1:]:
            m = jnp.maximum(m, s.max(axis=-1, keepdims=True))
    acc = None
    for s, v in zip(ss, vs):
        pv = _dot(jnp.exp2(s - m).astype(BF16), v)
        acc = pv if acc is None else acc + pv
    return acc[:, :LANE] / acc[:, LANE:]


BOUND_SLACK = 1.01
BOUND_LIMIT = 60.0


def _attn_kernel(nseg, bound, *refs):
    qa_ref, qc_ref = refs[0], refs[1]
    if bound:
        kq_ref, refs = refs[2], refs[:2] + refs[3:]
    segs = refs[2:2 + 4 * nseg]
    oa_ref, oc_ref = refs[2 + 4 * nseg:]
    ka_refs, va_refs, kc_refs, vc_refs = (segs[i::4] for i in range(4))
    lane = lax.broadcasted_iota(jnp.int32, (qa_ref.shape[0], LANE), 1)
    low = lane < V_A
    g_per = H_C // KV_C
    jobs = [("a", j, h) for j in range(H_A // 2) for h in (2 * j, 2 * j + 1)]
    jobs += [("c", g, h) for g in range(g_per) for h in (g, g + g_per)]

    def scores(job):
        kind, _, h = job
        hs = slice(h * LANE, (h + 1) * LANE)
        if kind == "a":
            return [_nt(qa_ref[:, hs], r[:, hs]) for r in ka_refs]
        return [_nt(qc_ref[:, hs], r[...]) for r in kc_refs]

    def values(job):
        kind, j, _ = job
        if kind == "a":
            return [r[:, 2 * j * LANE:2 * (j + 1) * LANE] for r in va_refs]
        return [r[...] for r in vc_refs]

    def row_bound(job):
        kind, _, h = job
        q = (qa_ref if kind == "a" else qc_ref)[:, h * LANE:(h + 1) * LANE].astype(F32)
        r = h if kind == "a" else H_A + h // g_per
        q2 = jnp.sum(q * q, axis=-1, keepdims=True)
        return jnp.sqrt(q2 * kq_ref[r:r + 1, 0:1]) * BOUND_SLACK

    ss_next = None if bound else scores(jobs[0])
    pv = []
    for n, job in enumerate(jobs):
        if bound:
            pv.append(_softmax_pv(scores(job), values(job), row_bound(job)))
        else:
            ss = ss_next
            if n + 1 < len(jobs):
                ss_next = scores(jobs[n + 1])
            pv.append(_softmax_pv(ss, values(job)))
        if len(pv) == 2:
            o_ref = oa_ref if job[0] == "a" else oc_ref
            o_ref[:, job[1] * LANE:(job[1] + 1) * LANE] = jnp.where(low, pv[0], pv[1])
            pv = []


def _attn_ctx_call(qa, ka, va, qc, kc, vc, seq):
    t = qa.shape[0]
    blk = lambda w: pl.BlockSpec((seq, w), lambda b: (b, 0))
    return pl.pallas_call(
        functools.partial(_attn_kernel, 1, False),
        out_shape=[jax.ShapeDtypeStruct((t, W_A), F32), jax.ShapeDtypeStruct((t, W_C), F32)],
        grid=(t // seq,),
        in_specs=[blk(HQ), blk(HQ), blk(HQ), blk(2 * W_A), blk(LANE), blk(2 * LANE)],
        out_specs=[blk(W_A), blk(W_C)],
        compiler_params=_cp(("parallel",), VMEM_LIMIT),
        name="attn_ctx",
    )(qa, qc, ka, va, kc, vc)


def _attn_lat_call(l, qa, ka, va, qc, kc, vc, kax, vax, kcx, vcx, seq):
    t = qa.shape[0]
    nb = t // seq
    tq = min(256, seq)
    nq = seq // tq
    past = kax.shape[2]
    qblk = lambda w: pl.BlockSpec((tq, w), lambda b, i: (b * nq + i, 0))
    sblk = lambda w: pl.BlockSpec((seq, w), lambda b, i: (b, 0))
    xblk = lambda w: pl.BlockSpec((None, None, past, w), lambda b, i: (l, b, 0, 0))
    cblk = lambda w: pl.BlockSpec((None, None, past, w), lambda b, i: (b, l, 0, 0))
    full = lambda w: pl.BlockSpec((seq, w), lambda b: (b, 0))

    kq = pl.pallas_call(
        _knorm_kernel,
        out_shape=jax.ShapeDtypeStruct((nb, NORM_ROWS, LANE), F32),
        grid=(nb,),
        in_specs=[full(HQ), pl.BlockSpec((None, None, past, HQ), lambda b: (l, b, 0, 0)),
                  full(LANE), pl.BlockSpec((None, None, past, LANE), lambda b: (b, l, 0, 0)), full(HQ), full(HQ)],
        out_specs=pl.BlockSpec((None, NORM_ROWS, LANE), lambda b: (b, 0, 0)),
        compiler_params=_cp(("parallel",), VMEM_LIMIT),
        name="attn_norms",
    )(ka, kax, kc, kcx, qa, qc)
    k2, q2a, q2c = kq[:, :H_A + KV_C, 0], kq[:, 8:8 + H_A, 0], kq[:, 8 + H_A:8 + H_A + H_C, 0]
    k2c = jnp.repeat(k2[:, H_A:], H_C // KV_C, axis=1)
    worst = jnp.sqrt(jnp.maximum(jnp.max(q2a * k2[:, :H_A]), jnp.max(q2c * k2c))) * BOUND_SLACK

    def call(bound):
        extra_specs = [pl.BlockSpec((None, NORM_ROWS, LANE), lambda b, i: (b, 0, 0))] if bound else []
        return pl.pallas_call(
            functools.partial(_attn_kernel, 2, bound),
            out_shape=[jax.ShapeDtypeStruct((t, W_A), F32), jax.ShapeDtypeStruct((t, W_C), F32)],
            grid=(nb, nq),
            in_specs=[qblk(HQ), qblk(HQ)] + extra_specs
            + [xblk(HQ), xblk(2 * W_A), cblk(LANE), cblk(2 * LANE),
               sblk(HQ), sblk(2 * W_A), sblk(LANE), sblk(2 * LANE)],
            out_specs=[qblk(W_A), qblk(W_C)],
            compiler_params=_cp(("parallel", "parallel"), VMEM_LIMIT),
            name="attn_lat_bound" if bound else "attn_lat",
        )(qa, qc, *([kq] if bound else []), kax, vax, kcx, vcx, ka, va, kc, vc)

    return lax.cond(worst < BOUND_LIMIT, lambda: call(True), lambda: call(False))


NORM_ROWS = 24


def _knorm_kernel(ka_ref, kax_ref, kc_ref, kcx_ref, qa_ref, qc_ref, o_ref):
    def max_n2(x, lanes=None):
        xf = x.astype(F32)
        x2 = xf * xf
        if lanes is not None:
            lane = lax.broadcasted_iota(jnp.int32, x2.shape, 1)
            x2 = jnp.where((lane >= lanes[0]) & (lane < lanes[1]), x2, 0.0)
        return jnp.max(jnp.sum(x2, axis=-1, keepdims=True), axis=0, keepdims=True)

    rows = []
    for h in range(H_A):
        hs = slice(h * LANE, (h + 1) * LANE)
        rows.append(jnp.maximum(max_n2(ka_ref[:, hs]), max_n2(kax_ref[:, hs])))
    for kv in range(KV_C):
        lanes = (kv * HD_C, (kv + 1) * HD_C)
        rows.append(jnp.maximum(max_n2(kc_ref[...], lanes), max_n2(kcx_ref[...], lanes)))
    rows += [max_n2(qa_ref[:, h * LANE:(h + 1) * LANE]) for h in range(H_A)]
    rows += [max_n2(qc_ref[:, h * LANE:(h + 1) * LANE]) for h in range(H_C)]
    o_ref[...] = jnp.zeros_like(o_ref)
    for r, v in enumerate(rows):
        o_ref[r:r + 1, :] = jnp.broadcast_to(v, (1, LANE))


def _filter_mlp(z, w_in_ref, b_in_ref, w_mid_ref, b_mid_ref, w_out_ref, b_out_ref, freq_ref):
    freq = freq_ref[...]
    a = jnp.sin(freq * (_dot3(z, w_in_ref[...]) + b_in_ref[...]))
    for i in range(HY_INNER):
        a = jnp.sin(freq * (_dot3(a, w_mid_ref[i]) + b_mid_ref[i]))
    return _dot3(a, w_out_ref[...]) + b_out_ref[...]


def _conv3(u, up, dn, cw_ref, cb_ref):
    return up * cw_ref[0:1, :] + u * cw_ref[1:2, :] + dn * cw_ref[2:3, :] + cb_ref[...]


def _filt_ctx_kernel(z_ref, dec_ref, w_in_ref, b_in_ref, w_mid_ref, b_mid_ref, w_out_ref, b_out_ref, freq_ref,
                     c_ref, s_ref, o_ref):
    n = z_ref.shape[0]
    h = _filter_mlp(z_ref[...], w_in_ref, b_in_ref, w_mid_ref, b_mid_ref, w_out_ref, b_out_ref, freq_ref)
    dec = dec_ref[...]
    row = lax.broadcasted_iota(jnp.int32, dec.shape, 0)
    half = HY_ORDER * HY_CH
    for o in range(HY_ORDER):
        hf = h[:, o * HY_CH:(o + 1) * HY_CH] * dec
        hb = jnp.where(row > 0, h[:, half + o * HY_CH:half + (o + 1) * HY_CH] * dec, 0.0)
        nrm = jnp.sum(jnp.abs(hf) + jnp.abs(hb), axis=0, keepdims=True) + EPS
        scale = (1.0 / n) / nrm
        o_ref[o, 0] = _dot3(c_ref[...], hf + hb) * scale
        o_ref[o, 1] = -_dot3(s_ref[...], hf - hb) * scale


def _filt_ctx_call(l, z, dec, hw, cmat, smat):
    n = z.shape[0]
    full = lambda a: pl.BlockSpec((None,) + a.shape[1:], lambda i: (l,) + (0,) * (a.ndim - 1))
    const = lambda a: pl.BlockSpec(a.shape, lambda i: (0,) * a.ndim)
    names = ["hy_w_in", "hy_b_in", "hy_w_mid", "hy_b_mid", "hy_w_out", "hy_b_out", "hy_freq"]
    return pl.pallas_call(
        _filt_ctx_kernel,
        out_shape=jax.ShapeDtypeStruct((HY_ORDER, 2, n, HY_CH), F32),
        grid=(1,),
        in_specs=[const(z), const(dec)] + [full(hw[k]) for k in names] + [const(cmat), const(smat)],
        out_specs=pl.BlockSpec((HY_ORDER, 2, n, HY_CH), lambda i: (0, 0, 0, 0)),
        compiler_params=_cp(("arbitrary",), VMEM_LIMIT),
        name="hy_filter_ctx",
    )(z, dec, *[hw[k] for k in names], cmat, smat)


def _hy_ctx_kernel(u_ref, cw_ref, cb_ref, skip_ref, kf_ref, fwd_ref, inv_ref, o_ref):
    u = u_ref[...]
    n = u.shape[0]
    row = lax.broadcasted_iota(jnp.int32, u.shape, 0)
    up = jnp.where(row > 0, pltpu.roll(u, 1, 0), 0.0)
    dn = jnp.where(row < n - 1, pltpu.roll(u, n - 1, 0), 0.0)
    z = _conv3(u, up, dn, cw_ref, cb_ref)
    s = z[:, :HY_CH]
    gates = (z[:, HY_CH:2 * HY_CH], z[:, 2 * HY_CH:])
    for o in range(HY_ORDER):
        xs = _dot(fwd_ref[...], s.astype(BF16))
        xr, xi = xs[:n], xs[n:]
        kr, ki = kf_ref[o, 0], kf_ref[o, 1]
        ycat = jnp.concatenate([xr * kr - xi * ki, xr * ki + xi * kr], axis=0)
        y = _dot(inv_ref[...], ycat.astype(BF16))
        s = gates[o] * (y + s * skip_ref[o:o + 1, :])
    o_ref[...] = s


def _hy_ctx_call(l, u, hw, kf, fwd, inv, seq):
    t = u.shape[0]
    full = lambda a: pl.BlockSpec((None,) + a.shape[1:], lambda b: (l,) + (0,) * (a.ndim - 1))
    const = lambda a: pl.BlockSpec(a.shape, lambda b: (0,) * a.ndim)
    return pl.pallas_call(
        _hy_ctx_kernel,
        out_shape=jax.ShapeDtypeStruct((t, HY_CH), F32),
        grid=(t // seq,),
        in_specs=[pl.BlockSpec((seq, 3 * HY_CH), lambda b: (b, 0)),
                  full(hw["hy_conv_w"]), full(hw["hy_conv_b"]), full(hw["hy_skip"]),
                  const(kf), const(fwd), const(inv)],
        out_specs=pl.BlockSpec((seq, HY_CH), lambda b: (b, 0)),
        compiler_params=_cp(("parallel",), VMEM_LIMIT),
        name="hyena_ctx",
    )(u, hw["hy_conv_w"], hw["hy_conv_b"], hw["hy_skip"], kf, fwd, inv)


def _filt_lat_kernel(seq, z_ref, dec_ref, w_in_ref, b_in_ref, w_mid_ref, b_mid_ref, w_out_ref, b_out_ref, freq_ref,
                     k_ref, n_ref):
    i = pl.program_id(0)
    tr = z_ref.shape[0]
    h = _filter_mlp(z_ref[...], w_in_ref, b_in_ref, w_mid_ref, b_mid_ref, w_out_ref, b_out_ref, freq_ref)
    half = HY_ORDER * HY_CH
    row = i * tr + lax.broadcasted_iota(jnp.int32, (tr, half), 0)
    dec = dec_ref[...]
    kern = jnp.where(row < seq, h[:, :half], h[:, half:]) * jnp.concatenate([dec] * HY_ORDER, axis=1)
    k_ref[...] = kern

    @pl.when(i == 0)
    def _():
        n_ref[...] = jnp.zeros_like(n_ref)

    n_ref[...] += jnp.sum(jnp.abs(kern), axis=0, keepdims=True)


def _filt_lat_call(l, z, dec, hw, seq):
    n = z.shape[0]
    tr = min(512, n)
    half = HY_ORDER * HY_CH
    full = lambda a: pl.BlockSpec((None,) + a.shape[1:], lambda i: (l,) + (0,) * (a.ndim - 1))
    names = ["hy_w_in", "hy_b_in", "hy_w_mid", "hy_b_mid", "hy_w_out", "hy_b_out", "hy_freq"]
    return pl.pallas_call(
        functools.partial(_filt_lat_kernel, seq),
        out_shape=[jax.ShapeDtypeStruct((n, half), F32), jax.ShapeDtypeStruct((1, half), F32)],
        grid=(n // tr,),
        in_specs=[pl.BlockSpec((tr, z.shape[1]), lambda i: (i, 0)), pl.BlockSpec((tr, HY_CH), lambda i: (i, 0))]
        + [full(hw[k]) for k in names],
        out_specs=[pl.BlockSpec((tr, half), lambda i: (i, 0)), pl.BlockSpec((1, half), lambda i: (0, 0))],
        compiler_params=_cp(("arbitrary",), VMEM_LIMIT),
        name="hy_filter_lat",
    )(z, dec, *[hw[k] for k in names])


def _fa_kernel(f1_ref, k_ref, o_ref):
    r = _dot3(f1_ref[...], k_ref[...])
    o_ref[0] = r[:FFT_N1]
    o_ref[1] = r[FFT_N1:]


def _fa_call(f1f, kern2d):
    n1, w = kern2d.shape
    tn = min(2048, w)
    return pl.pallas_call(
        _fa_kernel,
        out_shape=jax.ShapeDtypeStruct((2, FFT_N1, w), F32),
        grid=(w // tn,),
        in_specs=[pl.BlockSpec(f1f.shape, lambda j: (0, 0)), pl.BlockSpec((n1, tn), lambda j: (0, j))],
        out_specs=pl.BlockSpec((2, FFT_N1, tn), lambda j: (0, 0, j)),
        compiler_params=_cp(("parallel",), VMEM_LIMIT),
        name="hy_filter_dft1",
    )(f1f, kern2d)


def _fb_kernel(n_total, a_ref, fh_ref, fl_ref, n_ref, o_ref):
    a = jnp.concatenate([a_ref[0], a_ref[1]], axis=1)
    ah, al = _split(a)
    x = _bdot(fh_ref[...], ah) + _bdot(fh_ref[...], al) + _bdot(fl_ref[...], ah)
    scale = (1.0 / n_total) / (n_ref[...] + EPS)
    o_ref[...] = x * scale[None]


def _fb_call(af5, f2h, f2l, nrm, n_total):
    _, n1, n2, c = af5.shape[0], af5.shape[1], af5.shape[2], af5.shape[3]
    k1t = 8
    return pl.pallas_call(
        functools.partial(_fb_kernel, n_total),
        out_shape=jax.ShapeDtypeStruct((n1, 2 * n2, c), F32),
        grid=(n1 // k1t,),
        in_specs=[pl.BlockSpec((2, k1t, n2, c), lambda j: (0, j, 0, 0)),
                  pl.BlockSpec((k1t, 2 * n2, 2 * n2), lambda j: (j, 0, 0)),
                  pl.BlockSpec((k1t, 2 * n2, 2 * n2), lambda j: (j, 0, 0)),
                  pl.BlockSpec((1, c), lambda j: (0, 0))],
        out_specs=pl.BlockSpec((k1t, 2 * n2, c), lambda j: (j, 0, 0)),
        compiler_params=_cp(("parallel",), VMEM_LIMIT),
        name="hy_filter_dft2",
    )(af5, f2h, f2l, nrm)


def _hconv_kernel(seq, u_ref, p_ref, n_ref, cw_ref, cb_ref, v_ref, x1_ref, x2_ref):
    i = pl.program_id(0)
    u = u_ref[...]
    tt = u.shape[0]
    row = lax.broadcasted_iota(jnp.int32, u.shape, 0)
    pos = (i * tt) % seq
    prev = jnp.where(pos > 0, p_ref[7:8, :], 0.0)
    nxt = jnp.where(pos + tt < seq, n_ref[0:1, :], 0.0)
    up = jnp.where(row > 0, pltpu.roll(u, 1, 0), prev)
    dn = jnp.where(row < tt - 1, pltpu.roll(u, tt - 1, 0), nxt)
    z = _conv3(u, up, dn, cw_ref, cb_ref)
    v_ref[...] = z[:, :HY_CH]
    x1_ref[...] = z[:, HY_CH:2 * HY_CH]
    x2_ref[...] = z[:, 2 * HY_CH:]


def _hconv_call(l, u, hw, seq):
    t, w = u.shape
    tt = min(512, seq)
    nblk8 = t // 8
    full = lambda a: pl.BlockSpec((None,) + a.shape[1:], lambda i: (l,) + (0,) * (a.ndim - 1))
    ob = pl.BlockSpec((tt, HY_CH), lambda i: (i, 0))
    return pl.pallas_call(
        functools.partial(_hconv_kernel, seq),
        out_shape=[jax.ShapeDtypeStruct((t, HY_CH), F32)] * 3,
        grid=(t // tt,),
        in_specs=[pl.BlockSpec((tt, w), lambda i: (i, 0)),
                  pl.BlockSpec((8, w), lambda i: (jnp.maximum(i * (tt // 8) - 1, 0), 0)),
                  pl.BlockSpec((8, w), lambda i: (jnp.minimum((i + 1) * (tt // 8), nblk8 - 1), 0)),
                  full(hw["hy_conv_w"]), full(hw["hy_conv_b"])],
        out_specs=[ob, ob, ob],
        compiler_params=_cp(("parallel",), VMEM_LIMIT),
        name="hyena_conv3",
    )(u, u, u, hw["hy_conv_w"], hw["hy_conv_b"])


def _ha_kernel(f1_ref, x_ref, o_ref):
    r = _dot(f1_ref[...], x_ref[...].astype(BF16))
    o_ref[0] = r[:FFT_N1].astype(BF16)
    o_ref[1] = r[FFT_N1:].astype(BF16)


def _ha_call(f1d, x2d, nb):
    rows, w = x2d.shape
    n1h = rows // nb
    tn = min(2048, w)
    return pl.pallas_call(
        _ha_kernel,
        out_shape=jax.ShapeDtypeStruct((nb, 2, FFT_N1, w), BF16),
        grid=(nb, w // tn),
        in_specs=[pl.BlockSpec(f1d.shape, lambda b, j: (0, 0)), pl.BlockSpec((n1h, tn), lambda b, j: (b, j))],
        out_specs=pl.BlockSpec((None, 2, FFT_N1, tn), lambda b, j: (b, 0, 0, j)),
        compiler_params=_cp(("parallel", "parallel"), VMEM_LIMIT),
        name="hyena_dft1",
    )(f1d, x2d)


def _hb_kernel(a_ref, f_ref, g_ref, kf_ref, o_ref):
    n2 = a_ref.shape[2]
    a = jnp.concatenate([a_ref[0], a_ref[1]], axis=1)
    x = _bdot(f_ref[...], a)
    xr, xi = x[:, :n2], x[:, n2:]
    kr, ki = kf_ref[:, :n2], kf_ref[:, n2:]
    y = jnp.concatenate([xr * kr - xi * ki, xr * ki + xi * kr], axis=1).astype(BF16)
    b = _bdot(g_ref[...], y)
    o_ref[0] = b[:, :n2].astype(BF16)
    o_ref[1] = b[:, n2:].astype(BF16)


def _hb_call(o, a5, f2, g2, kf):
    nb, _, n1, n2, c = a5.shape
    k1t = 16
    blk = pl.BlockSpec((None, 2, k1t, n2, c), lambda b, j: (b, 0, j, 0, 0))
    mat = pl.BlockSpec((k1t, 2 * n2, 2 * n2), lambda b, j: (j, 0, 0))
    return pl.pallas_call(
        _hb_kernel,
        out_shape=jax.ShapeDtypeStruct(a5.shape, BF16),
        grid=(nb, n1 // k1t),
        in_specs=[blk, mat, mat, pl.BlockSpec((k1t, 2 * n2, c), lambda b, j: (j, 0, o))],
        out_specs=blk,
        compiler_params=_cp(("parallel", "parallel"), VMEM_LIMIT),
        name="hyena_dft2",
    )(a5, f2, g2, kf)


def _hc_kernel(o, fc_ref, b_ref, s_ref, g_ref, skip_ref, o_ref):
    bcat = jnp.concatenate([b_ref[0], b_ref[1]], axis=0)
    y = _dot(fc_ref[...], bcat)
    o_ref[...] = g_ref[...] * (y + s_ref[...] * skip_ref[o:o + 1, :])


def _hc_call(l, o, fc, b4, s2d, g2d, skip_t):
    nb, _, n1, w = b4.shape
    rows = s2d.shape[0] // nb
    tn = min(2048, w)
    blk = pl.BlockSpec((rows, tn), lambda b, j: (b, j))
    return pl.pallas_call(
        functools.partial(_hc_kernel, o),
        out_shape=jax.ShapeDtypeStruct(s2d.shape, F32),
        grid=(nb, w // tn),
        in_specs=[pl.BlockSpec(fc.shape, lambda b, j: (0, 0)),
                  pl.BlockSpec((None, 2, n1, tn), lambda b, j: (b, 0, 0, j)),
                  blk, blk,
                  pl.BlockSpec((None, HY_ORDER, tn), lambda b, j: (l, 0, j))],
        out_specs=blk,
        compiler_params=_cp(("parallel", "parallel"), VMEM_LIMIT),
        name="hyena_dft3",
    )(fc, b4, s2d, g2d, skip_t)


def _postmix_kernel(x_ref, oa_ref, ob_ref, oc_ref, mod_ref, g_ref, wa_ref, wb_ref, wc_ref, wrh_ref, wrl_ref, br_ref,
                    x1_ref, h2_ref, idx_ref, gate_ref):
    na = _rms(oa_ref[...]) * g_ref[:, :W_A]
    nb = _rms(ob_ref[...]) * g_ref[:, W_A:W_A + W_B]
    nc = _rms(oc_ref[...]) * g_ref[:, W_A + W_B:]
    mix = (_dot(na.astype(BF16), wa_ref[...]) + _dot(nb.astype(BF16), wb_ref[...])
           + _dot(nc.astype(BF16), wc_ref[...]))
    x1 = x_ref[...] + mod_ref[2:3, :] * mix
    x1_ref[...] = x1
    h2 = _rms(x1) * (1.0 + mod_ref[4:5, :]) + mod_ref[3:4, :]
    h2_ref[...] = h2
    hh, hl = _split(h2)
    vals = _dot(hh, wrh_ref[...]) + _dot(hh, wrl_ref[...]) + _dot(hl, wrh_ref[...]) + br_ref[...]
    lane = lax.broadcasted_iota(jnp.int32, vals.shape, 1).astype(F32)
    idx_out = jnp.zeros(vals.shape, F32)
    top = jnp.zeros(vals.shape, F32)
    m0 = None
    for k in range(TOP_K):
        m = vals.max(axis=-1, keepdims=True)
        sel = jnp.min(jnp.where(vals == m, lane, float(LANE)), axis=-1, keepdims=True)
        if m0 is None:
            m0 = m
        idx_out = jnp.where(lane == k, sel, idx_out)
        top = jnp.where(lane == k, jnp.exp(m - m0), top)
        vals = jnp.where(lane == sel, -jnp.inf, vals)
    idx_ref[...] = idx_out.astype(jnp.int32)
    gate_ref[...] = top / jnp.sum(top, axis=-1, keepdims=True)


def _postmix_call(l, x, oa, ob, oc, mod, seq, wts):
    t, d = x.shape
    tm = TOKEN_TILE
    ncond = mod.shape[1]

    def cond(i):
        return (i * tm) // seq if ncond > 1 else 0

    row = lambda w: pl.BlockSpec((tm, w), lambda i: (i, 0))
    lay = lambda a: pl.BlockSpec((None,) + a.shape[1:], lambda i: (l,) + (0,) * (a.ndim - 1))
    names = ["g_out", "wo_a", "wo_b", "wo_c", "wr_hi", "wr_lo", "b_router"]
    return pl.pallas_call(
        _postmix_kernel,
        out_shape=[jax.ShapeDtypeStruct((t, d), F32), jax.ShapeDtypeStruct((t, d), F32),
                   jax.ShapeDtypeStruct((t, LANE), jnp.int32), jax.ShapeDtypeStruct((t, LANE), F32)],
        grid=(t // tm,),
        in_specs=[row(d), row(W_A), row(W_B), row(W_C),
                  pl.BlockSpec((None, None, 6, d), lambda i: (l, cond(i), 0, 0))] + [lay(wts[k]) for k in names],
        out_specs=[row(d), row(d), row(LANE), row(LANE)],
        compiler_params=_cp(("parallel",), VMEM_LIMIT),
        name="postmix",
    )(x, oa, ob, oc, mod, *[wts[k] for k in names])


SUB = 8


NGRP = ROW_BLOCK // SUB


def _moe_kernel(l, be_ref, rs_ref, nv_ref, nu_ref, run_ref, nxt_ref, tok_ref, gate_ref,
                h_ref, wi_hbm, bi_ref, wo_hbm, bo_ref, y_ref, buf, obuf, xb, wi_buf, wo_buf, sem):
    s = pl.program_id(0)
    tc = h_ref.shape[0]

    def weight_copies(e, slot):
        return (pltpu.make_async_copy(wi_hbm.at[l, e], wi_buf.at[slot], sem.at[slot, 0]),
                pltpu.make_async_copy(wo_hbm.at[l, e], wo_buf.at[slot], sem.at[slot, 1]))

    @pl.when(s == 0)
    def _():
        y_ref[...] = jnp.zeros_like(y_ref)
        buf[...] = jnp.zeros_like(buf)
        obuf[...] = jnp.zeros_like(obuf)
        for cp in weight_copies(be_ref[0], 0):
            cp.start()

    run = run_ref[s]
    wslot = lax.rem(run, 2)
    first = jnp.logical_or(s == 0, run != run_ref[jnp.maximum(s - 1, 0)])

    @pl.when(jnp.logical_and(first, s < nu_ref[0] + 2))
    def _():
        for cp in weight_copies(be_ref[s], wslot):
            cp.wait()

        @pl.when(nxt_ref[s] >= 0)
        def _():
            for cp in weight_copies(nxt_ref[s], 1 - wslot):
                cp.start()

    @pl.when(s < nu_ref[0] + 2)
    def _():
        slot_g = lax.rem(s, 2)
        slot_c = 1 - slot_g

        xb[...] = buf[slot_c].reshape(ROW_BLOCK, buf.shape[3]).astype(BF16)

        base_s = rs_ref[s]
        nv = nv_ref[s]
        for g in range(NGRP):
            ts = [jnp.where(g * SUB + j < nv, tok_ref[base_s + g * SUB + j], tc) for j in range(SUB)]
            new = [y_ref[pl.ds(ts[j], 1), :] + gate_ref[base_s + g * SUB + j] * obuf[slot_g, g, j:j + 1, :]
                   for j in range(SUB)]
            for j in range(SUB):
                y_ref[pl.ds(ts[j], 1), :] = new[j]

        base_g = rs_ref[s + 2]
        for g in range(NGRP):
            rows = [h_ref[pl.ds(tok_ref[base_g + g * SUB + j], 1), :] for j in range(SUB)]
            for j in range(SUB):
                buf[slot_g, g, j:j + 1, :] = rows[j]

        gu = _dot(xb[...], wi_buf[wslot]) + bi_ref[...]
        gt = jnp.minimum(gu[:, :D_FF], SWIGLU_LIMIT)
        lin = jnp.clip(gu[:, D_FF:], -SWIGLU_LIMIT, SWIGLU_LIMIT)
        act = (lin + 1.0) * gt * (1.0 / (1.0 + jnp.exp(-SWIGLU_ALPHA * gt)))
        out = _dot(act.astype(BF16), wo_buf[wslot]) + bo_ref[...]
        obuf[slot_c] = out.reshape(obuf.shape[1:])


def _moe_kernel_entry(has_alias, l, *refs):
    refs = list(refs)
    if has_alias:
        del refs[13]
    _moe_kernel(l, *refs)


def _route(idx, gates, n_blocks):
    m = idx.shape[0] * TOP_K
    e = idx.reshape(m)
    flat = jnp.arange(m, dtype=jnp.int32)
    skey, gate = lax.sort((e * m + flat, gates.reshape(m)), num_keys=1)
    tok = (skey % m) // TOP_K
    experts = jnp.arange(N_EXPERTS, dtype=jnp.int32)
    cnt = jnp.sum((e[:, None] == experts[None, :]).astype(jnp.int32), axis=0)
    nblk = (cnt + ROW_BLOCK - 1) // ROW_BLOCK
    bend = jnp.cumsum(nblk)
    n_used = bend[-1]
    blk = jnp.arange(-2, n_blocks + 2, dtype=jnp.int32)
    bcl = jnp.clip(blk, 0, n_used - 1)
    be = jnp.sum((bend[None, :] <= bcl[:, None]).astype(jnp.int32), axis=1)
    oh = (be[:, None] == experts[None, :]).astype(jnp.int32)
    pick = lambda v: jnp.sum(oh * v[None, :], axis=1)
    off = (bcl - pick(bend - nblk)) * ROW_BLOCK
    valid = (blk >= 0) & (blk < n_used)
    rs = jnp.where(valid, pick(jnp.cumsum(cnt) - cnt) + off, 0)
    nv = jnp.where(valid, jnp.clip(pick(cnt) - off, 0, ROW_BLOCK), 0)
    pad = jnp.zeros((ROW_BLOCK,), jnp.int32)
    be_step = be[1:n_blocks + 3]
    change = jnp.concatenate([jnp.zeros((1,), jnp.int32), (be_step[1:] != be_step[:-1]).astype(jnp.int32)])
    later = (experts[None, :] > be_step[:, None]) & (nblk[None, :] > 0)
    nxt = jnp.min(jnp.where(later, experts[None, :], N_EXPERTS), axis=1)
    nxt = jnp.where(nxt == N_EXPERTS, -1, nxt)
    return (be_step, rs, nv, n_used.reshape(1), jnp.cumsum(change), nxt, jnp.concatenate([tok, pad]),
            jnp.concatenate([gate, pad.astype(F32)]))


def _moe_call(l, h2, idx, gates, wts):
    t, d = h2.shape
    tc = min(4096, t)
    n_blocks = tc * TOP_K // ROW_BLOCK + N_EXPERTS
    y = None
    for c in range(t // tc):
        route = _route(idx[c * tc:(c + 1) * tc], gates[c * tc:(c + 1) * tc], n_blocks)
        wspec = lambda r, w: pl.BlockSpec((None, None, r, w), lambda s, be, *_: (l, be[s], 0, 0))
        hbm = pl.BlockSpec(memory_space=pl.ANY)
        in_specs = [pl.BlockSpec((tc, d), lambda i, *_: (c, 0), pipeline_mode=pl.Buffered(1)),
                    hbm, wspec(1, 2 * D_FF), hbm, wspec(1, d)]
        args = [h2, wts["w_moe_in"], wts["b_moe_in"], wts["w_moe_out"], wts["b_moe_out"]]
        aliases = {}
        if y is not None:
            in_specs.append(pl.BlockSpec(memory_space=pl.ANY))
            args.append(y)
            aliases = {len(route) + len(args) - 1: 0}
        grid_spec = pltpu.PrefetchScalarGridSpec(
            num_scalar_prefetch=len(route),
            grid=(n_blocks + 2,),
            in_specs=in_specs,
            out_specs=pl.BlockSpec((None, tc + SUB, d), lambda i, *_: (c, 0, 0), pipeline_mode=pl.Buffered(1)),
            scratch_shapes=[pltpu.VMEM((2, NGRP, SUB, d), F32), pltpu.VMEM((2, NGRP, SUB, d), F32),
                            pltpu.VMEM((ROW_BLOCK, d), BF16),
                            pltpu.VMEM((2, d, 2 * D_FF), BF16), pltpu.VMEM((2, D_FF, d), BF16),
                            pltpu.SemaphoreType.DMA((2, 2))],
        )
        y = pl.pallas_call(
            functools.partial(_moe_kernel_entry, y is not None, l),
            out_shape=jax.ShapeDtypeStruct((t // tc, tc + SUB, d), F32),
            grid_spec=grid_spec,
            input_output_aliases=aliases,
            compiler_params=_cp(("arbitrary",), VMEM_LIMIT),
            name="moe_experts",
        )(*route, *args)
    return y


def _final_kernel(x_ref, y_ref, mod_ref, o_ref):
    o_ref[...] = x_ref[...] + mod_ref[5:6, :] * y_ref[...]


def _final_call(l, x, y, mod, seq):
    t, d = x.shape
    tm = TOKEN_TILE
    ncond = mod.shape[1]
    row = pl.BlockSpec((tm, d), lambda i: (i, 0))
    return pl.pallas_call(
        _final_kernel,
        out_shape=jax.ShapeDtypeStruct((t, d), F32),
        grid=(t // tm,),
        in_specs=[row, _chunk_rows_spec(y, tm),
                  pl.BlockSpec((None, None, 6, d), lambda i: (l, (i * tm) // seq if ncond > 1 else 0, 0, 0))],
        out_specs=row,
        compiler_params=_cp(("parallel",), VMEM_LIMIT),
        name="final_residual",
    )(x, y, mod)


def _rope_tables(seq, rot_dim, lane_map):
    shift = rot_dim // 2
    n_rows = seq // GRID_W
    rows = jnp.repeat(jnp.arange(n_rows, dtype=F32), GRID_W)
    cols = jnp.tile(jnp.arange(GRID_W, dtype=F32), n_rows)
    axis_dim = rot_dim // 2
    inv_freq = ROPE_THETA ** (-jnp.arange(0, axis_dim, 2, dtype=F32) / axis_dim)
    ang = jnp.concatenate([rows[:, None] * inv_freq, cols[:, None] * inv_freq], axis=-1)
    cos, sin = jnp.cos(ang), jnp.sin(ang)
    pair = np.zeros((LANE,), np.int32)
    in_rot = np.zeros((LANE,), np.float32)
    first = np.zeros((LANE,), np.float32)
    for ln in range(LANE):
        m = lane_map(ln)
        if m is not None:
            pair[ln], in_rot[ln], first[ln] = m[0], 1.0, 1.0 if m[1] == 0 else 0.0
    c = jnp.where(in_rot[None, :] > 0, cos[:, pair], 1.0)
    s = sin[:, pair] * in_rot[None, :] * (1.0 - 2.0 * first[None, :])
    perm = np.zeros((2 * LANE, 2 * LANE), np.float32)
    for ln in range(LANE):
        if in_rot[ln] > 0:
            src = ln + shift if first[ln] > 0 else ln - shift
            perm[src, ln] = perm[LANE + src, LANE + ln] = 1.0
    return jnp.stack([c, s]).astype(F32), jnp.asarray(perm, BF16)


def _lane_map_a(ln):
    o = ln - NOPE_A
    if 0 <= o < ROPE_A:
        return (o % (ROPE_A // 2), o // (ROPE_A // 2))
    return None


def _lane_map_c(ln):
    o = ln % HD_C
    return (o % (HD_C // 2), o // (HD_C // 2))


def _phase(num, den):
    ang = (2.0 * math.pi / den) * (num % den).astype(F32)
    return jnp.cos(ang), jnp.sin(ang)


def _ctx_dft(n):
    f = jnp.arange(n, dtype=jnp.int32)[:, None]
    t = jnp.arange(n, dtype=jnp.int32)[None, :]
    c, s = _phase((2 * f + 1) * t, 4 * n)
    fwd = jnp.concatenate([c, -s], axis=0).astype(BF16)
    inv = jnp.concatenate([c.T, -s.T], axis=1).astype(BF16)
    return c, s, fwd, inv


def _lat_dft(seq):
    n = 2 * seq
    n1, n2 = FFT_N1, n // FFT_N1
    k1 = jnp.arange(n1, dtype=jnp.int32)
    c1, s1 = _phase(k1[:, None] * k1[None, :], n1)
    f1f = jnp.concatenate([c1, -s1], axis=0)
    f1d = f1f[:, :n1 // 2].astype(BF16)
    fc = jnp.concatenate([c1[:, :n1 // 2].T, -s1[:, :n1 // 2].T], axis=1).astype(BF16)
    k2 = jnp.arange(n2, dtype=jnp.int32)
    num = (k2[None, :, None] * k2[None, None, :]) * n1 + k2[None, None, :] * k1[:, None, None]
    cm, sm = _phase(num, n)
    mr, mi = cm, -sm
    f2 = jnp.concatenate([jnp.concatenate([mr, -mi], axis=2), jnp.concatenate([mi, mr], axis=2)], axis=1)
    mrt, mit = jnp.swapaxes(mr, 1, 2), jnp.swapaxes(mi, 1, 2)
    g2 = jnp.concatenate([jnp.concatenate([mrt, mit], axis=2), jnp.concatenate([-mit, mrt], axis=2)], axis=1)
    f2h, f2l = _split(f2)
    return dict(f1f=f1f, f1d=f1d, fc=fc, f2=f2h, f2l=f2l, g2=g2.astype(BF16), n1=n1, n2=n2, n=n)


def _hy_features(seq):
    t = jnp.linspace(0.0, 1.0, seq, dtype=F32)[:, None]
    bands = (HY_EMB - 1) // 2
    f = jnp.linspace(1e-4, bands - 1, bands, dtype=F32)[None, :]
    w = 2.0 * math.pi * jnp.arange(seq, dtype=F32)[:, None] / seq
    z = jnp.concatenate([t, jnp.cos(f * w), jnp.sin(f * w)], axis=-1)
    z = jnp.pad(z, ((0, 0), (0, HY_FO - HY_EMB)))
    deltas = jnp.abs(jnp.linspace(HY_MIN_DECAY, HY_MAX_DECAY, HY_CH, dtype=F32))
    return z, jnp.exp(-t * deltas)


def _prep_weights(w_in, mla_g_qa, mla_w_uq, mla_g_kva, mla_w_ukv, mla_g_q, mla_g_k, gqa_g_q, gqa_g_k,
                  g_out, w_out, w_router, b_router, hy_w_in):
    depth = w_in.shape[0]
    cuts = np.cumsum([0, Q_RANK, KV_RANK, ROPE_A, 3 * HY_CH, H_C * HD_C, KV_C * HD_C, KV_C * HD_C])
    wb = w_in.astype(BF16)
    zeros = lambda w: jnp.zeros(wb.shape[:2] + (w,), BF16)
    parts = [wb[:, :, cuts[0]:cuts[2]], zeros(NOPE_A), wb[:, :, cuts[2]:cuts[3]], zeros(LANE - QK_A),
             wb[:, :, cuts[3]:cuts[4]]]
    for h in range(H_C):
        wh = wb[:, :, cuts[4] + h * HD_C:cuts[4] + (h + 1) * HD_C]
        parts += [wh, zeros(HD_C)] if h < H_C // KV_C else [zeros(HD_C), wh]
    w_in_p = jnp.concatenate(parts + [wb[:, :, cuts[5]:cuts[7]]], axis=-1)

    w_uq = jnp.pad(mla_w_uq.reshape(depth, Q_RANK, H_A, QK_A), ((0, 0), (0, 0), (0, 0), (0, LANE - QK_A)))
    w_uq = w_uq.reshape(depth, Q_RANK, HQ).astype(BF16)
    ukv = mla_w_ukv.reshape(depth, KV_RANK, H_A, NOPE_A + V_A)
    wk = jnp.pad(ukv[..., :NOPE_A], ((0, 0), (0, 0), (0, 0), (0, LANE - NOPE_A))).reshape(depth, KV_RANK, HQ)
    wv = ukv[..., NOPE_A:].reshape(depth, KV_RANK, W_A)
    w_kv = jnp.concatenate([wk, wv], axis=-1).astype(BF16)

    def pad_row(v):
        return jnp.pad(v, ((0, 0), (0, HQ - v.shape[1])))

    head_a = lambda g: jnp.tile(jnp.pad(g, ((0, 0), (0, LANE - QK_A))), (1, H_A))
    head_c = lambda g, reps: jnp.tile(g, (1, reps))
    gains = jnp.stack([
        pad_row(mla_g_qa), pad_row(mla_g_kva),
        head_a(mla_g_q) * (QK_A ** -0.5 * LOG2E), head_a(mla_g_k),
        head_c(gqa_g_q, 2 * H_C) * (HD_C ** -0.5 * LOG2E), pad_row(head_c(gqa_g_k, 2)),
        jnp.zeros((depth, HQ), F32), jnp.zeros((depth, HQ), F32)], axis=1).astype(F32)

    g_per = H_C // KV_C
    perm_c = np.concatenate([np.arange(h * HD_C, (h + 1) * HD_C) for g in range(g_per) for h in (g, g + g_per)])
    rows_c = W_A + W_B + perm_c
    g_o = jnp.concatenate([g_out[:, :W_A + W_B], g_out[:, rows_c]], axis=1).reshape(depth, 1, -1)
    wr = jnp.pad(w_router, ((0, 0), (0, 0), (0, LANE - N_EXPERTS)))
    wr_hi, wr_lo = _split(wr)
    br = jnp.pad(b_router, ((0, 0), (0, LANE - N_EXPERTS)), constant_values=-1e30).reshape(depth, 1, LANE)
    return dict(w_in=w_in_p, w_uq=w_uq, w_kv=w_kv, gains=gains, g_out=g_o,
                wo_a=w_out[:, :W_A].astype(BF16), wo_b=w_out[:, W_A:W_A + W_B].astype(BF16),
                wo_c=w_out[:, rows_c].astype(BF16), wr_hi=wr_hi, wr_lo=wr_lo, b_router=br,
                hy_w_in=jnp.pad(hy_w_in, ((0, 0), (0, HY_FO - HY_EMB), (0, 0))))


def _hyena_lat(l, u, hw, kf, dft, seq):
    t = u.shape[0]
    nb = t // seq
    n1, n2 = dft["n1"], dft["n2"]
    w2 = n2 * HY_CH
    v, x1, x2 = _hconv_call(l, u, hw, seq)
    to2d = lambda a: a.reshape(t // n2, w2)
    s = to2d(v)
    for o, gate in enumerate((x1, x2)):
        a = _ha_call(dft["f1d"], s, nb)
        b = _hb_call(o, a.reshape(nb, 2, n1, n2, HY_CH), dft["f2"], dft["g2"], kf)
        s = _hc_call(l, o, dft["fc"], b.reshape(nb, 2, n1, w2), s, to2d(gate), hw["skip_t"])
    return s.reshape(t, HY_CH)


def _filter_lat(l, z, dec, hw, dft, seq):
    kern, nrm = _filt_lat_call(l, z, dec, hw, seq)
    n1, n2, n = dft["n1"], dft["n2"], dft["n"]
    c = kern.shape[1]
    af = _fa_call(dft["f1f"], kern.reshape(n1, n2 * c))
    return _fb_call(af.reshape(2, n1, n2, c), dft["f2"], dft["f2l"], nrm, n)


def kernel(x_prompt, x_sample, c, c_ctx, cache_mla_ckv, cache_mla_kpe, cache_gqa_k, cache_gqa_v, w_mod, b_mod, w_in, mla_g_qa, mla_w_uq, mla_g_kva, mla_w_ukv, mla_g_q, mla_g_k, hy_conv_w, hy_conv_b, hy_w_in, hy_b_in, hy_w_mid, hy_b_mid, hy_w_out, hy_b_out, hy_freq, hy_skip, gqa_g_q, gqa_g_k, g_out, w_out, w_router, b_router, w_moe_in, b_moe_in, w_moe_out, b_moe_out):
    batch, seq_c, d = x_prompt.shape
    nb_l, seq_l, _ = x_sample.shape
    depth = w_in.shape[0]
    past = cache_mla_ckv.shape[2]
    assert d == D_MODEL and seq_l % GRID_W == 0 and (2 * seq_l) % FFT_N1 == 0

    wts = _prep_weights(w_in, mla_g_qa, mla_w_uq, mla_g_kva, mla_w_ukv, mla_g_q, mla_g_k, gqa_g_q, gqa_g_k,
                        g_out, w_out, w_router, b_router, hy_w_in)
    wts["w_moe_in"] = _cast_call(w_moe_in.reshape(depth * N_EXPERTS, d, 2 * D_FF), 512).reshape(w_moe_in.shape)
    wts["w_moe_out"] = _cast_call(w_moe_out.reshape(depth * N_EXPERTS, D_FF, d), 512).reshape(w_moe_out.shape)
    wts["b_moe_in"] = b_moe_in.reshape(depth, N_EXPERTS, 1, 2 * D_FF)
    wts["b_moe_out"] = b_moe_out.reshape(depth, N_EXPERTS, 1, d)
    hw = dict(hy_w_in=wts["hy_w_in"], hy_b_in=hy_b_in.reshape(depth, 1, HY_FO), hy_w_mid=hy_w_mid,
              hy_b_mid=hy_b_mid.reshape(depth, HY_INNER, 1, HY_FO), hy_w_out=hy_w_out,
              hy_b_out=hy_b_out.reshape(depth, 1, -1), hy_freq=hy_freq.reshape(depth, 1, HY_FO),
              hy_conv_w=hy_conv_w, hy_conv_b=hy_conv_b.reshape(depth, 1, -1), hy_skip=hy_skip)
    dft_l = _lat_dft(seq_l)
    hw["skip_t"] = jnp.tile(hy_skip, (1, 1, dft_l["n2"]))

    conds = jnp.zeros((8, d), F32).at[0].set(c_ctx).at[1:1 + nb_l].set(c)
    mod = _mod_call(conds, w_mod, b_mod).reshape(depth, 8, 6, d)
    mod_c, mod_l = mod[:, 0:1], mod[:, 1:1 + nb_l]

    rope_tabs = (_rope_tables(seq_l, ROPE_A, _lane_map_a), _rope_tables(seq_l, HD_C, _lane_map_c))
    kax, vax = _cachekv_call(cache_mla_ckv, jnp.pad(cache_mla_kpe, ((0, 0), (0, 0), (0, 0), (NOPE_A, LANE - QK_A))), wts)
    kcx = cache_gqa_k.reshape(nb_l, depth, past, KV_C * HD_C).astype(BF16)
    vcx = cache_gqa_v.reshape(nb_l, depth, past, KV_C * HD_C).astype(BF16)
    vcx = jnp.concatenate([vcx, jnp.ones_like(vcx)], axis=-1)

    z_c, dec_c = _hy_features(seq_c)
    cmat, smat, fwd_c, inv_c = _ctx_dft(seq_c)
    z_l, dec_l = _hy_features(seq_l)
    z_full = jnp.concatenate([z_l, jnp.zeros((1, HY_FO), F32), z_l[:0:-1]], axis=0)
    dec_full = jnp.concatenate([dec_l, jnp.zeros((1, HY_CH), F32), dec_l[:0:-1]], axis=0)

    xc = x_prompt.reshape(batch * seq_c, d)
    xl = x_sample.reshape(nb_l * seq_l, d)
    yc = yl = None
    new_ckv, new_kpe, new_k, new_v = [], [], [], []
    for l in range(depth):
        (xc, qa, ka, va, ckv, kpe, qc, kc, vc, kcf, vcf, u) = _premix_call(l, xc, yc, mod_c, seq_c, False, wts, None)
        new_ckv.append(ckv)
        new_kpe.append(kpe[:, NOPE_A:QK_A])
        new_k.append(kcf)
        new_v.append(vcf)
        oa, oc = _attn_ctx_call(qa, ka, va, qc, kc, vc, seq_c)
        kf_c = _filt_ctx_call(l, z_c, dec_c, hw, cmat, smat)
        ob = _hy_ctx_call(l, u, hw, kf_c, fwd_c, inv_c, seq_c)
        xc, h2, idx, gates = _postmix_call(l, xc, oa, ob, oc, mod_c, seq_c, wts)
        yc = _moe_call(l, h2, idx[:, :TOP_K], gates[:, :TOP_K], wts)
        (xl, qa, ka, va, _, _, qc, kc, vc, _, _, u) = _premix_call(l, xl, yl, mod_l, seq_l, True, wts, rope_tabs)
        oa, oc = _attn_lat_call(l, qa, ka, va, qc, kc, vc, kax, vax, kcx, vcx, seq_l)
        kf_l = _filter_lat(l, z_full, dec_full, hw, dft_l, seq_l)
        ob = _hyena_lat(l, u, hw, kf_l, dft_l, seq_l)
        xl, h2, idx, gates = _postmix_call(l, xl, oa, ob, oc, mod_l, seq_l, wts)
        yl = _moe_call(l, h2, idx[:, :TOP_K], gates[:, :TOP_K], wts)
    y_prompt = _final_call(depth - 1, xc, yc, mod_c, seq_c).reshape(batch, seq_c, d)
    y_sample = _final_call(depth - 1, xl, yl, mod_l, seq_l).reshape(nb_l, seq_l, d)
    stack = lambda xs, tail: jnp.stack([a.reshape((batch, seq_c) + tail) for a in xs], axis=1)
    return (y_prompt, y_sample, stack(new_ckv, (KV_RANK,)), stack(new_kpe, (ROPE_A,)),
            stack(new_k, (KV_C, HD_C)), stack(new_v, (KV_C, HD_C)))
```

```python
import functools
import math

import numpy as np
import jax
import jax.numpy as jnp
from jax import lax
from jax.experimental import pallas as pl
from jax.experimental.pallas import tpu as pltpu

F32 = jnp.float32
BF16 = jnp.bfloat16

D_MODEL = 1024
GRID_W = 64
EPS = 1e-6
ROPE_THETA = 10000.0
H_A = 6
Q_RANK = 256
KV_RANK = 128
NOPE_A = 64
ROPE_A = 32
V_A = 64
QK_A = NOPE_A + ROPE_A
HY_CH = 256
HY_ORDER = 2
HY_EMB = 33
HY_FO = 64
HY_INNER = 2
HY_MIN_DECAY = math.log(1e-2) / 1.5
HY_MAX_DECAY = math.log(1e-2) / 0.3
H_C = 6
KV_C = 2
HD_C = 64
N_EXPERTS = 32
TOP_K = 4
D_FF = 1024
SWIGLU_LIMIT = 7.0
SWIGLU_ALPHA = 1.702
LOG2E = 1.4426950408889634

LANE = 128
HQ = H_A * LANE
W_A = H_A * V_A
W_B = HY_CH
W_C = H_C * HD_C
IN_P = Q_RANK + KV_RANK + LANE + 3 * HY_CH + HQ + 2 * LANE
ROW_BLOCK = 256
TOKEN_TILE = 512
ATTN_Q_TILE = 512
VMEM_LIMIT = 60 * 1024 * 1024
FFT_N1 = 128
FFT_N2 = 64


def _cp(sem, vmem=None):
    return pltpu.CompilerParams(dimension_semantics=sem, vmem_limit_bytes=vmem)


def _dot(a, b):
    return jnp.dot(a, b, preferred_element_type=F32)


def _split(a):
    hi = a.astype(BF16)
    return hi, (a - hi.astype(F32)).astype(BF16)


def _dot3(a, b):
    ah, al = _split(a)
    bh, bl = _split(b)
    return _dot(ah, bh) + _dot(ah, bl) + _dot(al, bh)


def _bdot(a, b):
    return lax.dot_general(a, b, (((2,), (1,)), ((0,), (0,))), preferred_element_type=F32)


def _rms(x, n=None):
    ss = jnp.sum(x * x, axis=-1, keepdims=True) * (1.0 / (n or x.shape[-1]))
    return x * lax.rsqrt(ss + EPS)


def _head_norm(x, nvalid):
    outs = []
    for h in range(x.shape[1] // LANE):
        blk = x[:, h * LANE:(h + 1) * LANE]
        ss = jnp.sum(blk * blk, axis=-1, keepdims=True) * (1.0 / nvalid)
        outs.append(blk * lax.rsqrt(ss + EPS))
    return outs[0] if len(outs) == 1 else jnp.concatenate(outs, axis=1)


def _rope(x, rope):
    tab_ref, perm_ref = rope
    c, s = tab_ref[0], tab_ref[1]
    xb = x.astype(BF16)
    nblk = x.shape[1] // LANE
    outs = []
    b = 0
    while b < nblk:
        w = 2 if nblk - b >= 2 else 1
        perm = perm_ref[...] if w == 2 else perm_ref[:LANE, :LANE]
        sw = _dot(xb[:, b * LANE:(b + w) * LANE], perm)
        for k in range(w):
            outs.append(x[:, (b + k) * LANE:(b + k + 1) * LANE] * c + sw[:, k * LANE:(k + 1) * LANE] * s)
        b += w
    return outs[0] if len(outs) == 1 else jnp.concatenate(outs, axis=1)


def _mod_kernel(c_ref, w_ref, b_ref, o_ref):
    c = c_ref[...]
    s = c * (1.0 / (1.0 + jnp.exp(-c)))
    o_ref[...] = _dot3(s, w_ref[...]) + b_ref[...]


def _mod_call(conds, w_mod, b_mod):
    depth, d, n6 = w_mod.shape
    tn = 1536
    return pl.pallas_call(
        _mod_kernel,
        out_shape=jax.ShapeDtypeStruct((depth, 8, n6), F32),
        grid=(depth, n6 // tn),
        in_specs=[pl.BlockSpec((8, d), lambda l, j: (0, 0)),
                  pl.BlockSpec((None, d, tn), lambda l, j: (l, 0, j)),
                  pl.BlockSpec((None, 1, tn), lambda l, j: (l, 0, j))],
        out_specs=pl.BlockSpec((None, 8, tn), lambda l, j: (l, 0, j)),
        compiler_params=_cp(("parallel", "parallel"), VMEM_LIMIT),
        name="mod",
    )(conds, w_mod, b_mod.reshape(depth, 1, n6))


def _cast_kernel(x_ref, o_ref):
    o_ref[...] = x_ref[...].astype(BF16)


def _cast_call(w, tr):
    n, r, c = w.shape
    return pl.pallas_call(
        _cast_kernel,
        out_shape=jax.ShapeDtypeStruct(w.shape, BF16),
        grid=(n, r // tr),
        in_specs=[pl.BlockSpec((None, tr, c), lambda i, j: (i, j, 0))],
        out_specs=pl.BlockSpec((None, tr, c), lambda i, j: (i, j, 0)),
        compiler_params=_cp(("parallel", "parallel"), VMEM_LIMIT),
        name="cast_bf16",
    )(w)


def _kv_heads(ckv_bf, kpe, w_kv_ref, gk, rope_ref, use_rope):
    kvp = _dot(ckv_bf, w_kv_ref[...])
    kn = kvp[:, :HQ]
    ka = jnp.concatenate([kn[:, h * LANE:(h + 1) * LANE] + kpe for h in range(H_A)], axis=1)
    ka = _head_norm(ka, QK_A) * gk
    if use_rope:
        ka = _rope(ka, rope_ref)
    return ka, kvp[:, HQ:]


def _premix_kernel(has_prev, use_rope, *refs):
    it = iter(refs)
    x_ref = next(it)
    if has_prev:
        y_ref, modp_ref = next(it), next(it)
    mod_ref, w_in_ref, w_uq_ref, w_kv_ref, g_ref = next(it), next(it), next(it), next(it), next(it)
    if use_rope:
        ra_ref, rc_ref = (next(it), next(it)), (next(it), next(it))
    else:
        ra_ref = rc_ref = None
    (xo_ref, qa_ref, ka_ref, va_ref, ckv_ref, kpe_ref, qc_ref, kc_ref, vc_ref, kcf_ref, vcf_ref, u_ref) = it

    x = x_ref[...]
    if has_prev:
        x = x + modp_ref[5:6, :] * y_ref[...]
    xo_ref[...] = x
    h = _rms(x) * (1.0 + mod_ref[1:2, :]) + mod_ref[0:1, :]
    proj = _dot(h.astype(BF16), w_in_ref[...])
    o = 0
    c_q = proj[:, o:o + Q_RANK]; o += Q_RANK
    c_kv = proj[:, o:o + KV_RANK]; o += KV_RANK
    kpe = proj[:, o:o + LANE]; o += LANE
    u_ref[...] = proj[:, o:o + 3 * HY_CH]; o += 3 * HY_CH
    q_c = proj[:, o:o + HQ]; o += HQ
    k_c = proj[:, o:o + LANE]; o += LANE
    v_c = proj[:, o:o + LANE]

    cqn = _rms(c_q) * g_ref[0:1, :Q_RANK]
    qa = _head_norm(_dot(cqn.astype(BF16), w_uq_ref[...]), QK_A) * g_ref[2:3, :]
    if use_rope:
        qa = _rope(qa, ra_ref)
    qa_ref[...] = qa.astype(BF16)
    ckv = _rms(c_kv) * g_ref[1:2, :KV_RANK]
    ckv_ref[...] = ckv
    kpe_ref[...] = kpe
    ka, va = _kv_heads(ckv.astype(BF16), kpe, w_kv_ref, g_ref[3:4, :], ra_ref, use_rope)
    ka_ref[...] = ka.astype(BF16)
    va_ref[...] = _with_ones(va.astype(BF16))

    qc = _head_norm(q_c, HD_C) * g_ref[4:5, :]
    if use_rope:
        qc = _rope(qc, rc_ref)
    qc_ref[...] = qc.astype(BF16)
    lane = lax.broadcasted_iota(jnp.int32, k_c.shape, 1)
    k2 = k_c * k_c
    s0 = jnp.sum(jnp.where(lane < HD_C, k2, 0.0), axis=-1, keepdims=True) * (1.0 / HD_C)
    s1 = jnp.sum(jnp.where(lane >= HD_C, k2, 0.0), axis=-1, keepdims=True) * (1.0 / HD_C)
    kcn = k_c * jnp.where(lane < HD_C, lax.rsqrt(s0 + EPS), lax.rsqrt(s1 + EPS)) * g_ref[5:6, :LANE]
    kcf_ref[...] = kcn
    vcf_ref[...] = v_c
    kc_ref[...] = (_rope(kcn, rc_ref) if use_rope else kcn).astype(BF16)
    vc_ref[...] = _with_ones(v_c.astype(BF16))


def _chunk_rows_spec(y, tm):
    per = (y.shape[1] - SUB) // tm
    return pl.BlockSpec((None, tm, y.shape[2]), lambda i: (i // per, i % per, 0))


def _premix_call(l, x, yprev, mod, seq, use_rope, wts, rope_tabs):
    t, d = x.shape
    tm = TOKEN_TILE
    ncond = mod.shape[1]
    has_prev = yprev is not None

    def cond(i):
        return (i * tm) // seq if ncond > 1 else 0

    row = lambda w: pl.BlockSpec((tm, w), lambda i: (i, 0))
    ins, specs = [x], [row(d)]
    if has_prev:
        ins += [yprev, mod]
        specs += [_chunk_rows_spec(yprev, tm), pl.BlockSpec((None, None, 6, d), lambda i: (l - 1, cond(i), 0, 0))]
    ins += [mod, wts["w_in"], wts["w_uq"], wts["w_kv"], wts["gains"]]
    specs += [pl.BlockSpec((None, None, 6, d), lambda i: (l, cond(i), 0, 0)),
              pl.BlockSpec((None, d, IN_P), lambda i: (l, 0, 0)),
              pl.BlockSpec((None, Q_RANK, HQ), lambda i: (l, 0, 0)),
              pl.BlockSpec((None, KV_RANK, HQ + W_A), lambda i: (l, 0, 0)),
              pl.BlockSpec((None, 8, HQ), lambda i: (l, 0, 0))]
    if use_rope:
        nt = seq // tm
        tab = pl.BlockSpec((2, tm, LANE), lambda i: (0, i % nt, 0))
        perm = pl.BlockSpec((2 * LANE, 2 * LANE), lambda i: (0, 0))
        ins += [rope_tabs[0][0], rope_tabs[0][1], rope_tabs[1][0], rope_tabs[1][1]]
        specs += [tab, perm, tab, perm]
    outs = [(d, F32), (HQ, BF16), (HQ, BF16), (2 * W_A, BF16), (KV_RANK, F32), (LANE, F32), (HQ, BF16),
            (LANE, BF16), (2 * LANE, BF16), (LANE, F32), (LANE, F32), (3 * HY_CH, F32)]
    return pl.pallas_call(
        functools.partial(_premix_kernel, has_prev, use_rope),
        out_shape=[jax.ShapeDtypeStruct((t, w), dt) for w, dt in outs],
        grid=(t // tm,),
        in_specs=specs,
        out_specs=[row(w) for w, _ in outs],
        compiler_params=_cp(("parallel",), VMEM_LIMIT),
        name="premix",
    )(*ins)


def _cachekv_kernel(ckv_ref, kpe_ref, w_kv_ref, g_ref, ka_ref, va_ref):
    ka, va = _kv_heads(ckv_ref[...].astype(BF16), kpe_ref[...], w_kv_ref, g_ref[3:4, :], None, False)
    ka_ref[...] = ka.astype(BF16)
    va_ref[...] = _with_ones(va.astype(BF16))


def _cachekv_call(cache_ckv, cache_kpe_p, wts):
    nb, depth, past, _ = cache_ckv.shape
    return pl.pallas_call(
        _cachekv_kernel,
        out_shape=[jax.ShapeDtypeStruct((depth, nb, past, HQ), BF16),
                   jax.ShapeDtypeStruct((depth, nb, past, 2 * W_A), BF16)],
        grid=(depth, nb),
        in_specs=[pl.BlockSpec((None, None, past, KV_RANK), lambda l, b: (b, l, 0, 0)),
                  pl.BlockSpec((None, None, past, LANE), lambda l, b: (b, l, 0, 0)),
                  pl.BlockSpec((None, KV_RANK, HQ + W_A), lambda l, b: (l, 0, 0)),
                  pl.BlockSpec((None, 8, HQ), lambda l, b: (l, 0, 0))],
        out_specs=[pl.BlockSpec((None, None, past, HQ), lambda l, b: (l, b, 0, 0)),
                   pl.BlockSpec((None, None, past, 2 * W_A), lambda l, b: (l, b, 0, 0))],
        compiler_params=_cp(("parallel", "parallel"), VMEM_LIMIT),
        name="cache_kv",
    )(cache_ckv, cache_kpe_p, wts["w_kv"], wts["gains"])


def _nt(q, k):
    return lax.dot_general(q, k, (((1,), (1,)), ((), ())), preferred_element_type=F32)


def _with_ones(v):
    ones = jnp.ones((v.shape[0], LANE), v.dtype)
    parts = []
    for j in range(v.shape[1] // LANE):
        parts += [v[:, j * LANE:(j + 1) * LANE], ones]
    return jnp.concatenate(parts, axis=1)


def _attend(q, ks, vs):
    return _softmax_pv([_nt(q, k) for k in ks], vs)


def _softmax_pv(ss, vs, m=None):
    if m is None:
        m = ss[0].max(axis=-1, keepdims=True)
        for s in ss[1:]:
            m = jnp.maximum(m, s.max(axis=-1, keepdims=True))
    acc = None
    for s, v in zip(ss, vs):
        pv = _dot(jnp.exp2(s - m).astype(BF16), v)
        acc = pv if acc is None else acc + pv
    return acc[:, :LANE] / acc[:, LANE:]


BOUND_SLACK = 1.01
BOUND_LIMIT = 60.0


def _attn_kernel(nseg, bound, *refs):
    qa_ref, qc_ref = refs[0], refs[1]
    if bound:
        kq_ref, refs = refs[2], refs[:2] + refs[3:]
    segs = refs[2:2 + 4 * nseg]
    oa_ref, oc_ref = refs[2 + 4 * nseg:]
    ka_refs, va_refs, kc_refs, vc_refs = (segs[i::4] for i in range(4))
    lane = lax.broadcasted_iota(jnp.int32, (qa_ref.shape[0], LANE), 1)
    low = lane < V_A
    g_per = H_C // KV_C
    jobs = [("a", j, h) for j in range(H_A // 2) for h in (2 * j, 2 * j + 1)]
    jobs += [("c", g, h) for g in range(g_per) for h in (g, g + g_per)]

    def scores(job):
        kind, _, h = job
        hs = slice(h * LANE, (h + 1) * LANE)
        if kind == "a":
            return [_nt(qa_ref[:, hs], r[:, hs]) for r in ka_refs]
        return [_nt(qc_ref[:, hs], r[...]) for r in kc_refs]

    def values(job):
        kind, j, _ = job
        if kind == "a":
            return [r[:, 2 * j * LANE:2 * (j + 1) * LANE] for r in va_refs]
        return [r[...] for r in vc_refs]

    def row_bound(job):
        kind, _, h = job
        q = (qa_ref if kind == "a" else qc_ref)[:, h * LANE:(h + 1) * LANE].astype(F32)
        r = h if kind == "a" else H_A + h // g_per
        q2 = jnp.sum(q * q, axis=-1, keepdims=True)
        return jnp.sqrt(q2 * kq_ref[r:r + 1, 0:1]) * BOUND_SLACK

    ss_next = None if bound else scores(jobs[0])
    pv = []
    for n, job in enumerate(jobs):
        if bound:
            pv.append(_softmax_pv(scores(job), values(job), row_bound(job)))
        else:
            ss = ss_next
            if n + 1 < len(jobs):
                ss_next = scores(jobs[n + 1])
            pv.append(_softmax_pv(ss, values(job)))
        if len(pv) == 2:
            o_ref = oa_ref if job[0] == "a" else oc_ref
            o_ref[:, job[1] * LANE:(job[1] + 1) * LANE] = jnp.where(low, pv[0], pv[1])
            pv = []


def _attn_ctx_call(qa, ka, va, qc, kc, vc, seq):
    t = qa.shape[0]
    blk = lambda w: pl.BlockSpec((seq, w), lambda b: (b, 0))
    return pl.pallas_call(
        functools.partial(_attn_kernel, 1, False),
        out_shape=[jax.ShapeDtypeStruct((t, W_A), F32), jax.ShapeDtypeStruct((t, W_C), F32)],
        grid=(t // seq,),
        in_specs=[blk(HQ), blk(HQ), blk(HQ), blk(2 * W_A), blk(LANE), blk(2 * LANE)],
        out_specs=[blk(W_A), blk(W_C)],
        compiler_params=_cp(("parallel",), VMEM_LIMIT),
        name="attn_ctx",
    )(qa, qc, ka, va, kc, vc)


def _attn_lat_call(l, qa, ka, va, qc, kc, vc, kax, vax, kcx, vcx, seq):
    t = qa.shape[0]
    nb = t // seq
    tq = min(ATTN_Q_TILE, seq)
    nq = seq // tq
    past = kax.shape[2]
    qblk = lambda w: pl.BlockSpec((tq, w), lambda b, i: (b * nq + i, 0))
    sblk = lambda w: pl.BlockSpec((seq, w), lambda b, i: (b, 0))
    xblk = lambda w: pl.BlockSpec((None, None, past, w), lambda b, i: (l, b, 0, 0))
    cblk = lambda w: pl.BlockSpec((None, None, past, w), lambda b, i: (b, l, 0, 0))
    full = lambda w: pl.BlockSpec((seq, w), lambda b: (b, 0))

    kq = pl.pallas_call(
        _knorm_kernel,
        out_shape=jax.ShapeDtypeStruct((nb, NORM_ROWS, LANE), F32),
        grid=(nb,),
        in_specs=[full(HQ), pl.BlockSpec((None, None, past, HQ), lambda b: (l, b, 0, 0)),
                  full(LANE), pl.BlockSpec((None, None, past, LANE), lambda b: (b, l, 0, 0)), full(HQ), full(HQ)],
        out_specs=pl.BlockSpec((None, NORM_ROWS, LANE), lambda b: (b, 0, 0)),
        compiler_params=_cp(("parallel",), VMEM_LIMIT),
        name="attn_norms",
    )(ka, kax, kc, kcx, qa, qc)
    k2, q2a, q2c = kq[:, :H_A + KV_C, 0], kq[:, 8:8 + H_A, 0], kq[:, 8 + H_A:8 + H_A + H_C, 0]
    k2c = jnp.repeat(k2[:, H_A:], H_C // KV_C, axis=1)
    worst = jnp.sqrt(jnp.maximum(jnp.max(q2a * k2[:, :H_A]), jnp.max(q2c * k2c))) * BOUND_SLACK

    def call(bound):
        extra_specs = [pl.BlockSpec((None, NORM_ROWS, LANE), lambda b, i: (b, 0, 0))] if bound else []
        return pl.pallas_call(
            functools.partial(_attn_kernel, 2, bound),
            out_shape=[jax.ShapeDtypeStruct((t, W_A), F32), jax.ShapeDtypeStruct((t, W_C), F32)],
            grid=(nb, nq),
            in_specs=[qblk(HQ), qblk(HQ)] + extra_specs
            + [xblk(HQ), xblk(2 * W_A), cblk(LANE), cblk(2 * LANE),
               sblk(HQ), sblk(2 * W_A), sblk(LANE), sblk(2 * LANE)],
            out_specs=[qblk(W_A), qblk(W_C)],
            compiler_params=_cp(("parallel", "parallel"), VMEM_LIMIT),
            name="attn_lat_bound" if bound else "attn_lat",
        )(qa, qc, *([kq] if bound else []), kax, vax, kcx, vcx, ka, va, kc, vc)

    return lax.cond(worst < BOUND_LIMIT, lambda: call(True), lambda: call(False))


NORM_ROWS = 24


def _knorm_kernel(ka_ref, kax_ref, kc_ref, kcx_ref, qa_ref, qc_ref, o_ref):
    def max_n2(x, lanes=None):
        xf = x.astype(F32)
        x2 = xf * xf
        if lanes is not None:
            lane = lax.broadcasted_iota(jnp.int32, x2.shape, 1)
            x2 = jnp.where((lane >= lanes[0]) & (lane < lanes[1]), x2, 0.0)
        return jnp.max(jnp.sum(x2, axis=-1, keepdims=True), axis=0, keepdims=True)

    rows = []
    for h in range(H_A):
        hs = slice(h * LANE, (h + 1) * LANE)
        rows.append(jnp.maximum(max_n2(ka_ref[:, hs]), max_n2(kax_ref[:, hs])))
    for kv in range(KV_C):
        lanes = (kv * HD_C, (kv + 1) * HD_C)
        rows.append(jnp.maximum(max_n2(kc_ref[...], lanes), max_n2(kcx_ref[...], lanes)))
    rows += [max_n2(qa_ref[:, h * LANE:(h + 1) * LANE]) for h in range(H_A)]
    rows += [max_n2(qc_ref[:, h * LANE:(h + 1) * LANE]) for h in range(H_C)]
    o_ref[...] = jnp.zeros_like(o_ref)
    for r, v in enumerate(rows):
        o_ref[r:r + 1, :] = jnp.broadcast_to(v, (1, LANE))


def _filter_mlp(z, w_in_ref, b_in_ref, w_mid_ref, b_mid_ref, w_out_ref, b_out_ref, freq_ref):
    freq = freq_ref[...]
    a = jnp.sin(freq * (_dot3(z, w_in_ref[...]) + b_in_ref[...]))
    for i in range(HY_INNER):
        a = jnp.sin(freq * (_dot3(a, w_mid_ref[i]) + b_mid_ref[i]))
    return _dot3(a, w_out_ref[...]) + b_out_ref[...]


def _conv3(u, up, dn, cw_ref, cb_ref):
    return up * cw_ref[0:1, :] + u * cw_ref[1:2, :] + dn * cw_ref[2:3, :] + cb_ref[...]


def _filt_ctx_kernel(z_ref, dec_ref, w_in_ref, b_in_ref, w_mid_ref, b_mid_ref, w_out_ref, b_out_ref, freq_ref,
                     c_ref, s_ref, o_ref):
    n = z_ref.shape[0]
    h = _filter_mlp(z_ref[...], w_in_ref, b_in_ref, w_mid_ref, b_mid_ref, w_out_ref, b_out_ref, freq_ref)
    dec = dec_ref[...]
    row = lax.broadcasted_iota(jnp.int32, dec.shape, 0)
    half = HY_ORDER * HY_CH
    for o in range(HY_ORDER):
        hf = h[:, o * HY_CH:(o + 1) * HY_CH] * dec
        hb = jnp.where(row > 0, h[:, half + o * HY_CH:half + (o + 1) * HY_CH] * dec, 0.0)
        nrm = jnp.sum(jnp.abs(hf) + jnp.abs(hb), axis=0, keepdims=True) + EPS
        scale = (1.0 / n) / nrm
        o_ref[o, 0] = _dot3(c_ref[...], hf + hb) * scale
        o_ref[o, 1] = -_dot3(s_ref[...], hf - hb) * scale


def _filt_ctx_call(l, z, dec, hw, cmat, smat):
    n = z.shape[0]
    full = lambda a: pl.BlockSpec((None,) + a.shape[1:], lambda i: (l,) + (0,) * (a.ndim - 1))
    const = lambda a: pl.BlockSpec(a.shape, lambda i: (0,) * a.ndim)
    names = ["hy_w_in", "hy_b_in", "hy_w_mid", "hy_b_mid", "hy_w_out", "hy_b_out", "hy_freq"]
    return pl.pallas_call(
        _filt_ctx_kernel,
        out_shape=jax.ShapeDtypeStruct((HY_ORDER, 2, n, HY_CH), F32),
        grid=(1,),
        in_specs=[const(z), const(dec)] + [full(hw[k]) for k in names] + [const(cmat), const(smat)],
        out_specs=pl.BlockSpec((HY_ORDER, 2, n, HY_CH), lambda i: (0, 0, 0, 0)),
        compiler_params=_cp(("arbitrary",), VMEM_LIMIT),
        name="hy_filter_ctx",
    )(z, dec, *[hw[k] for k in names], cmat, smat)


def _hy_ctx_kernel(u_ref, cw_ref, cb_ref, skip_ref, kf_ref, fwd_ref, inv_ref, o_ref):
    u = u_ref[...]
    n = u.shape[0]
    row = lax.broadcasted_iota(jnp.int32, u.shape, 0)
    up = jnp.where(row > 0, pltpu.roll(u, 1, 0), 0.0)
    dn = jnp.where(row < n - 1, pltpu.roll(u, n - 1, 0), 0.0)
    z = _conv3(u, up, dn, cw_ref, cb_ref)
    s = z[:, :HY_CH]
    gates = (z[:, HY_CH:2 * HY_CH], z[:, 2 * HY_CH:])
    for o in range(HY_ORDER):
        xs = _dot(fwd_ref[...], s.astype(BF16))
        xr, xi = xs[:n], xs[n:]
        kr, ki = kf_ref[o, 0], kf_ref[o, 1]
        ycat = jnp.concatenate([xr * kr - xi * ki, xr * ki + xi * kr], axis=0)
        y = _dot(inv_ref[...], ycat.astype(BF16))
        s = gates[o] * (y + s * skip_ref[o:o + 1, :])
    o_ref[...] = s


def _hy_ctx_call(l, u, hw, kf, fwd, inv, seq):
    t = u.shape[0]
    full = lambda a: pl.BlockSpec((None,) + a.shape[1:], lambda b: (l,) + (0,) * (a.ndim - 1))
    const = lambda a: pl.BlockSpec(a.shape, lambda b: (0,) * a.ndim)
    return pl.pallas_call(
        _hy_ctx_kernel,
        out_shape=jax.ShapeDtypeStruct((t, HY_CH), F32),
        grid=(t // seq,),
        in_specs=[pl.BlockSpec((seq, 3 * HY_CH), lambda b: (b, 0)),
                  full(hw["hy_conv_w"]), full(hw["hy_conv_b"]), full(hw["hy_skip"]),
                  const(kf), const(fwd), const(inv)],
        out_specs=pl.BlockSpec((seq, HY_CH), lambda b: (b, 0)),
        compiler_params=_cp(("parallel",), VMEM_LIMIT),
        name="hyena_ctx",
    )(u, hw["hy_conv_w"], hw["hy_conv_b"], hw["hy_skip"], kf, fwd, inv)


def _filt_lat_kernel(seq, z_ref, dec_ref, w_in_ref, b_in_ref, w_mid_ref, b_mid_ref, w_out_ref, b_out_ref, freq_ref,
                     k_ref, n_ref):
    i = pl.program_id(0)
    tr = z_ref.shape[0]
    h = _filter_mlp(z_ref[...], w_in_ref, b_in_ref, w_mid_ref, b_mid_ref, w_out_ref, b_out_ref, freq_ref)
    half = HY_ORDER * HY_CH
    row = i * tr + lax.broadcasted_iota(jnp.int32, (tr, half), 0)
    dec = dec_ref[...]
    kern = jnp.where(row < seq, h[:, :half], h[:, half:]) * jnp.concatenate([dec] * HY_ORDER, axis=1)
    k_ref[...] = kern

    @pl.when(i == 0)
    def _():
        n_ref[...] = jnp.zeros_like(n_ref)

    n_ref[...] += jnp.sum(jnp.abs(kern), axis=0, keepdims=True)


def _filt_lat_call(l, z, dec, hw, seq):
    n = z.shape[0]
    tr = min(512, n)
    half = HY_ORDER * HY_CH
    full = lambda a: pl.BlockSpec((None,) + a.shape[1:], lambda i: (l,) + (0,) * (a.ndim - 1))
    names = ["hy_w_in", "hy_b_in", "hy_w_mid", "hy_b_mid", "hy_w_out", "hy_b_out", "hy_freq"]
    return pl.pallas_call(
        functools.partial(_filt_lat_kernel, seq),
        out_shape=[jax.ShapeDtypeStruct((n, half), F32), jax.ShapeDtypeStruct((1, half), F32)],
        grid=(n // tr,),
        in_specs=[pl.BlockSpec((tr, z.shape[1]), lambda i: (i, 0)), pl.BlockSpec((tr, HY_CH), lambda i: (i, 0))]
        + [full(hw[k]) for k in names],
        out_specs=[pl.BlockSpec((tr, half), lambda i: (i, 0)), pl.BlockSpec((1, half), lambda i: (0, 0))],
        compiler_params=_cp(("arbitrary",), VMEM_LIMIT),
        name="hy_filter_lat",
    )(z, dec, *[hw[k] for k in names])


def _fa_kernel(f1_ref, k_ref, o_ref):
    r = _dot3(f1_ref[...], k_ref[...])
    o_ref[0] = r[:FFT_N1]
    o_ref[1] = r[FFT_N1:]


def _fa_call(f1f, kern2d):
    n1, w = kern2d.shape
    tn = min(2048, w)
    return pl.pallas_call(
        _fa_kernel,
        out_shape=jax.ShapeDtypeStruct((2, FFT_N1, w), F32),
        grid=(w // tn,),
        in_specs=[pl.BlockSpec(f1f.shape, lambda j: (0, 0)), pl.BlockSpec((n1, tn), lambda j: (0, j))],
        out_specs=pl.BlockSpec((2, FFT_N1, tn), lambda j: (0, 0, j)),
        compiler_params=_cp(("parallel",), VMEM_LIMIT),
        name="hy_filter_dft1",
    )(f1f, kern2d)


def _fb_kernel(n_total, a_ref, fh_ref, fl_ref, n_ref, o_ref):
    a = jnp.concatenate([a_ref[0], a_ref[1]], axis=1)
    ah, al = _split(a)
    x = _bdot(fh_ref[...], ah) + _bdot(fh_ref[...], al) + _bdot(fl_ref[...], ah)
    scale = (1.0 / n_total) / (n_ref[...] + EPS)
    o_ref[...] = x * scale[None]


def _fb_call(af5, f2h, f2l, nrm, n_total):
    _, n1, n2, c = af5.shape[0], af5.shape[1], af5.shape[2], af5.shape[3]
    k1t = 8
    return pl.pallas_call(
        functools.partial(_fb_kernel, n_total),
        out_shape=jax.ShapeDtypeStruct((n1, 2 * n2, c), F32),
        grid=(n1 // k1t,),
        in_specs=[pl.BlockSpec((2, k1t, n2, c), lambda j: (0, j, 0, 0)),
                  pl.BlockSpec((k1t, 2 * n2, 2 * n2), lambda j: (j, 0, 0)),
                  pl.BlockSpec((k1t, 2 * n2, 2 * n2), lambda j: (j, 0, 0)),
                  pl.BlockSpec((1, c), lambda j: (0, 0))],
        out_specs=pl.BlockSpec((k1t, 2 * n2, c), lambda j: (j, 0, 0)),
        compiler_params=_cp(("parallel",), VMEM_LIMIT),
        name="hy_filter_dft2",
    )(af5, f2h, f2l, nrm)


def _hconv_kernel(seq, u_ref, p_ref, n_ref, cw_ref, cb_ref, v_ref, x1_ref, x2_ref):
    i = pl.program_id(0)
    u = u_ref[...]
    tt = u.shape[0]
    row = lax.broadcasted_iota(jnp.int32, u.shape, 0)
    pos = (i * tt) % seq
    prev = jnp.where(pos > 0, p_ref[7:8, :], 0.0)
    nxt = jnp.where(pos + tt < seq, n_ref[0:1, :], 0.0)
    up = jnp.where(row > 0, pltpu.roll(u, 1, 0), prev)
    dn = jnp.where(row < tt - 1, pltpu.roll(u, tt - 1, 0), nxt)
    z = _conv3(u, up, dn, cw_ref, cb_ref)
    v_ref[...] = z[:, :HY_CH]
    x1_ref[...] = z[:, HY_CH:2 * HY_CH]
    x2_ref[...] = z[:, 2 * HY_CH:]


def _hconv_call(l, u, hw, seq):
    t, w = u.shape
    tt = min(512, seq)
    nblk8 = t // 8
    full = lambda a: pl.BlockSpec((None,) + a.shape[1:], lambda i: (l,) + (0,) * (a.ndim - 1))
    ob = pl.BlockSpec((tt, HY_CH), lambda i: (i, 0))
    return pl.pallas_call(
        functools.partial(_hconv_kernel, seq),
        out_shape=[jax.ShapeDtypeStruct((t, HY_CH), F32)] * 3,
        grid=(t // tt,),
        in_specs=[pl.BlockSpec((tt, w), lambda i: (i, 0)),
                  pl.BlockSpec((8, w), lambda i: (jnp.maximum(i * (tt // 8) - 1, 0), 0)),
                  pl.BlockSpec((8, w), lambda i: (jnp.minimum((i + 1) * (tt // 8), nblk8 - 1), 0)),
                  full(hw["hy_conv_w"]), full(hw["hy_conv_b"])],
        out_specs=[ob, ob, ob],
        compiler_params=_cp(("parallel",), VMEM_LIMIT),
        name="hyena_conv3",
    )(u, u, u, hw["hy_conv_w"], hw["hy_conv_b"])


def _ha_kernel(f1_ref, x_ref, o_ref):
    r = _dot(f1_ref[...], x_ref[...].astype(BF16))
    o_ref[0] = r[:FFT_N1].astype(BF16)
    o_ref[1] = r[FFT_N1:].astype(BF16)


def _ha_call(f1d, x2d, nb):
    rows, w = x2d.shape
    n1h = rows // nb
    tn = min(2048, w)
    return pl.pallas_call(
        _ha_kernel,
        out_shape=jax.ShapeDtypeStruct((nb, 2, FFT_N1, w), BF16),
        grid=(nb, w // tn),
        in_specs=[pl.BlockSpec(f1d.shape, lambda b, j: (0, 0)), pl.BlockSpec((n1h, tn), lambda b, j: (b, j))],
        out_specs=pl.BlockSpec((None, 2, FFT_N1, tn), lambda b, j: (b, 0, 0, j)),
        compiler_params=_cp(("parallel", "parallel"), VMEM_LIMIT),
        name="hyena_dft1",
    )(f1d, x2d)


def _hb_kernel(a_ref, f_ref, g_ref, kf_ref, o_ref):
    n2 = a_ref.shape[2]
    a = jnp.concatenate([a_ref[0], a_ref[1]], axis=1)
    x = _bdot(f_ref[...], a)
    xr, xi = x[:, :n2], x[:, n2:]
    kr, ki = kf_ref[:, :n2], kf_ref[:, n2:]
    y = jnp.concatenate([xr * kr - xi * ki, xr * ki + xi * kr], axis=1).astype(BF16)
    b = _bdot(g_ref[...], y)
    o_ref[0] = b[:, :n2].astype(BF16)
    o_ref[1] = b[:, n2:].astype(BF16)


def _hb_call(o, a5, f2, g2, kf):
    nb, _, n1, n2, c = a5.shape
    k1t = 16
    blk = pl.BlockSpec((None, 2, k1t, n2, c), lambda b, j: (b, 0, j, 0, 0))
    mat = pl.BlockSpec((k1t, 2 * n2, 2 * n2), lambda b, j: (j, 0, 0))
    return pl.pallas_call(
        _hb_kernel,
        out_shape=jax.ShapeDtypeStruct(a5.shape, BF16),
        grid=(nb, n1 // k1t),
        in_specs=[blk, mat, mat, pl.BlockSpec((k1t, 2 * n2, c), lambda b, j: (j, 0, o))],
        out_specs=blk,
        compiler_params=_cp(("parallel", "parallel"), VMEM_LIMIT),
        name="hyena_dft2",
    )(a5, f2, g2, kf)


def _hc_kernel(o, fc_ref, b_ref, s_ref, g_ref, skip_ref, o_ref):
    bcat = jnp.concatenate([b_ref[0], b_ref[1]], axis=0)
    y = _dot(fc_ref[...], bcat)
    o_ref[...] = g_ref[...] * (y + s_ref[...] * skip_ref[o:o + 1, :])


def _hc_call(l, o, fc, b4, s2d, g2d, skip_t):
    nb, _, n1, w = b4.shape
    rows = s2d.shape[0] // nb
    tn = min(2048, w)
    blk = pl.BlockSpec((rows, tn), lambda b, j: (b, j))
    return pl.pallas_call(
        functools.partial(_hc_kernel, o),
        out_shape=jax.ShapeDtypeStruct(s2d.shape, F32),
        grid=(nb, w // tn),
        in_specs=[pl.BlockSpec(fc.shape, lambda b, j: (0, 0)),
                  pl.BlockSpec((None, 2, n1, tn), lambda b, j: (b, 0, 0, j)),
                  blk, blk,
                  pl.BlockSpec((None, HY_ORDER, tn), lambda b, j: (l, 0, j))],
        out_specs=blk,
        compiler_params=_cp(("parallel", "parallel"), VMEM_LIMIT),
        name="hyena_dft3",
    )(fc, b4, s2d, g2d, skip_t)


def _postmix_kernel(x_ref, oa_ref, ob_ref, oc_ref, mod_ref, g_ref, wa_ref, wb_ref, wc_ref, wrh_ref, wrl_ref, br_ref,
                    x1_ref, h2_ref, idx_ref, gate_ref):
    na = _rms(oa_ref[...]) * g_ref[:, :W_A]
    nb = _rms(ob_ref[...]) * g_ref[:, W_A:W_A + W_B]
    nc = _rms(oc_ref[...]) * g_ref[:, W_A + W_B:]
    mix = (_dot(na.astype(BF16), wa_ref[...]) + _dot(nb.astype(BF16), wb_ref[...])
           + _dot(nc.astype(BF16), wc_ref[...]))
    x1 = x_ref[...] + mod_ref[2:3, :] * mix
    x1_ref[...] = x1
    h2 = _rms(x1) * (1.0 + mod_ref[4:5, :]) + mod_ref[3:4, :]
    h2_ref[...] = h2
    hh, hl = _split(h2)
    vals = _dot(hh, wrh_ref[...]) + _dot(hh, wrl_ref[...]) + _dot(hl, wrh_ref[...]) + br_ref[...]
    lane = lax.broadcasted_iota(jnp.int32, vals.shape, 1).astype(F32)
    idx_out = jnp.zeros(vals.shape, F32)
    top = jnp.zeros(vals.shape, F32)
    m0 = None
    for k in range(TOP_K):
        m = vals.max(axis=-1, keepdims=True)
        sel = jnp.min(jnp.where(vals == m, lane, float(LANE)), axis=-1, keepdims=True)
        if m0 is None:
            m0 = m
        idx_out = jnp.where(lane == k, sel, idx_out)
        top = jnp.where(lane == k, jnp.exp(m - m0), top)
        vals = jnp.where(lane == sel, -jnp.inf, vals)
    idx_ref[...] = idx_out.astype(jnp.int32)
    gate_ref[...] = top / jnp.sum(top, axis=-1, keepdims=True)


def _postmix_call(l, x, oa, ob, oc, mod, seq, wts):
    t, d = x.shape
    tm = TOKEN_TILE
    ncond = mod.shape[1]

    def cond(i):
        return (i * tm) // seq if ncond > 1 else 0

    row = lambda w: pl.BlockSpec((tm, w), lambda i: (i, 0))
    lay = lambda a: pl.BlockSpec((None,) + a.shape[1:], lambda i: (l,) + (0,) * (a.ndim - 1))
    names = ["g_out", "wo_a", "wo_b", "wo_c", "wr_hi", "wr_lo", "b_router"]
    return pl.pallas_call(
        _postmix_kernel,
        out_shape=[jax.ShapeDtypeStruct((t, d), F32), jax.ShapeDtypeStruct((t, d), F32),
                   jax.ShapeDtypeStruct((t, LANE), jnp.int32), jax.ShapeDtypeStruct((t, LANE), F32)],
        grid=(t // tm,),
        in_specs=[row(d), row(W_A), row(W_B), row(W_C),
                  pl.BlockSpec((None, None, 6, d), lambda i: (l, cond(i), 0, 0))] + [lay(wts[k]) for k in names],
        out_specs=[row(d), row(d), row(LANE), row(LANE)],
        compiler_params=_cp(("parallel",), VMEM_LIMIT),
        name="postmix",
    )(x, oa, ob, oc, mod, *[wts[k] for k in names])


SUB = 8


NGRP = ROW_BLOCK // SUB


def _moe_kernel(l, be_ref, rs_ref, nv_ref, nu_ref, run_ref, nxt_ref, tok_ref, gate_ref,
                h_ref, wi_hbm, bi_ref, wo_hbm, bo_ref, y_ref, buf, obuf, xb, wi_buf, wo_buf, sem):
    s = pl.program_id(0)
    tc = h_ref.shape[0]

    def weight_copies(e, slot):
        return (pltpu.make_async_copy(wi_hbm.at[l, e], wi_buf.at[slot], sem.at[slot, 0]),
                pltpu.make_async_copy(wo_hbm.at[l, e], wo_buf.at[slot], sem.at[slot, 1]))

    @pl.when(s == 0)
    def _():
        y_ref[...] = jnp.zeros_like(y_ref)
        buf[...] = jnp.zeros_like(buf)
        obuf[...] = jnp.zeros_like(obuf)
        for cp in weight_copies(be_ref[0], 0):
            cp.start()

    run = run_ref[s]
    wslot = lax.rem(run, 2)
    first = jnp.logical_or(s == 0, run != run_ref[jnp.maximum(s - 1, 0)])

    @pl.when(jnp.logical_and(first, s < nu_ref[0] + 2))
    def _():
        for cp in weight_copies(be_ref[s], wslot):
            cp.wait()

        @pl.when(nxt_ref[s] >= 0)
        def _():
            for cp in weight_copies(nxt_ref[s], 1 - wslot):
                cp.start()

    @pl.when(s < nu_ref[0] + 2)
    def _():
        slot_g = lax.rem(s, 2)
        slot_c = 1 - slot_g

        xb[...] = buf[slot_c].reshape(ROW_BLOCK, buf.shape[3]).astype(BF16)

        base_s = rs_ref[s]
        nv = nv_ref[s]
        for g in range(NGRP):
            ts = [jnp.where(g * SUB + j < nv, tok_ref[base_s + g * SUB + j], tc) for j in range(SUB)]
            new = [y_ref[pl.ds(ts[j], 1), :] + gate_ref[base_s + g * SUB + j] * obuf[slot_g, g, j:j + 1, :]
                   for j in range(SUB)]
            for j in range(SUB):
                y_ref[pl.ds(ts[j], 1), :] = new[j]

        base_g = rs_ref[s + 2]
        for g in range(NGRP):
            rows = [h_ref[pl.ds(tok_ref[base_g + g * SUB + j], 1), :] for j in range(SUB)]
            for j in range(SUB):
                buf[slot_g, g, j:j + 1, :] = rows[j]

        gu = _dot(xb[...], wi_buf[wslot]) + bi_ref[...]
        gt = jnp.minimum(gu[:, :D_FF], SWIGLU_LIMIT)
        lin = jnp.clip(gu[:, D_FF:], -SWIGLU_LIMIT, SWIGLU_LIMIT)
        act = (lin + 1.0) * gt * (1.0 / (1.0 + jnp.exp(-SWIGLU_ALPHA * gt)))
        out = _dot(act.astype(BF16), wo_buf[wslot]) + bo_ref[...]
        obuf[slot_c] = out.reshape(obuf.shape[1:])


def _moe_kernel_entry(has_alias, l, *refs):
    refs = list(refs)
    if has_alias:
        del refs[13]
    _moe_kernel(l, *refs)


def _route(idx, gates, n_blocks):
    m = idx.shape[0] * TOP_K
    e = idx.reshape(m)
    flat = jnp.arange(m, dtype=jnp.int32)
    skey, gate = lax.sort((e * m + flat, gates.reshape(m)), num_keys=1)
    tok = (skey % m) // TOP_K
    experts = jnp.arange(N_EXPERTS, dtype=jnp.int32)
    cnt = jnp.sum((e[:, None] == experts[None, :]).astype(jnp.int32), axis=0)
    nblk = (cnt + ROW_BLOCK - 1) // ROW_BLOCK
    bend = jnp.cumsum(nblk)
    n_used = bend[-1]
    blk = jnp.arange(-2, n_blocks + 2, dtype=jnp.int32)
    bcl = jnp.clip(blk, 0, n_used - 1)
    be = jnp.sum((bend[None, :] <= bcl[:, None]).astype(jnp.int32), axis=1)
    oh = (be[:, None] == experts[None, :]).astype(jnp.int32)
    pick = lambda v: jnp.sum(oh * v[None, :], axis=1)
    off = (bcl - pick(bend - nblk)) * ROW_BLOCK
    valid = (blk >= 0) & (blk < n_used)
    rs = jnp.where(valid, pick(jnp.cumsum(cnt) - cnt) + off, 0)
    nv = jnp.where(valid, jnp.clip(pick(cnt) - off, 0, ROW_BLOCK), 0)
    pad = jnp.zeros((ROW_BLOCK,), jnp.int32)
    be_step = be[1:n_blocks + 3]
    change = jnp.concatenate([jnp.zeros((1,), jnp.int32), (be_step[1:] != be_step[:-1]).astype(jnp.int32)])
    later = (experts[None, :] > be_step[:, None]) & (nblk[None, :] > 0)
    nxt = jnp.min(jnp.where(later, experts[None, :], N_EXPERTS), axis=1)
    nxt = jnp.where(nxt == N_EXPERTS, -1, nxt)
    return (be_step, rs, nv, n_used.reshape(1), jnp.cumsum(change), nxt, jnp.concatenate([tok, pad]),
            jnp.concatenate([gate, pad.astype(F32)]))


def _moe_call(l, h2, idx, gates, wts):
    t, d = h2.shape
    tc = min(4096, t)
    n_blocks = tc * TOP_K // ROW_BLOCK + N_EXPERTS
    y = None
    for c in range(t // tc):
        route = _route(idx[c * tc:(c + 1) * tc], gates[c * tc:(c + 1) * tc], n_blocks)
        wspec = lambda r, w: pl.BlockSpec((None, None, r, w), lambda s, be, *_: (l, be[s], 0, 0))
        hbm = pl.BlockSpec(memory_space=pl.ANY)
        in_specs = [pl.BlockSpec((tc, d), lambda i, *_: (c, 0), pipeline_mode=pl.Buffered(1)),
                    hbm, wspec(1, 2 * D_FF), hbm, wspec(1, d)]
        args = [h2, wts["w_moe_in"], wts["b_moe_in"], wts["w_moe_out"], wts["b_moe_out"]]
        aliases = {}
        if y is not None:
            in_specs.append(pl.BlockSpec(memory_space=pl.ANY))
            args.append(y)
            aliases = {len(route) + len(args) - 1: 0}
        grid_spec = pltpu.PrefetchScalarGridSpec(
            num_scalar_prefetch=len(route),
            grid=(n_blocks + 2,),
            in_specs=in_specs,
            out_specs=pl.BlockSpec((None, tc + SUB, d), lambda i, *_: (c, 0, 0), pipeline_mode=pl.Buffered(1)),
            scratch_shapes=[pltpu.VMEM((2, NGRP, SUB, d), F32), pltpu.VMEM((2, NGRP, SUB, d), F32),
                            pltpu.VMEM((ROW_BLOCK, d), BF16),
                            pltpu.VMEM((2, d, 2 * D_FF), BF16), pltpu.VMEM((2, D_FF, d), BF16),
                            pltpu.SemaphoreType.DMA((2, 2))],
        )
        y = pl.pallas_call(
            functools.partial(_moe_kernel_entry, y is not None, l),
            out_shape=jax.ShapeDtypeStruct((t // tc, tc + SUB, d), F32),
            grid_spec=grid_spec,
            input_output_aliases=aliases,
            compiler_params=_cp(("arbitrary",), VMEM_LIMIT),
            name="moe_experts",
        )(*route, *args)
    return y


def _final_kernel(x_ref, y_ref, mod_ref, o_ref):
    o_ref[...] = x_ref[...] + mod_ref[5:6, :] * y_ref[...]


def _final_call(l, x, y, mod, seq):
    t, d = x.shape
    tm = TOKEN_TILE
    ncond = mod.shape[1]
    row = pl.BlockSpec((tm, d), lambda i: (i, 0))
    return pl.pallas_call(
        _final_kernel,
        out_shape=jax.ShapeDtypeStruct((t, d), F32),
        grid=(t // tm,),
        in_specs=[row, _chunk_rows_spec(y, tm),
                  pl.BlockSpec((None, None, 6, d), lambda i: (l, (i * tm) // seq if ncond > 1 else 0, 0, 0))],
        out_specs=row,
        compiler_params=_cp(("parallel",), VMEM_LIMIT),
        name="final_residual",
    )(x, y, mod)


def _rope_tables(seq, rot_dim, lane_map):
    shift = rot_dim // 2
    n_rows = seq // GRID_W
    rows = jnp.repeat(jnp.arange(n_rows, dtype=F32), GRID_W)
    cols = jnp.tile(jnp.arange(GRID_W, dtype=F32), n_rows)
    axis_dim = rot_dim // 2
    inv_freq = ROPE_THETA ** (-jnp.arange(0, axis_dim, 2, dtype=F32) / axis_dim)
    ang = jnp.concatenate([rows[:, None] * inv_freq, cols[:, None] * inv_freq], axis=-1)
    cos, sin = jnp.cos(ang), jnp.sin(ang)
    pair = np.zeros((LANE,), np.int32)
    in_rot = np.zeros((LANE,), np.float32)
    first = np.zeros((LANE,), np.float32)
    for ln in range(LANE):
        m = lane_map(ln)
        if m is not None:
            pair[ln], in_rot[ln], first[ln] = m[0], 1.0, 1.0 if m[1] == 0 else 0.0
    c = jnp.where(in_rot[None, :] > 0, cos[:, pair], 1.0)
    s = sin[:, pair] * in_rot[None, :] * (1.0 - 2.0 * first[None, :])
    perm = np.zeros((2 * LANE, 2 * LANE), np.float32)
    for ln in range(LANE):
        if in_rot[ln] > 0:
            src = ln + shift if first[ln] > 0 else ln - shift
            perm[src, ln] = perm[LANE + src, LANE + ln] = 1.0
    return jnp.stack([c, s]).astype(F32), jnp.asarray(perm, BF16)


def _lane_map_a(ln):
    o = ln - NOPE_A
    if 0 <= o < ROPE_A:
        return (o % (ROPE_A // 2), o // (ROPE_A // 2))
    return None


def _lane_map_c(ln):
    o = ln % HD_C
    return (o % (HD_C // 2), o // (HD_C // 2))


def _phase(num, den):
    ang = (2.0 * math.pi / den) * (num % den).astype(F32)
    return jnp.cos(ang), jnp.sin(ang)


def _ctx_dft(n):
    f = jnp.arange(n, dtype=jnp.int32)[:, None]
    t = jnp.arange(n, dtype=jnp.int32)[None, :]
    c, s = _phase((2 * f + 1) * t, 4 * n)
    fwd = jnp.concatenate([c, -s], axis=0).astype(BF16)
    inv = jnp.concatenate([c.T, -s.T], axis=1).astype(BF16)
    return c, s, fwd, inv


def _lat_dft(seq):
    n = 2 * seq
    n1, n2 = FFT_N1, n // FFT_N1
    k1 = jnp.arange(n1, dtype=jnp.int32)
    c1, s1 = _phase(k1[:, None] * k1[None, :], n1)
    f1f = jnp.concatenate([c1, -s1], axis=0)
    f1d = f1f[:, :n1 // 2].astype(BF16)
    fc = jnp.concatenate([c1[:, :n1 // 2].T, -s1[:, :n1 // 2].T], axis=1).astype(BF16)
    k2 = jnp.arange(n2, dtype=jnp.int32)
    num = (k2[None, :, None] * k2[None, None, :]) * n1 + k2[None, None, :] * k1[:, None, None]
    cm, sm = _phase(num, n)
    mr, mi = cm, -sm
    f2 = jnp.concatenate([jnp.concatenate([mr, -mi], axis=2), jnp.concatenate([mi, mr], axis=2)], axis=1)
    mrt, mit = jnp.swapaxes(mr, 1, 2), jnp.swapaxes(mi, 1, 2)
    g2 = jnp.concatenate([jnp.concatenate([mrt, mit], axis=2), jnp.concatenate([-mit, mrt], axis=2)], axis=1)
    f2h, f2l = _split(f2)
    return dict(f1f=f1f, f1d=f1d, fc=fc, f2=f2h, f2l=f2l, g2=g2.astype(BF16), n1=n1, n2=n2, n=n)


def _hy_features(seq):
    t = jnp.linspace(0.0, 1.0, seq, dtype=F32)[:, None]
    bands = (HY_EMB - 1) // 2
    f = jnp.linspace(1e-4, bands - 1, bands, dtype=F32)[None, :]
    w = 2.0 * math.pi * jnp.arange(seq, dtype=F32)[:, None] / seq
    z = jnp.concatenate([t, jnp.cos(f * w), jnp.sin(f * w)], axis=-1)
    z = jnp.pad(z, ((0, 0), (0, HY_FO - HY_EMB)))
    deltas = jnp.abs(jnp.linspace(HY_MIN_DECAY, HY_MAX_DECAY, HY_CH, dtype=F32))
    return z, jnp.exp(-t * deltas)


def _prep_weights(w_in, mla_g_qa, mla_w_uq, mla_g_kva, mla_w_ukv, mla_g_q, mla_g_k, gqa_g_q, gqa_g_k,
                  g_out, w_out, w_router, b_router, hy_w_in):
    depth = w_in.shape[0]
    cuts = np.cumsum([0, Q_RANK, KV_RANK, ROPE_A, 3 * HY_CH, H_C * HD_C, KV_C * HD_C, KV_C * HD_C])
    wb = w_in.astype(BF16)
    zeros = lambda w: jnp.zeros(wb.shape[:2] + (w,), BF16)
    parts = [wb[:, :, cuts[0]:cuts[2]], zeros(NOPE_A), wb[:, :, cuts[2]:cuts[3]], zeros(LANE - QK_A),
             wb[:, :, cuts[3]:cuts[4]]]
    for h in range(H_C):
        wh = wb[:, :, cuts[4] + h * HD_C:cuts[4] + (h + 1) * HD_C]
        parts += [wh, zeros(HD_C)] if h < H_C // KV_C else [zeros(HD_C), wh]
    w_in_p = jnp.concatenate(parts + [wb[:, :, cuts[5]:cuts[7]]], axis=-1)

    w_uq = jnp.pad(mla_w_uq.reshape(depth, Q_RANK, H_A, QK_A), ((0, 0), (0, 0), (0, 0), (0, LANE - QK_A)))
    w_uq = w_uq.reshape(depth, Q_RANK, HQ).astype(BF16)
    ukv = mla_w_ukv.reshape(depth, KV_RANK, H_A, NOPE_A + V_A)
    wk = jnp.pad(ukv[..., :NOPE_A], ((0, 0), (0, 0), (0, 0), (0, LANE - NOPE_A))).reshape(depth, KV_RANK, HQ)
    wv = ukv[..., NOPE_A:].reshape(depth, KV_RANK, W_A)
    w_kv = jnp.concatenate([wk, wv], axis=-1).astype(BF16)

    def pad_row(v):
        return jnp.pad(v, ((0, 0), (0, HQ - v.shape[1])))

    head_a = lambda g: jnp.tile(jnp.pad(g, ((0, 0), (0, LANE - QK_A))), (1, H_A))
    head_c = lambda g, reps: jnp.tile(g, (1, reps))
    gains = jnp.stack([
        pad_row(mla_g_qa), pad_row(mla_g_kva),
        head_a(mla_g_q) * (QK_A ** -0.5 * LOG2E), head_a(mla_g_k),
        head_c(gqa_g_q, 2 * H_C) * (HD_C ** -0.5 * LOG2E), pad_row(head_c(gqa_g_k, 2)),
        jnp.zeros((depth, HQ), F32), jnp.zeros((depth, HQ), F32)], axis=1).astype(F32)

    g_per = H_C // KV_C
    perm_c = np.concatenate([np.arange(h * HD_C, (h + 1) * HD_C) for g in range(g_per) for h in (g, g + g_per)])
    rows_c = W_A + W_B + perm_c
    g_o = jnp.concatenate([g_out[:, :W_A + W_B], g_out[:, rows_c]], axis=1).reshape(depth, 1, -1)
    wr = jnp.pad(w_router, ((0, 0), (0, 0), (0, LANE - N_EXPERTS)))
    wr_hi, wr_lo = _split(wr)
    br = jnp.pad(b_router, ((0, 0), (0, LANE - N_EXPERTS)), constant_values=-1e30).reshape(depth, 1, LANE)
    return dict(w_in=w_in_p, w_uq=w_uq, w_kv=w_kv, gains=gains, g_out=g_o,
                wo_a=w_out[:, :W_A].astype(BF16), wo_b=w_out[:, W_A:W_A + W_B].astype(BF16),
                wo_c=w_out[:, rows_c].astype(BF16), wr_hi=wr_hi, wr_lo=wr_lo, b_router=br,
                hy_w_in=jnp.pad(hy_w_in, ((0, 0), (0, HY_FO - HY_EMB), (0, 0))))


def _hyena_lat(l, u, hw, kf, dft, seq):
    t = u.shape[0]
    nb = t // seq
    n1, n2 = dft["n1"], dft["n2"]
    w2 = n2 * HY_CH
    v, x1, x2 = _hconv_call(l, u, hw, seq)
    to2d = lambda a: a.reshape(t // n2, w2)
    s = to2d(v)
    for o, gate in enumerate((x1, x2)):
        a = _ha_call(dft["f1d"], s, nb)
        b = _hb_call(o, a.reshape(nb, 2, n1, n2, HY_CH), dft["f2"], dft["g2"], kf)
        s = _hc_call(l, o, dft["fc"], b.reshape(nb, 2, n1, w2), s, to2d(gate), hw["skip_t"])
    return s.reshape(t, HY_CH)


def _filter_lat(l, z, dec, hw, dft, seq):
    kern, nrm = _filt_lat_call(l, z, dec, hw, seq)
    n1, n2, n = dft["n1"], dft["n2"], dft["n"]
    c = kern.shape[1]
    af = _fa_call(dft["f1f"], kern.reshape(n1, n2 * c))
    return _fb_call(af.reshape(2, n1, n2, c), dft["f2"], dft["f2l"], nrm, n)


def kernel(x_prompt, x_sample, c, c_ctx, cache_mla_ckv, cache_mla_kpe, cache_gqa_k, cache_gqa_v, w_mod, b_mod, w_in, mla_g_qa, mla_w_uq, mla_g_kva, mla_w_ukv, mla_g_q, mla_g_k, hy_conv_w, hy_conv_b, hy_w_in, hy_b_in, hy_w_mid, hy_b_mid, hy_w_out, hy_b_out, hy_freq, hy_skip, gqa_g_q, gqa_g_k, g_out, w_out, w_router, b_router, w_moe_in, b_moe_in, w_moe_out, b_moe_out):
    batch, seq_c, d = x_prompt.shape
    nb_l, seq_l, _ = x_sample.shape
    depth = w_in.shape[0]
    past = cache_mla_ckv.shape[2]
    assert d == D_MODEL and seq_l % GRID_W == 0 and (2 * seq_l) % FFT_N1 == 0

    wts = _prep_weights(w_in, mla_g_qa, mla_w_uq, mla_g_kva, mla_w_ukv, mla_g_q, mla_g_k, gqa_g_q, gqa_g_k,
                        g_out, w_out, w_router, b_router, hy_w_in)
    wts["w_moe_in"] = _cast_call(w_moe_in.reshape(depth * N_EXPERTS, d, 2 * D_FF), 1024).reshape(w_moe_in.shape)
    wts["w_moe_out"] = _cast_call(w_moe_out.reshape(depth * N_EXPERTS, D_FF, d), 1024).reshape(w_moe_out.shape)
    wts["b_moe_in"] = b_moe_in.reshape(depth, N_EXPERTS, 1, 2 * D_FF)
    wts["b_moe_out"] = b_moe_out.reshape(depth, N_EXPERTS, 1, d)
    hw = dict(hy_w_in=wts["hy_w_in"], hy_b_in=hy_b_in.reshape(depth, 1, HY_FO), hy_w_mid=hy_w_mid,
              hy_b_mid=hy_b_mid.reshape(depth, HY_INNER, 1, HY_FO), hy_w_out=hy_w_out,
              hy_b_out=hy_b_out.reshape(depth, 1, -1), hy_freq=hy_freq.reshape(depth, 1, HY_FO),
              hy_conv_w=hy_conv_w, hy_conv_b=hy_conv_b.reshape(depth, 1, -1), hy_skip=hy_skip)
    dft_l = _lat_dft(seq_l)
    hw["skip_t"] = jnp.tile(hy_skip, (1, 1, dft_l["n2"]))

    conds = jnp.zeros((8, d), F32).at[0].set(c_ctx).at[1:1 + nb_l].set(c)
    mod = _mod_call(conds, w_mod, b_mod).reshape(depth, 8, 6, d)
    mod_c, mod_l = mod[:, 0:1], mod[:, 1:1 + nb_l]

    rope_tabs = (_rope_tables(seq_l, ROPE_A, _lane_map_a), _rope_tables(seq_l, HD_C, _lane_map_c))
    kax, vax = _cachekv_call(cache_mla_ckv, jnp.pad(cache_mla_kpe, ((0, 0), (0, 0), (0, 0), (NOPE_A, LANE - QK_A))), wts)
    kcx = cache_gqa_k.reshape(nb_l, depth, past, KV_C * HD_C).astype(BF16)
    vcx = cache_gqa_v.reshape(nb_l, depth, past, KV_C * HD_C).astype(BF16)
    vcx = jnp.concatenate([vcx, jnp.ones_like(vcx)], axis=-1)

    z_c, dec_c = _hy_features(seq_c)
    cmat, smat, fwd_c, inv_c = _ctx_dft(seq_c)
    z_l, dec_l = _hy_features(seq_l)
    z_full = jnp.concatenate([z_l, jnp.zeros((1, HY_FO), F32), z_l[:0:-1]], axis=0)
    dec_full = jnp.concatenate([dec_l, jnp.zeros((1, HY_CH), F32), dec_l[:0:-1]], axis=0)

    xc = x_prompt.reshape(batch * seq_c, d)
    xl = x_sample.reshape(nb_l * seq_l, d)
    yc = yl = None
    new_ckv, new_kpe, new_k, new_v = [], [], [], []
    for l in range(depth):
        (xc, qa, ka, va, ckv, kpe, qc, kc, vc, kcf, vcf, u) = _premix_call(l, xc, yc, mod_c, seq_c, False, wts, None)
        new_ckv.append(ckv)
        new_kpe.append(kpe[:, NOPE_A:QK_A])
        new_k.append(kcf)
        new_v.append(vcf)
        oa, oc = _attn_ctx_call(qa, ka, va, qc, kc, vc, seq_c)
        kf_c = _filt_ctx_call(l, z_c, dec_c, hw, cmat, smat)
        ob = _hy_ctx_call(l, u, hw, kf_c, fwd_c, inv_c, seq_c)
        xc, h2, idx, gates = _postmix_call(l, xc, oa, ob, oc, mod_c, seq_c, wts)
        yc = _moe_call(l, h2, idx[:, :TOP_K], gates[:, :TOP_K], wts)
        (xl, qa, ka, va, _, _, qc, kc, vc, _, _, u) = _premix_call(l, xl, yl, mod_l, seq_l, True, wts, rope_tabs)
        oa, oc = _attn_lat_call(l, qa, ka, va, qc, kc, vc, kax, vax, kcx, vcx, seq_l)
        kf_l = _filter_lat(l, z_full, dec_full, hw, dft_l, seq_l)
        ob = _hyena_lat(l, u, hw, kf_l, dft_l, seq_l)
        xl, h2, idx, gates = _postmix_call(l, xl, oa, ob, oc, mod_l, seq_l, wts)
        yl = _moe_call(l, h2, idx[:, :TOP_K], gates[:, :TOP_K], wts)
    y_prompt = _final_call(depth - 1, xc, yc, mod_c, seq_c).reshape(batch, seq_c, d)
    y_sample = _final_call(depth - 1, xl, yl, mod_l, seq_l).reshape(nb_l, seq_l, d)
    stack = lambda xs, tail: jnp.stack([a.reshape((batch, seq_c) + tail) for a in xs], axis=1)
    return (y_prompt, y_sample, stack(new_ckv, (KV_RANK,)), stack(new_kpe, (ROPE_A,)),
            stack(new_k, (KV_C, HD_C)), stack(new_v, (KV_C, HD_C)))
```

```python
import functools
import math

import numpy as np
import jax
import jax.numpy as jnp
from jax import lax
from jax.experimental import pallas as pl
from jax.experimental.pallas import tpu as pltpu

F32 = jnp.float32
BF16 = jnp.bfloat16

D_MODEL = 1024
GRID_W = 64
EPS = 1e-6
ROPE_THETA = 10000.0
H_A = 6
Q_RANK = 256
KV_RANK = 128
NOPE_A = 64
ROPE_A = 32
V_A = 64
QK_A = NOPE_A + ROPE_A
HY_CH = 256
HY_ORDER = 2
HY_EMB = 33
HY_FO = 64
HY_INNER = 2
HY_MIN_DECAY = math.log(1e-2) / 1.5
HY_MAX_DECAY = math.log(1e-2) / 0.3
H_C = 6
KV_C = 2
HD_C = 64
N_EXPERTS = 32
TOP_K = 4
D_FF = 1024
SWIGLU_LIMIT = 7.0
SWIGLU_ALPHA = 1.702
LOG2E = 1.4426950408889634

LANE = 128
HQ = H_A * LANE
W_A = H_A * V_A
W_B = HY_CH
W_C = H_C * HD_C
IN_P = Q_RANK + KV_RANK + LANE + 3 * HY_CH + HQ + 2 * LANE
ROW_BLOCK = 256
TOKEN_TILE = 512
ATTN_Q_TILE = 512
VMEM_LIMIT = 60 * 1024 * 1024
FFT_N1 = 128
FFT_N2 = 64


def _cp(sem, vmem=None):
    return pltpu.CompilerParams(dimension_semantics=sem, vmem_limit_bytes=vmem)


def _dot(a, b):
    return jnp.dot(a, b, preferred_element_type=F32)


def _split(a):
    hi = a.astype(BF16)
    return hi, (a - hi.astype(F32)).astype(BF16)


def _dot3(a, b):
    ah, al = _split(a)
    bh, bl = _split(b)
    return _dot(ah, bh) + _dot(ah, bl) + _dot(al, bh)


def _bdot(a, b):
    return lax.dot_general(a, b, (((2,), (1,)), ((0,), (0,))), preferred_element_type=F32)


def _rms(x, n=None):
    ss = jnp.sum(x * x, axis=-1, keepdims=True) * (1.0 / (n or x.shape[-1]))
    return x * lax.rsqrt(ss + EPS)


def _head_norm(x, nvalid):
    outs = []
    for h in range(x.shape[1] // LANE):
        blk = x[:, h * LANE:(h + 1) * LANE]
        ss = jnp.sum(blk * blk, axis=-1, keepdims=True) * (1.0 / nvalid)
        outs.append(blk * lax.rsqrt(ss + EPS))
    return outs[0] if len(outs) == 1 else jnp.concatenate(outs, axis=1)


def _rope(x, rope):
    tab_ref, perm_ref = rope
    c, s = tab_ref[0], tab_ref[1]
    xb = x.astype(BF16)
    nblk = x.shape[1] // LANE
    outs = []
    b = 0
    while b < nblk:
        w = 2 if nblk - b >= 2 else 1
        perm = perm_ref[...] if w == 2 else perm_ref[:LANE, :LANE]
        sw = _dot(xb[:, b * LANE:(b + w) * LANE], perm)
        for k in range(w):
            outs.append(x[:, (b + k) * LANE:(b + k + 1) * LANE] * c + sw[:, k * LANE:(k + 1) * LANE] * s)
        b += w
    return outs[0] if len(outs) == 1 else jnp.concatenate(outs, axis=1)


def _mod_kernel(c_ref, w_ref, b_ref, o_ref):
    c = c_ref[...]
    s = c * (1.0 / (1.0 + jnp.exp(-c)))
    o_ref[...] = _dot3(s, w_ref[...]) + b_ref[...]


def _mod_call(conds, w_mod, b_mod):
    depth, d, n6 = w_mod.shape
    tn = 1536
    return pl.pallas_call(
        _mod_kernel,
        out_shape=jax.ShapeDtypeStruct((depth, 8, n6), F32),
        grid=(depth, n6 // tn),
        in_specs=[pl.BlockSpec((8, d), lambda l, j: (0, 0)),
                  pl.BlockSpec((None, d, tn), lambda l, j: (l, 0, j)),
                  pl.BlockSpec((None, 1, tn), lambda l, j: (l, 0, j))],
        out_specs=pl.BlockSpec((None, 8, tn), lambda l, j: (l, 0, j)),
        compiler_params=_cp(("parallel", "parallel"), VMEM_LIMIT),
        name="mod",
    )(conds, w_mod, b_mod.reshape(depth, 1, n6))


def _cast_kernel(x_ref, o_ref):
    o_ref[...] = x_ref[...].astype(BF16)


def _cast_call(w, tr):
    n, r, c = w.shape
    return pl.pallas_call(
        _cast_kernel,
        out_shape=jax.ShapeDtypeStruct(w.shape, BF16),
        grid=(n, r // tr),
        in_specs=[pl.BlockSpec((None, tr, c), lambda i, j: (i, j, 0))],
        out_specs=pl.BlockSpec((None, tr, c), lambda i, j: (i, j, 0)),
        compiler_params=_cp(("parallel", "parallel"), VMEM_LIMIT),
        name="cast_bf16",
    )(w)


def _kv_heads(ckv_bf, kpe, w_kv_ref, gk, rope_ref, use_rope):
    kvp = _dot(ckv_bf, w_kv_ref[...])
    kn = kvp[:, :HQ]
    ka = jnp.concatenate([kn[:, h * LANE:(h + 1) * LANE] + kpe for h in range(H_A)], axis=1)
    ka = _head_norm(ka, QK_A) * gk
    if use_rope:
        ka = _rope(ka, rope_ref)
    return ka, kvp[:, HQ:]


def _premix_kernel(has_prev, use_rope, *refs):
    it = iter(refs)
    x_ref = next(it)
    if has_prev:
        y_ref, modp_ref = next(it), next(it)
    mod_ref, w_in_ref, w_uq_ref, w_kv_ref, g_ref = next(it), next(it), next(it), next(it), next(it)
    if use_rope:
        ra_ref, rc_ref = (next(it), next(it)), (next(it), next(it))
    else:
        ra_ref = rc_ref = None
    (xo_ref, qa_ref, ka_ref, va_ref, ckv_ref, kpe_ref, qc_ref, kc_ref, vc_ref, kcf_ref, vcf_ref, u_ref) = it

    x = x_ref[...]
    if has_prev:
        x = x + modp_ref[5:6, :] * y_ref[...]
    xo_ref[...] = x
    h = _rms(x) * (1.0 + mod_ref[1:2, :]) + mod_ref[0:1, :]
    proj = _dot(h.astype(BF16), w_in_ref[...])
    o = 0
    c_q = proj[:, o:o + Q_RANK]; o += Q_RANK
    c_kv = proj[:, o:o + KV_RANK]; o += KV_RANK
    kpe = proj[:, o:o + LANE]; o += LANE
    u_ref[...] = proj[:, o:o + 3 * HY_CH]; o += 3 * HY_CH
    q_c = proj[:, o:o + HQ]; o += HQ
    k_c = proj[:, o:o + LANE]; o += LANE
    v_c = proj[:, o:o + LANE]

    cqn = _rms(c_q) * g_ref[0:1, :Q_RANK]
    qa = _head_norm(_dot(cqn.astype(BF16), w_uq_ref[...]), QK_A) * g_ref[2:3, :]
    if use_rope:
        qa = _rope(qa, ra_ref)
    qa_ref[...] = qa.astype(BF16)
    ckv = _rms(c_kv) * g_ref[1:2, :KV_RANK]
    ckv_ref[...] = ckv
    kpe_ref[...] = kpe
    ka, va = _kv_heads(ckv.astype(BF16), kpe, w_kv_ref, g_ref[3:4, :], ra_ref, use_rope)
    ka_ref[...] = ka.astype(BF16)
    va_ref[...] = _with_ones(va.astype(BF16))

    qc = _head_norm(q_c, HD_C) * g_ref[4:5, :]
    if use_rope:
        qc = _rope(qc, rc_ref)
    qc_ref[...] = qc.astype(BF16)
    lane = lax.broadcasted_iota(jnp.int32, k_c.shape, 1)
    k2 = k_c * k_c
    s0 = jnp.sum(jnp.where(lane < HD_C, k2, 0.0), axis=-1, keepdims=True) * (1.0 / HD_C)
    s1 = jnp.sum(jnp.where(lane >= HD_C, k2, 0.0), axis=-1, keepdims=True) * (1.0 / HD_C)
    kcn = k_c * jnp.where(lane < HD_C, lax.rsqrt(s0 + EPS), lax.rsqrt(s1 + EPS)) * g_ref[5:6, :LANE]
    kcf_ref[...] = kcn
    vcf_ref[...] = v_c
    kc_ref[...] = (_rope(kcn, rc_ref) if use_rope else kcn).astype(BF16)
    vc_ref[...] = _with_ones(v_c.astype(BF16))


def _chunk_rows_spec(y, tm):
    per = (y.shape[1] - SUB) // tm
    return pl.BlockSpec((None, tm, y.shape[2]), lambda i: (i // per, i % per, 0))


def _premix_call(l, x, yprev, mod, seq, use_rope, wts, rope_tabs):
    t, d = x.shape
    tm = TOKEN_TILE
    ncond = mod.shape[1]
    has_prev = yprev is not None

    def cond(i):
        return (i * tm) // seq if ncond > 1 else 0

    row = lambda w: pl.BlockSpec((tm, w), lambda i: (i, 0))
    ins, specs = [x], [row(d)]
    if has_prev:
        ins += [yprev, mod]
        specs += [_chunk_rows_spec(yprev, tm), pl.BlockSpec((None, None, 6, d), lambda i: (l - 1, cond(i), 0, 0))]
    ins += [mod, wts["w_in"], wts["w_uq"], wts["w_kv"], wts["gains"]]
    specs += [pl.BlockSpec((None, None, 6, d), lambda i: (l, cond(i), 0, 0)),
              pl.BlockSpec((None, d, IN_P), lambda i: (l, 0, 0)),
              pl.BlockSpec((None, Q_RANK, HQ), lambda i: (l, 0, 0)),
              pl.BlockSpec((None, KV_RANK, HQ + W_A), lambda i: (l, 0, 0)),
              pl.BlockSpec((None, 8, HQ), lambda i: (l, 0, 0))]
    if use_rope:
        nt = seq // tm
        tab = pl.BlockSpec((2, tm, LANE), lambda i: (0, i % nt, 0))
        perm = pl.BlockSpec((2 * LANE, 2 * LANE), lambda i: (0, 0))
        ins += [rope_tabs[0][0], rope_tabs[0][1], rope_tabs[1][0], rope_tabs[1][1]]
        specs += [tab, perm, tab, perm]
    outs = [(d, F32), (HQ, BF16), (HQ, BF16), (2 * W_A, BF16), (KV_RANK, F32), (LANE, F32), (HQ, BF16),
            (LANE, BF16), (2 * LANE, BF16), (LANE, F32), (LANE, F32), (3 * HY_CH, F32)]
    return pl.pallas_call(
        functools.partial(_premix_kernel, has_prev, use_rope),
        out_shape=[jax.ShapeDtypeStruct((t, w), dt) for w, dt in outs],
        grid=(t // tm,),
        in_specs=specs,
        out_specs=[row(w) for w, _ in outs],
        compiler_params=_cp(("parallel",), VMEM_LIMIT),
        name="premix",
    )(*ins)


def _cachekv_kernel(ckv_ref, kpe_ref, w_kv_ref, g_ref, ka_ref, va_ref):
    ka, va = _kv_heads(ckv_ref[...].astype(BF16), kpe_ref[...], w_kv_ref, g_ref[3:4, :], None, False)
    ka_ref[...] = ka.astype(BF16)
    va_ref[...] = _with_ones(va.astype(BF16))


def _cachekv_call(cache_ckv, cache_kpe_p, wts):
    nb, depth, past, _ = cache_ckv.shape
    return pl.pallas_call(
        _cachekv_kernel,
        out_shape=[jax.ShapeDtypeStruct((depth, nb, past, HQ), BF16),
                   jax.ShapeDtypeStruct((depth, nb, past, 2 * W_A), BF16)],
        grid=(depth, nb),
        in_specs=[pl.BlockSpec((None, None, past, KV_RANK), lambda l, b: (b, l, 0, 0)),
                  pl.BlockSpec((None, None, past, LANE), lambda l, b: (b, l, 0, 0)),
                  pl.BlockSpec((None, KV_RANK, HQ + W_A), lambda l, b: (l, 0, 0)),
                  pl.BlockSpec((None, 8, HQ), lambda l, b: (l, 0, 0))],
        out_specs=[pl.BlockSpec((None, None, past, HQ), lambda l, b: (l, b, 0, 0)),
                   pl.BlockSpec((None, None, past, 2 * W_A), lambda l, b: (l, b, 0, 0))],
        compiler_params=_cp(("parallel", "parallel"), VMEM_LIMIT),
        name="cache_kv",
    )(cache_ckv, cache_kpe_p, wts["w_kv"], wts["gains"])


def _nt(q, k):
    return lax.dot_general(q, k, (((1,), (1,)), ((), ())), preferred_element_type=F32)


def _with_ones(v):
    ones = jnp.ones((v.shape[0], LANE), v.dtype)
    parts = []
    for j in range(v.shape[1] // LANE):
        parts += [v[:, j * LANE:(j + 1) * LANE], ones]
    return jnp.concatenate(parts, axis=1)


def _attend(q, ks, vs):
    return _softmax_pv([_nt(q, k) for k in ks], vs)


def _softmax_pv(ss, vs, m=None):
    if m is None:
        m = ss[0].max(axis=-1, keepdims=True)
        for s in ss[1:]:
            m = jnp.maximum(m, s.max(axis=-1, keepdims=True))
    acc = None
    for s, v in zip(ss, vs):
        pv = _dot(jnp.exp2(s - m).astype(BF16), v)
        acc = pv if acc is None else acc + pv
    return acc[:, :LANE] / acc[:, LANE:]


BOUND_SLACK = 1.01
BOUND_LIMIT = 60.0


def _attn_kernel(nseg, bound, *refs):
    qa_ref, qc_ref = refs[0], refs[1]
    if bound:
        kq_ref, refs = refs[2], refs[:2] + refs[3:]
    segs = refs[2:2 + 4 * nseg]
    oa_ref, oc_ref = refs[2 + 4 * nseg:]
    ka_refs, va_refs, kc_refs, vc_refs = (segs[i::4] for i in range(4))
    lane = lax.broadcasted_iota(jnp.int32, (qa_ref.shape[0], LANE), 1)
    low = lane < V_A
    g_per = H_C // KV_C
    jobs = [("a", j, h) for j in range(H_A // 2) for h in (2 * j, 2 * j + 1)]
    jobs += [("c", g, h) for g in range(g_per) for h in (g, g + g_per)]

    def scores(job):
        kind, _, h = job
        hs = slice(h * LANE, (h + 1) * LANE)
        if kind == "a":
            return [_nt(qa_ref[:, hs], r[:, hs]) for r in ka_refs]
        return [_nt(qc_ref[:, hs], r[...]) for r in kc_refs]

    def values(job):
        kind, j, _ = job
        if kind == "a":
            return [r[:, 2 * j * LANE:2 * (j + 1) * LANE] for r in va_refs]
        return [r[...] for r in vc_refs]

    def row_bound(job):
        kind, _, h = job
        q = (qa_ref if kind == "a" else qc_ref)[:, h * LANE:(h + 1) * LANE].astype(F32)
        r = h if kind == "a" else H_A + h // g_per
        q2 = jnp.sum(q * q, axis=-1, keepdims=True)
        return jnp.sqrt(q2 * kq_ref[r:r + 1, 0:1]) * BOUND_SLACK

    ss_next = None if bound else scores(jobs[0])
    pv = []
    for n, job in enumerate(jobs):
        if bound:
            pv.append(_softmax_pv(scores(job), values(job), row_bound(job)))
        else:
            ss = ss_next
            if n + 1 < len(jobs):
                ss_next = scores(jobs[n + 1])
            pv.append(_softmax_pv(ss, values(job)))
        if len(pv) == 2:
            o_ref = oa_ref if job[0] == "a" else oc_ref
            o_ref[:, job[1] * LANE:(job[1] + 1) * LANE] = jnp.where(low, pv[0], pv[1])
            pv = []


def _attn_ctx_call(qa, ka, va, qc, kc, vc, seq):
    t = qa.shape[0]
    blk = lambda w: pl.BlockSpec((seq, w), lambda b: (b, 0))
    return pl.pallas_call(
        functools.partial(_attn_kernel, 1, False),
        out_shape=[jax.ShapeDtypeStruct((t, W_A), F32), jax.ShapeDtypeStruct((t, W_C), F32)],
        grid=(t // seq,),
        in_specs=[blk(HQ), blk(HQ), blk(HQ), blk(2 * W_A), blk(LANE), blk(2 * LANE)],
        out_specs=[blk(W_A), blk(W_C)],
        compiler_params=_cp(("parallel",), VMEM_LIMIT),
        name="attn_ctx",
    )(qa, qc, ka, va, kc, vc)


def _attn_lat_call(l, qa, ka, va, qc, kc, vc, kax, vax, kcx, vcx, seq):
    t = qa.shape[0]
    nb = t // seq
    tq = min(ATTN_Q_TILE, seq)
    nq = seq // tq
    past = kax.shape[2]
    qblk = lambda w: pl.BlockSpec((tq, w), lambda b, i: (b * nq + i, 0))
    sblk = lambda w: pl.BlockSpec((seq, w), lambda b, i: (b, 0))
    xblk = lambda w: pl.BlockSpec((None, None, past, w), lambda b, i: (l, b, 0, 0))
    cblk = lambda w: pl.BlockSpec((None, None, past, w), lambda b, i: (b, l, 0, 0))
    full = lambda w: pl.BlockSpec((seq, w), lambda b: (b, 0))

    kq = pl.pallas_call(
        _knorm_kernel,
        out_shape=jax.ShapeDtypeStruct((nb, NORM_ROWS, LANE), F32),
        grid=(nb,),
        in_specs=[full(HQ), pl.BlockSpec((None, None, past, HQ), lambda b: (l, b, 0, 0)),
                  full(LANE), pl.BlockSpec((None, None, past, LANE), lambda b: (b, l, 0, 0)), full(HQ), full(HQ)],
        out_specs=pl.BlockSpec((None, NORM_ROWS, LANE), lambda b: (b, 0, 0)),
        compiler_params=_cp(("parallel",), VMEM_LIMIT),
        name="attn_norms",
    )(ka, kax, kc, kcx, qa, qc)
    k2, q2a, q2c = kq[:, :H_A + KV_C, 0], kq[:, 8:8 + H_A, 0], kq[:, 8 + H_A:8 + H_A + H_C, 0]
    k2c = jnp.repeat(k2[:, H_A:], H_C // KV_C, axis=1)
    worst = jnp.sqrt(jnp.maximum(jnp.max(q2a * k2[:, :H_A]), jnp.max(q2c * k2c))) * BOUND_SLACK

    def call(bound):
        extra_specs = [pl.BlockSpec((None, NORM_ROWS, LANE), lambda b, i: (b, 0, 0))] if bound else []
        return pl.pallas_call(
            functools.partial(_attn_kernel, 2, bound),
            out_shape=[jax.ShapeDtypeStruct((t, W_A), F32), jax.ShapeDtypeStruct((t, W_C), F32)],
            grid=(nb, nq),
            in_specs=[qblk(HQ), qblk(HQ)] + extra_specs
            + [xblk(HQ), xblk(2 * W_A), cblk(LANE), cblk(2 * LANE),
               sblk(HQ), sblk(2 * W_A), sblk(LANE), sblk(2 * LANE)],
            out_specs=[qblk(W_A), qblk(W_C)],
            compiler_params=_cp(("parallel", "parallel"), VMEM_LIMIT),
            name="attn_lat_bound" if bound else "attn_lat",
        )(qa, qc, *([kq] if bound else []), kax, vax, kcx, vcx, ka, va, kc, vc)

    return lax.cond(worst < BOUND_LIMIT, lambda: call(True), lambda: call(False))


NORM_ROWS = 24


def _knorm_kernel(ka_ref, kax_ref, kc_ref, kcx_ref, qa_ref, qc_ref, o_ref):
    def max_n2(x, lanes=None):
        xf = x.astype(F32)
        x2 = xf * xf
        if lanes is not None:
            lane = lax.broadcasted_iota(jnp.int32, x2.shape, 1)
            x2 = jnp.where((lane >= lanes[0]) & (lane < lanes[1]), x2, 0.0)
        return jnp.max(jnp.sum(x2, axis=-1, keepdims=True), axis=0, keepdims=True)

    rows = []
    for h in range(H_A):
        hs = slice(h * LANE, (h + 1) * LANE)
        rows.append(jnp.maximum(max_n2(ka_ref[:, hs]), max_n2(kax_ref[:, hs])))
    for kv in range(KV_C):
        lanes = (kv * HD_C, (kv + 1) * HD_C)
        rows.append(jnp.maximum(max_n2(kc_ref[...], lanes), max_n2(kcx_ref[...], lanes)))
    rows += [max_n2(qa_ref[:, h * LANE:(h + 1) * LANE]) for h in range(H_A)]
    rows += [max_n2(qc_ref[:, h * LANE:(h + 1) * LANE]) for h in range(H_C)]
    o_ref[...] = jnp.zeros_like(o_ref)
    for r, v in enumerate(rows):
        o_ref[r:r + 1, :] = jnp.broadcast_to(v, (1, LANE))


def _filter_mlp(z, w_in_ref, b_in_ref, w_mid_ref, b_mid_ref, w_out_ref, b_out_ref, freq_ref):
    freq = freq_ref[...]
    a = jnp.sin(freq * (_dot3(z, w_in_ref[...]) + b_in_ref[...]))
    for i in range(HY_INNER):
        a = jnp.sin(freq * (_dot3(a, w_mid_ref[i]) + b_mid_ref[i]))
    return _dot3(a, w_out_ref[...]) + b_out_ref[...]


def _conv3(u, up, dn, cw_ref, cb_ref):
    return up * cw_ref[0:1, :] + u * cw_ref[1:2, :] + dn * cw_ref[2:3, :] + cb_ref[...]


def _filt_ctx_kernel(z_ref, dec_ref, w_in_ref, b_in_ref, w_mid_ref, b_mid_ref, w_out_ref, b_out_ref, freq_ref,
                     c_ref, s_ref, o_ref):
    n = z_ref.shape[0]
    h = _filter_mlp(z_ref[...], w_in_ref, b_in_ref, w_mid_ref, b_mid_ref, w_out_ref, b_out_ref, freq_ref)
    dec = dec_ref[...]
    row = lax.broadcasted_iota(jnp.int32, dec.shape, 0)
    half = HY_ORDER * HY_CH
    for o in range(HY_ORDER):
        hf = h[:, o * HY_CH:(o + 1) * HY_CH] * dec
        hb = jnp.where(row > 0, h[:, half + o * HY_CH:half + (o + 1) * HY_CH] * dec, 0.0)
        nrm = jnp.sum(jnp.abs(hf) + jnp.abs(hb), axis=0, keepdims=True) + EPS
        scale = (1.0 / n) / nrm
        o_ref[o, 0] = _dot3(c_ref[...], hf + hb) * scale
        o_ref[o, 1] = -_dot3(s_ref[...], hf - hb) * scale


def _filt_ctx_call(l, z, dec, hw, cmat, smat):
    n = z.shape[0]
    full = lambda a: pl.BlockSpec((None,) + a.shape[1:], lambda i: (l,) + (0,) * (a.ndim - 1))
    const = lambda a: pl.BlockSpec(a.shape, lambda i: (0,) * a.ndim)
    names = ["hy_w_in", "hy_b_in", "hy_w_mid", "hy_b_mid", "hy_w_out", "hy_b_out", "hy_freq"]
    return pl.pallas_call(
        _filt_ctx_kernel,
        out_shape=jax.ShapeDtypeStruct((HY_ORDER, 2, n, HY_CH), F32),
        grid=(1,),
        in_specs=[const(z), const(dec)] + [full(hw[k]) for k in names] + [const(cmat), const(smat)],
        out_specs=pl.BlockSpec((HY_ORDER, 2, n, HY_CH), lambda i: (0, 0, 0, 0)),
        compiler_params=_cp(("arbitrary",), VMEM_LIMIT),
        name="hy_filter_ctx",
    )(z, dec, *[hw[k] for k in names], cmat, smat)


def _hy_ctx_kernel(u_ref, cw_ref, cb_ref, skip_ref, kf_ref, fwd_ref, inv_ref, o_ref):
    u = u_ref[...]
    n = u.shape[0]
    row = lax.broadcasted_iota(jnp.int32, u.shape, 0)
    up = jnp.where(row > 0, pltpu.roll(u, 1, 0), 0.0)
    dn = jnp.where(row < n - 1, pltpu.roll(u, n - 1, 0), 0.0)
    z = _conv3(u, up, dn, cw_ref, cb_ref)
    s = z[:, :HY_CH]
    gates = (z[:, HY_CH:2 * HY_CH], z[:, 2 * HY_CH:])
    for o in range(HY_ORDER):
        xs = _dot(fwd_ref[...], s.astype(BF16))
        xr, xi = xs[:n], xs[n:]
        kr, ki = kf_ref[o, 0], kf_ref[o, 1]
        ycat = jnp.concatenate([xr * kr - xi * ki, xr * ki + xi * kr], axis=0)
        y = _dot(inv_ref[...], ycat.astype(BF16))
        s = gates[o] * (y + s * skip_ref[o:o + 1, :])
    o_ref[...] = s


def _hy_ctx_call(l, u, hw, kf, fwd, inv, seq):
    t = u.shape[0]
    full = lambda a: pl.BlockSpec((None,) + a.shape[1:], lambda b: (l,) + (0,) * (a.ndim - 1))
    const = lambda a: pl.BlockSpec(a.shape, lambda b: (0,) * a.ndim)
    return pl.pallas_call(
        _hy_ctx_kernel,
        out_shape=jax.ShapeDtypeStruct((t, HY_CH), F32),
        grid=(t // seq,),
        in_specs=[pl.BlockSpec((seq, 3 * HY_CH), lambda b: (b, 0)),
                  full(hw["hy_conv_w"]), full(hw["hy_conv_b"]), full(hw["hy_skip"]),
                  const(kf), const(fwd), const(inv)],
        out_specs=pl.BlockSpec((seq, HY_CH), lambda b: (b, 0)),
        compiler_params=_cp(("parallel",), VMEM_LIMIT),
        name="hyena_ctx",
    )(u, hw["hy_conv_w"], hw["hy_conv_b"], hw["hy_skip"], kf, fwd, inv)


def _filt_lat_kernel(seq, z_ref, dec_ref, w_in_ref, b_in_ref, w_mid_ref, b_mid_ref, w_out_ref, b_out_ref, freq_ref,
                     k_ref, n_ref):
    i = pl.program_id(0)
    tr = z_ref.shape[0]
    h = _filter_mlp(z_ref[...], w_in_ref, b_in_ref, w_mid_ref, b_mid_ref, w_out_ref, b_out_ref, freq_ref)
    half = HY_ORDER * HY_CH
    row = i * tr + lax.broadcasted_iota(jnp.int32, (tr, half), 0)
    dec = dec_ref[...]
    kern = jnp.where(row < seq, h[:, :half], h[:, half:]) * jnp.concatenate([dec] * HY_ORDER, axis=1)
    k_ref[...] = kern

    @pl.when(i == 0)
    def _():
        n_ref[...] = jnp.zeros_like(n_ref)

    n_ref[...] += jnp.sum(jnp.abs(kern), axis=0, keepdims=True)


def _filt_lat_call(l, z, dec, hw, seq):
    n = z.shape[0]
    tr = min(512, n)
    half = HY_ORDER * HY_CH
    full = lambda a: pl.BlockSpec((None,) + a.shape[1:], lambda i: (l,) + (0,) * (a.ndim - 1))
    names = ["hy_w_in", "hy_b_in", "hy_w_mid", "hy_b_mid", "hy_w_out", "hy_b_out", "hy_freq"]
    return pl.pallas_call(
        functools.partial(_filt_lat_kernel, seq),
        out_shape=[jax.ShapeDtypeStruct((n, half), F32), jax.ShapeDtypeStruct((1, half), F32)],
        grid=(n // tr,),
        in_specs=[pl.BlockSpec((tr, z.shape[1]), lambda i: (i, 0)), pl.BlockSpec((tr, HY_CH), lambda i: (i, 0))]
        + [full(hw[k]) for k in names],
        out_specs=[pl.BlockSpec((tr, half), lambda i: (i, 0)), pl.BlockSpec((1, half), lambda i: (0, 0))],
        compiler_params=_cp(("arbitrary",), VMEM_LIMIT),
        name="hy_filter_lat",
    )(z, dec, *[hw[k] for k in names])


def _fa_kernel(f1_ref, k_ref, o_ref):
    r = _dot3(f1_ref[...], k_ref[...])
    o_ref[0] = r[:FFT_N1]
    o_ref[1] = r[FFT_N1:]


def _fa_call(f1f, kern2d):
    n1, w = kern2d.shape
    tn = min(2048, w)
    return pl.pallas_call(
        _fa_kernel,
        out_shape=jax.ShapeDtypeStruct((2, FFT_N1, w), F32),
        grid=(w // tn,),
        in_specs=[pl.BlockSpec(f1f.shape, lambda j: (0, 0)), pl.BlockSpec((n1, tn), lambda j: (0, j))],
        out_specs=pl.BlockSpec((2, FFT_N1, tn), lambda j: (0, 0, j)),
        compiler_params=_cp(("parallel",), VMEM_LIMIT),
        name="hy_filter_dft1",
    )(f1f, kern2d)


def _fb_kernel(n_total, a_ref, fh_ref, fl_ref, n_ref, o_ref):
    a = jnp.concatenate([a_ref[0], a_ref[1]], axis=1)
    ah, al = _split(a)
    x = _bdot(fh_ref[...], ah) + _bdot(fh_ref[...], al) + _bdot(fl_ref[...], ah)
    scale = (1.0 / n_total) / (n_ref[...] + EPS)
    o_ref[...] = x * scale[None]


def _fb_call(af5, f2h, f2l, nrm, n_total):
    _, n1, n2, c = af5.shape[0], af5.shape[1], af5.shape[2], af5.shape[3]
    k1t = 8
    return pl.pallas_call(
        functools.partial(_fb_kernel, n_total),
        out_shape=jax.ShapeDtypeStruct((n1, 2 * n2, c), F32),
        grid=(n1 // k1t,),
        in_specs=[pl.BlockSpec((2, k1t, n2, c), lambda j: (0, j, 0, 0)),
                  pl.BlockSpec((k1t, 2 * n2, 2 * n2), lambda j: (j, 0, 0)),
                  pl.BlockSpec((k1t, 2 * n2, 2 * n2), lambda j: (j, 0, 0)),
                  pl.BlockSpec((1, c), lambda j: (0, 0))],
        out_specs=pl.BlockSpec((k1t, 2 * n2, c), lambda j: (j, 0, 0)),
        compiler_params=_cp(("parallel",), VMEM_LIMIT),
        name="hy_filter_dft2",
    )(af5, f2h, f2l, nrm)


def _hconv_kernel(seq, n2, u_ref, p_ref, n_ref, cw_ref, cb_ref, v_ref, x1_ref, x2_ref, zs):
    i = pl.program_id(0)
    u = u_ref[...]
    tt = u.shape[0]
    row = lax.broadcasted_iota(jnp.int32, u.shape, 0)
    pos = (i * tt) % seq
    prev = jnp.where(pos > 0, p_ref[7:8, :], 0.0)
    nxt = jnp.where(pos + tt < seq, n_ref[0:1, :], 0.0)
    up = jnp.where(row > 0, pltpu.roll(u, 1, 0), prev)
    dn = jnp.where(row < tt - 1, pltpu.roll(u, tt - 1, 0), nxt)
    z = _conv3(u, up, dn, cw_ref, cb_ref)
    nlb = z.shape[1] // LANE
    for k in range(nlb):
        zs[k] = z[:, k * LANE:(k + 1) * LANE]
    per = HY_CH // LANE
    for p, o_ref in enumerate((v_ref, x1_ref, x2_ref)):
        for t2 in range(n2):
            for k in range(per):
                o_ref[:, t2 * HY_CH + k * LANE:t2 * HY_CH + (k + 1) * LANE] = (
                    zs[p * per + k, pl.ds(t2, tt // n2, stride=n2), :])


def _hconv_call(l, u, hw, seq, n2):
    t, w = u.shape
    tt = min(512, seq)
    nblk8 = t // 8
    full = lambda a: pl.BlockSpec((None,) + a.shape[1:], lambda i: (l,) + (0,) * (a.ndim - 1))
    ob = pl.BlockSpec((tt // n2, n2 * HY_CH), lambda i: (i, 0))
    return pl.pallas_call(
        functools.partial(_hconv_kernel, seq, n2),
        out_shape=[jax.ShapeDtypeStruct((t // n2, n2 * HY_CH), F32)] * 3,
        scratch_shapes=[pltpu.VMEM((w // LANE, tt, LANE), F32)],
        grid=(t // tt,),
        in_specs=[pl.BlockSpec((tt, w), lambda i: (i, 0)),
                  pl.BlockSpec((8, w), lambda i: (jnp.maximum(i * (tt // 8) - 1, 0), 0)),
                  pl.BlockSpec((8, w), lambda i: (jnp.minimum((i + 1) * (tt // 8), nblk8 - 1), 0)),
                  full(hw["hy_conv_w"]), full(hw["hy_conv_b"])],
        out_specs=[ob, ob, ob],
        compiler_params=_cp(("parallel",), VMEM_LIMIT),
        name="hyena_conv3",
    )(u, u, u, hw["hy_conv_w"], hw["hy_conv_b"])


def _ha_kernel(f1_ref, x_ref, o_ref):
    r = _dot(f1_ref[...], x_ref[...].astype(BF16))
    o_ref[0] = r[:FFT_N1].astype(BF16)
    o_ref[1] = r[FFT_N1:].astype(BF16)


def _ha_call(f1d, x2d, nb):
    rows, w = x2d.shape
    n1h = rows // nb
    tn = min(2048, w)
    return pl.pallas_call(
        _ha_kernel,
        out_shape=jax.ShapeDtypeStruct((nb, 2, FFT_N1, w), BF16),
        grid=(nb, w // tn),
        in_specs=[pl.BlockSpec(f1d.shape, lambda b, j: (0, 0)), pl.BlockSpec((n1h, tn), lambda b, j: (b, j))],
        out_specs=pl.BlockSpec((None, 2, FFT_N1, tn), lambda b, j: (b, 0, 0, j)),
        compiler_params=_cp(("parallel", "parallel"), VMEM_LIMIT),
        name="hyena_dft1",
    )(f1d, x2d)


def _hb_kernel(a_ref, f_ref, g_ref, kf_ref, o_ref):
    n2 = a_ref.shape[2]
    a = jnp.concatenate([a_ref[0], a_ref[1]], axis=1)
    x = _bdot(f_ref[...], a)
    xr, xi = x[:, :n2], x[:, n2:]
    kr, ki = kf_ref[:, :n2], kf_ref[:, n2:]
    y = jnp.concatenate([xr * kr - xi * ki, xr * ki + xi * kr], axis=1).astype(BF16)
    b = _bdot(g_ref[...], y)
    o_ref[0] = b[:, :n2].astype(BF16)
    o_ref[1] = b[:, n2:].astype(BF16)


def _hb_call(o, a5, f2, g2, kf):
    nb, _, n1, n2, c = a5.shape
    k1t = 16
    blk = pl.BlockSpec((None, 2, k1t, n2, c), lambda b, j: (b, 0, j, 0, 0))
    mat = pl.BlockSpec((k1t, 2 * n2, 2 * n2), lambda b, j: (j, 0, 0))
    return pl.pallas_call(
        _hb_kernel,
        out_shape=jax.ShapeDtypeStruct(a5.shape, BF16),
        grid=(nb, n1 // k1t),
        in_specs=[blk, mat, mat, pl.BlockSpec((k1t, 2 * n2, c), lambda b, j: (j, 0, o))],
        out_specs=blk,
        compiler_params=_cp(("parallel", "parallel"), VMEM_LIMIT),
        name="hyena_dft2",
    )(a5, f2, g2, kf)


def _hc_kernel(o, fc_ref, b_ref, s_ref, g_ref, skip_ref, o_ref):
    bcat = jnp.concatenate([b_ref[0], b_ref[1]], axis=0)
    y = _dot(fc_ref[...], bcat)
    o_ref[...] = g_ref[...] * (y + s_ref[...] * skip_ref[o:o + 1, :])


def _hc_call(l, o, fc, b4, s2d, g2d, skip_t):
    nb, _, n1, w = b4.shape
    rows = s2d.shape[0] // nb
    tn = min(2048, w)
    blk = pl.BlockSpec((rows, tn), lambda b, j: (b, j))
    return pl.pallas_call(
        functools.partial(_hc_kernel, o),
        out_shape=jax.ShapeDtypeStruct(s2d.shape, F32),
        grid=(nb, w // tn),
        in_specs=[pl.BlockSpec(fc.shape, lambda b, j: (0, 0)),
                  pl.BlockSpec((None, 2, n1, tn), lambda b, j: (b, 0, 0, j)),
                  blk, blk,
                  pl.BlockSpec((None, HY_ORDER, tn), lambda b, j: (l, 0, j))],
        out_specs=blk,
        compiler_params=_cp(("parallel", "parallel"), VMEM_LIMIT),
        name="hyena_dft3",
    )(fc, b4, s2d, g2d, skip_t)


def _postmix_kernel(x_ref, oa_ref, ob_ref, oc_ref, mod_ref, g_ref, wa_ref, wb_ref, wc_ref, wrh_ref, wrl_ref, br_ref,
                    x1_ref, h2_ref, idx_ref, gate_ref):
    na = _rms(oa_ref[...]) * g_ref[:, :W_A]
    nb = _rms(ob_ref[...]) * g_ref[:, W_A:W_A + W_B]
    nc = _rms(oc_ref[...]) * g_ref[:, W_A + W_B:]
    mix = (_dot(na.astype(BF16), wa_ref[...]) + _dot(nb.astype(BF16), wb_ref[...])
           + _dot(nc.astype(BF16), wc_ref[...]))
    x1 = x_ref[...] + mod_ref[2:3, :] * mix
    x1_ref[...] = x1
    h2 = _rms(x1) * (1.0 + mod_ref[4:5, :]) + mod_ref[3:4, :]
    h2_ref[...] = h2
    hh, hl = _split(h2)
    vals = _dot(hh, wrh_ref[...]) + _dot(hh, wrl_ref[...]) + _dot(hl, wrh_ref[...]) + br_ref[...]
    lane = lax.broadcasted_iota(jnp.int32, vals.shape, 1).astype(F32)
    idx_out = jnp.zeros(vals.shape, F32)
    top = jnp.zeros(vals.shape, F32)
    m0 = None
    for k in range(TOP_K):
        m = vals.max(axis=-1, keepdims=True)
        sel = jnp.min(jnp.where(vals == m, lane, float(LANE)), axis=-1, keepdims=True)
        if m0 is None:
            m0 = m
        idx_out = jnp.where(lane == k, sel, idx_out)
        top = jnp.where(lane == k, jnp.exp(m - m0), top)
        vals = jnp.where(lane == sel, -jnp.inf, vals)
    idx_ref[...] = idx_out.astype(jnp.int32)
    gate_ref[...] = top / jnp.sum(top, axis=-1, keepdims=True)


def _postmix_call(l, x, oa, ob, oc, mod, seq, wts):
    t, d = x.shape
    tm = TOKEN_TILE
    ncond = mod.shape[1]

    def cond(i):
        return (i * tm) // seq if ncond > 1 else 0

    row = lambda w: pl.BlockSpec((tm, w), lambda i: (i, 0))
    lay = lambda a: pl.BlockSpec((None,) + a.shape[1:], lambda i: (l,) + (0,) * (a.ndim - 1))
    names = ["g_out", "wo_a", "wo_b", "wo_c", "wr_hi", "wr_lo", "b_router"]
    return pl.pallas_call(
        _postmix_kernel,
        out_shape=[jax.ShapeDtypeStruct((t, d), F32), jax.ShapeDtypeStruct((t, d), F32),
                   jax.ShapeDtypeStruct((t, LANE), jnp.int32), jax.ShapeDtypeStruct((t, LANE), F32)],
        grid=(t // tm,),
        in_specs=[row(d), row(W_A), row(W_B), row(W_C),
                  pl.BlockSpec((None, None, 6, d), lambda i: (l, cond(i), 0, 0))] + [lay(wts[k]) for k in names],
        out_specs=[row(d), row(d), row(LANE), row(LANE)],
        compiler_params=_cp(("parallel",), VMEM_LIMIT),
        name="postmix",
    )(x, oa, ob, oc, mod, *[wts[k] for k in names])


SUB = 8


NGRP = ROW_BLOCK // SUB


def _moe_kernel(l, be_ref, rs_ref, nv_ref, nu_ref, run_ref, nxt_ref, tok_ref, gate_ref,
                h_ref, wi_hbm, bi_ref, wo_hbm, bo_ref, y_ref, buf, obuf, xb, wi_buf, wo_buf, sem):
    s = pl.program_id(0)
    tc = h_ref.shape[0]

    def weight_copies(e, slot):
        return (pltpu.make_async_copy(wi_hbm.at[l, e], wi_buf.at[slot], sem.at[slot, 0]),
                pltpu.make_async_copy(wo_hbm.at[l, e], wo_buf.at[slot], sem.at[slot, 1]))

    @pl.when(s == 0)
    def _():
        y_ref[...] = jnp.zeros_like(y_ref)
        buf[...] = jnp.zeros_like(buf)
        obuf[...] = jnp.zeros_like(obuf)
        for cp in weight_copies(be_ref[0], 0):
            cp.start()

    run = run_ref[s]
    wslot = lax.rem(run, 2)
    first = jnp.logical_or(s == 0, run != run_ref[jnp.maximum(s - 1, 0)])

    @pl.when(jnp.logical_and(first, s < nu_ref[0] + 2))
    def _():
        for cp in weight_copies(be_ref[s], wslot):
            cp.wait()

        @pl.when(nxt_ref[s] >= 0)
        def _():
            for cp in weight_copies(nxt_ref[s], 1 - wslot):
                cp.start()

    @pl.when(s < nu_ref[0] + 2)
    def _():
        slot_g = lax.rem(s, 2)
        slot_c = 1 - slot_g

        xb[...] = buf[slot_c].reshape(ROW_BLOCK, buf.shape[3]).astype(BF16)

        base_s = rs_ref[s]
        nv = nv_ref[s]
        for g in range(NGRP):
            ts = [jnp.where(g * SUB + j < nv, tok_ref[base_s + g * SUB + j], tc) for j in range(SUB)]
            new = [y_ref[pl.ds(ts[j], 1), :] + gate_ref[base_s + g * SUB + j] * obuf[slot_g, g, j:j + 1, :]
                   for j in range(SUB)]
            for j in range(SUB):
                y_ref[pl.ds(ts[j], 1), :] = new[j]

        base_g = rs_ref[s + 2]
        for g in range(NGRP):
            rows = [h_ref[pl.ds(tok_ref[base_g + g * SUB + j], 1), :] for j in range(SUB)]
            for j in range(SUB):
                buf[slot_g, g, j:j + 1, :] = rows[j]

        gu = _dot(xb[...], wi_buf[wslot]) + bi_ref[...]
        gt = jnp.minimum(gu[:, :D_FF], SWIGLU_LIMIT)
        lin = jnp.clip(gu[:, D_FF:], -SWIGLU_LIMIT, SWIGLU_LIMIT)
        act = (lin + 1.0) * gt * (1.0 / (1.0 + jnp.exp(-SWIGLU_ALPHA * gt)))
        out = _dot(act.astype(BF16), wo_buf[wslot]) + bo_ref[...]
        obuf[slot_c] = out.reshape(obuf.shape[1:])


def _moe_kernel_entry(has_alias, l, *refs):
    refs = list(refs)
    if has_alias:
        del refs[13]
    _moe_kernel(l, *refs)


def _route(idx, gates, n_blocks):
    m = idx.shape[0] * TOP_K
    e = idx.reshape(m)
    flat = jnp.arange(m, dtype=jnp.int32)
    skey, gate = lax.sort((e * m + flat, gates.reshape(m)), num_keys=1)
    tok = (skey % m) // TOP_K
    experts = jnp.arange(N_EXPERTS, dtype=jnp.int32)
    cnt = jnp.sum((e[:, None] == experts[None, :]).astype(jnp.int32), axis=0)
    nblk = (cnt + ROW_BLOCK - 1) // ROW_BLOCK
    bend = jnp.cumsum(nblk)
    n_used = bend[-1]
    blk = jnp.arange(-2, n_blocks + 2, dtype=jnp.int32)
    bcl = jnp.clip(blk, 0, n_used - 1)
    be = jnp.sum((bend[None, :] <= bcl[:, None]).astype(jnp.int32), axis=1)
    oh = (be[:, None] == experts[None, :]).astype(jnp.int32)
    pick = lambda v: jnp.sum(oh * v[None, :], axis=1)
    off = (bcl - pick(bend - nblk)) * ROW_BLOCK
    valid = (blk >= 0) & (blk < n_used)
    rs = jnp.where(valid, pick(jnp.cumsum(cnt) - cnt) + off, 0)
    nv = jnp.where(valid, jnp.clip(pick(cnt) - off, 0, ROW_BLOCK), 0)
    pad = jnp.zeros((ROW_BLOCK,), jnp.int32)
    be_step = be[1:n_blocks + 3]
    change = jnp.concatenate([jnp.zeros((1,), jnp.int32), (be_step[1:] != be_step[:-1]).astype(jnp.int32)])
    later = (experts[None, :] > be_step[:, None]) & (nblk[None, :] > 0)
    nxt = jnp.min(jnp.where(later, experts[None, :], N_EXPERTS), axis=1)
    nxt = jnp.where(nxt == N_EXPERTS, -1, nxt)
    return (be_step, rs, nv, n_used.reshape(1), jnp.cumsum(change), nxt, jnp.concatenate([tok, pad]),
            jnp.concatenate([gate, pad.astype(F32)]))


def _moe_call(l, h2, idx, gates, wts):
    t, d = h2.shape
    tc = min(4096, t)
    n_blocks = tc * TOP_K // ROW_BLOCK + N_EXPERTS
    y = None
    for c in range(t // tc):
        route = _route(idx[c * tc:(c + 1) * tc], gates[c * tc:(c + 1) * tc], n_blocks)
        wspec = lambda r, w: pl.BlockSpec((None, None, r, w), lambda s, be, *_: (l, be[s], 0, 0))
        hbm = pl.BlockSpec(memory_space=pl.ANY)
        in_specs = [pl.BlockSpec((tc, d), lambda i, *_: (c, 0), pipeline_mode=pl.Buffered(1)),
                    hbm, wspec(1, 2 * D_FF), hbm, wspec(1, d)]
        args = [h2, wts["w_moe_in"], wts["b_moe_in"], wts["w_moe_out"], wts["b_moe_out"]]
        aliases = {}
        if y is not None:
            in_specs.append(pl.BlockSpec(memory_space=pl.ANY))
            args.append(y)
            aliases = {len(route) + len(args) - 1: 0}
        grid_spec = pltpu.PrefetchScalarGridSpec(
            num_scalar_prefetch=len(route),
            grid=(n_blocks + 2,),
            in_specs=in_specs,
            out_specs=pl.BlockSpec((None, tc + SUB, d), lambda i, *_: (c, 0, 0), pipeline_mode=pl.Buffered(1)),
            scratch_shapes=[pltpu.VMEM((2, NGRP, SUB, d), F32), pltpu.VMEM((2, NGRP, SUB, d), F32),
                            pltpu.VMEM((ROW_BLOCK, d), BF16),
                            pltpu.VMEM((2, d, 2 * D_FF), BF16), pltpu.VMEM((2, D_FF, d), BF16),
                            pltpu.SemaphoreType.DMA((2, 2))],
        )
        y = pl.pallas_call(
            functools.partial(_moe_kernel_entry, y is not None, l),
            out_shape=jax.ShapeDtypeStruct((t // tc, tc + SUB, d), F32),
            grid_spec=grid_spec,
            input_output_aliases=aliases,
            compiler_params=_cp(("arbitrary",), VMEM_LIMIT),
            name="moe_experts",
        )(*route, *args)
    return y


def _final_kernel(x_ref, y_ref, mod_ref, o_ref):
    o_ref[...] = x_ref[...] + mod_ref[5:6, :] * y_ref[...]


def _final_call(l, x, y, mod, seq):
    t, d = x.shape
    tm = TOKEN_TILE
    ncond = mod.shape[1]
    row = pl.BlockSpec((tm, d), lambda i: (i, 0))
    return pl.pallas_call(
        _final_kernel,
        out_shape=jax.ShapeDtypeStruct((t, d), F32),
        grid=(t // tm,),
        in_specs=[row, _chunk_rows_spec(y, tm),
                  pl.BlockSpec((None, None, 6, d), lambda i: (l, (i * tm) // seq if ncond > 1 else 0, 0, 0))],
        out_specs=row,
        compiler_params=_cp(("parallel",), VMEM_LIMIT),
        name="final_residual",
    )(x, y, mod)


def _rope_tables(seq, rot_dim, lane_map):
    shift = rot_dim // 2
    n_rows = seq // GRID_W
    rows = jnp.repeat(jnp.arange(n_rows, dtype=F32), GRID_W)
    cols = jnp.tile(jnp.arange(GRID_W, dtype=F32), n_rows)
    axis_dim = rot_dim // 2
    inv_freq = ROPE_THETA ** (-jnp.arange(0, axis_dim, 2, dtype=F32) / axis_dim)
    ang = jnp.concatenate([rows[:, None] * inv_freq, cols[:, None] * inv_freq], axis=-1)
    cos, sin = jnp.cos(ang), jnp.sin(ang)
    pair = np.zeros((LANE,), np.int32)
    in_rot = np.zeros((LANE,), np.float32)
    first = np.zeros((LANE,), np.float32)
    for ln in range(LANE):
        m = lane_map(ln)
        if m is not None:
            pair[ln], in_rot[ln], first[ln] = m[0], 1.0, 1.0 if m[1] == 0 else 0.0
    c = jnp.where(in_rot[None, :] > 0, cos[:, pair], 1.0)
    s = sin[:, pair] * in_rot[None, :] * (1.0 - 2.0 * first[None, :])
    perm = np.zeros((2 * LANE, 2 * LANE), np.float32)
    for ln in range(LANE):
        if in_rot[ln] > 0:
            src = ln + shift if first[ln] > 0 else ln - shift
            perm[src, ln] = perm[LANE + src, LANE + ln] = 1.0
    return jnp.stack([c, s]).astype(F32), jnp.asarray(perm, BF16)


def _lane_map_a(ln):
    o = ln - NOPE_A
    if 0 <= o < ROPE_A:
        return (o % (ROPE_A // 2), o // (ROPE_A // 2))
    return None


def _lane_map_c(ln):
    o = ln % HD_C
    return (o % (HD_C // 2), o // (HD_C // 2))


def _phase(num, den):
    ang = (2.0 * math.pi / den) * (num % den).astype(F32)
    return jnp.cos(ang), jnp.sin(ang)


def _ctx_dft(n):
    f = jnp.arange(n, dtype=jnp.int32)[:, None]
    t = jnp.arange(n, dtype=jnp.int32)[None, :]
    c, s = _phase((2 * f + 1) * t, 4 * n)
    fwd = jnp.concatenate([c, -s], axis=0).astype(BF16)
    inv = jnp.concatenate([c.T, -s.T], axis=1).astype(BF16)
    return c, s, fwd, inv


def _lat_dft(seq):
    n = 2 * seq
    n1, n2 = FFT_N1, n // FFT_N1
    k1 = jnp.arange(n1, dtype=jnp.int32)
    c1, s1 = _phase(k1[:, None] * k1[None, :], n1)
    f1f = jnp.concatenate([c1, -s1], axis=0)
    f1d = f1f[:, :n1 // 2].astype(BF16)
    fc = jnp.concatenate([c1[:, :n1 // 2].T, -s1[:, :n1 // 2].T], axis=1).astype(BF16)
    k2 = jnp.arange(n2, dtype=jnp.int32)
    num = (k2[None, :, None] * k2[None, None, :]) * n1 + k2[None, None, :] * k1[:, None, None]
    cm, sm = _phase(num, n)
    mr, mi = cm, -sm
    f2 = jnp.concatenate([jnp.concatenate([mr, -mi], axis=2), jnp.concatenate([mi, mr], axis=2)], axis=1)
    mrt, mit = jnp.swapaxes(mr, 1, 2), jnp.swapaxes(mi, 1, 2)
    g2 = jnp.concatenate([jnp.concatenate([mrt, mit], axis=2), jnp.concatenate([-mit, mrt], axis=2)], axis=1)
    f2h, f2l = _split(f2)
    return dict(f1f=f1f, f1d=f1d, fc=fc, f2=f2h, f2l=f2l, g2=g2.astype(BF16), n1=n1, n2=n2, n=n)


def _hy_features(seq):
    t = jnp.linspace(0.0, 1.0, seq, dtype=F32)[:, None]
    bands = (HY_EMB - 1) // 2
    f = jnp.linspace(1e-4, bands - 1, bands, dtype=F32)[None, :]
    w = 2.0 * math.pi * jnp.arange(seq, dtype=F32)[:, None] / seq
    z = jnp.concatenate([t, jnp.cos(f * w), jnp.sin(f * w)], axis=-1)
    z = jnp.pad(z, ((0, 0), (0, HY_FO - HY_EMB)))
    deltas = jnp.abs(jnp.linspace(HY_MIN_DECAY, HY_MAX_DECAY, HY_CH, dtype=F32))
    return z, jnp.exp(-t * deltas)


def _prep_weights(w_in, mla_g_qa, mla_w_uq, mla_g_kva, mla_w_ukv, mla_g_q, mla_g_k, gqa_g_q, gqa_g_k,
                  g_out, w_out, w_router, b_router, hy_w_in):
    depth = w_in.shape[0]
    cuts = np.cumsum([0, Q_RANK, KV_RANK, ROPE_A, 3 * HY_CH, H_C * HD_C, KV_C * HD_C, KV_C * HD_C])
    wb = w_in.astype(BF16)
    zeros = lambda w: jnp.zeros(wb.shape[:2] + (w,), BF16)
    parts = [wb[:, :, cuts[0]:cuts[2]], zeros(NOPE_A), wb[:, :, cuts[2]:cuts[3]], zeros(LANE - QK_A),
             wb[:, :, cuts[3]:cuts[4]]]
    for h in range(H_C):
        wh = wb[:, :, cuts[4] + h * HD_C:cuts[4] + (h + 1) * HD_C]
        parts += [wh, zeros(HD_C)] if h < H_C // KV_C else [zeros(HD_C), wh]
    w_in_p = jnp.concatenate(parts + [wb[:, :, cuts[5]:cuts[7]]], axis=-1)

    w_uq = jnp.pad(mla_w_uq.reshape(depth, Q_RANK, H_A, QK_A), ((0, 0), (0, 0), (0, 0), (0, LANE - QK_A)))
    w_uq = w_uq.reshape(depth, Q_RANK, HQ).astype(BF16)
    ukv = mla_w_ukv.reshape(depth, KV_RANK, H_A, NOPE_A + V_A)
    wk = jnp.pad(ukv[..., :NOPE_A], ((0, 0), (0, 0), (0, 0), (0, LANE - NOPE_A))).reshape(depth, KV_RANK, HQ)
    wv = ukv[..., NOPE_A:].reshape(depth, KV_RANK, W_A)
    w_kv = jnp.concatenate([wk, wv], axis=-1).astype(BF16)

    def pad_row(v):
        return jnp.pad(v, ((0, 0), (0, HQ - v.shape[1])))

    head_a = lambda g: jnp.tile(jnp.pad(g, ((0, 0), (0, LANE - QK_A))), (1, H_A))
    head_c = lambda g, reps: jnp.tile(g, (1, reps))
    gains = jnp.stack([
        pad_row(mla_g_qa), pad_row(mla_g_kva),
        head_a(mla_g_q) * (QK_A ** -0.5 * LOG2E), head_a(mla_g_k),
        head_c(gqa_g_q, 2 * H_C) * (HD_C ** -0.5 * LOG2E), pad_row(head_c(gqa_g_k, 2)),
        jnp.zeros((depth, HQ), F32), jnp.zeros((depth, HQ), F32)], axis=1).astype(F32)

    g_per = H_C // KV_C
    perm_c = np.concatenate([np.arange(h * HD_C, (h + 1) * HD_C) for g in range(g_per) for h in (g, g + g_per)])
    rows_c = W_A + W_B + perm_c
    g_o = jnp.concatenate([g_out[:, :W_A + W_B], g_out[:, rows_c]], axis=1).reshape(depth, 1, -1)
    wr = jnp.pad(w_router, ((0, 0), (0, 0), (0, LANE - N_EXPERTS)))
    wr_hi, wr_lo = _split(wr)
    br = jnp.pad(b_router, ((0, 0), (0, LANE - N_EXPERTS)), constant_values=-1e30).reshape(depth, 1, LANE)
    return dict(w_in=w_in_p, w_uq=w_uq, w_kv=w_kv, gains=gains, g_out=g_o,
                wo_a=w_out[:, :W_A].astype(BF16), wo_b=w_out[:, W_A:W_A + W_B].astype(BF16),
                wo_c=w_out[:, rows_c].astype(BF16), wr_hi=wr_hi, wr_lo=wr_lo, b_router=br,
                hy_w_in=jnp.pad(hy_w_in, ((0, 0), (0, HY_FO - HY_EMB), (0, 0))))


def _hyena_lat(l, u, hw, kf, dft, seq):
    t = u.shape[0]
    nb = t // seq
    n1, n2 = dft["n1"], dft["n2"]
    w2 = n2 * HY_CH
    s, x1, x2 = _hconv_call(l, u, hw, seq, n2)
    for o, gate in enumerate((x1, x2)):
        a = _ha_call(dft["f1d"], s, nb)
        b = _hb_call(o, a.reshape(nb, 2, n1, n2, HY_CH), dft["f2"], dft["g2"], kf)
        s = _hc_call(l, o, dft["fc"], b.reshape(nb, 2, n1, w2), s, gate, hw["skip_t"])
    return s.reshape(t, HY_CH)


def _filter_lat(l, z, dec, hw, dft, seq):
    kern, nrm = _filt_lat_call(l, z, dec, hw, seq)
    n1, n2, n = dft["n1"], dft["n2"], dft["n"]
    c = kern.shape[1]
    af = _fa_call(dft["f1f"], kern.reshape(n1, n2 * c))
    return _fb_call(af.reshape(2, n1, n2, c), dft["f2"], dft["f2l"], nrm, n)


def kernel(x_prompt, x_sample, c, c_ctx, cache_mla_ckv, cache_mla_kpe, cache_gqa_k, cache_gqa_v, w_mod, b_mod, w_in, mla_g_qa, mla_w_uq, mla_g_kva, mla_w_ukv, mla_g_q, mla_g_k, hy_conv_w, hy_conv_b, hy_w_in, hy_b_in, hy_w_mid, hy_b_mid, hy_w_out, hy_b_out, hy_freq, hy_skip, gqa_g_q, gqa_g_k, g_out, w_out, w_router, b_router, w_moe_in, b_moe_in, w_moe_out, b_moe_out):
    batch, seq_c, d = x_prompt.shape
    nb_l, seq_l, _ = x_sample.shape
    depth = w_in.shape[0]
    past = cache_mla_ckv.shape[2]
    assert d == D_MODEL and seq_l % GRID_W == 0 and (2 * seq_l) % FFT_N1 == 0

    wts = _prep_weights(w_in, mla_g_qa, mla_w_uq, mla_g_kva, mla_w_ukv, mla_g_q, mla_g_k, gqa_g_q, gqa_g_k,
                        g_out, w_out, w_router, b_router, hy_w_in)
    wts["w_moe_in"] = _cast_call(w_moe_in.reshape(depth * N_EXPERTS, d, 2 * D_FF), 1024).reshape(w_moe_in.shape)
    wts["w_moe_out"] = _cast_call(w_moe_out.reshape(depth * N_EXPERTS, D_FF, d), 1024).reshape(w_moe_out.shape)
    wts["b_moe_in"] = b_moe_in.reshape(depth, N_EXPERTS, 1, 2 * D_FF)
    wts["b_moe_out"] = b_moe_out.reshape(depth, N_EXPERTS, 1, d)
    hw = dict(hy_w_in=wts["hy_w_in"], hy_b_in=hy_b_in.reshape(depth, 1, HY_FO), hy_w_mid=hy_w_mid,
              hy_b_mid=hy_b_mid.reshape(depth, HY_INNER, 1, HY_FO), hy_w_out=hy_w_out,
              hy_b_out=hy_b_out.reshape(depth, 1, -1), hy_freq=hy_freq.reshape(depth, 1, HY_FO),
              hy_conv_w=hy_conv_w, hy_conv_b=hy_conv_b.reshape(depth, 1, -1), hy_skip=hy_skip)
    dft_l = _lat_dft(seq_l)
    hw["skip_t"] = jnp.tile(hy_skip, (1, 1, dft_l["n2"]))

    conds = jnp.zeros((8, d), F32).at[0].set(c_ctx).at[1:1 + nb_l].set(c)
    mod = _mod_call(conds, w_mod, b_mod).reshape(depth, 8, 6, d)
    mod_c, mod_l = mod[:, 0:1], mod[:, 1:1 + nb_l]

    rope_tabs = (_rope_tables(seq_l, ROPE_A, _lane_map_a), _rope_tables(seq_l, HD_C, _lane_map_c))
    kax, vax = _cachekv_call(cache_mla_ckv, jnp.pad(cache_mla_kpe, ((0, 0), (0, 0), (0, 0), (NOPE_A, LANE - QK_A))), wts)
    kcx = cache_gqa_k.reshape(nb_l, depth, past, KV_C * HD_C).astype(BF16)
    vcx = cache_gqa_v.reshape(nb_l, depth, past, KV_C * HD_C).astype(BF16)
    vcx = jnp.concatenate([vcx, jnp.ones_like(vcx)], axis=-1)

    z_c, dec_c = _hy_features(seq_c)
    cmat, smat, fwd_c, inv_c = _ctx_dft(seq_c)
    z_l, dec_l = _hy_features(seq_l)
    z_full = jnp.concatenate([z_l, jnp.zeros((1, HY_FO), F32), z_l[:0:-1]], axis=0)
    dec_full = jnp.concatenate([dec_l, jnp.zeros((1, HY_CH), F32), dec_l[:0:-1]], axis=0)

    xc = x_prompt.reshape(batch * seq_c, d)
    xl = x_sample.reshape(nb_l * seq_l, d)
    yc = yl = None
    new_ckv, new_kpe, new_k, new_v = [], [], [], []
    for l in range(depth):
        (xc, qa, ka, va, ckv, kpe, qc, kc, vc, kcf, vcf, u) = _premix_call(l, xc, yc, mod_c, seq_c, False, wts, None)
        new_ckv.append(ckv)
        new_kpe.append(kpe[:, NOPE_A:QK_A])
        new_k.append(kcf)
        new_v.append(vcf)
        oa, oc = _attn_ctx_call(qa, ka, va, qc, kc, vc, seq_c)
        kf_c = _filt_ctx_call(l, z_c, dec_c, hw, cmat, smat)
        ob = _hy_ctx_call(l, u, hw, kf_c, fwd_c, inv_c, seq_c)
        xc, h2, idx, gates = _postmix_call(l, xc, oa, ob, oc, mod_c, seq_c, wts)
        yc = _moe_call(l, h2, idx[:, :TOP_K], gates[:, :TOP_K], wts)
        (xl, qa, ka, va, _, _, qc, kc, vc, _, _, u) = _premix_call(l, xl, yl, mod_l, seq_l, True, wts, rope_tabs)
        oa, oc = _attn_lat_call(l, qa, ka, va, qc, kc, vc, kax, vax, kcx, vcx, seq_l)
        kf_l = _filter_lat(l, z_full, dec_full, hw, dft_l, seq_l)
        ob = _hyena_lat(l, u, hw, kf_l, dft_l, seq_l)
        xl, h2, idx, gates = _postmix_call(l, xl, oa, ob, oc, mod_l, seq_l, wts)
        yl = _moe_call(l, h2, idx[:, :TOP_K], gates[:, :TOP_K], wts)
    y_prompt = _final_call(depth - 1, xc, yc, mod_c, seq_c).reshape(batch, seq_c, d)
    y_sample = _final_call(depth - 1, xl, yl, mod_l, seq_l).reshape(nb_l, seq_l, d)
    stack = lambda xs, tail: jnp.stack([a.reshape((batch, seq_c) + tail) for a in xs], axis=1)
    return (y_prompt, y_sample, stack(new_ckv, (KV_RANK,)), stack(new_kpe, (ROPE_A,)),
            stack(new_k, (KV_C, HD_C)), stack(new_v, (KV_C, HD_C)))
```

```python
import functools
import math

import numpy as np
import jax
import jax.numpy as jnp
from jax import lax
from jax.experimental import pallas as pl
from jax.experimental.pallas import tpu as pltpu

F32 = jnp.float32
BF16 = jnp.bfloat16

D_MODEL = 1024
GRID_W = 64
EPS = 1e-6
ROPE_THETA = 10000.0
H_A = 6
Q_RANK = 256
KV_RANK = 128
NOPE_A = 64
ROPE_A = 32
V_A = 64
QK_A = NOPE_A + ROPE_A
HY_CH = 256
HY_ORDER = 2
HY_EMB = 33
HY_FO = 64
HY_INNER = 2
HY_MIN_DECAY = math.log(1e-2) / 1.5
HY_MAX_DECAY = math.log(1e-2) / 0.3
H_C = 6
KV_C = 2
HD_C = 64
N_EXPERTS = 32
TOP_K = 4
D_FF = 1024
SWIGLU_LIMIT = 7.0
SWIGLU_ALPHA = 1.702
LOG2E = 1.4426950408889634

LANE = 128
HQ = H_A * LANE
W_A = H_A * V_A
W_B = HY_CH
W_C = H_C * HD_C
IN_P = Q_RANK + KV_RANK + LANE + 3 * HY_CH + HQ + 2 * LANE
ROW_BLOCK = 256
TOKEN_TILE = 512
ATTN_Q_TILE = 512
VMEM_LIMIT = 60 * 1024 * 1024
FFT_N1 = 128
FFT_N2 = 64


def _cp(sem, vmem=None):
    return pltpu.CompilerParams(dimension_semantics=sem, vmem_limit_bytes=vmem)


def _dot(a, b):
    return jnp.dot(a, b, preferred_element_type=F32)


def _split(a):
    hi = a.astype(BF16)
    return hi, (a - hi.astype(F32)).astype(BF16)


def _dot3(a, b):
    ah, al = _split(a)
    bh, bl = _split(b)
    return _dot(ah, bh) + _dot(ah, bl) + _dot(al, bh)


def _bdot(a, b):
    return lax.dot_general(a, b, (((2,), (1,)), ((0,), (0,))), preferred_element_type=F32)


def _rms(x, n=None):
    ss = jnp.sum(x * x, axis=-1, keepdims=True) * (1.0 / (n or x.shape[-1]))
    return x * lax.rsqrt(ss + EPS)


def _head_norm(x, nvalid):
    outs = []
    for h in range(x.shape[1] // LANE):
        blk = x[:, h * LANE:(h + 1) * LANE]
        ss = jnp.sum(blk * blk, axis=-1, keepdims=True) * (1.0 / nvalid)
        outs.append(blk * lax.rsqrt(ss + EPS))
    return outs[0] if len(outs) == 1 else jnp.concatenate(outs, axis=1)


def _rope(x, rope):
    tab_ref, perm_ref = rope
    c, s = tab_ref[0], tab_ref[1]
    xb = x.astype(BF16)
    nblk = x.shape[1] // LANE
    outs = []
    b = 0
    while b < nblk:
        w = 2 if nblk - b >= 2 else 1
        perm = perm_ref[...] if w == 2 else perm_ref[:LANE, :LANE]
        sw = _dot(xb[:, b * LANE:(b + w) * LANE], perm)
        for k in range(w):
            outs.append(x[:, (b + k) * LANE:(b + k + 1) * LANE] * c + sw[:, k * LANE:(k + 1) * LANE] * s)
        b += w
    return outs[0] if len(outs) == 1 else jnp.concatenate(outs, axis=1)


def _mod_kernel(c_ref, w_ref, b_ref, o_ref):
    c = c_ref[...]
    s = c * (1.0 / (1.0 + jnp.exp(-c)))
    o_ref[...] = _dot3(s, w_ref[...]) + b_ref[...]


def _mod_call(conds, w_mod, b_mod):
    depth, d, n6 = w_mod.shape
    tn = 1536
    return pl.pallas_call(
        _mod_kernel,
        out_shape=jax.ShapeDtypeStruct((depth, 8, n6), F32),
        grid=(depth, n6 // tn),
        in_specs=[pl.BlockSpec((8, d), lambda l, j: (0, 0)),
                  pl.BlockSpec((None, d, tn), lambda l, j: (l, 0, j)),
                  pl.BlockSpec((None, 1, tn), lambda l, j: (l, 0, j))],
        out_specs=pl.BlockSpec((None, 8, tn), lambda l, j: (l, 0, j)),
        compiler_params=_cp(("parallel", "parallel"), VMEM_LIMIT),
        name="mod",
    )(conds, w_mod, b_mod.reshape(depth, 1, n6))


def _cast_kernel(x_ref, o_ref):
    o_ref[...] = x_ref[...].astype(BF16)


def _cast_call(w, tr):
    n, r, c = w.shape
    return pl.pallas_call(
        _cast_kernel,
        out_shape=jax.ShapeDtypeStruct(w.shape, BF16),
        grid=(n, r // tr),
        in_specs=[pl.BlockSpec((None, tr, c), lambda i, j: (i, j, 0))],
        out_specs=pl.BlockSpec((None, tr, c), lambda i, j: (i, j, 0)),
        compiler_params=_cp(("parallel", "parallel"), VMEM_LIMIT),
        name="cast_bf16",
    )(w)


def _kv_heads(ckv_bf, kpe, w_kv_ref, gk, rope_ref, use_rope):
    kvp = _dot(ckv_bf, w_kv_ref[...])
    kn = kvp[:, :HQ]
    ka = jnp.concatenate([kn[:, h * LANE:(h + 1) * LANE] + kpe for h in range(H_A)], axis=1)
    ka = _head_norm(ka, QK_A) * gk
    if use_rope:
        ka = _rope(ka, rope_ref)
    return ka, kvp[:, HQ:]


ROW_SPLIT = 2


def _premix_kernel(has_prev, use_rope, *refs):
    n_out = 12
    base = 3 if has_prev else 1
    rows = {0} | ({1} if has_prev else set()) | set(range(len(refs) - n_out, len(refs)))
    tabs = {base + 5, base + 7} if use_rope else set()
    sub = refs[0].shape[0] // ROW_SPLIT
    for k in range(ROW_SPLIT):
        sl = pl.ds(k * sub, sub)
        _premix_body(has_prev, use_rope,
                     *[r.at[sl] if i in rows else (r.at[:, sl] if i in tabs else r) for i, r in enumerate(refs)])


def _premix_body(has_prev, use_rope, *refs):
    it = iter(refs)
    x_ref = next(it)
    if has_prev:
        y_ref, modp_ref = next(it), next(it)
    mod_ref, w_in_ref, w_uq_ref, w_kv_ref, g_ref = next(it), next(it), next(it), next(it), next(it)
    if use_rope:
        ra_ref, rc_ref = (next(it), next(it)), (next(it), next(it))
    else:
        ra_ref = rc_ref = None
    (xo_ref, qa_ref, ka_ref, va_ref, ckv_ref, kpe_ref, qc_ref, kc_ref, vc_ref, kcf_ref, vcf_ref, u_ref) = it

    x = x_ref[...]
    if has_prev:
        x = x + modp_ref[5:6, :] * y_ref[...]
    xo_ref[...] = x
    h = _rms(x) * (1.0 + mod_ref[1:2, :]) + mod_ref[0:1, :]
    proj = _dot(h.astype(BF16), w_in_ref[...])
    o = 0
    c_q = proj[:, o:o + Q_RANK]; o += Q_RANK
    c_kv = proj[:, o:o + KV_RANK]; o += KV_RANK
    kpe = proj[:, o:o + LANE]; o += LANE
    u_ref[...] = proj[:, o:o + 3 * HY_CH]; o += 3 * HY_CH
    q_c = proj[:, o:o + HQ]; o += HQ
    k_c = proj[:, o:o + LANE]; o += LANE
    v_c = proj[:, o:o + LANE]

    cqn = _rms(c_q) * g_ref[0:1, :Q_RANK]
    qa = _head_norm(_dot(cqn.astype(BF16), w_uq_ref[...]), QK_A) * g_ref[2:3, :]
    if use_rope:
        qa = _rope(qa, ra_ref)
    qa_ref[...] = qa.astype(BF16)
    ckv = _rms(c_kv) * g_ref[1:2, :KV_RANK]
    ckv_ref[...] = ckv
    kpe_ref[...] = kpe
    ka, va = _kv_heads(ckv.astype(BF16), kpe, w_kv_ref, g_ref[3:4, :], ra_ref, use_rope)
    ka_ref[...] = ka.astype(BF16)
    va_ref[...] = _with_ones(va.astype(BF16))

    qc = _head_norm(q_c, HD_C) * g_ref[4:5, :]
    if use_rope:
        qc = _rope(qc, rc_ref)
    qc_ref[...] = qc.astype(BF16)
    lane = lax.broadcasted_iota(jnp.int32, k_c.shape, 1)
    k2 = k_c * k_c
    s0 = jnp.sum(jnp.where(lane < HD_C, k2, 0.0), axis=-1, keepdims=True) * (1.0 / HD_C)
    s1 = jnp.sum(jnp.where(lane >= HD_C, k2, 0.0), axis=-1, keepdims=True) * (1.0 / HD_C)
    kcn = k_c * jnp.where(lane < HD_C, lax.rsqrt(s0 + EPS), lax.rsqrt(s1 + EPS)) * g_ref[5:6, :LANE]
    kcf_ref[...] = kcn
    vcf_ref[...] = v_c
    kc_ref[...] = (_rope(kcn, rc_ref) if use_rope else kcn).astype(BF16)
    vc_ref[...] = _with_ones(v_c.astype(BF16))


def _chunk_rows_spec(y, tm):
    per = (y.shape[1] - SUB) // tm
    return pl.BlockSpec((None, tm, y.shape[2]), lambda i: (i // per, i % per, 0))


def _premix_call(l, x, yprev, mod, seq, use_rope, wts, rope_tabs):
    t, d = x.shape
    tm = TOKEN_TILE
    ncond = mod.shape[1]
    has_prev = yprev is not None

    def cond(i):
        return (i * tm) // seq if ncond > 1 else 0

    row = lambda w: pl.BlockSpec((tm, w), lambda i: (i, 0))
    ins, specs = [x], [row(d)]
    if has_prev:
        ins += [yprev, mod]
        specs += [_chunk_rows_spec(yprev, tm), pl.BlockSpec((None, None, 6, d), lambda i: (l - 1, cond(i), 0, 0))]
    ins += [mod, wts["w_in"], wts["w_uq"], wts["w_kv"], wts["gains"]]
    specs += [pl.BlockSpec((None, None, 6, d), lambda i: (l, cond(i), 0, 0)),
              pl.BlockSpec((None, d, IN_P), lambda i: (l, 0, 0)),
              pl.BlockSpec((None, Q_RANK, HQ), lambda i: (l, 0, 0)),
              pl.BlockSpec((None, KV_RANK, HQ + W_A), lambda i: (l, 0, 0)),
              pl.BlockSpec((None, 8, HQ), lambda i: (l, 0, 0))]
    if use_rope:
        nt = seq // tm
        tab = pl.BlockSpec((2, tm, LANE), lambda i: (0, i % nt, 0))
        perm = pl.BlockSpec((2 * LANE, 2 * LANE), lambda i: (0, 0))
        ins += [rope_tabs[0][0], rope_tabs[0][1], rope_tabs[1][0], rope_tabs[1][1]]
        specs += [tab, perm, tab, perm]
    outs = [(d, F32), (HQ, BF16), (HQ, BF16), (2 * W_A, BF16), (KV_RANK, F32), (LANE, F32), (HQ, BF16),
            (LANE, BF16), (2 * LANE, BF16), (LANE, F32), (LANE, F32), (3 * HY_CH, F32)]
    return pl.pallas_call(
        functools.partial(_premix_kernel, has_prev, use_rope),
        out_shape=[jax.ShapeDtypeStruct((t, w), dt) for w, dt in outs],
        grid=(t // tm,),
        in_specs=specs,
        out_specs=[row(w) for w, _ in outs],
        compiler_params=_cp(("parallel",), VMEM_LIMIT),
        name="premix",
    )(*ins)


def _cachekv_kernel(ckv_ref, kpe_ref, w_kv_ref, g_ref, ka_ref, va_ref):
    ka, va = _kv_heads(ckv_ref[...].astype(BF16), kpe_ref[...], w_kv_ref, g_ref[3:4, :], None, False)
    ka_ref[...] = ka.astype(BF16)
    va_ref[...] = _with_ones(va.astype(BF16))


def _cachekv_call(cache_ckv, cache_kpe_p, wts):
    nb, depth, past, _ = cache_ckv.shape
    return pl.pallas_call(
        _cachekv_kernel,
        out_shape=[jax.ShapeDtypeStruct((depth, nb, past, HQ), BF16),
                   jax.ShapeDtypeStruct((depth, nb, past, 2 * W_A), BF16)],
        grid=(depth, nb),
        in_specs=[pl.BlockSpec((None, None, past, KV_RANK), lambda l, b: (b, l, 0, 0)),
                  pl.BlockSpec((None, None, past, LANE), lambda l, b: (b, l, 0, 0)),
                  pl.BlockSpec((None, KV_RANK, HQ + W_A), lambda l, b: (l, 0, 0)),
                  pl.BlockSpec((None, 8, HQ), lambda l, b: (l, 0, 0))],
        out_specs=[pl.BlockSpec((None, None, past, HQ), lambda l, b: (l, b, 0, 0)),
                   pl.BlockSpec((None, None, past, 2 * W_A), lambda l, b: (l, b, 0, 0))],
        compiler_params=_cp(("parallel", "parallel"), VMEM_LIMIT),
        name="cache_kv",
    )(cache_ckv, cache_kpe_p, wts["w_kv"], wts["gains"])


def _nt(q, k):
    return lax.dot_general(q, k, (((1,), (1,)), ((), ())), preferred_element_type=F32)


def _with_ones(v):
    ones = jnp.ones((v.shape[0], LANE), v.dtype)
    parts = []
    for j in range(v.shape[1] // LANE):
        parts += [v[:, j * LANE:(j + 1) * LANE], ones]
    return jnp.concatenate(parts, axis=1)


def _attend(q, ks, vs):
    return _softmax_pv([_nt(q, k) for k in ks], vs)


def _softmax_pv(ss, vs, m=None):
    if m is None:
        m = ss[0].max(axis=-1, keepdims=True)
        for s in ss[1:]:
            m = jnp.maximum(m, s.max(axis=-1, keepdims=True))
    acc = None
    for s, v in zip(ss, vs):
        pv = _dot(jnp.exp2(s - m).astype(BF16), v)
        acc = pv if acc is None else acc + pv
    return acc[:, :LANE] / acc[:, LANE:]


BOUND_SLACK = 1.01
BOUND_LIMIT = 60.0


def _attn_kernel(nseg, bound, *refs):
    qa_ref, qc_ref = refs[0], refs[1]
    if bound:
        kq_ref, refs = refs[2], refs[:2] + refs[3:]
    segs = refs[2:2 + 4 * nseg]
    oa_ref, oc_ref = refs[2 + 4 * nseg:]
    ka_refs, va_refs, kc_refs, vc_refs = (segs[i::4] for i in range(4))
    lane = lax.broadcasted_iota(jnp.int32, (qa_ref.shape[0], LANE), 1)
    low = lane < V_A
    g_per = H_C // KV_C
    jobs = [("a", j, h) for j in range(H_A // 2) for h in (2 * j, 2 * j + 1)]
    jobs += [("c", g, h) for g in range(g_per) for h in (g, g + g_per)]

    def scores(job):
        kind, _, h = job
        hs = slice(h * LANE, (h + 1) * LANE)
        if kind == "a":
            return [_nt(qa_ref[:, hs], r[:, hs]) for r in ka_refs]
        return [_nt(qc_ref[:, hs], r[...]) for r in kc_refs]

    def values(job):
        kind, j, _ = job
        if kind == "a":
            return [r[:, 2 * j * LANE:2 * (j + 1) * LANE] for r in va_refs]
        return [r[...] for r in vc_refs]

    def row_bound(job):
        kind, _, h = job
        q = (qa_ref if kind == "a" else qc_ref)[:, h * LANE:(h + 1) * LANE].astype(F32)
        r = h if kind == "a" else H_A + h // g_per
        q2 = jnp.sum(q * q, axis=-1, keepdims=True)
        return jnp.sqrt(q2 * kq_ref[r:r + 1, 0:1]) * BOUND_SLACK

    ss_next = None if bound else scores(jobs[0])
    pv = []
    for n, job in enumerate(jobs):
        if bound:
            pv.append(_softmax_pv(scores(job), values(job), row_bound(job)))
        else:
            ss = ss_next
            if n + 1 < len(jobs):
                ss_next = scores(jobs[n + 1])
            pv.append(_softmax_pv(ss, values(job)))
        if len(pv) == 2:
            o_ref = oa_ref if job[0] == "a" else oc_ref
            o_ref[:, job[1] * LANE:(job[1] + 1) * LANE] = jnp.where(low, pv[0], pv[1])
            pv = []


def _attn_ctx_call(qa, ka, va, qc, kc, vc, seq):
    t = qa.shape[0]
    blk = lambda w: pl.BlockSpec((seq, w), lambda b: (b, 0))
    return pl.pallas_call(
        functools.partial(_attn_kernel, 1, False),
        out_shape=[jax.ShapeDtypeStruct((t, W_A), F32), jax.ShapeDtypeStruct((t, W_C), F32)],
        grid=(t // seq,),
        in_specs=[blk(HQ), blk(HQ), blk(HQ), blk(2 * W_A), blk(LANE), blk(2 * LANE)],
        out_specs=[blk(W_A), blk(W_C)],
        compiler_params=_cp(("parallel",), VMEM_LIMIT),
        name="attn_ctx",
    )(qa, qc, ka, va, kc, vc)


def _attn_lat_call(l, qa, ka, va, qc, kc, vc, kax, vax, kcx, vcx, seq):
    t = qa.shape[0]
    nb = t // seq
    tq = min(ATTN_Q_TILE, seq)
    nq = seq // tq
    past = kax.shape[2]
    qblk = lambda w: pl.BlockSpec((tq, w), lambda b, i: (b * nq + i, 0))
    sblk = lambda w: pl.BlockSpec((seq, w), lambda b, i: (b, 0))
    xblk = lambda w: pl.BlockSpec((None, None, past, w), lambda b, i: (l, b, 0, 0))
    cblk = lambda w: pl.BlockSpec((None, None, past, w), lambda b, i: (b, l, 0, 0))
    full = lambda w: pl.BlockSpec((seq, w), lambda b: (b, 0))

    kq = pl.pallas_call(
        _knorm_kernel,
        out_shape=jax.ShapeDtypeStruct((nb, NORM_ROWS, LANE), F32),
        grid=(nb,),
        in_specs=[full(HQ), pl.BlockSpec((None, None, past, HQ), lambda b: (l, b, 0, 0)),
                  full(LANE), pl.BlockSpec((None, None, past, LANE), lambda b: (b, l, 0, 0)), full(HQ), full(HQ)],
        out_specs=pl.BlockSpec((None, NORM_ROWS, LANE), lambda b: (b, 0, 0)),
        compiler_params=_cp(("parallel",), VMEM_LIMIT),
        name="attn_norms",
    )(ka, kax, kc, kcx, qa, qc)
    k2, q2a, q2c = kq[:, :H_A + KV_C, 0], kq[:, 8:8 + H_A, 0], kq[:, 8 + H_A:8 + H_A + H_C, 0]
    k2c = jnp.repeat(k2[:, H_A:], H_C // KV_C, axis=1)
    worst = jnp.sqrt(jnp.maximum(jnp.max(q2a * k2[:, :H_A]), jnp.max(q2c * k2c))) * BOUND_SLACK

    def call(bound):
        extra_specs = [pl.BlockSpec((None, NORM_ROWS, LANE), lambda b, i: (b, 0, 0))] if bound else []
        return pl.pallas_call(
            functools.partial(_attn_kernel, 2, bound),
            out_shape=[jax.ShapeDtypeStruct((t, W_A), F32), jax.ShapeDtypeStruct((t, W_C), F32)],
            grid=(nb, nq),
            in_specs=[qblk(HQ), qblk(HQ)] + extra_specs
            + [xblk(HQ), xblk(2 * W_A), cblk(LANE), cblk(2 * LANE),
               sblk(HQ), sblk(2 * W_A), sblk(LANE), sblk(2 * LANE)],
            out_specs=[qblk(W_A), qblk(W_C)],
            compiler_params=_cp(("parallel", "parallel"), VMEM_LIMIT),
            name="attn_lat_bound" if bound else "attn_lat",
        )(qa, qc, *([kq] if bound else []), kax, vax, kcx, vcx, ka, va, kc, vc)

    return lax.cond(worst < BOUND_LIMIT, lambda: call(True), lambda: call(False))


NORM_ROWS = 24


def _knorm_kernel(ka_ref, kax_ref, kc_ref, kcx_ref, qa_ref, qc_ref, o_ref):
    def max_n2(x, lanes=None):
        xf = x.astype(F32)
        x2 = xf * xf
        if lanes is not None:
            lane = lax.broadcasted_iota(jnp.int32, x2.shape, 1)
            x2 = jnp.where((lane >= lanes[0]) & (lane < lanes[1]), x2, 0.0)
        return jnp.max(jnp.sum(x2, axis=-1, keepdims=True), axis=0, keepdims=True)

    rows = []
    for h in range(H_A):
        hs = slice(h * LANE, (h + 1) * LANE)
        rows.append(jnp.maximum(max_n2(ka_ref[:, hs]), max_n2(kax_ref[:, hs])))
    for kv in range(KV_C):
        lanes = (kv * HD_C, (kv + 1) * HD_C)
        rows.append(jnp.maximum(max_n2(kc_ref[...], lanes), max_n2(kcx_ref[...], lanes)))
    rows += [max_n2(qa_ref[:, h * LANE:(h + 1) * LANE]) for h in range(H_A)]
    rows += [max_n2(qc_ref[:, h * LANE:(h + 1) * LANE]) for h in range(H_C)]
    o_ref[...] = jnp.zeros_like(o_ref)
    for r, v in enumerate(rows):
        o_ref[r:r + 1, :] = jnp.broadcast_to(v, (1, LANE))


def _filter_mlp(z, w_in_ref, b_in_ref, w_mid_ref, b_mid_ref, w_out_ref, b_out_ref, freq_ref):
    freq = freq_ref[...]
    a = jnp.sin(freq * (_dot3(z, w_in_ref[...]) + b_in_ref[...]))
    for i in range(HY_INNER):
        a = jnp.sin(freq * (_dot3(a, w_mid_ref[i]) + b_mid_ref[i]))
    return _dot3(a, w_out_ref[...]) + b_out_ref[...]


def _conv3(u, up, dn, cw_ref, cb_ref):
    return up * cw_ref[0:1, :] + u * cw_ref[1:2, :] + dn * cw_ref[2:3, :] + cb_ref[...]


def _filt_ctx_kernel(z_ref, dec_ref, w_in_ref, b_in_ref, w_mid_ref, b_mid_ref, w_out_ref, b_out_ref, freq_ref,
                     c_ref, s_ref, o_ref):
    n = z_ref.shape[0]
    h = _filter_mlp(z_ref[...], w_in_ref, b_in_ref, w_mid_ref, b_mid_ref, w_out_ref, b_out_ref, freq_ref)
    dec = dec_ref[...]
    row = lax.broadcasted_iota(jnp.int32, dec.shape, 0)
    half = HY_ORDER * HY_CH
    for o in range(HY_ORDER):
        hf = h[:, o * HY_CH:(o + 1) * HY_CH] * dec
        hb = jnp.where(row > 0, h[:, half + o * HY_CH:half + (o + 1) * HY_CH] * dec, 0.0)
        nrm = jnp.sum(jnp.abs(hf) + jnp.abs(hb), axis=0, keepdims=True) + EPS
        scale = (1.0 / n) / nrm
        o_ref[o, 0] = _dot3(c_ref[...], hf + hb) * scale
        o_ref[o, 1] = -_dot3(s_ref[...], hf - hb) * scale


def _filt_ctx_call(l, z, dec, hw, cmat, smat):
    n = z.shape[0]
    full = lambda a: pl.BlockSpec((None,) + a.shape[1:], lambda i: (l,) + (0,) * (a.ndim - 1))
    const = lambda a: pl.BlockSpec(a.shape, lambda i: (0,) * a.ndim)
    names = ["hy_w_in", "hy_b_in", "hy_w_mid", "hy_b_mid", "hy_w_out", "hy_b_out", "hy_freq"]
    return pl.pallas_call(
        _filt_ctx_kernel,
        out_shape=jax.ShapeDtypeStruct((HY_ORDER, 2, n, HY_CH), F32),
        grid=(1,),
        in_specs=[const(z), const(dec)] + [full(hw[k]) for k in names] + [const(cmat), const(smat)],
        out_specs=pl.BlockSpec((HY_ORDER, 2, n, HY_CH), lambda i: (0, 0, 0, 0)),
        compiler_params=_cp(("arbitrary",), VMEM_LIMIT),
        name="hy_filter_ctx",
    )(z, dec, *[hw[k] for k in names], cmat, smat)


def _hy_ctx_kernel(u_ref, cw_ref, cb_ref, skip_ref, kf_ref, fwd_ref, inv_ref, o_ref):
    u = u_ref[...]
    n = u.shape[0]
    row = lax.broadcasted_iota(jnp.int32, u.shape, 0)
    up = jnp.where(row > 0, pltpu.roll(u, 1, 0), 0.0)
    dn = jnp.where(row < n - 1, pltpu.roll(u, n - 1, 0), 0.0)
    z = _conv3(u, up, dn, cw_ref, cb_ref)
    s = z[:, :HY_CH]
    gates = (z[:, HY_CH:2 * HY_CH], z[:, 2 * HY_CH:])
    for o in range(HY_ORDER):
        xs = _dot(fwd_ref[...], s.astype(BF16))
        xr, xi = xs[:n], xs[n:]
        kr, ki = kf_ref[o, 0], kf_ref[o, 1]
        ycat = jnp.concatenate([xr * kr - xi * ki, xr * ki + xi * kr], axis=0)
        y = _dot(inv_ref[...], ycat.astype(BF16))
        s = gates[o] * (y + s * skip_ref[o:o + 1, :])
    o_ref[...] = s


def _hy_ctx_call(l, u, hw, kf, fwd, inv, seq):
    t = u.shape[0]
    full = lambda a: pl.BlockSpec((None,) + a.shape[1:], lambda b: (l,) + (0,) * (a.ndim - 1))
    const = lambda a: pl.BlockSpec(a.shape, lambda b: (0,) * a.ndim)
    return pl.pallas_call(
        _hy_ctx_kernel,
        out_shape=jax.ShapeDtypeStruct((t, HY_CH), F32),
        grid=(t // seq,),
        in_specs=[pl.BlockSpec((seq, 3 * HY_CH), lambda b: (b, 0)),
                  full(hw["hy_conv_w"]), full(hw["hy_conv_b"]), full(hw["hy_skip"]),
                  const(kf), const(fwd), const(inv)],
        out_specs=pl.BlockSpec((seq, HY_CH), lambda b: (b, 0)),
        compiler_params=_cp(("parallel",), VMEM_LIMIT),
        name="hyena_ctx",
    )(u, hw["hy_conv_w"], hw["hy_conv_b"], hw["hy_skip"], kf, fwd, inv)


def _filt_lat_kernel(seq, z_ref, dec_ref, w_in_ref, b_in_ref, w_mid_ref, b_mid_ref, w_out_ref, b_out_ref, freq_ref,
                     k_ref, n_ref):
    i = pl.program_id(0)
    tr = z_ref.shape[0]
    h = _filter_mlp(z_ref[...], w_in_ref, b_in_ref, w_mid_ref, b_mid_ref, w_out_ref, b_out_ref, freq_ref)
    half = HY_ORDER * HY_CH
    row = i * tr + lax.broadcasted_iota(jnp.int32, (tr, half), 0)
    dec = dec_ref[...]
    kern = jnp.where(row < seq, h[:, :half], h[:, half:]) * jnp.concatenate([dec] * HY_ORDER, axis=1)
    k_ref[...] = kern

    @pl.when(i == 0)
    def _():
        n_ref[...] = jnp.zeros_like(n_ref)

    n_ref[...] += jnp.sum(jnp.abs(kern), axis=0, keepdims=True)


def _filt_lat_call(l, z, dec, hw, seq):
    n = z.shape[0]
    tr = min(512, n)
    half = HY_ORDER * HY_CH
    full = lambda a: pl.BlockSpec((None,) + a.shape[1:], lambda i: (l,) + (0,) * (a.ndim - 1))
    names = ["hy_w_in", "hy_b_in", "hy_w_mid", "hy_b_mid", "hy_w_out", "hy_b_out", "hy_freq"]
    return pl.pallas_call(
        functools.partial(_filt_lat_kernel, seq),
        out_shape=[jax.ShapeDtypeStruct((n, half), F32), jax.ShapeDtypeStruct((1, half), F32)],
        grid=(n // tr,),
        in_specs=[pl.BlockSpec((tr, z.shape[1]), lambda i: (i, 0)), pl.BlockSpec((tr, HY_CH), lambda i: (i, 0))]
        + [full(hw[k]) for k in names],
        out_specs=[pl.BlockSpec((tr, half), lambda i: (i, 0)), pl.BlockSpec((1, half), lambda i: (0, 0))],
        compiler_params=_cp(("arbitrary",), VMEM_LIMIT),
        name="hy_filter_lat",
    )(z, dec, *[hw[k] for k in names])


def _fa_kernel(f1_ref, k_ref, o_ref):
    r = _dot3(f1_ref[...], k_ref[...])
    o_ref[0] = r[:FFT_N1]
    o_ref[1] = r[FFT_N1:]


def _fa_call(f1f, kern2d):
    n1, w = kern2d.shape
    tn = min(2048, w)
    return pl.pallas_call(
        _fa_kernel,
        out_shape=jax.ShapeDtypeStruct((2, FFT_N1, w), F32),
        grid=(w // tn,),
        in_specs=[pl.BlockSpec(f1f.shape, lambda j: (0, 0)), pl.BlockSpec((n1, tn), lambda j: (0, j))],
        out_specs=pl.BlockSpec((2, FFT_N1, tn), lambda j: (0, 0, j)),
        compiler_params=_cp(("parallel",), VMEM_LIMIT),
        name="hy_filter_dft1",
    )(f1f, kern2d)


def _fb_kernel(n_total, a_ref, fh_ref, fl_ref, n_ref, o_ref):
    a = jnp.concatenate([a_ref[0], a_ref[1]], axis=1)
    ah, al = _split(a)
    x = _bdot(fh_ref[...], ah) + _bdot(fh_ref[...], al) + _bdot(fl_ref[...], ah)
    scale = (1.0 / n_total) / (n_ref[...] + EPS)
    o_ref[...] = x * scale[None]


def _fb_call(af5, f2h, f2l, nrm, n_total):
    _, n1, n2, c = af5.shape[0], af5.shape[1], af5.shape[2], af5.shape[3]
    k1t = 8
    return pl.pallas_call(
        functools.partial(_fb_kernel, n_total),
        out_shape=jax.ShapeDtypeStruct((n1, 2 * n2, c), F32),
        grid=(n1 // k1t,),
        in_specs=[pl.BlockSpec((2, k1t, n2, c), lambda j: (0, j, 0, 0)),
                  pl.BlockSpec((k1t, 2 * n2, 2 * n2), lambda j: (j, 0, 0)),
                  pl.BlockSpec((k1t, 2 * n2, 2 * n2), lambda j: (j, 0, 0)),
                  pl.BlockSpec((1, c), lambda j: (0, 0))],
        out_specs=pl.BlockSpec((k1t, 2 * n2, c), lambda j: (j, 0, 0)),
        compiler_params=_cp(("parallel",), VMEM_LIMIT),
        name="hy_filter_dft2",
    )(af5, f2h, f2l, nrm)


def _hconv_kernel(seq, n2, u_ref, p_ref, n_ref, cw_ref, cb_ref, v_ref, x1_ref, x2_ref, zs):
    i = pl.program_id(0)
    u = u_ref[...]
    tt = u.shape[0]
    row = lax.broadcasted_iota(jnp.int32, u.shape, 0)
    pos = (i * tt) % seq
    prev = jnp.where(pos > 0, p_ref[7:8, :], 0.0)
    nxt = jnp.where(pos + tt < seq, n_ref[0:1, :], 0.0)
    up = jnp.where(row > 0, pltpu.roll(u, 1, 0), prev)
    dn = jnp.where(row < tt - 1, pltpu.roll(u, tt - 1, 0), nxt)
    z = _conv3(u, up, dn, cw_ref, cb_ref)
    nlb = z.shape[1] // LANE
    for k in range(nlb):
        zs[k] = z[:, k * LANE:(k + 1) * LANE]
    per = HY_CH // LANE
    for p, o_ref in enumerate((v_ref, x1_ref, x2_ref)):
        for t2 in range(n2):
            for k in range(per):
                o_ref[:, t2 * HY_CH + k * LANE:t2 * HY_CH + (k + 1) * LANE] = (
                    zs[p * per + k, pl.ds(t2, tt // n2, stride=n2), :])


def _hconv_call(l, u, hw, seq, n2):
    t, w = u.shape
    tt = min(512, seq)
    nblk8 = t // 8
    full = lambda a: pl.BlockSpec((None,) + a.shape[1:], lambda i: (l,) + (0,) * (a.ndim - 1))
    ob = pl.BlockSpec((tt // n2, n2 * HY_CH), lambda i: (i, 0))
    return pl.pallas_call(
        functools.partial(_hconv_kernel, seq, n2),
        out_shape=[jax.ShapeDtypeStruct((t // n2, n2 * HY_CH), F32)] * 3,
        scratch_shapes=[pltpu.VMEM((w // LANE, tt, LANE), F32)],
        grid=(t // tt,),
        in_specs=[pl.BlockSpec((tt, w), lambda i: (i, 0)),
                  pl.BlockSpec((8, w), lambda i: (jnp.maximum(i * (tt // 8) - 1, 0), 0)),
                  pl.BlockSpec((8, w), lambda i: (jnp.minimum((i + 1) * (tt // 8), nblk8 - 1), 0)),
                  full(hw["hy_conv_w"]), full(hw["hy_conv_b"])],
        out_specs=[ob, ob, ob],
        compiler_params=_cp(("parallel",), VMEM_LIMIT),
        name="hyena_conv3",
    )(u, u, u, hw["hy_conv_w"], hw["hy_conv_b"])


def _ha_kernel(f1_ref, x_ref, o_ref):
    r = _dot(f1_ref[...], x_ref[...].astype(BF16))
    o_ref[0] = r[:FFT_N1].astype(BF16)
    o_ref[1] = r[FFT_N1:].astype(BF16)


def _ha_call(f1d, x2d, nb):
    rows, w = x2d.shape
    n1h = rows // nb
    tn = min(2048, w)
    return pl.pallas_call(
        _ha_kernel,
        out_shape=jax.ShapeDtypeStruct((nb, 2, FFT_N1, w), BF16),
        grid=(nb, w // tn),
        in_specs=[pl.BlockSpec(f1d.shape, lambda b, j: (0, 0)), pl.BlockSpec((n1h, tn), lambda b, j: (b, j))],
        out_specs=pl.BlockSpec((None, 2, FFT_N1, tn), lambda b, j: (b, 0, 0, j)),
        compiler_params=_cp(("parallel", "parallel"), VMEM_LIMIT),
        name="hyena_dft1",
    )(f1d, x2d)


def _hb_kernel(a_ref, f_ref, g_ref, kf_ref, o_ref):
    n2 = a_ref.shape[2]
    a = jnp.concatenate([a_ref[0], a_ref[1]], axis=1)
    x = _bdot(f_ref[...], a)
    xr, xi = x[:, :n2], x[:, n2:]
    kr, ki = kf_ref[:, :n2], kf_ref[:, n2:]
    y = jnp.concatenate([xr * kr - xi * ki, xr * ki + xi * kr], axis=1).astype(BF16)
    b = _bdot(g_ref[...], y)
    o_ref[0] = b[:, :n2].astype(BF16)
    o_ref[1] = b[:, n2:].astype(BF16)


def _hb_call(o, a5, f2, g2, kf):
    nb, _, n1, n2, c = a5.shape
    k1t = 16
    blk = pl.BlockSpec((None, 2, k1t, n2, c), lambda b, j: (b, 0, j, 0, 0))
    mat = pl.BlockSpec((k1t, 2 * n2, 2 * n2), lambda b, j: (j, 0, 0))
    return pl.pallas_call(
        _hb_kernel,
        out_shape=jax.ShapeDtypeStruct(a5.shape, BF16),
        grid=(nb, n1 // k1t),
        in_specs=[blk, mat, mat, pl.BlockSpec((k1t, 2 * n2, c), lambda b, j: (j, 0, o))],
        out_specs=blk,
        compiler_params=_cp(("parallel", "parallel"), VMEM_LIMIT),
        name="hyena_dft2",
    )(a5, f2, g2, kf)


def _hc_kernel(o, fc_ref, b_ref, s_ref, g_ref, skip_ref, o_ref):
    bcat = jnp.concatenate([b_ref[0], b_ref[1]], axis=0)
    y = _dot(fc_ref[...], bcat)
    o_ref[...] = g_ref[...] * (y + s_ref[...] * skip_ref[o:o + 1, :])


def _hc_call(l, o, fc, b4, s2d, g2d, skip_t):
    nb, _, n1, w = b4.shape
    rows = s2d.shape[0] // nb
    tn = min(2048, w)
    blk = pl.BlockSpec((rows, tn), lambda b, j: (b, j))
    return pl.pallas_call(
        functools.partial(_hc_kernel, o),
        out_shape=jax.ShapeDtypeStruct(s2d.shape, F32),
        grid=(nb, w // tn),
        in_specs=[pl.BlockSpec(fc.shape, lambda b, j: (0, 0)),
                  pl.BlockSpec((None, 2, n1, tn), lambda b, j: (b, 0, 0, j)),
                  blk, blk,
                  pl.BlockSpec((None, HY_ORDER, tn), lambda b, j: (l, 0, j))],
        out_specs=blk,
        compiler_params=_cp(("parallel", "parallel"), VMEM_LIMIT),
        name="hyena_dft3",
    )(fc, b4, s2d, g2d, skip_t)


def _postmix_kernel(*refs):
    rows = {0, 1, 2, 3} | set(range(len(refs) - 4, len(refs)))
    sub = refs[0].shape[0] // ROW_SPLIT
    for k in range(ROW_SPLIT):
        sl = pl.ds(k * sub, sub)
        _postmix_body(*[r.at[sl] if i in rows else r for i, r in enumerate(refs)])


def _postmix_body(x_ref, oa_ref, ob_ref, oc_ref, mod_ref, g_ref, wa_ref, wb_ref, wc_ref, wrh_ref, wrl_ref, br_ref,
                  x1_ref, h2_ref, idx_ref, gate_ref):
    na = _rms(oa_ref[...]) * g_ref[:, :W_A]
    nb = _rms(ob_ref[...]) * g_ref[:, W_A:W_A + W_B]
    nc = _rms(oc_ref[...]) * g_ref[:, W_A + W_B:]
    mix = (_dot(na.astype(BF16), wa_ref[...]) + _dot(nb.astype(BF16), wb_ref[...])
           + _dot(nc.astype(BF16), wc_ref[...]))
    x1 = x_ref[...] + mod_ref[2:3, :] * mix
    x1_ref[...] = x1
    h2 = _rms(x1) * (1.0 + mod_ref[4:5, :]) + mod_ref[3:4, :]
    h2_ref[...] = h2
    hh, hl = _split(h2)
    vals = _dot(hh, wrh_ref[...]) + _dot(hh, wrl_ref[...]) + _dot(hl, wrh_ref[...]) + br_ref[...]
    lane = lax.broadcasted_iota(jnp.int32, vals.shape, 1).astype(F32)
    idx_out = jnp.zeros(vals.shape, F32)
    top = jnp.zeros(vals.shape, F32)
    m0 = None
    for k in range(TOP_K):
        m = vals.max(axis=-1, keepdims=True)
        sel = jnp.min(jnp.where(vals == m, lane, float(LANE)), axis=-1, keepdims=True)
        if m0 is None:
            m0 = m
        idx_out = jnp.where(lane == k, sel, idx_out)
        top = jnp.where(lane == k, jnp.exp(m - m0), top)
        vals = jnp.where(lane == sel, -jnp.inf, vals)
    idx_ref[...] = idx_out.astype(jnp.int32)
    gate_ref[...] = top / jnp.sum(top, axis=-1, keepdims=True)


def _postmix_call(l, x, oa, ob, oc, mod, seq, wts):
    t, d = x.shape
    tm = TOKEN_TILE
    ncond = mod.shape[1]

    def cond(i):
        return (i * tm) // seq if ncond > 1 else 0

    row = lambda w: pl.BlockSpec((tm, w), lambda i: (i, 0))
    lay = lambda a: pl.BlockSpec((None,) + a.shape[1:], lambda i: (l,) + (0,) * (a.ndim - 1))
    names = ["g_out", "wo_a", "wo_b", "wo_c", "wr_hi", "wr_lo", "b_router"]
    return pl.pallas_call(
        _postmix_kernel,
        out_shape=[jax.ShapeDtypeStruct((t, d), F32), jax.ShapeDtypeStruct((t, d), F32),
                   jax.ShapeDtypeStruct((t, LANE), jnp.int32), jax.ShapeDtypeStruct((t, LANE), F32)],
        grid=(t // tm,),
        in_specs=[row(d), row(W_A), row(W_B), row(W_C),
                  pl.BlockSpec((None, None, 6, d), lambda i: (l, cond(i), 0, 0))] + [lay(wts[k]) for k in names],
        out_specs=[row(d), row(d), row(LANE), row(LANE)],
        compiler_params=_cp(("parallel",), VMEM_LIMIT),
        name="postmix",
    )(x, oa, ob, oc, mod, *[wts[k] for k in names])


SUB = 8


NGRP = ROW_BLOCK // SUB


def _moe_kernel(l, be_ref, rs_ref, nv_ref, nu_ref, run_ref, nxt_ref, tok_ref, gate_ref,
                h_ref, wi_hbm, bi_ref, wo_hbm, bo_ref, y_ref, buf, obuf, xb, wi_buf, wo_buf, sem):
    s = pl.program_id(0)
    tc = h_ref.shape[0]

    def weight_copies(e, slot):
        return (pltpu.make_async_copy(wi_hbm.at[l, e], wi_buf.at[slot], sem.at[slot, 0]),
                pltpu.make_async_copy(wo_hbm.at[l, e], wo_buf.at[slot], sem.at[slot, 1]))

    @pl.when(s == 0)
    def _():
        y_ref[...] = jnp.zeros_like(y_ref)
        buf[...] = jnp.zeros_like(buf)
        obuf[...] = jnp.zeros_like(obuf)
        for cp in weight_copies(be_ref[0], 0):
            cp.start()

    run = run_ref[s]
    wslot = lax.rem(run, 2)
    first = jnp.logical_or(s == 0, run != run_ref[jnp.maximum(s - 1, 0)])

    @pl.when(jnp.logical_and(first, s < nu_ref[0] + 2))
    def _():
        for cp in weight_copies(be_ref[s], wslot):
            cp.wait()

        @pl.when(nxt_ref[s] >= 0)
        def _():
            for cp in weight_copies(nxt_ref[s], 1 - wslot):
                cp.start()

    @pl.when(s < nu_ref[0] + 2)
    def _():
        slot_g = lax.rem(s, 2)
        slot_c = 1 - slot_g

        xb[...] = buf[slot_c].reshape(ROW_BLOCK, buf.shape[3]).astype(BF16)

        base_s = rs_ref[s]
        nv = nv_ref[s]
        for g in range(NGRP):
            ts = [jnp.where(g * SUB + j < nv, tok_ref[base_s + g * SUB + j], tc) for j in range(SUB)]
            new = [y_ref[pl.ds(ts[j], 1), :] + gate_ref[base_s + g * SUB + j] * obuf[slot_g, g, j:j + 1, :]
                   for j in range(SUB)]
            for j in range(SUB):
                y_ref[pl.ds(ts[j], 1), :] = new[j]

        base_g = rs_ref[s + 2]
        for g in range(NGRP):
            rows = [h_ref[pl.ds(tok_ref[base_g + g * SUB + j], 1), :] for j in range(SUB)]
            for j in range(SUB):
                buf[slot_g, g, j:j + 1, :] = rows[j]

        gu = _dot(xb[...], wi_buf[wslot]) + bi_ref[...]
        gt = jnp.minimum(gu[:, :D_FF], SWIGLU_LIMIT)
        lin = jnp.clip(gu[:, D_FF:], -SWIGLU_LIMIT, SWIGLU_LIMIT)
        act = (lin + 1.0) * gt * (1.0 / (1.0 + jnp.exp(-SWIGLU_ALPHA * gt)))
        out = _dot(act.astype(BF16), wo_buf[wslot]) + bo_ref[...]
        obuf[slot_c] = out.reshape(obuf.shape[1:])


def _moe_kernel_entry(has_alias, l, *refs):
    refs = list(refs)
    if has_alias:
        del refs[13]
    _moe_kernel(l, *refs)


def _route(idx, gates, n_blocks):
    m = idx.shape[0] * TOP_K
    e = idx.reshape(m)
    flat = jnp.arange(m, dtype=jnp.int32)
    skey, gate = lax.sort((e * m + flat, gates.reshape(m)), num_keys=1)
    tok = (skey % m) // TOP_K
    experts = jnp.arange(N_EXPERTS, dtype=jnp.int32)
    cnt = jnp.sum((e[:, None] == experts[None, :]).astype(jnp.int32), axis=0)
    nblk = (cnt + ROW_BLOCK - 1) // ROW_BLOCK
    bend = jnp.cumsum(nblk)
    n_used = bend[-1]
    blk = jnp.arange(-2, n_blocks + 2, dtype=jnp.int32)
    bcl = jnp.clip(blk, 0, n_used - 1)
    be = jnp.sum((bend[None, :] <= bcl[:, None]).astype(jnp.int32), axis=1)
    oh = (be[:, None] == experts[None, :]).astype(jnp.int32)
    pick = lambda v: jnp.sum(oh * v[None, :], axis=1)
    off = (bcl - pick(bend - nblk)) * ROW_BLOCK
    valid = (blk >= 0) & (blk < n_used)
    rs = jnp.where(valid, pick(jnp.cumsum(cnt) - cnt) + off, 0)
    nv = jnp.where(valid, jnp.clip(pick(cnt) - off, 0, ROW_BLOCK), 0)
    pad = jnp.zeros((ROW_BLOCK,), jnp.int32)
    be_step = be[1:n_blocks + 3]
    change = jnp.concatenate([jnp.zeros((1,), jnp.int32), (be_step[1:] != be_step[:-1]).astype(jnp.int32)])
    later = (experts[None, :] > be_step[:, None]) & (nblk[None, :] > 0)
    nxt = jnp.min(jnp.where(later, experts[None, :], N_EXPERTS), axis=1)
    nxt = jnp.where(nxt == N_EXPERTS, -1, nxt)
    return (be_step, rs, nv, n_used.reshape(1), jnp.cumsum(change), nxt, jnp.concatenate([tok, pad]),
            jnp.concatenate([gate, pad.astype(F32)]))


def _moe_call(l, h2, idx, gates, wts):
    t, d = h2.shape
    tc = min(4096, t)
    n_blocks = tc * TOP_K // ROW_BLOCK + N_EXPERTS
    y = None
    for c in range(t // tc):
        route = _route(idx[c * tc:(c + 1) * tc], gates[c * tc:(c + 1) * tc], n_blocks)
        wspec = lambda r, w: pl.BlockSpec((None, None, r, w), lambda s, be, *_: (l, be[s], 0, 0))
        hbm = pl.BlockSpec(memory_space=pl.ANY)
        in_specs = [pl.BlockSpec((tc, d), lambda i, *_: (c, 0), pipeline_mode=pl.Buffered(1)),
                    hbm, wspec(1, 2 * D_FF), hbm, wspec(1, d)]
        args = [h2, wts["w_moe_in"], wts["b_moe_in"], wts["w_moe_out"], wts["b_moe_out"]]
        aliases = {}
        if y is not None:
            in_specs.append(pl.BlockSpec(memory_space=pl.ANY))
            args.append(y)
            aliases = {len(route) + len(args) - 1: 0}
        grid_spec = pltpu.PrefetchScalarGridSpec(
            num_scalar_prefetch=len(route),
            grid=(n_blocks + 2,),
            in_specs=in_specs,
            out_specs=pl.BlockSpec((None, tc + SUB, d), lambda i, *_: (c, 0, 0), pipeline_mode=pl.Buffered(1)),
            scratch_shapes=[pltpu.VMEM((2, NGRP, SUB, d), F32), pltpu.VMEM((2, NGRP, SUB, d), F32),
                            pltpu.VMEM((ROW_BLOCK, d), BF16),
                            pltpu.VMEM((2, d, 2 * D_FF), BF16), pltpu.VMEM((2, D_FF, d), BF16),
                            pltpu.SemaphoreType.DMA((2, 2))],
        )
        y = pl.pallas_call(
            functools.partial(_moe_kernel_entry, y is not None, l),
            out_shape=jax.ShapeDtypeStruct((t // tc, tc + SUB, d), F32),
            grid_spec=grid_spec,
            input_output_aliases=aliases,
            compiler_params=_cp(("arbitrary",), VMEM_LIMIT),
            name="moe_experts",
        )(*route, *args)
    return y


def _final_kernel(x_ref, y_ref, mod_ref, o_ref):
    o_ref[...] = x_ref[...] + mod_ref[5:6, :] * y_ref[...]


def _final_call(l, x, y, mod, seq):
    t, d = x.shape
    tm = TOKEN_TILE
    ncond = mod.shape[1]
    row = pl.BlockSpec((tm, d), lambda i: (i, 0))
    return pl.pallas_call(
        _final_kernel,
        out_shape=jax.ShapeDtypeStruct((t, d), F32),
        grid=(t // tm,),
        in_specs=[row, _chunk_rows_spec(y, tm),
                  pl.BlockSpec((None, None, 6, d), lambda i: (l, (i * tm) // seq if ncond > 1 else 0, 0, 0))],
        out_specs=row,
        compiler_params=_cp(("parallel",), VMEM_LIMIT),
        name="final_residual",
    )(x, y, mod)


def _rope_tables(seq, rot_dim, lane_map):
    shift = rot_dim // 2
    n_rows = seq // GRID_W
    rows = jnp.repeat(jnp.arange(n_rows, dtype=F32), GRID_W)
    cols = jnp.tile(jnp.arange(GRID_W, dtype=F32), n_rows)
    axis_dim = rot_dim // 2
    inv_freq = ROPE_THETA ** (-jnp.arange(0, axis_dim, 2, dtype=F32) / axis_dim)
    ang = jnp.concatenate([rows[:, None] * inv_freq, cols[:, None] * inv_freq], axis=-1)
    cos, sin = jnp.cos(ang), jnp.sin(ang)
    pair = np.zeros((LANE,), np.int32)
    in_rot = np.zeros((LANE,), np.float32)
    first = np.zeros((LANE,), np.float32)
    for ln in range(LANE):
        m = lane_map(ln)
        if m is not None:
            pair[ln], in_rot[ln], first[ln] = m[0], 1.0, 1.0 if m[1] == 0 else 0.0
    c = jnp.where(in_rot[None, :] > 0, cos[:, pair], 1.0)
    s = sin[:, pair] * in_rot[None, :] * (1.0 - 2.0 * first[None, :])
    perm = np.zeros((2 * LANE, 2 * LANE), np.float32)
    for ln in range(LANE):
        if in_rot[ln] > 0:
            src = ln + shift if first[ln] > 0 else ln - shift
            perm[src, ln] = perm[LANE + src, LANE + ln] = 1.0
    return jnp.stack([c, s]).astype(F32), jnp.asarray(perm, BF16)


def _lane_map_a(ln):
    o = ln - NOPE_A
    if 0 <= o < ROPE_A:
        return (o % (ROPE_A // 2), o // (ROPE_A // 2))
    return None


def _lane_map_c(ln):
    o = ln % HD_C
    return (o % (HD_C // 2), o // (HD_C // 2))


def _phase(num, den):
    ang = (2.0 * math.pi / den) * (num % den).astype(F32)
    return jnp.cos(ang), jnp.sin(ang)


def _ctx_dft(n):
    f = jnp.arange(n, dtype=jnp.int32)[:, None]
    t = jnp.arange(n, dtype=jnp.int32)[None, :]
    c, s = _phase((2 * f + 1) * t, 4 * n)
    fwd = jnp.concatenate([c, -s], axis=0).astype(BF16)
    inv = jnp.concatenate([c.T, -s.T], axis=1).astype(BF16)
    return c, s, fwd, inv


def _lat_dft(seq):
    n = 2 * seq
    n1, n2 = FFT_N1, n // FFT_N1
    k1 = jnp.arange(n1, dtype=jnp.int32)
    c1, s1 = _phase(k1[:, None] * k1[None, :], n1)
    f1f = jnp.concatenate([c1, -s1], axis=0)
    f1d = f1f[:, :n1 // 2].astype(BF16)
    fc = jnp.concatenate([c1[:, :n1 // 2].T, -s1[:, :n1 // 2].T], axis=1).astype(BF16)
    k2 = jnp.arange(n2, dtype=jnp.int32)
    num = (k2[None, :, None] * k2[None, None, :]) * n1 + k2[None, None, :] * k1[:, None, None]
    cm, sm = _phase(num, n)
    mr, mi = cm, -sm
    f2 = jnp.concatenate([jnp.concatenate([mr, -mi], axis=2), jnp.concatenate([mi, mr], axis=2)], axis=1)
    mrt, mit = jnp.swapaxes(mr, 1, 2), jnp.swapaxes(mi, 1, 2)
    g2 = jnp.concatenate([jnp.concatenate([mrt, mit], axis=2), jnp.concatenate([-mit, mrt], axis=2)], axis=1)
    f2h, f2l = _split(f2)
    return dict(f1f=f1f, f1d=f1d, fc=fc, f2=f2h, f2l=f2l, g2=g2.astype(BF16), n1=n1, n2=n2, n=n)


def _hy_features(seq):
    t = jnp.linspace(0.0, 1.0, seq, dtype=F32)[:, None]
    bands = (HY_EMB - 1) // 2
    f = jnp.linspace(1e-4, bands - 1, bands, dtype=F32)[None, :]
    w = 2.0 * math.pi * jnp.arange(seq, dtype=F32)[:, None] / seq
    z = jnp.concatenate([t, jnp.cos(f * w), jnp.sin(f * w)], axis=-1)
    z = jnp.pad(z, ((0, 0), (0, HY_FO - HY_EMB)))
    deltas = jnp.abs(jnp.linspace(HY_MIN_DECAY, HY_MAX_DECAY, HY_CH, dtype=F32))
    return z, jnp.exp(-t * deltas)


def _prep_weights(w_in, mla_g_qa, mla_w_uq, mla_g_kva, mla_w_ukv, mla_g_q, mla_g_k, gqa_g_q, gqa_g_k,
                  g_out, w_out, w_router, b_router, hy_w_in):
    depth = w_in.shape[0]
    cuts = np.cumsum([0, Q_RANK, KV_RANK, ROPE_A, 3 * HY_CH, H_C * HD_C, KV_C * HD_C, KV_C * HD_C])
    wb = w_in.astype(BF16)
    zeros = lambda w: jnp.zeros(wb.shape[:2] + (w,), BF16)
    parts = [wb[:, :, cuts[0]:cuts[2]], zeros(NOPE_A), wb[:, :, cuts[2]:cuts[3]], zeros(LANE - QK_A),
             wb[:, :, cuts[3]:cuts[4]]]
    for h in range(H_C):
        wh = wb[:, :, cuts[4] + h * HD_C:cuts[4] + (h + 1) * HD_C]
        parts += [wh, zeros(HD_C)] if h < H_C // KV_C else [zeros(HD_C), wh]
    w_in_p = jnp.concatenate(parts + [wb[:, :, cuts[5]:cuts[7]]], axis=-1)

    w_uq = jnp.pad(mla_w_uq.reshape(depth, Q_RANK, H_A, QK_A), ((0, 0), (0, 0), (0, 0), (0, LANE - QK_A)))
    w_uq = w_uq.reshape(depth, Q_RANK, HQ).astype(BF16)
    ukv = mla_w_ukv.reshape(depth, KV_RANK, H_A, NOPE_A + V_A)
    wk = jnp.pad(ukv[..., :NOPE_A], ((0, 0), (0, 0), (0, 0), (0, LANE - NOPE_A))).reshape(depth, KV_RANK, HQ)
    wv = ukv[..., NOPE_A:].reshape(depth, KV_RANK, W_A)
    w_kv = jnp.concatenate([wk, wv], axis=-1).astype(BF16)

    def pad_row(v):
        return jnp.pad(v, ((0, 0), (0, HQ - v.shape[1])))

    head_a = lambda g: jnp.tile(jnp.pad(g, ((0, 0), (0, LANE - QK_A))), (1, H_A))
    head_c = lambda g, reps: jnp.tile(g, (1, reps))
    gains = jnp.stack([
        pad_row(mla_g_qa), pad_row(mla_g_kva),
        head_a(mla_g_q) * (QK_A ** -0.5 * LOG2E), head_a(mla_g_k),
        head_c(gqa_g_q, 2 * H_C) * (HD_C ** -0.5 * LOG2E), pad_row(head_c(gqa_g_k, 2)),
        jnp.zeros((depth, HQ), F32), jnp.zeros((depth, HQ), F32)], axis=1).astype(F32)

    g_per = H_C // KV_C
    perm_c = np.concatenate([np.arange(h * HD_C, (h + 1) * HD_C) for g in range(g_per) for h in (g, g + g_per)])
    rows_c = W_A + W_B + perm_c
    g_o = jnp.concatenate([g_out[:, :W_A + W_B], g_out[:, rows_c]], axis=1).reshape(depth, 1, -1)
    wr = jnp.pad(w_router, ((0, 0), (0, 0), (0, LANE - N_EXPERTS)))
    wr_hi, wr_lo = _split(wr)
    br = jnp.pad(b_router, ((0, 0), (0, LANE - N_EXPERTS)), constant_values=-1e30).reshape(depth, 1, LANE)
    return dict(w_in=w_in_p, w_uq=w_uq, w_kv=w_kv, gains=gains, g_out=g_o,
                wo_a=w_out[:, :W_A].astype(BF16), wo_b=w_out[:, W_A:W_A + W_B].astype(BF16),
                wo_c=w_out[:, rows_c].astype(BF16), wr_hi=wr_hi, wr_lo=wr_lo, b_router=br,
                hy_w_in=jnp.pad(hy_w_in, ((0, 0), (0, HY_FO - HY_EMB), (0, 0))))


def _hyena_lat(l, u, hw, kf, dft, seq):
    t = u.shape[0]
    nb = t // seq
    n1, n2 = dft["n1"], dft["n2"]
    w2 = n2 * HY_CH
    s, x1, x2 = _hconv_call(l, u, hw, seq, n2)
    for o, gate in enumerate((x1, x2)):
        a = _ha_call(dft["f1d"], s, nb)
        b = _hb_call(o, a.reshape(nb, 2, n1, n2, HY_CH), dft["f2"], dft["g2"], kf)
        s = _hc_call(l, o, dft["fc"], b.reshape(nb, 2, n1, w2), s, gate, hw["skip_t"])
    return s.reshape(t, HY_CH)


def _filter_lat(l, z, dec, hw, dft, seq):
    kern, nrm = _filt_lat_call(l, z, dec, hw, seq)
    n1, n2, n = dft["n1"], dft["n2"], dft["n"]
    c = kern.shape[1]
    af = _fa_call(dft["f1f"], kern.reshape(n1, n2 * c))
    return _fb_call(af.reshape(2, n1, n2, c), dft["f2"], dft["f2l"], nrm, n)


def kernel(x_prompt, x_sample, c, c_ctx, cache_mla_ckv, cache_mla_kpe, cache_gqa_k, cache_gqa_v, w_mod, b_mod, w_in, mla_g_qa, mla_w_uq, mla_g_kva, mla_w_ukv, mla_g_q, mla_g_k, hy_conv_w, hy_conv_b, hy_w_in, hy_b_in, hy_w_mid, hy_b_mid, hy_w_out, hy_b_out, hy_freq, hy_skip, gqa_g_q, gqa_g_k, g_out, w_out, w_router, b_router, w_moe_in, b_moe_in, w_moe_out, b_moe_out):
    batch, seq_c, d = x_prompt.shape
    nb_l, seq_l, _ = x_sample.shape
    depth = w_in.shape[0]
    past = cache_mla_ckv.shape[2]
    assert d == D_MODEL and seq_l % GRID_W == 0 and (2 * seq_l) % FFT_N1 == 0

    wts = _prep_weights(w_in, mla_g_qa, mla_w_uq, mla_g_kva, mla_w_ukv, mla_g_q, mla_g_k, gqa_g_q, gqa_g_k,
                        g_out, w_out, w_router, b_router, hy_w_in)
    wts["w_moe_in"] = _cast_call(w_moe_in.reshape(depth * N_EXPERTS, d, 2 * D_FF), 1024).reshape(w_moe_in.shape)
    wts["w_moe_out"] = _cast_call(w_moe_out.reshape(depth * N_EXPERTS, D_FF, d), 1024).reshape(w_moe_out.shape)
    wts["b_moe_in"] = b_moe_in.reshape(depth, N_EXPERTS, 1, 2 * D_FF)
    wts["b_moe_out"] = b_moe_out.reshape(depth, N_EXPERTS, 1, d)
    hw = dict(hy_w_in=wts["hy_w_in"], hy_b_in=hy_b_in.reshape(depth, 1, HY_FO), hy_w_mid=hy_w_mid,
              hy_b_mid=hy_b_mid.reshape(depth, HY_INNER, 1, HY_FO), hy_w_out=hy_w_out,
              hy_b_out=hy_b_out.reshape(depth, 1, -1), hy_freq=hy_freq.reshape(depth, 1, HY_FO),
              hy_conv_w=hy_conv_w, hy_conv_b=hy_conv_b.reshape(depth, 1, -1), hy_skip=hy_skip)
    dft_l = _lat_dft(seq_l)
    hw["skip_t"] = jnp.tile(hy_skip, (1, 1, dft_l["n2"]))

    conds = jnp.zeros((8, d), F32).at[0].set(c_ctx).at[1:1 + nb_l].set(c)
    mod = _mod_call(conds, w_mod, b_mod).reshape(depth, 8, 6, d)
    mod_c, mod_l = mod[:, 0:1], mod[:, 1:1 + nb_l]

    rope_tabs = (_rope_tables(seq_l, ROPE_A, _lane_map_a), _rope_tables(seq_l, HD_C, _lane_map_c))
    kax, vax = _cachekv_call(cache_mla_ckv, jnp.pad(cache_mla_kpe, ((0, 0), (0, 0), (0, 0), (NOPE_A, LANE - QK_A))), wts)
    kcx = cache_gqa_k.reshape(nb_l, depth, past, KV_C * HD_C).astype(BF16)
    vcx = cache_gqa_v.reshape(nb_l, depth, past, KV_C * HD_C).astype(BF16)
    vcx = jnp.concatenate([vcx, jnp.ones_like(vcx)], axis=-1)

    z_c, dec_c = _hy_features(seq_c)
    cmat, smat, fwd_c, inv_c = _ctx_dft(seq_c)
    z_l, dec_l = _hy_features(seq_l)
    z_full = jnp.concatenate([z_l, jnp.zeros((1, HY_FO), F32), z_l[:0:-1]], axis=0)
    dec_full = jnp.concatenate([dec_l, jnp.zeros((1, HY_CH), F32), dec_l[:0:-1]], axis=0)

    xc = x_prompt.reshape(batch * seq_c, d)
    xl = x_sample.reshape(nb_l * seq_l, d)
    yc = yl = None
    new_ckv, new_kpe, new_k, new_v = [], [], [], []
    for l in range(depth):
        (xc, qa, ka, va, ckv, kpe, qc, kc, vc, kcf, vcf, u) = _premix_call(l, xc, yc, mod_c, seq_c, False, wts, None)
        new_ckv.append(ckv)
        new_kpe.append(kpe[:, NOPE_A:QK_A])
        new_k.append(kcf)
        new_v.append(vcf)
        oa, oc = _attn_ctx_call(qa, ka, va, qc, kc, vc, seq_c)
        kf_c = _filt_ctx_call(l, z_c, dec_c, hw, cmat, smat)
        ob = _hy_ctx_call(l, u, hw, kf_c, fwd_c, inv_c, seq_c)
        xc, h2, idx, gates = _postmix_call(l, xc, oa, ob, oc, mod_c, seq_c, wts)
        yc = _moe_call(l, h2, idx[:, :TOP_K], gates[:, :TOP_K], wts)
        (xl, qa, ka, va, _, _, qc, kc, vc, _, _, u) = _premix_call(l, xl, yl, mod_l, seq_l, True, wts, rope_tabs)
        oa, oc = _attn_lat_call(l, qa, ka, va, qc, kc, vc, kax, vax, kcx, vcx, seq_l)
        kf_l = _filter_lat(l, z_full, dec_full, hw, dft_l, seq_l)
        ob = _hyena_lat(l, u, hw, kf_l, dft_l, seq_l)
        xl, h2, idx, gates = _postmix_call(l, xl, oa, ob, oc, mod_l, seq_l, wts)
        yl = _moe_call(l, h2, idx[:, :TOP_K], gates[:, :TOP_K], wts)
    y_prompt = _final_call(depth - 1, xc, yc, mod_c, seq_c).reshape(batch, seq_c, d)
    y_sample = _final_call(depth - 1, xl, yl, mod_l, seq_l).reshape(nb_l, seq_l, d)
    stack = lambda xs, tail: jnp.stack([a.reshape((batch, seq_c) + tail) for a in xs], axis=1)
    return (y_prompt, y_sample, stack(new_ckv, (KV_RANK,)), stack(new_kpe, (ROPE_A,)),
            stack(new_k, (KV_C, HD_C)), stack(new_v, (KV_C, HD_C)))
```
